```python
import jax
import jax.numpy as jnp
from jax import lax
import numpy as np

D_MODEL = 1024
BATCH = 4
SEQ = 4096
DEPTH = 1
DEC_BATCH = 128
DEC_SEQ = 1
PAST_LEN = 16384
PAGE_SIZE = 128

HEAD_DIM = 64
ATT_HEADS = (D_MODEL // 2) // HEAD_DIM
KV_HEADS = 2
GROUP = ATT_HEADS // KV_HEADS
ATT_WIDTH = ATT_HEADS * HEAD_DIM
RWKV_HEADS = (D_MODEL // 2) // HEAD_DIM
RWKV_WIDTH = RWKV_HEADS * HEAD_DIM
MIX_WIDTH = ATT_WIDTH + RWKV_WIDTH
WINDOW = 128
BLOCK = 128
LORA_W = 64
LORA_A = 64
LORA_G = 128
Q_COLS = ATT_WIDTH
KV_COLS = KV_HEADS * HEAD_DIM
ATT_PROJ = Q_COLS + 2 * KV_COLS
RWKV_SPLITS = (RWKV_WIDTH, 2 * RWKV_WIDTH, 3 * RWKV_WIDTH,
               3 * RWKV_WIDTH + LORA_W, 3 * RWKV_WIDTH + LORA_W + LORA_A)
RWKV_PROJ = 3 * RWKV_WIDTH + LORA_W + LORA_A + LORA_G
IN_PROJ = ATT_PROJ + RWKV_PROJ
PLE_DIM = 256
N_EXPERTS = 32
TOP_K = 4
D_FF = 1024
SWIGLU_LIMIT = 7.0
SWIGLU_ALPHA = 1.702
LN_EPS = 1e-5
GN_EPS = 64e-5
DEEPNORM_ALPHA = (2 * DEPTH) ** 0.25
DEEPNORM_BETA = (8 * DEPTH) ** -0.25

kernel_name = 'hymba_swa_sink_rwkv7_moe_deepnorm_step'


def layer_norm(x, g, b):
    xf = x.astype(jnp.float32)
    mu = jnp.mean(xf, -1, keepdims=True)
    var = jnp.mean(jnp.square(xf - mu), -1, keepdims=True)
    return ((xf - mu) * lax.rsqrt(var + LN_EPS) * g + b).astype(x.dtype)


def alibi_slopes():
    return 2.0 ** (-8.0 * jnp.arange(1, ATT_HEADS + 1, dtype=jnp.float32) / ATT_HEADS)


def sink_softmax(scores, sinks):
    sk = sinks.astype(jnp.float32)[:, :, None, None]
    m = jnp.maximum(jnp.max(scores, -1, keepdims=True), sk)
    e = jnp.exp(scores - m)
    return e / (jnp.sum(e, -1, keepdims=True) + jnp.exp(sk - m))


def swa_prompt(q, k, v, sinks):
    b, s = q.shape[0], q.shape[1]
    nb = s // BLOCK
    qb = q.reshape(b, nb, BLOCK, KV_HEADS, GROUP, HEAD_DIM)

    def band(t):
        tp = jnp.pad(t, ((0, 0), (BLOCK, 0), (0, 0), (0, 0))).reshape(b, nb + 1, BLOCK, KV_HEADS, HEAD_DIM)
        return jnp.concatenate([tp[:, :-1], tp[:, 1:]], axis=2)

    kb, vb = band(k), band(v)
    scores = jnp.einsum('bnqkgd,bnskd->bnkgqs', qb, kb).astype(jnp.float32) * (HEAD_DIM ** -0.5)
    dist = (jnp.arange(BLOCK)[:, None] + BLOCK) - jnp.arange(2 * BLOCK)[None, :]
    kabs = (jnp.arange(nb)[:, None] - 1) * BLOCK + jnp.arange(2 * BLOCK)[None, :]
    valid = ((dist >= 0) & (dist <= WINDOW))[None] & (kabs >= 0)[:, None, :]
    slopes = alibi_slopes().reshape(KV_HEADS, GROUP)
    bias = -slopes[:, :, None, None] * dist.astype(jnp.float32)
    scores = jnp.where(valid[:, None, None], scores + bias, -jnp.inf)
    probs = sink_softmax(scores, sinks.reshape(KV_HEADS, GROUP))
    out = jnp.einsum('bnkgqs,bnskd->bnqkgd', probs.astype(v.dtype), vb).reshape(b, s, ATT_WIDTH)
    w_keep = min(WINDOW, s)
    return out, k[:, s - w_keep:], v[:, s - w_keep:]


def swa_sample(q, k, v, sinks, cache_k, cache_v):
    bd, t = q.shape[0], q.shape[1]
    w_buf = cache_k.shape[1]
    keys = jnp.concatenate([cache_k.astype(k.dtype), k], axis=1)
    vals = jnp.concatenate([cache_v.astype(v.dtype), v], axis=1)
    qg = q.reshape(bd, t, KV_HEADS, GROUP, HEAD_DIM)
    scores = jnp.einsum('btkgd,bskd->bkgts', qg, keys).astype(jnp.float32) * (HEAD_DIM ** -0.5)
    qpos = PAST_LEN + jnp.arange(t)
    kpos = PAST_LEN - w_buf + jnp.arange(w_buf + t)
    dist = qpos[:, None] - kpos[None, :]
    valid = (dist >= 0) & (dist <= WINDOW)
    slopes = alibi_slopes().reshape(KV_HEADS, GROUP)
    bias = -slopes[:, :, None, None] * dist.astype(jnp.float32)
    scores = jnp.where(valid, scores + bias, -jnp.inf)
    probs = sink_softmax(scores, sinks.reshape(KV_HEADS, GROUP))
    out = jnp.einsum('bkgts,bskd->btkgd', probs.astype(vals.dtype), vals).reshape(bd, t, ATT_WIDTH)
    return out, keys[:, t:], vals[:, t:]


def wkv_scan(s0, r, w, k, v, kk, a):
    def step(s, inp):
        r_t, w_t, k_t, v_t, kk_t, a_t = inp
        sa = jnp.einsum('bhvk,bhk->bhv', s, kk_t)
        s = (s * w_t[:, :, None, :] - sa[..., None] * (kk_t * a_t)[:, :, None, :]
             + v_t[..., None] * k_t[:, :, None, :])
        return s, jnp.einsum('bhvk,bhk->bhv', s, r_t)

    xs = tuple(jnp.swapaxes(z.astype(jnp.float32), 0, 1) for z in (r, w, k, v, kk, a))
    s, ys = lax.scan(step, s0.astype(jnp.float32), xs)
    return s, jnp.swapaxes(ys, 0, 1)


def rwkv_mix(rp, shift0, wkv0, lp):
    b, t = rp.shape[0], rp.shape[1]
    prev = jnp.concatenate([shift0[:, None].astype(rp.dtype), rp[:, :-1]], axis=1)
    xs = rp + lp['mu'] * (prev - rp)
    r, k, v, wd, ad, gd = jnp.split(xs, RWKV_SPLITS, axis=-1)
    w_log = -jax.nn.softplus(-(lp['w0'] + jnp.tanh(wd) @ lp['w_lora_up'])) - 0.5
    decay = jnp.exp(-jnp.exp(w_log.astype(jnp.float32)))
    a = jax.nn.sigmoid(lp['a0'] + ad @ lp['a_lora_up'])
    g = jax.nn.sigmoid(gd) @ lp['g_lora_up']
    heads = lambda z: z.reshape(b, t, RWKV_HEADS, HEAD_DIM)
    kk = heads(k * lp['k_k']).astype(jnp.float32)
    kk = kk / jnp.maximum(jnp.sqrt(jnp.sum(kk * kk, -1, keepdims=True)), 1e-12)
    k = k * (1.0 + (a - 1.0) * lp['k_a'])
    rh, kh, vh = heads(r), heads(k), heads(v)
    s_new, y = wkv_scan(wkv0, rh, heads(decay), kh, vh, kk, heads(a))
    mu = jnp.mean(y, -1, keepdims=True)
    var = jnp.mean(jnp.square(y - mu), -1, keepdims=True)
    yn = ((y - mu) * lax.rsqrt(var + GN_EPS)).reshape(b, t, RWKV_WIDTH) * lp['gn_g'] + lp['gn_b']
    bonus = jnp.sum((rh * kh * lp['r_k']).astype(jnp.float32), -1, keepdims=True) * vh
    out = (yn + bonus.reshape(b, t, RWKV_WIDTH)) * g
    return out.astype(rp.dtype), s_new.astype(wkv0.dtype), rp[:, -1]


def moe(x2d, w_router, b_router, w_gate_up, b_gate_up, w_down, b_down):
    logits = (x2d @ w_router + b_router).astype(jnp.float32)
    top_val, top_idx = lax.top_k(logits, TOP_K)
    gates = jax.nn.softmax(top_val, axis=-1)
    flat_e = top_idx.reshape(-1)
    order = jnp.argsort(flat_e)
    tok = order // TOP_K
    e_sorted = flat_e[order]
    sizes = jnp.bincount(flat_e, length=N_EXPERTS).astype(jnp.int32)
    h = lax.ragged_dot(x2d[tok], w_gate_up, sizes) + b_gate_up[e_sorted]
    gate = jnp.minimum(h[:, 0::2], SWIGLU_LIMIT)
    up = jnp.clip(h[:, 1::2], -SWIGLU_LIMIT, SWIGLU_LIMIT)
    act = (up + 1.0) * gate * jax.nn.sigmoid(SWIGLU_ALPHA * gate)
    out = lax.ragged_dot(act, w_down, sizes) + b_down[e_sorted]
    contrib = out.astype(jnp.float32) * gates.reshape(-1)[order][:, None]
    y = jnp.zeros(x2d.shape, jnp.float32).at[tok].add(contrib)
    return y.astype(x2d.dtype)


def trunk_layer(x, pe, attend, shift0, wkv0, lp):
    b, t = x.shape[0], x.shape[1]
    proj = x @ lp['w_in']
    q = proj[..., :Q_COLS].reshape(b, t, ATT_HEADS, HEAD_DIM)
    k = proj[..., Q_COLS:Q_COLS + KV_COLS].reshape(b, t, KV_HEADS, HEAD_DIM)
    v = proj[..., Q_COLS + KV_COLS:ATT_PROJ].reshape(b, t, KV_HEADS, HEAD_DIM)
    att, k_buf, v_buf = attend(q, k, v, lp['sinks'])
    rw, wkv, shift = rwkv_mix(proj[..., ATT_PROJ:], shift0, wkv0, lp)
    mixed = jnp.concatenate([att, rw], axis=-1) @ lp['w_out']
    x = layer_norm(DEEPNORM_ALPHA * x + mixed, lp['ln1_g'], lp['ln1_b'])
    ffn = moe(x.reshape(b * t, D_MODEL), lp['w_router'], lp['b_router'], lp['w_gate_up'],
              lp['b_gate_up'], lp['w_down'], lp['b_down']).reshape(b, t, D_MODEL)
    x = layer_norm(DEEPNORM_ALPHA * x + ffn, lp['ln2_g'], lp['ln2_b'])
    x = x + jax.nn.sigmoid(x @ lp['w_ple_gate']) * (pe @ lp['w_ple'])
    return x, (k_buf, v_buf, shift, wkv)


def setup_inputs(seed: int = 0) -> dict:
    key = jax.random.key(seed)
    keys = jax.random.split(key, 48)
    ctr = [0]

    def nxt():
        kk = keys[ctr[0]]
        ctr[0] += 1
        return kk

    def nrm(shape, scale):
        return scale * jax.random.normal(nxt(), shape, jnp.float32)

    def uni(shape, lo, hi):
        return jax.random.uniform(nxt(), shape, jnp.float32, lo, hi)

    w_buf = min(WINDOW, PAST_LEN)
    col_scale = jnp.concatenate([
        jnp.ones((Q_COLS + KV_COLS,), jnp.float32), jnp.full((KV_COLS,), DEEPNORM_BETA, jnp.float32),
        jnp.ones((2 * RWKV_WIDTH,), jnp.float32), jnp.full((RWKV_WIDTH,), DEEPNORM_BETA, jnp.float32),
        jnp.ones((LORA_W + LORA_A + LORA_G,), jnp.float32)])
    return {
        'x_prompt': nrm((BATCH, SEQ, D_MODEL), 1.0),
        'x_sample': nrm((DEC_BATCH, DEC_SEQ, D_MODEL), 1.0),
        'cache_k': nrm((DEPTH, DEC_BATCH, w_buf, KV_HEADS, HEAD_DIM), 1.0),
        'cache_v': nrm((DEPTH, DEC_BATCH, w_buf, KV_HEADS, HEAD_DIM), DEEPNORM_BETA),
        'state_shift': nrm((DEPTH, DEC_BATCH, RWKV_PROJ), 1.0),
        'state_wkv': nrm((DEPTH, DEC_BATCH, RWKV_HEADS, HEAD_DIM, HEAD_DIM), 0.3),
        'p_prompt': nrm((DEPTH, BATCH, SEQ, PLE_DIM), 1.0),
        'p_sample': nrm((DEPTH, DEC_BATCH, DEC_SEQ, PLE_DIM), 1.0),
        'ln_emb_g': 1.0 + nrm((D_MODEL,), 0.01),
        'ln_emb_b': nrm((D_MODEL,), 0.01),
        'w_in': nrm((DEPTH, D_MODEL, IN_PROJ), D_MODEL ** -0.5) * col_scale,
        'attn_sinks': nrm((DEPTH, ATT_HEADS), 0.5),
        'rwkv_mu': uni((DEPTH, RWKV_PROJ), 0.0, 1.0),
        'rwkv_w0': uni((DEPTH, RWKV_WIDTH), -6.0, -1.0),
        'rwkv_w_lora_up': nrm((DEPTH, LORA_W, RWKV_WIDTH), 0.1 * LORA_W ** -0.5),
        'rwkv_a0': nrm((DEPTH, RWKV_WIDTH), 0.1),
        'rwkv_a_lora_up': nrm((DEPTH, LORA_A, RWKV_WIDTH), 0.1 * LORA_A ** -0.5),
        'rwkv_g_lora_up': nrm((DEPTH, LORA_G, RWKV_WIDTH), LORA_G ** -0.5),
        'rwkv_k_k': 0.85 + nrm((DEPTH, RWKV_WIDTH), 0.05),
        'rwkv_k_a': 1.0 + nrm((DEPTH, RWKV_WIDTH), 0.05),
        'rwkv_r_k': nrm((DEPTH, RWKV_HEADS, HEAD_DIM), 0.1),
        'rwkv_gn_g': 1.0 + nrm((DEPTH, RWKV_WIDTH), 0.1),
        'rwkv_gn_b': nrm((DEPTH, RWKV_WIDTH), 0.01),
        'w_out': nrm((DEPTH, MIX_WIDTH, D_MODEL), DEEPNORM_BETA * MIX_WIDTH ** -0.5),
        'ln1_g': 1.0 + nrm((DEPTH, D_MODEL), 0.01),
        'ln1_b': nrm((DEPTH, D_MODEL), 0.01),
        'w_router': nrm((DEPTH, D_MODEL, N_EXPERTS), D_MODEL ** -0.5),
        'b_router': nrm((DEPTH, N_EXPERTS), 0.01),
        'w_gate_up': nrm((DEPTH, N_EXPERTS, D_MODEL, 2 * D_FF), D_MODEL ** -0.5),
        'b_gate_up': nrm((DEPTH, N_EXPERTS, 2 * D_FF), 0.01),
        'w_down': nrm((DEPTH, N_EXPERTS, D_FF, D_MODEL), DEEPNORM_BETA * D_FF ** -0.5),
        'b_down': nrm((DEPTH, N_EXPERTS, D_MODEL), 0.01),
        'ln2_g': 1.0 + nrm((DEPTH, D_MODEL), 0.01),
        'ln2_b': nrm((DEPTH, D_MODEL), 0.01),
        'w_ple': nrm((DEPTH, PLE_DIM, D_MODEL), DEEPNORM_BETA * PLE_DIM ** -0.5),
        'w_ple_gate': nrm((DEPTH, D_MODEL, D_MODEL), D_MODEL ** -0.5),
    }


def reference(x_prompt, x_sample, cache_k, cache_v, state_shift, state_wkv, p_prompt, p_sample,
              ln_emb_g, ln_emb_b, w_in, attn_sinks, rwkv_mu, rwkv_w0, rwkv_w_lora_up, rwkv_a0,
              rwkv_a_lora_up, rwkv_g_lora_up, rwkv_k_k, rwkv_k_a, rwkv_r_k, rwkv_gn_g, rwkv_gn_b,
              w_out, ln1_g, ln1_b, w_router, b_router, w_gate_up, b_gate_up, w_down, b_down,
              ln2_g, ln2_b, w_ple, w_ple_gate):
    xp = layer_norm(x_prompt, ln_emb_g, ln_emb_b)
    xs = layer_norm(x_sample, ln_emb_g, ln_emb_b)
    bp = xp.shape[0]
    st_p = []
    st_s = []
    for i in range(DEPTH):
        lp = {
            'w_in': w_in[i], 'sinks': attn_sinks[i], 'mu': rwkv_mu[i], 'w0': rwkv_w0[i],
            'w_lora_up': rwkv_w_lora_up[i], 'a0': rwkv_a0[i], 'a_lora_up': rwkv_a_lora_up[i],
            'g_lora_up': rwkv_g_lora_up[i], 'k_k': rwkv_k_k[i], 'k_a': rwkv_k_a[i],
            'r_k': rwkv_r_k[i], 'gn_g': rwkv_gn_g[i], 'gn_b': rwkv_gn_b[i], 'w_out': w_out[i],
            'ln1_g': ln1_g[i], 'ln1_b': ln1_b[i], 'w_router': w_router[i], 'b_router': b_router[i],
            'w_gate_up': w_gate_up[i], 'b_gate_up': b_gate_up[i], 'w_down': w_down[i],
            'b_down': b_down[i], 'ln2_g': ln2_g[i], 'ln2_b': ln2_b[i], 'w_ple': w_ple[i],
            'w_ple_gate': w_ple_gate[i],
        }
        shift0 = jnp.zeros((bp, RWKV_PROJ), xp.dtype)
        wkv0 = jnp.zeros((bp, RWKV_HEADS, HEAD_DIM, HEAD_DIM), xp.dtype)
        xp, sp = trunk_layer(xp, p_prompt[i], swa_prompt, shift0, wkv0, lp)
        attend_s = lambda q, k, v, sinks, ck=cache_k[i], cv=cache_v[i]: swa_sample(q, k, v, sinks, ck, cv)
        xs, ss = trunk_layer(xs, p_sample[i], attend_s, state_shift[i], state_wkv[i], lp)
        st_p.append(sp)
        st_s.append(ss)
    stack = lambda sts, j: jnp.stack([s[j] for s in sts])
    return (xp, xs,
            stack(st_p, 0), stack(st_p, 1), stack(st_p, 2), stack(st_p, 3),
            stack(st_s, 0), stack(st_s, 1), stack(st_s, 2), stack(st_s, 3))
```

```python
import functools

import jax
import jax.numpy as jnp
from jax import lax
from jax.experimental import pallas as pl
from jax.experimental.pallas import tpu as pltpu

F32 = jnp.float32
BF16 = jnp.bfloat16
HIGHEST = lax.Precision.HIGHEST

D_MODEL = 1024
HEAD_DIM = 64
ATT_HEADS = 8
KV_HEADS = 2
GROUP = ATT_HEADS // KV_HEADS
ATT_WIDTH = ATT_HEADS * HEAD_DIM
KV_COLS = KV_HEADS * HEAD_DIM
RWKV_HEADS = 8
RWKV_WIDTH = RWKV_HEADS * HEAD_DIM
LORA_W = 64
LORA_A = 64
LORA_G = 128
RWKV_PROJ = 3 * RWKV_WIDTH + LORA_W + LORA_A + LORA_G
WINDOW = 128
BLOCK = 128
PLE_DIM = 256
N_EXPERTS = 32
TOP_K = 4
D_FF = 1024
SWIGLU_LIMIT = 7.0
SWIGLU_ALPHA = 1.702
LN_EPS = 1e-5
GN_EPS = 64e-5
DEPTH = 1
DEEPNORM_ALPHA = (2 * DEPTH) ** 0.25

LANES = 128
CHUNK = 64
EXPERT_TILE = 512
VMEM_LIMIT = 48 * 1024 * 1024


def _cparams(sem):
    return pltpu.CompilerParams(dimension_semantics=sem, vmem_limit_bytes=VMEM_LIMIT)


def _bdot(a, b):
    return jnp.dot(a.astype(BF16), b.astype(BF16), preferred_element_type=F32)


def _bdot_nt(a, b):
    return lax.dot_general(a.astype(BF16), b.astype(BF16), (((1,), (1,)), ((), ())),
                           preferred_element_type=F32)


def _bdot_tn(a, b):
    return lax.dot_general(a.astype(BF16), b.astype(BF16), (((0,), (0,)), ((), ())),
                           preferred_element_type=F32)


def _split_dot(m_bf16, x, passes):
    acc = None
    rem = x
    for _ in range(passes):
        hi = rem.astype(BF16)
        part = jnp.dot(m_bf16, hi, preferred_element_type=F32)
        acc = part if acc is None else acc + part
        rem = rem - hi.astype(F32)
    return acc


def _head_sum(x, bd_bf16):
    acc = None
    rem = x
    for _ in range(3):
        hi = rem.astype(BF16)
        part = jnp.dot(hi, bd_bf16, preferred_element_type=F32)
        acc = part if acc is None else acc + part
        rem = rem - hi.astype(F32)
    return acc


def _sigmoid(x):
    return 1.0 / (1.0 + jnp.exp(-x))


def _layer_norm(x, g, b):
    mu = jnp.mean(x, axis=-1, keepdims=True)
    xc = x - mu
    var = jnp.mean(xc * xc, axis=-1, keepdims=True)
    return xc * lax.rsqrt(var + LN_EPS) * g + b


def _inproj_body(x_ref, g_ref, b_ref, w_ref, xn_ref, q_ref, kv_ref, rp_ref):
    xn = _layer_norm(x_ref[...], g_ref[...], b_ref[...])
    xn_ref[...] = xn
    xb = xn.astype(BF16)
    q_ref[...] = jnp.dot(xb, w_ref[:, 0:ATT_WIDTH], preferred_element_type=F32)
    kv_ref[...] = jnp.dot(xb, w_ref[:, ATT_WIDTH:ATT_WIDTH + 2 * KV_COLS],
                          preferred_element_type=F32)
    rp_ref[...] = jnp.dot(xb, w_ref[:, ATT_WIDTH + 2 * KV_COLS:], preferred_element_type=F32)


def _inproj(x2d, g, b, w_bf16, tm):
    t = x2d.shape[0]
    in_proj = w_bf16.shape[1]
    row = lambda i: (i, 0)
    fixed = lambda i: (0, 0)
    return pl.pallas_call(
        _inproj_body,
        grid=(t // tm,),
        in_specs=[pl.BlockSpec((tm, D_MODEL), row),
                  pl.BlockSpec((1, D_MODEL), fixed),
                  pl.BlockSpec((1, D_MODEL), fixed),
                  pl.BlockSpec((D_MODEL, in_proj), fixed)],
        out_specs=[pl.BlockSpec((tm, D_MODEL), row),
                   pl.BlockSpec((tm, ATT_WIDTH), row),
                   pl.BlockSpec((tm, 2 * KV_COLS), row),
                   pl.BlockSpec((tm, RWKV_PROJ), row)],
        out_shape=[jax.ShapeDtypeStruct((t, D_MODEL), F32),
                   jax.ShapeDtypeStruct((t, ATT_WIDTH), F32),
                   jax.ShapeDtypeStruct((t, 2 * KV_COLS), F32),
                   jax.ShapeDtypeStruct((t, RWKV_PROJ), F32)],
        compiler_params=_cparams(("parallel",)),
        name="inproj",
    )(x2d, g, b, w_bf16)


def _alibi_slope(h):
    return 2.0 ** (-8.0 * (h + 1) / ATT_HEADS)


def _swa_prompt_body(sink_ref, q_ref, kvc_ref, kvp_ref, o_ref):
    n = pl.program_id(1)
    q = (q_ref[...] * (HEAD_DIM ** -0.5)).astype(BF16)
    kvc = kvc_ref[...].astype(BF16)
    kvp = kvp_ref[...].astype(BF16)
    row = lax.broadcasted_iota(jnp.int32, (BLOCK, 2 * BLOCK), 0)
    col = lax.broadcasted_iota(jnp.int32, (BLOCK, 2 * BLOCK), 1)
    dist = row + BLOCK - col
    valid = (dist >= 0) & (dist <= WINDOW) & ((col >= BLOCK) | (n > 0))
    distf = dist.astype(F32)
    outs = []
    for g in range(KV_HEADS):
        ks = slice(g * HEAD_DIM, (g + 1) * HEAD_DIM)
        vs = slice(KV_COLS + g * HEAD_DIM, KV_COLS + (g + 1) * HEAD_DIM)
        kband = jnp.concatenate([kvp[:, ks], kvc[:, ks]], axis=0)
        vband = jnp.concatenate([kvp[:, vs], kvc[:, vs]], axis=0)
        for j in range(GROUP):
            h = g * GROUP + j
            qh = q[:, h * HEAD_DIM:(h + 1) * HEAD_DIM]
            s = lax.dot_general(qh, kband, (((1,), (1,)), ((), ())), preferred_element_type=F32)
            s = jnp.where(valid, s - _alibi_slope(h) * distf, -jnp.inf)
            sink = sink_ref[h]
            m = jnp.maximum(jnp.max(s, axis=1, keepdims=True), sink)
            e = jnp.exp(s - m)
            den = jnp.sum(e, axis=1, keepdims=True) + jnp.exp(sink - m)
            p = e / den
            outs.append(jnp.dot(p.astype(BF16), vband, preferred_element_type=F32))
    o_ref[...] = jnp.concatenate(outs, axis=1)


def _swa_prompt(sinks, q, kv, nbatch, seq):
    nb = seq // BLOCK
    cur = lambda b, n: (b * nb + n, 0)
    prv = lambda b, n: (b * nb + jnp.maximum(n - 1, 0), 0)
    return pl.pallas_call(
        _swa_prompt_body,
        grid=(nbatch, nb),
        in_specs=[pl.BlockSpec(memory_space=pltpu.SMEM),
                  pl.BlockSpec((BLOCK, ATT_WIDTH), cur),
                  pl.BlockSpec((BLOCK, 2 * KV_COLS), cur),
                  pl.BlockSpec((BLOCK, 2 * KV_COLS), prv)],
        out_specs=pl.BlockSpec((BLOCK, ATT_WIDTH), cur),
        out_shape=jax.ShapeDtypeStruct((nbatch * seq, ATT_WIDTH), F32),
        compiler_params=_cparams(("parallel", "parallel")),
        name="swa_prompt",
    )(sinks, q, kv, kv)


def _swa_sample_body(sink_ref, q_ref, kvn_ref, ck_ref, cv_ref, o_ref, *, bb, w_buf):
    hrow = lax.broadcasted_iota(jnp.int32, (ATT_HEADS, 1), 0)
    slope = jnp.zeros((ATT_HEADS, 1), F32)
    sink = jnp.zeros((ATT_HEADS, 1), F32)
    for h in range(ATT_HEADS):
        slope = jnp.where(hrow == h, _alibi_slope(h), slope)
        sink = jnp.where(hrow == h, sink_ref[h], sink)
    jcol = lax.broadcasted_iota(jnp.int32, (ATT_HEADS, w_buf), 1)
    bias = -slope * (w_buf - jcol).astype(F32)
    lower = hrow < GROUP
    for b in range(bb):
        q2 = q_ref[b] * (HEAD_DIM ** -0.5)
        kvn = kvn_ref[b]
        ck = ck_ref[b].astype(BF16)
        cv = cv_ref[b].astype(BF16)
        q2b = q2.astype(BF16)
        s0 = _bdot_nt(q2b, ck[:, 0:HEAD_DIM])
        s1 = _bdot_nt(q2b, ck[:, HEAD_DIM:2 * HEAD_DIM])
        s = jnp.where(lower, s0, s1) + bias
        knew = jnp.where(lower, kvn[0:1, :], kvn[1:2, :])
        vnew = jnp.where(lower, kvn[2:3, :], kvn[3:4, :])
        q2r = q2b.astype(F32)
        snew = jnp.sum(q2r * knew.astype(BF16).astype(F32), axis=1, keepdims=True)
        m = jnp.maximum(jnp.maximum(jnp.max(s, axis=1, keepdims=True), snew), sink)
        e = jnp.exp(s - m)
        enew = jnp.exp(snew - m)
        den = jnp.sum(e, axis=1, keepdims=True) + enew + jnp.exp(sink - m)
        p = (e / den).astype(BF16)
        pnew = (enew / den).astype(BF16).astype(F32)
        o0 = jnp.dot(p, cv[:, 0:HEAD_DIM], preferred_element_type=F32)
        o1 = jnp.dot(p, cv[:, HEAD_DIM:2 * HEAD_DIM], preferred_element_type=F32)
        o = jnp.where(lower, o0, o1) + pnew * vnew.astype(BF16).astype(F32)
        o_ref[b] = o


def _swa_sample(sinks, q3, kvn3, ck3, cv3, bb=8):
    nb, w_buf = ck3.shape[0], ck3.shape[1]
    blk = lambda i: (i, 0, 0)
    return pl.pallas_call(
        functools.partial(_swa_sample_body, bb=bb, w_buf=w_buf),
        grid=(nb // bb,),
        in_specs=[pl.BlockSpec(memory_space=pltpu.SMEM),
                  pl.BlockSpec((bb, ATT_HEADS, HEAD_DIM), blk),
                  pl.BlockSpec((bb, 2 * KV_HEADS, HEAD_DIM), blk),
                  pl.BlockSpec((bb, w_buf, KV_COLS), blk),
                  pl.BlockSpec((bb, w_buf, KV_COLS), blk)],
        out_specs=pl.BlockSpec((bb, ATT_HEADS, HEAD_DIM), blk),
        out_shape=jax.ShapeDtypeStruct((nb, ATT_HEADS, HEAD_DIM), F32),
        compiler_params=_cparams(("parallel",)),
        name="swa_sample",
    )(sinks, q3, kvn3, ck3, cv3)


def _rwkv_prep(rp, prev, mu, w0, wlu, a0, alu, glu, k_k, k_a, bd):
    xs = rp + mu * (prev - rp)
    r = xs[:, 0:RWKV_WIDTH]
    k = xs[:, RWKV_WIDTH:2 * RWKV_WIDTH]
    v = xs[:, 2 * RWKV_WIDTH:3 * RWKV_WIDTH]
    o = 3 * RWKV_WIDTH
    wd = xs[:, o:o + LORA_W]
    ad = xs[:, o + LORA_W:o + LORA_W + LORA_A]
    gd = xs[:, o + LORA_W + LORA_A:]
    z = -(w0 + _bdot(jnp.tanh(wd), wlu))
    softplus = jnp.maximum(z, 0.0) + jnp.log(1.0 + jnp.exp(-jnp.abs(z)))
    log_decay = -jnp.exp(-softplus - 0.5)
    a = _sigmoid(a0 + _bdot(ad, alu))
    g = _bdot(_sigmoid(gd), glu)
    kkr = k * k_k
    kk = kkr / jnp.maximum(jnp.sqrt(_head_sum(kkr * kkr, bd)), 1e-12)
    k2 = k * (1.0 + (a - 1.0) * k_a)
    return r, log_decay, k2, v, kk, a, g


def _rwkv_finish(y, r, k2, v, g, r_k, gn_g, gn_b, bd):
    inv = 1.0 / HEAD_DIM
    mu = _head_sum(y, bd) * inv
    yc = y - mu
    var = _head_sum(yc * yc, bd) * inv
    yn = yc * lax.rsqrt(var + GN_EPS) * gn_g + gn_b
    bonus = _head_sum(r * k2 * r_k, bd) * v
    return (yn + bonus) * g


def _rwkv_prompt_body(rp_ref, mu_ref, w0_ref, wlu_ref, a0_ref, alu_ref, glu_ref, kk_ref, ka_ref,
                      rk_ref, gng_ref, gnb_ref, bd_ref, ltri_ref,
                      out_ref, sfin_ref, prev_scr, st_scr, y_scr, *, tt):
    c = pl.program_id(1)

    @pl.when(c == 0)
    def _():
        prev_scr[...] = jnp.zeros_like(prev_scr)
        st_scr[...] = jnp.zeros_like(st_scr)

    rp = rp_ref[...]
    rolled = pltpu.roll(rp, 1, 0)
    rowi = lax.broadcasted_iota(jnp.int32, (tt, 1), 0)
    prev = jnp.where(rowi == 0, prev_scr[...], rolled)
    prev_scr[...] = rp[tt - 1:tt, :]
    bd = bd_ref[...]
    r, ld, k2, v, kk, a, g = _rwkv_prep(rp, prev, mu_ref[...], w0_ref[...], wlu_ref[...],
                                        a0_ref[...], alu_ref[...], glu_ref[...], kk_ref[...],
                                        ka_ref[...], bd)
    cum = _split_dot(ltri_ref[...], ld, 3)
    ecum = jnp.exp(cum)
    einv = jnp.exp(-cum)
    at = -kk * jnp.exp(cum - ld)
    bh = kk * a * einv
    kh = k2 * einv
    rt = r * ecum

    ii = lax.broadcasted_iota(jnp.int32, (CHUNK, CHUNK), 0)
    jj = lax.broadcasted_iota(jnp.int32, (CHUNK, CHUNK), 1)
    strict = ii > jj
    incl = ii >= jj
    eye = (ii == jj).astype(F32)

    states = [st_scr[h] for h in range(RWKV_HEADS)]
    for s in range(tt // CHUNK):
        rs = slice(s * CHUNK, (s + 1) * CHUNK)
        for h in range(RWKV_HEADS):
            hs = slice(h * HEAD_DIM, (h + 1) * HEAD_DIM)
            st = states[h]
            x2 = jnp.concatenate([at[rs, hs], rt[rs, hs]], axis=0)
            y2 = jnp.concatenate([bh[rs, hs], kh[rs, hs]], axis=0)
            amat = _bdot_nt(x2, y2)
            nmat = jnp.where(strict, amat[0:CHUNK, 0:CHUNK], 0.0)
            a_ak = jnp.where(strict, amat[0:CHUNK, CHUNK:], 0.0)
            a_rb = jnp.where(incl, amat[CHUNK:, 0:CHUNK], 0.0)
            a_rk = jnp.where(incl, amat[CHUNK:, CHUNK:], 0.0)
            tmat = eye + nmat
            npow = nmat
            for _ in range(5):
                npow = _bdot(npow, npow)
                tmat = tmat + _bdot(npow, tmat)
            xs_t = _bdot_nt(x2, st)
            vh = v[rs, hs]
            u = _bdot(tmat, xs_t[0:CHUNK] + _bdot(a_ak, vh))
            vu = jnp.concatenate([vh, u], axis=0)
            y = xs_t[CHUNK:] + _bdot(jnp.concatenate([a_rk, a_rb], axis=1), vu)
            y_scr[rs, hs] = y
            kb = jnp.concatenate([kh[rs, hs], bh[rs, hs]], axis=0)
            pl_row = ecum[(s + 1) * CHUNK - 1:(s + 1) * CHUNK, hs]
            states[h] = (st + _bdot_tn(vu, kb)) * pl_row
    for h in range(RWKV_HEADS):
        st_scr[h] = states[h]
        sfin_ref[0, h] = states[h]
    out_ref[...] = _rwkv_finish(y_scr[...], r, k2, v, g, rk_ref[...], gng_ref[...], gnb_ref[...], bd)


def _chunk_tril(tt):
    i = jnp.arange(tt)
    same = (i[:, None] // CHUNK) == (i[None, :] // CHUNK)
    return (same & (i[:, None] >= i[None, :])).astype(BF16)


def _head_blockdiag():
    i = jnp.arange(RWKV_WIDTH) // HEAD_DIM
    return (i[:, None] == i[None, :]).astype(BF16)


def _rwkv_prompt(rp, prm, nbatch, seq, tt):
    nc = seq // tt
    fixed = lambda b, c: (0, 0)
    row = lambda b, c: (b * nc + c, 0)
    vec = lambda n: pl.BlockSpec((1, n), fixed)
    return pl.pallas_call(
        functools.partial(_rwkv_prompt_body, tt=tt),
        grid=(nbatch, nc),
        in_specs=[pl.BlockSpec((tt, RWKV_PROJ), row),
                  vec(RWKV_PROJ), vec(RWKV_WIDTH),
                  pl.BlockSpec((LORA_W, RWKV_WIDTH), fixed), vec(RWKV_WIDTH),
                  pl.BlockSpec((LORA_A, RWKV_WIDTH), fixed),
                  pl.BlockSpec((LORA_G, RWKV_WIDTH), fixed),
                  vec(RWKV_WIDTH), vec(RWKV_WIDTH), vec(RWKV_WIDTH), vec(RWKV_WIDTH),
                  vec(RWKV_WIDTH),
                  pl.BlockSpec((RWKV_WIDTH, RWKV_WIDTH), fixed),
                  pl.BlockSpec((tt, tt), fixed)],
        out_specs=[pl.BlockSpec((tt, RWKV_WIDTH), row),
                   pl.BlockSpec((1, RWKV_HEADS, HEAD_DIM, HEAD_DIM), lambda b, c: (b, 0, 0, 0))],
        out_shape=[jax.ShapeDtypeStruct((nbatch * seq, RWKV_WIDTH), F32),
                   jax.ShapeDtypeStruct((nbatch, RWKV_HEADS, HEAD_DIM, HEAD_DIM), F32)],
        scratch_shapes=[pltpu.VMEM((1, RWKV_PROJ), F32),
                        pltpu.VMEM((RWKV_HEADS, HEAD_DIM, HEAD_DIM), F32),
                        pltpu.VMEM((tt, RWKV_WIDTH), F32)],
        compiler_params=_cparams(("parallel", "arbitrary")),
        name="rwkv_prompt",
    )(rp, prm["mu"], prm["w0"], prm["wlu"], prm["a0"], prm["alu"], prm["glu"], prm["k_k"],
      prm["k_a"], prm["r_k"], prm["gn_g"], prm["gn_b"], _head_blockdiag(), _chunk_tril(tt))


def _rwkv_sample_prep_body(rp_ref, prev_ref, mu_ref, w0_ref, wlu_ref, a0_ref, alu_ref, glu_ref,
                           kk_ref, ka_ref, bd_ref, o_ref):
    r, ld, k2, v, kk, a, g = _rwkv_prep(rp_ref[...], prev_ref[...], mu_ref[...], w0_ref[...],
                                        wlu_ref[...], a0_ref[...], alu_ref[...], glu_ref[...],
                                        kk_ref[...], ka_ref[...], bd_ref[...])
    for i, z in enumerate((r, jnp.exp(ld), k2, v, kk, a, g)):
        o_ref[i] = z


def _rwkv_sample_prep(rp, prev, prm):
    nb = rp.shape[0]
    return pl.pallas_call(
        _rwkv_sample_prep_body,
        out_shape=jax.ShapeDtypeStruct((7, nb, RWKV_WIDTH), F32),
        compiler_params=pltpu.CompilerParams(vmem_limit_bytes=VMEM_LIMIT),
        name="rwkv_sample_prep",
    )(rp, prev, prm["mu"], prm["w0"], prm["wlu"], prm["a0"], prm["alu"], prm["glu"], prm["k_k"],
      prm["k_a"], _head_blockdiag())


def _rwkv_sample_step_body(p_ref, s_ref, rk_ref, gng_ref, gnb_ref, so_ref, o_ref, *, bb):
    ii = lax.broadcasted_iota(jnp.int32, (HEAD_DIM, HEAD_DIM), 0)
    jj = lax.broadcasted_iota(jnp.int32, (HEAD_DIM, HEAD_DIM), 1)
    eye = ii == jj
    for b in range(bb):
        for h in range(RWKV_HEADS):
            row = lambda i: p_ref[i, b, h:h + 1, :]
            r_r, w_r, k_r, v_r, kk_r, a_r = (row(i) for i in range(6))
            st = s_ref[b, h]
            sa = jnp.sum(st * kk_r, axis=1, keepdims=True)
            v_col = jnp.sum(jnp.where(eye, v_r, 0.0), axis=1, keepdims=True)
            st = st * w_r - sa * (kk_r * a_r) + v_col * k_r
            so_ref[b, h] = st
            y_col = jnp.sum(st * r_r, axis=1, keepdims=True)
            o_ref[b, h:h + 1, :] = jnp.sum(jnp.where(eye, y_col, 0.0), axis=0, keepdims=True)
    y = o_ref[...]
    r, k2, v, g = p_ref[0], p_ref[2], p_ref[3], p_ref[6]
    mu = jnp.mean(y, axis=-1, keepdims=True)
    yc = y - mu
    var = jnp.mean(yc * yc, axis=-1, keepdims=True)
    yn = yc * lax.rsqrt(var + GN_EPS) * gng_ref[...] + gnb_ref[...]
    bonus = jnp.sum(r * k2 * rk_ref[...], axis=-1, keepdims=True) * v
    o_ref[...] = (yn + bonus) * g


def _rwkv_sample_step(p4, state, r_k, gn_g, gn_b, bb=8):
    nb = state.shape[0]
    hv = pl.BlockSpec((RWKV_HEADS, HEAD_DIM), lambda i: (0, 0))
    return pl.pallas_call(
        functools.partial(_rwkv_sample_step_body, bb=bb),
        grid=(nb // bb,),
        in_specs=[pl.BlockSpec((7, bb, RWKV_HEADS, HEAD_DIM), lambda i: (0, i, 0, 0)),
                  pl.BlockSpec((bb, RWKV_HEADS, HEAD_DIM, HEAD_DIM), lambda i: (i, 0, 0, 0)),
                  hv, hv, hv],
        out_specs=[pl.BlockSpec((bb, RWKV_HEADS, HEAD_DIM, HEAD_DIM), lambda i: (i, 0, 0, 0)),
                   pl.BlockSpec((bb, RWKV_HEADS, HEAD_DIM), lambda i: (i, 0, 0))],
        out_shape=[jax.ShapeDtypeStruct(state.shape, F32),
                   jax.ShapeDtypeStruct((nb, RWKV_HEADS, HEAD_DIM), F32)],
        compiler_params=_cparams(("parallel",)),
        name="rwkv_sample_step",
    )(p4, state, r_k, gn_g, gn_b)


def _mix_router_body(att_ref, rw_ref, xn_ref, wo_ref, g_ref, b_ref, wr_ref, br_ref, base_ref,
                     x1_ref, info_ref, gate_ref, cnt_ref, base_scr, *, tm):
    i = pl.program_id(0)

    @pl.when(i == 0)
    def _():
        base_scr[...] = base_ref[...]

    mixed = (jnp.dot(att_ref[...].astype(BF16), wo_ref[0:ATT_WIDTH, :], preferred_element_type=F32)
             + jnp.dot(rw_ref[...].astype(BF16), wo_ref[ATT_WIDTH:, :], preferred_element_type=F32))
    x1 = _layer_norm(DEEPNORM_ALPHA * xn_ref[...] + mixed, g_ref[...], b_ref[...])
    x1_ref[...] = x1

    logits = jnp.dot(x1, wr_ref[...], preferred_element_type=F32, precision=HIGHEST) + br_ref[...]
    lane = lax.broadcasted_iota(jnp.int32, (tm, LANES), 1)
    lanef = lane.astype(F32)
    vals, idxs, hots = [], [], []
    cur = logits
    for _ in range(TOP_K):
        m = jnp.max(cur, axis=1, keepdims=True)
        idx = jnp.min(jnp.where(cur == m, lanef, float(LANES)), axis=1, keepdims=True)
        hot = lanef == idx
        cur = jnp.where(hot, -jnp.inf, cur)
        vals.append(m)
        idxs.append(idx)
        hots.append(hot)
    es = [jnp.exp(vk - vals[0]) for vk in vals]
    den = es[0] + es[1] + es[2] + es[3]
    multi = jnp.zeros((tm, LANES), F32)
    for hot in hots:
        multi = multi + hot.astype(F32)
    ti = lax.broadcasted_iota(jnp.int32, (tm, tm), 0)
    tj = lax.broadcasted_iota(jnp.int32, (tm, tm), 1)
    before = jnp.dot((ti > tj).astype(BF16), multi.astype(BF16), preferred_element_type=F32)
    before = before + base_scr[...]
    info = jnp.zeros((tm, LANES), jnp.int32)
    gates = jnp.zeros((tm, LANES), F32)
    for k in range(TOP_K):
        rank = jnp.sum(jnp.where(hots[k], before, 0.0), axis=1, keepdims=True)
        info = jnp.where(lane == k, idxs[k].astype(jnp.int32), info)
        info = jnp.where(lane == TOP_K + k, rank.astype(jnp.int32), info)
        gates = jnp.where(lane == k, es[k] / den, gates)
    info_ref[...] = info
    gate_ref[...] = gates
    base_scr[...] = base_scr[...] + jnp.sum(multi, axis=0, keepdims=True)
    cnt_ref[...] = base_scr[...]


def _mix_router(att, rw, xn, wo_bf16, g, b, wr_pad, br_pad, base, tm):
    t = att.shape[0]
    row = lambda i: (i, 0)
    fixed = lambda i: (0, 0)
    return pl.pallas_call(
        functools.partial(_mix_router_body, tm=tm),
        grid=(t // tm,),
        in_specs=[pl.BlockSpec((tm, ATT_WIDTH), row),
                  pl.BlockSpec((tm, RWKV_WIDTH), row),
                  pl.BlockSpec((tm, D_MODEL), row),
                  pl.BlockSpec((D_MODEL, D_MODEL), fixed),
                  pl.BlockSpec((1, D_MODEL), fixed),
                  pl.BlockSpec((1, D_MODEL), fixed),
                  pl.BlockSpec((D_MODEL, LANES), fixed),
                  pl.BlockSpec((1, LANES), fixed),
                  pl.BlockSpec((1, LANES), fixed)],
        out_specs=[pl.BlockSpec((tm, D_MODEL), row),
                   pl.BlockSpec((tm, LANES), row),
                   pl.BlockSpec((tm, LANES), row),
                   pl.BlockSpec((1, LANES), fixed)],
        out_shape=[jax.ShapeDtypeStruct((t, D_MODEL), F32),
                   jax.ShapeDtypeStruct((t, LANES), jnp.int32),
                   jax.ShapeDtypeStruct((t, LANES), F32),
                   jax.ShapeDtypeStruct((1, LANES), F32)],
        scratch_shapes=[pltpu.VMEM((1, LANES), F32)],
        compiler_params=_cparams(("arbitrary",)),
        name="mix_router",
    )(att, rw, xn, wo_bf16, g, b, wr_pad, br_pad, base)


def _dispatch_body(pos_ref, x_ref, xs_in_ref, xs_ref, sem, *, tm):
    del xs_in_ref
    base = pl.program_id(0) * tm

    def copy(r, k):
        p = pos_ref[(base + r) * TOP_K + k]
        return pltpu.make_async_copy(x_ref.at[pl.ds(r, 1)], xs_ref.at[pl.ds(p, 1)], sem)

    def start(r, carry):
        for k in range(TOP_K):
            copy(r, k).start()
        return carry

    def wait(r, carry):
        for k in range(TOP_K):
            copy(r, k).wait()
        return carry

    lax.fori_loop(0, tm, start, 0)
    lax.fori_loop(0, tm, wait, 0)


def _dispatch(pos_flat, x1, xs, tm):
    t = x1.shape[0]
    return pl.pallas_call(
        functools.partial(_dispatch_body, tm=tm),
        grid_spec=pltpu.PrefetchScalarGridSpec(
            num_scalar_prefetch=1,
            grid=(t // tm,),
            in_specs=[pl.BlockSpec((tm, D_MODEL), lambda i, pos: (i, 0)),
                      pl.BlockSpec(memory_space=pl.ANY)],
            out_specs=pl.BlockSpec(memory_space=pl.ANY),
            scratch_shapes=[pltpu.SemaphoreType.DMA]),
        out_shape=jax.ShapeDtypeStruct(xs.shape, xs.dtype),
        input_output_aliases={2: 0},
        compiler_params=_cparams(("arbitrary",)),
        name="moe_dispatch",
    )(pos_flat, x1, xs)


def _experts_body(te_ref, nu_ref, xs_ref, wg_ref, wu_ref, wd_ref, bg_ref, bu_ref, bd_ref, ys_ref):
    i = pl.program_id(0)

    @pl.when(i < nu_ref[0])
    def _():
        x = xs_ref[...].astype(BF16)
        gate = jnp.dot(x, wg_ref[...], preferred_element_type=F32) + bg_ref[...]
        up = jnp.dot(x, wu_ref[...], preferred_element_type=F32) + bu_ref[...]
        gate = jnp.minimum(gate, SWIGLU_LIMIT)
        up = jnp.clip(up, -SWIGLU_LIMIT, SWIGLU_LIMIT)
        act = (up + 1.0) * gate * _sigmoid(SWIGLU_ALPHA * gate)
        ys_ref[...] = jnp.dot(act.astype(BF16), wd_ref[...], preferred_element_type=F32) + bd_ref[...]

    @pl.when(i >= nu_ref[0])
    def _():
        ys_ref[...] = jnp.zeros_like(ys_ref)


def _experts(tile_e, n_used, xs, wg, wu, wd, bg, bu, bd):
    ns = xs.shape[0]
    tm = EXPERT_TILE
    wspec = lambda a, b: pl.BlockSpec((None, a, b), lambda i, te, nu: (te[i], 0, 0))
    return pl.pallas_call(
        _experts_body,
        grid_spec=pltpu.PrefetchScalarGridSpec(
            num_scalar_prefetch=2,
            grid=(ns // tm,),
            in_specs=[pl.BlockSpec((tm, D_MODEL), lambda i, te, nu: (i, 0)),
                      wspec(D_MODEL, D_FF), wspec(D_MODEL, D_FF), wspec(D_FF, D_MODEL),
                      wspec(1, D_FF), wspec(1, D_FF), wspec(1, D_MODEL)],
            out_specs=pl.BlockSpec((tm, D_MODEL), lambda i, te, nu: (i, 0))),
        out_shape=jax.ShapeDtypeStruct((ns, D_MODEL), F32),
        compiler_params=_cparams(("arbitrary",)),
        name="moe_experts",
    )(tile_e, n_used, xs, wg, wu, wd, bg, bu, bd)


def _combine_body(pos_ref, ys_ref, gate_ref, x1_ref, pe_ref, g_ref, b_ref, wple_ref, wpg_ref,
                  o_ref, buf, sem, *, tm):
    base = pl.program_id(0) * tm

    def copy(r, k):
        p = pos_ref[(base + r) * TOP_K + k]
        return pltpu.make_async_copy(ys_ref.at[pl.ds(p, 1)], buf.at[k, pl.ds(r, 1)], sem)

    def start(r, carry):
        for k in range(TOP_K):
            copy(r, k).start()
        return carry

    def wait(r, carry):
        for k in range(TOP_K):
            copy(r, k).wait()
        return carry

    lax.fori_loop(0, tm, start, 0)
    lax.fori_loop(0, tm, wait, 0)
    gates = gate_ref[...]
    ffn = gates[:, 0:1] * buf[0]
    for k in range(1, TOP_K):
        ffn = ffn + gates[:, k:k + 1] * buf[k]
    x2 = _layer_norm(DEEPNORM_ALPHA * x1_ref[...] + ffn, g_ref[...], b_ref[...])
    gate = _sigmoid(jnp.dot(x2.astype(BF16), wpg_ref[...], preferred_element_type=F32))
    emb = jnp.dot(pe_ref[...].astype(BF16), wple_ref[...], preferred_element_type=F32)
    o_ref[...] = x2 + gate * emb


def _combine(pos_flat, ys, gates, x1, pe, g, b, wple_bf16, wpg_bf16, tm):
    t = x1.shape[0]
    row = lambda i, pos: (i, 0)
    fixed = lambda i, pos: (0, 0)
    return pl.pallas_call(
        functools.partial(_combine_body, tm=tm),
        grid_spec=pltpu.PrefetchScalarGridSpec(
            num_scalar_prefetch=1,
            grid=(t // tm,),
            in_specs=[pl.BlockSpec(memory_space=pl.ANY),
                      pl.BlockSpec((tm, LANES), row),
                      pl.BlockSpec((tm, D_MODEL), row),
                      pl.BlockSpec((tm, PLE_DIM), row),
                      pl.BlockSpec((1, D_MODEL), fixed),
                      pl.BlockSpec((1, D_MODEL), fixed),
                      pl.BlockSpec((PLE_DIM, D_MODEL), fixed),
                      pl.BlockSpec((D_MODEL, D_MODEL), fixed)],
            out_specs=pl.BlockSpec((tm, D_MODEL), row),
            scratch_shapes=[pltpu.VMEM((TOP_K, tm, D_MODEL), F32), pltpu.SemaphoreType.DMA]),
        out_shape=jax.ShapeDtypeStruct((t, D_MODEL), F32),
        compiler_params=_cparams(("arbitrary",)),
        name="moe_combine",
    )(pos_flat, ys, gates, x1, pe, g, b, wple_bf16, wpg_bf16)


def kernel(x_prompt, x_sample, cache_k, cache_v, state_shift, state_wkv, p_prompt, p_sample,
           ln_emb_g, ln_emb_b, w_in, attn_sinks, rwkv_mu, rwkv_w0, rwkv_w_lora_up, rwkv_a0,
           rwkv_a_lora_up, rwkv_g_lora_up, rwkv_k_k, rwkv_k_a, rwkv_r_k, rwkv_gn_g, rwkv_gn_b,
           w_out, ln1_g, ln1_b, w_router, b_router, w_gate_up, b_gate_up, w_down, b_down,
           ln2_g, ln2_b, w_ple, w_ple_gate):
    assert w_in.shape[0] == DEPTH == 1
    nbp, seq, _ = x_prompt.shape
    nbs, dec_seq, _ = x_sample.shape
    assert dec_seq == 1
    tp, ts = nbp * seq, nbs
    w_buf = cache_k.shape[2]
    rowv = lambda z: z.reshape(1, -1)

    w_in_b = w_in[0].astype(BF16)
    w_out_b = w_out[0].astype(BF16)
    w_ple_b = w_ple[0].astype(BF16)
    w_pg_b = w_ple_gate[0].astype(BF16)
    sinks = attn_sinks[0]
    prm = dict(mu=rowv(rwkv_mu[0]), w0=rowv(rwkv_w0[0]), wlu=rwkv_w_lora_up[0],
               a0=rowv(rwkv_a0[0]), alu=rwkv_a_lora_up[0], glu=rwkv_g_lora_up[0],
               k_k=rowv(rwkv_k_k[0]), k_a=rowv(rwkv_k_a[0]), r_k=rowv(rwkv_r_k[0]),
               gn_g=rowv(rwkv_gn_g[0]), gn_b=rowv(rwkv_gn_b[0]))
    ge, be = rowv(ln_emb_g), rowv(ln_emb_b)

    xn_p, q_p, kv_p, rp_p = _inproj(x_prompt.reshape(tp, D_MODEL), ge, be, w_in_b, 512)
    att_p = _swa_prompt(sinks, q_p, kv_p, nbp, seq)
    rw_p, wkv_p = _rwkv_prompt(rp_p, prm, nbp, seq, 256)

    xn_s, q_s, kv_s, rp_s = _inproj(x_sample.reshape(ts, D_MODEL), ge, be, w_in_b, ts)
    ck3 = cache_k[0].reshape(nbs, w_buf, KV_COLS)
    cv3 = cache_v[0].reshape(nbs, w_buf, KV_COLS)
    att_s = _swa_sample(sinks, q_s.reshape(nbs, ATT_HEADS, HEAD_DIM),
                        kv_s.reshape(nbs, 2 * KV_HEADS, HEAD_DIM), ck3, cv3)
    att_s = att_s.reshape(ts, ATT_WIDTH)
    p7 = _rwkv_sample_prep(rp_s, state_shift[0], prm)
    hv = lambda z: z.reshape(RWKV_HEADS, HEAD_DIM)
    wkv_s, rw_s = _rwkv_sample_step(p7.reshape(7, nbs, RWKV_HEADS, HEAD_DIM), state_wkv[0],
                                    hv(rwkv_r_k[0]), hv(rwkv_gn_g[0]), hv(rwkv_gn_b[0]))
    rw_s = rw_s.reshape(ts, RWKV_WIDTH)

    wr_pad = jnp.zeros((D_MODEL, LANES), F32).at[:, :N_EXPERTS].set(w_router[0])
    br_pad = jnp.full((1, LANES), -jnp.inf, F32).at[0, :N_EXPERTS].set(b_router[0])
    g1, b1 = rowv(ln1_g[0]), rowv(ln1_b[0])
    x1_p, info_p, gate_p, cnt_p = _mix_router(att_p, rw_p, xn_p, w_out_b, g1, b1, wr_pad, br_pad,
                                              jnp.zeros((1, LANES), F32), 256)
    x1_s, info_s, gate_s, cnt = _mix_router(att_s, rw_s, xn_s, w_out_b, g1, b1, wr_pad, br_pad,
                                            cnt_p, ts)

    counts = cnt[0, :N_EXPERTS].astype(jnp.int32)
    padded = ((counts + EXPERT_TILE - 1) // EXPERT_TILE) * EXPERT_TILE
    ends = jnp.cumsum(padded)
    offs = ends - padded
    n_slots = (tp + ts) * TOP_K + N_EXPERTS * EXPERT_TILE
    n_tiles = n_slots // EXPERT_TILE
    n_used = (ends[-1] // EXPERT_TILE).reshape(1).astype(jnp.int32)
    tile_e = jnp.searchsorted(ends, jnp.arange(n_tiles, dtype=jnp.int32) * EXPERT_TILE,
                              side="right").astype(jnp.int32)
    last_e = jnp.max(jnp.where(padded > 0, jnp.arange(N_EXPERTS), 0)).astype(jnp.int32)
    tile_e = jnp.minimum(tile_e, last_e)

    def slots(info):
        return (offs[info[:, 0:TOP_K]] + info[:, TOP_K:2 * TOP_K]).reshape(-1).astype(jnp.int32)

    pos_p, pos_s = slots(info_p), slots(info_s)

    xs = jnp.zeros((n_slots, D_MODEL), F32)
    xs = _dispatch(pos_p, x1_p, xs, 256)
    xs = _dispatch(pos_s, x1_s, xs, ts)
    wgu = w_gate_up[0]
    wg = wgu[:, :, 0::2].astype(BF16)
    wu = wgu[:, :, 1::2].astype(BF16)
    wd = w_down[0].astype(BF16)
    bgu = b_gate_up[0]
    bg = bgu[:, 0::2].reshape(N_EXPERTS, 1, D_FF)
    bu = bgu[:, 1::2].reshape(N_EXPERTS, 1, D_FF)
    bdn = b_down[0].reshape(N_EXPERTS, 1, D_MODEL)
    ys = _experts(tile_e, n_used, xs, wg, wu, wd, bg, bu, bdn)

    g2, b2 = rowv(ln2_g[0]), rowv(ln2_b[0])
    y_p = _combine(pos_p, ys, gate_p, x1_p, p_prompt[0].reshape(tp, PLE_DIM), g2, b2,
                   w_ple_b, w_pg_b, 256)
    y_s = _combine(pos_s, ys, gate_s, x1_s, p_sample[0].reshape(ts, PLE_DIM), g2, b2,
                   w_ple_b, w_pg_b, ts)

    w_keep = min(WINDOW, seq)
    kv_p3 = kv_p.reshape(nbp, seq, 2 * KV_COLS)[:, seq - w_keep:]
    k_win_p = kv_p3[:, :, 0:KV_COLS].reshape(1, nbp, w_keep, KV_HEADS, HEAD_DIM)
    v_win_p = kv_p3[:, :, KV_COLS:].reshape(1, nbp, w_keep, KV_HEADS, HEAD_DIM)
    shift_p = rp_p.reshape(nbp, seq, RWKV_PROJ)[:, seq - 1][None]
    k_new = kv_s[:, 0:KV_COLS].reshape(nbs, 1, KV_HEADS, HEAD_DIM)
    v_new = kv_s[:, KV_COLS:].reshape(nbs, 1, KV_HEADS, HEAD_DIM)
    k_win_s = jnp.concatenate([cache_k[0], k_new], axis=1)[:, 1:][None]
    v_win_s = jnp.concatenate([cache_v[0], v_new], axis=1)[:, 1:][None]
    return (y_p.reshape(nbp, seq, D_MODEL), y_s.reshape(nbs, 1, D_MODEL),
            k_win_p, v_win_p, shift_p, wkv_p[None],
            k_win_s, v_win_s, rp_s[None], wkv_s[None])
```

```python
import functools

import jax
import jax.numpy as jnp
from jax import lax
from jax.experimental import pallas as pl
from jax.experimental.pallas import tpu as pltpu

F32 = jnp.float32
BF16 = jnp.bfloat16
HIGHEST = lax.Precision.HIGHEST

D_MODEL = 1024
HEAD_DIM = 64
ATT_HEADS = 8
KV_HEADS = 2
GROUP = ATT_HEADS // KV_HEADS
ATT_WIDTH = ATT_HEADS * HEAD_DIM
KV_COLS = KV_HEADS * HEAD_DIM
RWKV_HEADS = 8
RWKV_WIDTH = RWKV_HEADS * HEAD_DIM
LORA_W = 64
LORA_A = 64
LORA_G = 128
RWKV_PROJ = 3 * RWKV_WIDTH + LORA_W + LORA_A + LORA_G
WINDOW = 128
BLOCK = 128
PLE_DIM = 256
N_EXPERTS = 32
TOP_K = 4
D_FF = 1024
SWIGLU_LIMIT = 7.0
SWIGLU_ALPHA = 1.702
LN_EPS = 1e-5
GN_EPS = 64e-5
DEPTH = 1
DEEPNORM_ALPHA = (2 * DEPTH) ** 0.25

LANES = 128
CHUNK = 64
EXPERT_TILE = 512
VMEM_LIMIT = 48 * 1024 * 1024
EXPERTS_VMEM_LIMIT = 58 * 1024 * 1024


def _cparams(sem):
    return pltpu.CompilerParams(dimension_semantics=sem, vmem_limit_bytes=VMEM_LIMIT)


def _bdot(a, b):
    return jnp.dot(a.astype(BF16), b.astype(BF16), preferred_element_type=F32)


def _bdot_nt(a, b):
    return lax.dot_general(a.astype(BF16), b.astype(BF16), (((1,), (1,)), ((), ())),
                           preferred_element_type=F32)


def _bdot_tn(a, b):
    return lax.dot_general(a.astype(BF16), b.astype(BF16), (((0,), (0,)), ((), ())),
                           preferred_element_type=F32)


def _split_dot(m_bf16, x, passes):
    acc = None
    rem = x
    for _ in range(passes):
        hi = rem.astype(BF16)
        part = jnp.dot(m_bf16, hi, preferred_element_type=F32)
        acc = part if acc is None else acc + part
        rem = rem - hi.astype(F32)
    return acc


def _head_sum(x, bd_bf16):
    acc = None
    rem = x
    for _ in range(3):
        hi = rem.astype(BF16)
        part = jnp.dot(hi, bd_bf16, preferred_element_type=F32)
        acc = part if acc is None else acc + part
        rem = rem - hi.astype(F32)
    return acc


def _sigmoid(x):
    return 1.0 / (1.0 + jnp.exp(-x))


def _layer_norm(x, g, b):
    mu = jnp.mean(x, axis=-1, keepdims=True)
    xc = x - mu
    var = jnp.mean(xc * xc, axis=-1, keepdims=True)
    return xc * lax.rsqrt(var + LN_EPS) * g + b


def _inproj_body(x_ref, g_ref, b_ref, w_ref, xn_ref, q_ref, kv_ref, rp_ref):
    xn = _layer_norm(x_ref[...], g_ref[...], b_ref[...])
    xn_ref[...] = xn
    xb = xn.astype(BF16)
    q_ref[...] = jnp.dot(xb, w_ref[:, 0:ATT_WIDTH], preferred_element_type=F32)
    kv_ref[...] = jnp.dot(xb, w_ref[:, ATT_WIDTH:ATT_WIDTH + 2 * KV_COLS],
                          preferred_element_type=F32)
    rp_ref[...] = jnp.dot(xb, w_ref[:, ATT_WIDTH + 2 * KV_COLS:], preferred_element_type=F32)


def _inproj(x2d, g, b, w_bf16, tm):
    t = x2d.shape[0]
    in_proj = w_bf16.shape[1]
    row = lambda i: (i, 0)
    fixed = lambda i: (0, 0)
    return pl.pallas_call(
        _inproj_body,
        grid=(t // tm,),
        in_specs=[pl.BlockSpec((tm, D_MODEL), row),
                  pl.BlockSpec((1, D_MODEL), fixed),
                  pl.BlockSpec((1, D_MODEL), fixed),
                  pl.BlockSpec((D_MODEL, in_proj), fixed)],
        out_specs=[pl.BlockSpec((tm, D_MODEL), row),
                   pl.BlockSpec((tm, ATT_WIDTH), row),
                   pl.BlockSpec((tm, 2 * KV_COLS), row),
                   pl.BlockSpec((tm, RWKV_PROJ), row)],
        out_shape=[jax.ShapeDtypeStruct((t, D_MODEL), F32),
                   jax.ShapeDtypeStruct((t, ATT_WIDTH), F32),
                   jax.ShapeDtypeStruct((t, 2 * KV_COLS), F32),
                   jax.ShapeDtypeStruct((t, RWKV_PROJ), F32)],
        compiler_params=_cparams(("parallel",)),
        name="inproj",
    )(x2d, g, b, w_bf16)


def _alibi_slope(h):
    return 2.0 ** (-8.0 * (h + 1) / ATT_HEADS)


def _swa_prompt_body(sink_ref, q_ref, kvc_ref, kvp_ref, o_ref):
    n = pl.program_id(1)
    q = (q_ref[...] * (HEAD_DIM ** -0.5)).astype(BF16)
    kvc = kvc_ref[...].astype(BF16)
    kvp = kvp_ref[...].astype(BF16)
    row = lax.broadcasted_iota(jnp.int32, (BLOCK, 2 * BLOCK), 0)
    col = lax.broadcasted_iota(jnp.int32, (BLOCK, 2 * BLOCK), 1)
    dist = row + BLOCK - col
    valid = (dist >= 0) & (dist <= WINDOW) & ((col >= BLOCK) | (n > 0))
    distf = dist.astype(F32)
    outs = []
    for g in range(KV_HEADS):
        ks = slice(g * HEAD_DIM, (g + 1) * HEAD_DIM)
        vs = slice(KV_COLS + g * HEAD_DIM, KV_COLS + (g + 1) * HEAD_DIM)
        kband = jnp.concatenate([kvp[:, ks], kvc[:, ks]], axis=0)
        vband = jnp.concatenate([kvp[:, vs], kvc[:, vs]], axis=0)
        for j in range(GROUP):
            h = g * GROUP + j
            qh = q[:, h * HEAD_DIM:(h + 1) * HEAD_DIM]
            s = lax.dot_general(qh, kband, (((1,), (1,)), ((), ())), preferred_element_type=F32)
            s = jnp.where(valid, s - _alibi_slope(h) * distf, -jnp.inf)
            sink = sink_ref[h]
            m = jnp.maximum(jnp.max(s, axis=1, keepdims=True), sink)
            e = jnp.exp(s - m)
            den = jnp.sum(e, axis=1, keepdims=True) + jnp.exp(sink - m)
            p = e / den
            outs.append(jnp.dot(p.astype(BF16), vband, preferred_element_type=F32))
    o_ref[...] = jnp.concatenate(outs, axis=1)


def _swa_prompt(sinks, q, kv, nbatch, seq):
    nb = seq // BLOCK
    cur = lambda b, n: (b * nb + n, 0)
    prv = lambda b, n: (b * nb + jnp.maximum(n - 1, 0), 0)
    return pl.pallas_call(
        _swa_prompt_body,
        grid=(nbatch, nb),
        in_specs=[pl.BlockSpec(memory_space=pltpu.SMEM),
                  pl.BlockSpec((BLOCK, ATT_WIDTH), cur),
                  pl.BlockSpec((BLOCK, 2 * KV_COLS), cur),
                  pl.BlockSpec((BLOCK, 2 * KV_COLS), prv)],
        out_specs=pl.BlockSpec((BLOCK, ATT_WIDTH), cur),
        out_shape=jax.ShapeDtypeStruct((nbatch * seq, ATT_WIDTH), F32),
        compiler_params=_cparams(("parallel", "parallel")),
        name="swa_prompt",
    )(sinks, q, kv, kv)


def _swa_sample_body(sink_ref, q_ref, kvn_ref, ck_ref, cv_ref, o_ref, *, bb, w_buf):
    hrow = lax.broadcasted_iota(jnp.int32, (ATT_HEADS, 1), 0)
    slope = jnp.zeros((ATT_HEADS, 1), F32)
    sink = jnp.zeros((ATT_HEADS, 1), F32)
    for h in range(ATT_HEADS):
        slope = jnp.where(hrow == h, _alibi_slope(h), slope)
        sink = jnp.where(hrow == h, sink_ref[h], sink)
    jcol = lax.broadcasted_iota(jnp.int32, (ATT_HEADS, w_buf), 1)
    bias = -slope * (w_buf - jcol).astype(F32)
    lower = hrow < GROUP
    for b in range(bb):
        q2 = q_ref[b] * (HEAD_DIM ** -0.5)
        kvn = kvn_ref[b]
        ck = ck_ref[b].astype(BF16)
        cv = cv_ref[b].astype(BF16)
        q2b = q2.astype(BF16)
        s0 = _bdot_nt(q2b, ck[:, 0:HEAD_DIM])
        s1 = _bdot_nt(q2b, ck[:, HEAD_DIM:2 * HEAD_DIM])
        s = jnp.where(lower, s0, s1) + bias
        knew = jnp.where(lower, kvn[0:1, :], kvn[1:2, :])
        vnew = jnp.where(lower, kvn[2:3, :], kvn[3:4, :])
        q2r = q2b.astype(F32)
        snew = jnp.sum(q2r * knew.astype(BF16).astype(F32), axis=1, keepdims=True)
        m = jnp.maximum(jnp.maximum(jnp.max(s, axis=1, keepdims=True), snew), sink)
        e = jnp.exp(s - m)
        enew = jnp.exp(snew - m)
        den = jnp.sum(e, axis=1, keepdims=True) + enew + jnp.exp(sink - m)
        p = (e / den).astype(BF16)
        pnew = (enew / den).astype(BF16).astype(F32)
        o0 = jnp.dot(p, cv[:, 0:HEAD_DIM], preferred_element_type=F32)
        o1 = jnp.dot(p, cv[:, HEAD_DIM:2 * HEAD_DIM], preferred_element_type=F32)
        o = jnp.where(lower, o0, o1) + pnew * vnew.astype(BF16).astype(F32)
        o_ref[b] = o


def _swa_sample(sinks, q3, kvn3, ck3, cv3, bb=8):
    nb, w_buf = ck3.shape[0], ck3.shape[1]
    blk = lambda i: (i, 0, 0)
    return pl.pallas_call(
        functools.partial(_swa_sample_body, bb=bb, w_buf=w_buf),
        grid=(nb // bb,),
        in_specs=[pl.BlockSpec(memory_space=pltpu.SMEM),
                  pl.BlockSpec((bb, ATT_HEADS, HEAD_DIM), blk),
                  pl.BlockSpec((bb, 2 * KV_HEADS, HEAD_DIM), blk),
                  pl.BlockSpec((bb, w_buf, KV_COLS), blk),
                  pl.BlockSpec((bb, w_buf, KV_COLS), blk)],
        out_specs=pl.BlockSpec((bb, ATT_HEADS, HEAD_DIM), blk),
        out_shape=jax.ShapeDtypeStruct((nb, ATT_HEADS, HEAD_DIM), F32),
        compiler_params=_cparams(("parallel",)),
        name="swa_sample",
    )(sinks, q3, kvn3, ck3, cv3)


def _rwkv_prep(rp, prev, mu, w0, wlu, a0, alu, glu, k_k, k_a, bd):
    xs = rp + mu * (prev - rp)
    r = xs[:, 0:RWKV_WIDTH]
    k = xs[:, RWKV_WIDTH:2 * RWKV_WIDTH]
    v = xs[:, 2 * RWKV_WIDTH:3 * RWKV_WIDTH]
    o = 3 * RWKV_WIDTH
    wd = xs[:, o:o + LORA_W]
    ad = xs[:, o + LORA_W:o + LORA_W + LORA_A]
    gd = xs[:, o + LORA_W + LORA_A:]
    z = -(w0 + _bdot(jnp.tanh(wd), wlu))
    softplus = jnp.maximum(z, 0.0) + jnp.log(1.0 + jnp.exp(-jnp.abs(z)))
    log_decay = -jnp.exp(-softplus - 0.5)
    a = _sigmoid(a0 + _bdot(ad, alu))
    g = _bdot(_sigmoid(gd), glu)
    kkr = k * k_k
    kk = kkr / jnp.maximum(jnp.sqrt(_head_sum(kkr * kkr, bd)), 1e-12)
    k2 = k * (1.0 + (a - 1.0) * k_a)
    return r, log_decay, k2, v, kk, a, g


def _rwkv_finish(y, r, k2, v, g, r_k, gn_g, gn_b, bd):
    inv = 1.0 / HEAD_DIM
    mu = _head_sum(y, bd) * inv
    yc = y - mu
    var = _head_sum(yc * yc, bd) * inv
    yn = yc * lax.rsqrt(var + GN_EPS) * gn_g + gn_b
    bonus = _head_sum(r * k2 * r_k, bd) * v
    return (yn + bonus) * g


def _rwkv_prompt_body(rp_ref, mu_ref, w0_ref, wlu_ref, a0_ref, alu_ref, glu_ref, kk_ref, ka_ref,
                      rk_ref, gng_ref, gnb_ref, bd_ref, ltri_ref,
                      out_ref, sfin_ref, prev_scr, st_scr, y_scr, *, tt):
    c = pl.program_id(1)

    @pl.when(c == 0)
    def _():
        prev_scr[...] = jnp.zeros_like(prev_scr)
        st_scr[...] = jnp.zeros_like(st_scr)

    rp = rp_ref[...]
    rolled = pltpu.roll(rp, 1, 0)
    rowi = lax.broadcasted_iota(jnp.int32, (tt, 1), 0)
    prev = jnp.where(rowi == 0, prev_scr[...], rolled)
    prev_scr[...] = rp[tt - 1:tt, :]
    bd = bd_ref[...]
    r, ld, k2, v, kk, a, g = _rwkv_prep(rp, prev, mu_ref[...], w0_ref[...], wlu_ref[...],
                                        a0_ref[...], alu_ref[...], glu_ref[...], kk_ref[...],
                                        ka_ref[...], bd)
    cum = _split_dot(ltri_ref[...], ld, 3)
    ecum = jnp.exp(cum)
    einv = jnp.exp(-cum)
    at = -kk * jnp.exp(cum - ld)
    bh = kk * a * einv
    kh = k2 * einv
    rt = r * ecum

    ii = lax.broadcasted_iota(jnp.int32, (CHUNK, CHUNK), 0)
    jj = lax.broadcasted_iota(jnp.int32, (CHUNK, CHUNK), 1)
    strict = ii > jj
    incl = ii >= jj
    eye = (ii == jj).astype(F32)

    nsub = tt // CHUNK
    units = [(s, h) for s in range(nsub) for h in range(RWKV_HEADS)]
    rows = lambda s: slice(s * CHUNK, (s + 1) * CHUNK)
    cols = lambda h: slice(h * HEAD_DIM, (h + 1) * HEAD_DIM)
    cut = lambda z, u: z[rows(u[0]), cols(u[1])]
    kka = kk * a
    p_last, kh_end, bh_end = [], [], []
    for s in range(nsub):
        c_last = cum[(s + 1) * CHUNK - 1:(s + 1) * CHUNK, :]
        p_last.append(jnp.exp(c_last))
        tail = jnp.exp(c_last - cum[rows(s)])
        kh_end.append(k2[rows(s)] * tail)
        bh_end.append(kka[rows(s)] * tail)
    amat = {u: _bdot_nt(jnp.concatenate([cut(at, u), cut(rt, u)], axis=0),
                        jnp.concatenate([cut(bh, u), cut(kh, u)], axis=0)) for u in units}
    nmat = {u: jnp.where(strict, amat[u][0:CHUNK, 0:CHUNK], 0.0) for u in units}
    a_ak = {u: jnp.where(strict, amat[u][0:CHUNK, CHUNK:], 0.0) for u in units}
    a_rb = {u: jnp.where(incl, amat[u][CHUNK:, 0:CHUNK], 0.0) for u in units}
    a_rk = {u: jnp.where(incl, amat[u][CHUNK:, CHUNK:], 0.0) for u in units}
    tmat = {u: eye + nmat[u] for u in units}
    npow = nmat
    for _ in range(5):
        npow = {u: _bdot(npow[u], npow[u]) for u in units}
        tmat = {u: tmat[u] + _bdot(npow[u], tmat[u]) for u in units}
    av = {u: _bdot(a_ak[u], cut(v, u)) for u in units}
    tw = {u: _bdot(tmat[u], jnp.concatenate([cut(at, u), av[u]], axis=1)) for u in units}
    rb = {u: _bdot(a_rb[u], tw[u]) for u in units}
    rprime = {u: cut(rt, u) + rb[u][:, 0:HEAD_DIM] for u in units}
    y0 = {u: _bdot(a_rk[u], cut(v, u)) + rb[u][:, HEAD_DIM:] for u in units}
    hmat = {u: _bdot_tn(jnp.concatenate([cut(v, u), tw[u][:, HEAD_DIM:]], axis=0),
                        jnp.concatenate([kh_end[u[0]][:, cols(u[1])], bh_end[u[0]][:, cols(u[1])]],
                                        axis=0)) for u in units}
    g0 = {u: _bdot_tn(tw[u][:, 0:HEAD_DIM], bh_end[u[0]][:, cols(u[1])]) for u in units}

    states = [st_scr[h] for h in range(RWKV_HEADS)]
    for s in range(nsub):
        for h in range(RWKV_HEADS):
            u = (s, h)
            st = states[h]
            y_scr[rows(s), cols(h)] = _bdot_nt(rprime[u], st) + y0[u]
            states[h] = st * p_last[s][:, cols(h)] + _bdot(st, g0[u]) + hmat[u]
    for h in range(RWKV_HEADS):
        st_scr[h] = states[h]
        sfin_ref[0, h] = states[h]
    out_ref[...] = _rwkv_finish(y_scr[...], r, k2, v, g, rk_ref[...], gng_ref[...], gnb_ref[...], bd)


def _chunk_tril(tt):
    i = jnp.arange(tt)
    same = (i[:, None] // CHUNK) == (i[None, :] // CHUNK)
    return (same & (i[:, None] >= i[None, :])).astype(BF16)


def _head_blockdiag():
    i = jnp.arange(RWKV_WIDTH) // HEAD_DIM
    return (i[:, None] == i[None, :]).astype(BF16)


def _rwkv_prompt(rp, prm, nbatch, seq, tt):
    nc = seq // tt
    fixed = lambda b, c: (0, 0)
    row = lambda b, c: (b * nc + c, 0)
    vec = lambda n: pl.BlockSpec((1, n), fixed)
    return pl.pallas_call(
        functools.partial(_rwkv_prompt_body, tt=tt),
        grid=(nbatch, nc),
        in_specs=[pl.BlockSpec((tt, RWKV_PROJ), row),
                  vec(RWKV_PROJ), vec(RWKV_WIDTH),
                  pl.BlockSpec((LORA_W, RWKV_WIDTH), fixed), vec(RWKV_WIDTH),
                  pl.BlockSpec((LORA_A, RWKV_WIDTH), fixed),
                  pl.BlockSpec((LORA_G, RWKV_WIDTH), fixed),
                  vec(RWKV_WIDTH), vec(RWKV_WIDTH), vec(RWKV_WIDTH), vec(RWKV_WIDTH),
                  vec(RWKV_WIDTH),
                  pl.BlockSpec((RWKV_WIDTH, RWKV_WIDTH), fixed),
                  pl.BlockSpec((tt, tt), fixed)],
        out_specs=[pl.BlockSpec((tt, RWKV_WIDTH), row),
                   pl.BlockSpec((1, RWKV_HEADS, HEAD_DIM, HEAD_DIM), lambda b, c: (b, 0, 0, 0))],
        out_shape=[jax.ShapeDtypeStruct((nbatch * seq, RWKV_WIDTH), F32),
                   jax.ShapeDtypeStruct((nbatch, RWKV_HEADS, HEAD_DIM, HEAD_DIM), F32)],
        scratch_shapes=[pltpu.VMEM((1, RWKV_PROJ), F32),
                        pltpu.VMEM((RWKV_HEADS, HEAD_DIM, HEAD_DIM), F32),
                        pltpu.VMEM((tt, RWKV_WIDTH), F32)],
        compiler_params=_cparams(("parallel", "arbitrary")),
        name="rwkv_prompt",
    )(rp, prm["mu"], prm["w0"], prm["wlu"], prm["a0"], prm["alu"], prm["glu"], prm["k_k"],
      prm["k_a"], prm["r_k"], prm["gn_g"], prm["gn_b"], _head_blockdiag(), _chunk_tril(tt))


def _rwkv_sample_prep_body(rp_ref, prev_ref, mu_ref, w0_ref, wlu_ref, a0_ref, alu_ref, glu_ref,
                           kk_ref, ka_ref, bd_ref, o_ref):
    r, ld, k2, v, kk, a, g = _rwkv_prep(rp_ref[...], prev_ref[...], mu_ref[...], w0_ref[...],
                                        wlu_ref[...], a0_ref[...], alu_ref[...], glu_ref[...],
                                        kk_ref[...], ka_ref[...], bd_ref[...])
    for i, z in enumerate((r, jnp.exp(ld), k2, v, kk, a, g)):
        o_ref[i] = z


def _rwkv_sample_prep(rp, prev, prm):
    nb = rp.shape[0]
    return pl.pallas_call(
        _rwkv_sample_prep_body,
        out_shape=jax.ShapeDtypeStruct((7, nb, RWKV_WIDTH), F32),
        compiler_params=pltpu.CompilerParams(vmem_limit_bytes=VMEM_LIMIT),
        name="rwkv_sample_prep",
    )(rp, prev, prm["mu"], prm["w0"], prm["wlu"], prm["a0"], prm["alu"], prm["glu"], prm["k_k"],
      prm["k_a"], _head_blockdiag())


def _rwkv_sample_step_body(p_ref, s_ref, rk_ref, gng_ref, gnb_ref, so_ref, o_ref, *, bb):
    ii = lax.broadcasted_iota(jnp.int32, (HEAD_DIM, HEAD_DIM), 0)
    jj = lax.broadcasted_iota(jnp.int32, (HEAD_DIM, HEAD_DIM), 1)
    eye = ii == jj
    for b in range(bb):
        for h in range(RWKV_HEADS):
            row = lambda i: p_ref[i, b, h:h + 1, :]
            r_r, w_r, k_r, v_r, kk_r, a_r = (row(i) for i in range(6))
            st = s_ref[b, h]
            sa = jnp.sum(st * kk_r, axis=1, keepdims=True)
            v_col = jnp.sum(jnp.where(eye, v_r, 0.0), axis=1, keepdims=True)
            st = st * w_r - sa * (kk_r * a_r) + v_col * k_r
            so_ref[b, h] = st
            y_col = jnp.sum(st * r_r, axis=1, keepdims=True)
            o_ref[b, h:h + 1, :] = jnp.sum(jnp.where(eye, y_col, 0.0), axis=0, keepdims=True)
    y = o_ref[...]
    r, k2, v, g = p_ref[0], p_ref[2], p_ref[3], p_ref[6]
    mu = jnp.mean(y, axis=-1, keepdims=True)
    yc = y - mu
    var = jnp.mean(yc * yc, axis=-1, keepdims=True)
    yn = yc * lax.rsqrt(var + GN_EPS) * gng_ref[...] + gnb_ref[...]
    bonus = jnp.sum(r * k2 * rk_ref[...], axis=-1, keepdims=True) * v
    o_ref[...] = (yn + bonus) * g


def _rwkv_sample_step(p4, state, r_k, gn_g, gn_b, bb=8):
    nb = state.shape[0]
    hv = pl.BlockSpec((RWKV_HEADS, HEAD_DIM), lambda i: (0, 0))
    return pl.pallas_call(
        functools.partial(_rwkv_sample_step_body, bb=bb),
        grid=(nb // bb,),
        in_specs=[pl.BlockSpec((7, bb, RWKV_HEADS, HEAD_DIM), lambda i: (0, i, 0, 0)),
                  pl.BlockSpec((bb, RWKV_HEADS, HEAD_DIM, HEAD_DIM), lambda i: (i, 0, 0, 0)),
                  hv, hv, hv],
        out_specs=[pl.BlockSpec((bb, RWKV_HEADS, HEAD_DIM, HEAD_DIM), lambda i: (i, 0, 0, 0)),
                   pl.BlockSpec((bb, RWKV_HEADS, HEAD_DIM), lambda i: (i, 0, 0))],
        out_shape=[jax.ShapeDtypeStruct(state.shape, F32),
                   jax.ShapeDtypeStruct((nb, RWKV_HEADS, HEAD_DIM), F32)],
        compiler_params=_cparams(("parallel",)),
        name="rwkv_sample_step",
    )(p4, state, r_k, gn_g, gn_b)


def _mix_router_body(att_ref, rw_ref, xn_ref, wo_ref, g_ref, b_ref, wr_ref, br_ref, base_ref,
                     x1_ref, info_ref, gate_ref, cnt_ref, base_scr, *, tm):
    i = pl.program_id(0)

    @pl.when(i == 0)
    def _():
        base_scr[...] = base_ref[...]

    mixed = (jnp.dot(att_ref[...].astype(BF16), wo_ref[0:ATT_WIDTH, :], preferred_element_type=F32)
             + jnp.dot(rw_ref[...].astype(BF16), wo_ref[ATT_WIDTH:, :], preferred_element_type=F32))
    x1 = _layer_norm(DEEPNORM_ALPHA * xn_ref[...] + mixed, g_ref[...], b_ref[...])
    x1_ref[...] = x1

    logits = jnp.dot(x1, wr_ref[...], preferred_element_type=F32, precision=HIGHEST) + br_ref[...]
    lane = lax.broadcasted_iota(jnp.int32, (tm, LANES), 1)
    lanef = lane.astype(F32)
    vals, idxs, hots = [], [], []
    cur = logits
    for _ in range(TOP_K):
        m = jnp.max(cur, axis=1, keepdims=True)
        idx = jnp.min(jnp.where(cur == m, lanef, float(LANES)), axis=1, keepdims=True)
        hot = lanef == idx
        cur = jnp.where(hot, -jnp.inf, cur)
        vals.append(m)
        idxs.append(idx)
        hots.append(hot)
    es = [jnp.exp(vk - vals[0]) for vk in vals]
    den = es[0] + es[1] + es[2] + es[3]
    multi = jnp.zeros((tm, LANES), F32)
    for hot in hots:
        multi = multi + hot.astype(F32)
    ti = lax.broadcasted_iota(jnp.int32, (tm, tm), 0)
    tj = lax.broadcasted_iota(jnp.int32, (tm, tm), 1)
    before = jnp.dot((ti > tj).astype(BF16), multi.astype(BF16), preferred_element_type=F32)
    before = before + base_scr[...]
    info = jnp.zeros((tm, LANES), jnp.int32)
    gates = jnp.zeros((tm, LANES), F32)
    for k in range(TOP_K):
        rank = jnp.sum(jnp.where(hots[k], before, 0.0), axis=1, keepdims=True)
        info = jnp.where(lane == k, idxs[k].astype(jnp.int32), info)
        info = jnp.where(lane == TOP_K + k, rank.astype(jnp.int32), info)
        gates = jnp.where(lane == k, es[k] / den, gates)
    info_ref[...] = info
    gate_ref[...] = gates
    base_scr[...] = base_scr[...] + jnp.sum(multi, axis=0, keepdims=True)
    cnt_ref[...] = base_scr[...]


def _mix_router(att, rw, xn, wo_bf16, g, b, wr_pad, br_pad, base, tm):
    t = att.shape[0]
    row = lambda i: (i, 0)
    fixed = lambda i: (0, 0)
    return pl.pallas_call(
        functools.partial(_mix_router_body, tm=tm),
        grid=(t // tm,),
        in_specs=[pl.BlockSpec((tm, ATT_WIDTH), row),
                  pl.BlockSpec((tm, RWKV_WIDTH), row),
                  pl.BlockSpec((tm, D_MODEL), row),
                  pl.BlockSpec((D_MODEL, D_MODEL), fixed),
                  pl.BlockSpec((1, D_MODEL), fixed),
                  pl.BlockSpec((1, D_MODEL), fixed),
                  pl.BlockSpec((D_MODEL, LANES), fixed),
                  pl.BlockSpec((1, LANES), fixed),
                  pl.BlockSpec((1, LANES), fixed)],
        out_specs=[pl.BlockSpec((tm, D_MODEL), row),
                   pl.BlockSpec((tm, LANES), row),
                   pl.BlockSpec((tm, LANES), row),
                   pl.BlockSpec((1, LANES), fixed)],
        out_shape=[jax.ShapeDtypeStruct((t, D_MODEL), F32),
                   jax.ShapeDtypeStruct((t, LANES), jnp.int32),
                   jax.ShapeDtypeStruct((t, LANES), F32),
                   jax.ShapeDtypeStruct((1, LANES), F32)],
        scratch_shapes=[pltpu.VMEM((1, LANES), F32)],
        compiler_params=_cparams(("arbitrary",)),
        name="mix_router",
    )(att, rw, xn, wo_bf16, g, b, wr_pad, br_pad, base)


def _dispatch_body(pos_ref, x_ref, xs_in_ref, xs_ref, sem, *, tm):
    del xs_in_ref
    base = pl.program_id(0) * tm

    def copy(r, k):
        p = pos_ref[(base + r) * TOP_K + k]
        return pltpu.make_async_copy(x_ref.at[pl.ds(r, 1)], xs_ref.at[pl.ds(p, 1)], sem)

    def start(r, carry):
        for k in range(TOP_K):
            copy(r, k).start()
        return carry

    def wait(r, carry):
        for k in range(TOP_K):
            copy(r, k).wait()
        return carry

    lax.fori_loop(0, tm, start, 0)
    lax.fori_loop(0, tm, wait, 0)


def _dispatch(pos_flat, x1, xs, tm):
    t = x1.shape[0]
    return pl.pallas_call(
        functools.partial(_dispatch_body, tm=tm),
        grid_spec=pltpu.PrefetchScalarGridSpec(
            num_scalar_prefetch=1,
            grid=(t // tm,),
            in_specs=[pl.BlockSpec((tm, D_MODEL), lambda i, pos: (i, 0)),
                      pl.BlockSpec(memory_space=pl.ANY)],
            out_specs=pl.BlockSpec(memory_space=pl.ANY),
            scratch_shapes=[pltpu.SemaphoreType.DMA]),
        out_shape=jax.ShapeDtypeStruct(xs.shape, xs.dtype),
        input_output_aliases={2: 0},
        compiler_params=_cparams(("arbitrary",)),
        name="moe_dispatch",
    )(pos_flat, x1, xs)


def _experts_body(te_ref, nu_ref, xs_ref, wgu_ref, wd_ref, bg_ref, bu_ref, bd_ref, ys_ref,
                  wg_scr, wu_scr, wd_scr):
    i = pl.program_id(0)
    new_expert = (i == 0) | (te_ref[i] != te_ref[jnp.maximum(i - 1, 0)])

    @pl.when(new_expert)
    def _():
        wd_scr[...] = wd_ref[...].astype(BF16)
        lane = lax.broadcasted_iota(jnp.int32, (LANES, LANES), 1)
        low = lane < LANES // 2
        even = (2 * lane) % LANES
        odd = (2 * lane + 1) % LANES
        pick = lambda z, idx: jnp.take_along_axis(z, idx, axis=1)

        def block(rb, carry):
            r0 = pl.multiple_of(rb * LANES, LANES)
            for m in range(D_FF // LANES):
                a = wgu_ref[pl.ds(r0, LANES), (2 * m) * LANES:(2 * m + 1) * LANES]
                b = wgu_ref[pl.ds(r0, LANES), (2 * m + 1) * LANES:(2 * m + 2) * LANES]
                dst = (pl.ds(r0, LANES), slice(m * LANES, (m + 1) * LANES))
                wg_scr[dst] = jnp.where(low, pick(a, even), pick(b, even)).astype(BF16)
                wu_scr[dst] = jnp.where(low, pick(a, odd), pick(b, odd)).astype(BF16)
            return carry

        lax.fori_loop(0, D_MODEL // LANES, block, 0)

    @pl.when(i < nu_ref[0])
    def _():
        x = xs_ref[...].astype(BF16)
        gate = jnp.dot(x, wg_scr[...], preferred_element_type=F32) + bg_ref[...]
        up = jnp.dot(x, wu_scr[...], preferred_element_type=F32) + bu_ref[...]
        gate = jnp.minimum(gate, SWIGLU_LIMIT)
        up = jnp.clip(up, -SWIGLU_LIMIT, SWIGLU_LIMIT)
        act = (up + 1.0) * gate * _sigmoid(SWIGLU_ALPHA * gate)
        ys_ref[...] = jnp.dot(act.astype(BF16), wd_scr[...], preferred_element_type=F32) + bd_ref[...]

    @pl.when(i >= nu_ref[0])
    def _():
        ys_ref[...] = jnp.zeros_like(ys_ref)


def _experts(tile_e, n_used, xs, wgu, wd, bg, bu, bd):
    ns = xs.shape[0]
    tm = EXPERT_TILE
    wspec = lambda a, b: pl.BlockSpec((None, a, b), lambda i, te, nu: (te[i], 0, 0))
    return pl.pallas_call(
        _experts_body,
        grid_spec=pltpu.PrefetchScalarGridSpec(
            num_scalar_prefetch=2,
            grid=(ns // tm,),
            in_specs=[pl.BlockSpec((tm, D_MODEL), lambda i, te, nu: (i, 0)),
                      wspec(D_MODEL, 2 * D_FF), wspec(D_FF, D_MODEL),
                      wspec(1, D_FF), wspec(1, D_FF), wspec(1, D_MODEL)],
            out_specs=pl.BlockSpec((tm, D_MODEL), lambda i, te, nu: (i, 0)),
            scratch_shapes=[pltpu.VMEM((D_MODEL, D_FF), BF16), pltpu.VMEM((D_MODEL, D_FF), BF16),
                            pltpu.VMEM((D_FF, D_MODEL), BF16)]),
        out_shape=jax.ShapeDtypeStruct((ns, D_MODEL), F32),
        compiler_params=pltpu.CompilerParams(dimension_semantics=("arbitrary",),
                                             vmem_limit_bytes=EXPERTS_VMEM_LIMIT),
        name="moe_experts",
    )(tile_e, n_used, xs, wgu, wd, bg, bu, bd)


def _combine_body(pos_ref, ys_ref, gate_ref, x1_ref, pe_ref, g_ref, b_ref, wple_ref, wpg_ref,
                  o_ref, buf, sem, *, tm):
    base = pl.program_id(0) * tm

    def copy(r, k):
        p = pos_ref[(base + r) * TOP_K + k]
        return pltpu.make_async_copy(ys_ref.at[pl.ds(p, 1)], buf.at[k, pl.ds(r, 1)], sem)

    def start(r, carry):
        for k in range(TOP_K):
            copy(r, k).start()
        return carry

    def wait(r, carry):
        for k in range(TOP_K):
            copy(r, k).wait()
        return carry

    lax.fori_loop(0, tm, start, 0)
    lax.fori_loop(0, tm, wait, 0)
    gates = gate_ref[...]
    ffn = gates[:, 0:1] * buf[0]
    for k in range(1, TOP_K):
        ffn = ffn + gates[:, k:k + 1] * buf[k]
    x2 = _layer_norm(DEEPNORM_ALPHA * x1_ref[...] + ffn, g_ref[...], b_ref[...])
    gate = _sigmoid(jnp.dot(x2.astype(BF16), wpg_ref[...], preferred_element_type=F32))
    emb = jnp.dot(pe_ref[...].astype(BF16), wple_ref[...], preferred_element_type=F32)
    o_ref[...] = x2 + gate * emb


def _combine(pos_flat, ys, gates, x1, pe, g, b, wple_bf16, wpg_bf16, tm):
    t = x1.shape[0]
    row = lambda i, pos: (i, 0)
    fixed = lambda i, pos: (0, 0)
    return pl.pallas_call(
        functools.partial(_combine_body, tm=tm),
        grid_spec=pltpu.PrefetchScalarGridSpec(
            num_scalar_prefetch=1,
            grid=(t // tm,),
            in_specs=[pl.BlockSpec(memory_space=pl.ANY),
                      pl.BlockSpec((tm, LANES), row),
                      pl.BlockSpec((tm, D_MODEL), row),
                      pl.BlockSpec((tm, PLE_DIM), row),
                      pl.BlockSpec((1, D_MODEL), fixed),
                      pl.BlockSpec((1, D_MODEL), fixed),
                      pl.BlockSpec((PLE_DIM, D_MODEL), fixed),
                      pl.BlockSpec((D_MODEL, D_MODEL), fixed)],
            out_specs=pl.BlockSpec((tm, D_MODEL), row),
            scratch_shapes=[pltpu.VMEM((TOP_K, tm, D_MODEL), F32), pltpu.SemaphoreType.DMA]),
        out_shape=jax.ShapeDtypeStruct((t, D_MODEL), F32),
        compiler_params=_cparams(("arbitrary",)),
        name="moe_combine",
    )(pos_flat, ys, gates, x1, pe, g, b, wple_bf16, wpg_bf16)


def kernel(x_prompt, x_sample, cache_k, cache_v, state_shift, state_wkv, p_prompt, p_sample,
           ln_emb_g, ln_emb_b, w_in, attn_sinks, rwkv_mu, rwkv_w0, rwkv_w_lora_up, rwkv_a0,
           rwkv_a_lora_up, rwkv_g_lora_up, rwkv_k_k, rwkv_k_a, rwkv_r_k, rwkv_gn_g, rwkv_gn_b,
           w_out, ln1_g, ln1_b, w_router, b_router, w_gate_up, b_gate_up, w_down, b_down,
           ln2_g, ln2_b, w_ple, w_ple_gate):
    assert w_in.shape[0] == DEPTH == 1
    nbp, seq, _ = x_prompt.shape
    nbs, dec_seq, _ = x_sample.shape
    assert dec_seq == 1
    tp, ts = nbp * seq, nbs
    w_buf = cache_k.shape[2]
    rowv = lambda z: z.reshape(1, -1)

    w_in_b = w_in[0].astype(BF16)
    w_out_b = w_out[0].astype(BF16)
    w_ple_b = w_ple[0].astype(BF16)
    w_pg_b = w_ple_gate[0].astype(BF16)
    sinks = attn_sinks[0]
    prm = dict(mu=rowv(rwkv_mu[0]), w0=rowv(rwkv_w0[0]), wlu=rwkv_w_lora_up[0],
               a0=rowv(rwkv_a0[0]), alu=rwkv_a_lora_up[0], glu=rwkv_g_lora_up[0],
               k_k=rowv(rwkv_k_k[0]), k_a=rowv(rwkv_k_a[0]), r_k=rowv(rwkv_r_k[0]),
               gn_g=rowv(rwkv_gn_g[0]), gn_b=rowv(rwkv_gn_b[0]))
    ge, be = rowv(ln_emb_g), rowv(ln_emb_b)

    xn_p, q_p, kv_p, rp_p = _inproj(x_prompt.reshape(tp, D_MODEL), ge, be, w_in_b, 512)
    att_p = _swa_prompt(sinks, q_p, kv_p, nbp, seq)
    rw_p, wkv_p = _rwkv_prompt(rp_p, prm, nbp, seq, 256)

    xn_s, q_s, kv_s, rp_s = _inproj(x_sample.reshape(ts, D_MODEL), ge, be, w_in_b, ts)
    ck3 = cache_k[0].reshape(nbs, w_buf, KV_COLS)
    cv3 = cache_v[0].reshape(nbs, w_buf, KV_COLS)
    att_s = _swa_sample(sinks, q_s.reshape(nbs, ATT_HEADS, HEAD_DIM),
                        kv_s.reshape(nbs, 2 * KV_HEADS, HEAD_DIM), ck3, cv3)
    att_s = att_s.reshape(ts, ATT_WIDTH)
    p7 = _rwkv_sample_prep(rp_s, state_shift[0], prm)
    hv = lambda z: z.reshape(RWKV_HEADS, HEAD_DIM)
    wkv_s, rw_s = _rwkv_sample_step(p7.reshape(7, nbs, RWKV_HEADS, HEAD_DIM), state_wkv[0],
                                    hv(rwkv_r_k[0]), hv(rwkv_gn_g[0]), hv(rwkv_gn_b[0]))
    rw_s = rw_s.reshape(ts, RWKV_WIDTH)

    wr_pad = jnp.zeros((D_MODEL, LANES), F32).at[:, :N_EXPERTS].set(w_router[0])
    br_pad = jnp.full((1, LANES), -jnp.inf, F32).at[0, :N_EXPERTS].set(b_router[0])
    g1, b1 = rowv(ln1_g[0]), rowv(ln1_b[0])
    x1_p, info_p, gate_p, cnt_p = _mix_router(att_p, rw_p, xn_p, w_out_b, g1, b1, wr_pad, br_pad,
                                              jnp.zeros((1, LANES), F32), 256)
    x1_s, info_s, gate_s, cnt = _mix_router(att_s, rw_s, xn_s, w_out_b, g1, b1, wr_pad, br_pad,
                                            cnt_p, ts)

    counts = cnt[0, :N_EXPERTS].astype(jnp.int32)
    padded = ((counts + EXPERT_TILE - 1) // EXPERT_TILE) * EXPERT_TILE
    ends = jnp.cumsum(padded)
    offs = ends - padded
    n_slots = (tp + ts) * TOP_K + N_EXPERTS * EXPERT_TILE
    n_tiles = n_slots // EXPERT_TILE
    n_used = (ends[-1] // EXPERT_TILE).reshape(1).astype(jnp.int32)
    tile_start = jnp.arange(n_tiles, dtype=jnp.int32) * EXPERT_TILE
    tile_e = jnp.sum(ends[None, :] <= tile_start[:, None], axis=1).astype(jnp.int32)
    last_e = jnp.max(jnp.where(padded > 0, jnp.arange(N_EXPERTS), 0)).astype(jnp.int32)
    tile_e = jnp.minimum(tile_e, last_e)

    def slots(info):
        return (offs[info[:, 0:TOP_K]] + info[:, TOP_K:2 * TOP_K]).reshape(-1).astype(jnp.int32)

    pos_p, pos_s = slots(info_p), slots(info_s)

    xs = jnp.zeros((n_slots, D_MODEL), F32)
    xs = _dispatch(pos_p, x1_p, xs, 256)
    xs = _dispatch(pos_s, x1_s, xs, ts)
    bgu = b_gate_up[0]
    bg = bgu[:, 0::2].reshape(N_EXPERTS, 1, D_FF)
    bu = bgu[:, 1::2].reshape(N_EXPERTS, 1, D_FF)
    bdn = b_down[0].reshape(N_EXPERTS, 1, D_MODEL)
    ys = _experts(tile_e, n_used, xs, w_gate_up[0], w_down[0], bg, bu, bdn)

    g2, b2 = rowv(ln2_g[0]), rowv(ln2_b[0])
    y_p = _combine(pos_p, ys, gate_p, x1_p, p_prompt[0].reshape(tp, PLE_DIM), g2, b2,
                   w_ple_b, w_pg_b, 256)
    y_s = _combine(pos_s, ys, gate_s, x1_s, p_sample[0].reshape(ts, PLE_DIM), g2, b2,
                   w_ple_b, w_pg_b, ts)

    w_keep = min(WINDOW, seq)
    kv_p3 = kv_p.reshape(nbp, seq, 2 * KV_COLS)[:, seq - w_keep:]
    k_win_p = kv_p3[:, :, 0:KV_COLS].reshape(1, nbp, w_keep, KV_HEADS, HEAD_DIM)
    v_win_p = kv_p3[:, :, KV_COLS:].reshape(1, nbp, w_keep, KV_HEADS, HEAD_DIM)
    shift_p = rp_p.reshape(nbp, seq, RWKV_PROJ)[:, seq - 1][None]
    k_new = kv_s[:, 0:KV_COLS].reshape(nbs, 1, KV_HEADS, HEAD_DIM)
    v_new = kv_s[:, KV_COLS:].reshape(nbs, 1, KV_HEADS, HEAD_DIM)
    k_win_s = jnp.concatenate([cache_k[0], k_new], axis=1)[:, 1:][None]
    v_win_s = jnp.concatenate([cache_v[0], v_new], axis=1)[:, 1:][None]
    return (y_p.reshape(nbp, seq, D_MODEL), y_s.reshape(nbs, 1, D_MODEL),
            k_win_p, v_win_p, shift_p, wkv_p[None],
            k_win_s, v_win_s, rp_s[None], wkv_s[None])
```

```python
import functools

import jax
import jax.numpy as jnp
from jax import lax
from jax.experimental import pallas as pl
from jax.experimental.pallas import tpu as pltpu

F32 = jnp.float32
BF16 = jnp.bfloat16
HIGHEST = lax.Precision.HIGHEST

D_MODEL = 1024
HEAD_DIM = 64
ATT_HEADS = 8
KV_HEADS = 2
GROUP = ATT_HEADS // KV_HEADS
ATT_WIDTH = ATT_HEADS * HEAD_DIM
KV_COLS = KV_HEADS * HEAD_DIM
RWKV_HEADS = 8
RWKV_WIDTH = RWKV_HEADS * HEAD_DIM
LORA_W = 64
LORA_A = 64
LORA_G = 128
RWKV_PROJ = 3 * RWKV_WIDTH + LORA_W + LORA_A + LORA_G
WINDOW = 128
BLOCK = 128
PLE_DIM = 256
N_EXPERTS = 32
TOP_K = 4
D_FF = 1024
SWIGLU_LIMIT = 7.0
SWIGLU_ALPHA = 1.702
LN_EPS = 1e-5
GN_EPS = 64e-5
DEPTH = 1
DEEPNORM_ALPHA = (2 * DEPTH) ** 0.25

LANES = 128
CHUNK = 64
EXPERT_TILE = 512
SLOT_CHUNK = 16
VMEM_LIMIT = 48 * 1024 * 1024
EXPERTS_VMEM_LIMIT = 58 * 1024 * 1024


def _cparams(sem):
    return pltpu.CompilerParams(dimension_semantics=sem, vmem_limit_bytes=VMEM_LIMIT)


def _bdot(a, b):
    return jnp.dot(a.astype(BF16), b.astype(BF16), preferred_element_type=F32)


def _bdot_nt(a, b):
    return lax.dot_general(a.astype(BF16), b.astype(BF16), (((1,), (1,)), ((), ())),
                           preferred_element_type=F32)


def _bdot_tn(a, b):
    return lax.dot_general(a.astype(BF16), b.astype(BF16), (((0,), (0,)), ((), ())),
                           preferred_element_type=F32)


def _split_dot(m_bf16, x, passes):
    acc = None
    rem = x
    for _ in range(passes):
        hi = rem.astype(BF16)
        part = jnp.dot(m_bf16, hi, preferred_element_type=F32)
        acc = part if acc is None else acc + part
        rem = rem - hi.astype(F32)
    return acc


def _head_sum(x, bd_bf16):
    acc = None
    rem = x
    for _ in range(3):
        hi = rem.astype(BF16)
        part = jnp.dot(hi, bd_bf16, preferred_element_type=F32)
        acc = part if acc is None else acc + part
        rem = rem - hi.astype(F32)
    return acc


def _sigmoid(x):
    return 1.0 / (1.0 + jnp.exp(-x))


def _layer_norm(x, g, b):
    mu = jnp.mean(x, axis=-1, keepdims=True)
    xc = x - mu
    var = jnp.mean(xc * xc, axis=-1, keepdims=True)
    return xc * lax.rsqrt(var + LN_EPS) * g + b


def _inproj_body(x_ref, g_ref, b_ref, w_ref, xn_ref, q_ref, kv_ref, rp_ref):
    xn = _layer_norm(x_ref[...], g_ref[...], b_ref[...])
    xn_ref[...] = xn
    xb = xn.astype(BF16)
    q_ref[...] = jnp.dot(xb, w_ref[:, 0:ATT_WIDTH], preferred_element_type=F32)
    kv_ref[...] = jnp.dot(xb, w_ref[:, ATT_WIDTH:ATT_WIDTH + 2 * KV_COLS],
                          preferred_element_type=F32)
    rp_ref[...] = jnp.dot(xb, w_ref[:, ATT_WIDTH + 2 * KV_COLS:], preferred_element_type=F32)


def _inproj(x2d, g, b, w_bf16, tm):
    t = x2d.shape[0]
    in_proj = w_bf16.shape[1]
    row = lambda i: (i, 0)
    fixed = lambda i: (0, 0)
    return pl.pallas_call(
        _inproj_body,
        grid=(t // tm,),
        in_specs=[pl.BlockSpec((tm, D_MODEL), row),
                  pl.BlockSpec((1, D_MODEL), fixed),
                  pl.BlockSpec((1, D_MODEL), fixed),
                  pl.BlockSpec((D_MODEL, in_proj), fixed)],
        out_specs=[pl.BlockSpec((tm, D_MODEL), row),
                   pl.BlockSpec((tm, ATT_WIDTH), row),
                   pl.BlockSpec((tm, 2 * KV_COLS), row),
                   pl.BlockSpec((tm, RWKV_PROJ), row)],
        out_shape=[jax.ShapeDtypeStruct((t, D_MODEL), F32),
                   jax.ShapeDtypeStruct((t, ATT_WIDTH), F32),
                   jax.ShapeDtypeStruct((t, 2 * KV_COLS), F32),
                   jax.ShapeDtypeStruct((t, RWKV_PROJ), F32)],
        compiler_params=_cparams(("parallel",)),
        name="inproj",
    )(x2d, g, b, w_bf16)


def _alibi_slope(h):
    return 2.0 ** (-8.0 * (h + 1) / ATT_HEADS)


def _swa_prompt_body(sink_ref, q_ref, kvc_ref, kvp_ref, o_ref):
    n = pl.program_id(1)
    q = (q_ref[...] * (HEAD_DIM ** -0.5)).astype(BF16)
    kvc = kvc_ref[...].astype(BF16)
    kvp = kvp_ref[...].astype(BF16)
    row = lax.broadcasted_iota(jnp.int32, (BLOCK, 2 * BLOCK), 0)
    col = lax.broadcasted_iota(jnp.int32, (BLOCK, 2 * BLOCK), 1)
    dist = row + BLOCK - col
    valid = (dist >= 0) & (dist <= WINDOW) & ((col >= BLOCK) | (n > 0))
    distf = dist.astype(F32)
    outs = []
    for g in range(KV_HEADS):
        ks = slice(g * HEAD_DIM, (g + 1) * HEAD_DIM)
        vs = slice(KV_COLS + g * HEAD_DIM, KV_COLS + (g + 1) * HEAD_DIM)
        kband = jnp.concatenate([kvp[:, ks], kvc[:, ks]], axis=0)
        vband = jnp.concatenate([kvp[:, vs], kvc[:, vs]], axis=0)
        for j in range(GROUP):
            h = g * GROUP + j
            qh = q[:, h * HEAD_DIM:(h + 1) * HEAD_DIM]
            s = lax.dot_general(qh, kband, (((1,), (1,)), ((), ())), preferred_element_type=F32)
            s = jnp.where(valid, s - _alibi_slope(h) * distf, -jnp.inf)
            sink = sink_ref[h]
            m = jnp.maximum(jnp.max(s, axis=1, keepdims=True), sink)
            e = jnp.exp(s - m)
            den = jnp.sum(e, axis=1, keepdims=True) + jnp.exp(sink - m)
            p = e / den
            outs.append(jnp.dot(p.astype(BF16), vband, preferred_element_type=F32))
    o_ref[...] = jnp.concatenate(outs, axis=1)


def _swa_prompt(sinks, q, kv, nbatch, seq):
    nb = seq // BLOCK
    cur = lambda b, n: (b * nb + n, 0)
    prv = lambda b, n: (b * nb + jnp.maximum(n - 1, 0), 0)
    return pl.pallas_call(
        _swa_prompt_body,
        grid=(nbatch, nb),
        in_specs=[pl.BlockSpec(memory_space=pltpu.SMEM),
                  pl.BlockSpec((BLOCK, ATT_WIDTH), cur),
                  pl.BlockSpec((BLOCK, 2 * KV_COLS), cur),
                  pl.BlockSpec((BLOCK, 2 * KV_COLS), prv)],
        out_specs=pl.BlockSpec((BLOCK, ATT_WIDTH), cur),
        out_shape=jax.ShapeDtypeStruct((nbatch * seq, ATT_WIDTH), F32),
        compiler_params=_cparams(("parallel", "parallel")),
        name="swa_prompt",
    )(sinks, q, kv, kv)


def _swa_sample_body(sink_ref, q_ref, kvn_ref, ck_ref, cv_ref, o_ref, *, bb, w_buf):
    hrow = lax.broadcasted_iota(jnp.int32, (ATT_HEADS, 1), 0)
    slope = jnp.zeros((ATT_HEADS, 1), F32)
    sink = jnp.zeros((ATT_HEADS, 1), F32)
    for h in range(ATT_HEADS):
        slope = jnp.where(hrow == h, _alibi_slope(h), slope)
        sink = jnp.where(hrow == h, sink_ref[h], sink)
    jcol = lax.broadcasted_iota(jnp.int32, (ATT_HEADS, w_buf), 1)
    bias = -slope * (w_buf - jcol).astype(F32)
    lower = hrow < GROUP
    for b in range(bb):
        q2 = q_ref[b] * (HEAD_DIM ** -0.5)
        kvn = kvn_ref[b]
        ck = ck_ref[b].astype(BF16)
        cv = cv_ref[b].astype(BF16)
        q2b = q2.astype(BF16)
        s0 = _bdot_nt(q2b, ck[:, 0:HEAD_DIM])
        s1 = _bdot_nt(q2b, ck[:, HEAD_DIM:2 * HEAD_DIM])
        s = jnp.where(lower, s0, s1) + bias
        knew = jnp.where(lower, kvn[0:1, :], kvn[1:2, :])
        vnew = jnp.where(lower, kvn[2:3, :], kvn[3:4, :])
        q2r = q2b.astype(F32)
        snew = jnp.sum(q2r * knew.astype(BF16).astype(F32), axis=1, keepdims=True)
        m = jnp.maximum(jnp.maximum(jnp.max(s, axis=1, keepdims=True), snew), sink)
        e = jnp.exp(s - m)
        enew = jnp.exp(snew - m)
        den = jnp.sum(e, axis=1, keepdims=True) + enew + jnp.exp(sink - m)
        p = (e / den).astype(BF16)
        pnew = (enew / den).astype(BF16).astype(F32)
        o0 = jnp.dot(p, cv[:, 0:HEAD_DIM], preferred_element_type=F32)
        o1 = jnp.dot(p, cv[:, HEAD_DIM:2 * HEAD_DIM], preferred_element_type=F32)
        o = jnp.where(lower, o0, o1) + pnew * vnew.astype(BF16).astype(F32)
        o_ref[b] = o


def _swa_sample(sinks, q3, kvn3, ck3, cv3, bb=8):
    nb, w_buf = ck3.shape[0], ck3.shape[1]
    blk = lambda i: (i, 0, 0)
    return pl.pallas_call(
        functools.partial(_swa_sample_body, bb=bb, w_buf=w_buf),
        grid=(nb // bb,),
        in_specs=[pl.BlockSpec(memory_space=pltpu.SMEM),
                  pl.BlockSpec((bb, ATT_HEADS, HEAD_DIM), blk),
                  pl.BlockSpec((bb, 2 * KV_HEADS, HEAD_DIM), blk),
                  pl.BlockSpec((bb, w_buf, KV_COLS), blk),
                  pl.BlockSpec((bb, w_buf, KV_COLS), blk)],
        out_specs=pl.BlockSpec((bb, ATT_HEADS, HEAD_DIM), blk),
        out_shape=jax.ShapeDtypeStruct((nb, ATT_HEADS, HEAD_DIM), F32),
        compiler_params=_cparams(("parallel",)),
        name="swa_sample",
    )(sinks, q3, kvn3, ck3, cv3)


def _rwkv_prep(rp, prev, mu, w0, wlu, a0, alu, glu, k_k, k_a, bd):
    xs = rp + mu * (prev - rp)
    r = xs[:, 0:RWKV_WIDTH]
    k = xs[:, RWKV_WIDTH:2 * RWKV_WIDTH]
    v = xs[:, 2 * RWKV_WIDTH:3 * RWKV_WIDTH]
    o = 3 * RWKV_WIDTH
    wd = xs[:, o:o + LORA_W]
    ad = xs[:, o + LORA_W:o + LORA_W + LORA_A]
    gd = xs[:, o + LORA_W + LORA_A:]
    z = -(w0 + _bdot(jnp.tanh(wd), wlu))
    softplus = jnp.maximum(z, 0.0) + jnp.log(1.0 + jnp.exp(-jnp.abs(z)))
    log_decay = -jnp.exp(-softplus - 0.5)
    a = _sigmoid(a0 + _bdot(ad, alu))
    g = _bdot(_sigmoid(gd), glu)
    kkr = k * k_k
    kk = kkr / jnp.maximum(jnp.sqrt(_head_sum(kkr * kkr, bd)), 1e-12)
    k2 = k * (1.0 + (a - 1.0) * k_a)
    return r, log_decay, k2, v, kk, a, g


def _rwkv_finish(y, r, k2, v, g, r_k, gn_g, gn_b, bd):
    inv = 1.0 / HEAD_DIM
    mu = _head_sum(y, bd) * inv
    yc = y - mu
    var = _head_sum(yc * yc, bd) * inv
    yn = yc * lax.rsqrt(var + GN_EPS) * gn_g + gn_b
    bonus = _head_sum(r * k2 * r_k, bd) * v
    return (yn + bonus) * g


def _rwkv_prompt_body(rp_ref, mu_ref, w0_ref, wlu_ref, a0_ref, alu_ref, glu_ref, kk_ref, ka_ref,
                      rk_ref, gng_ref, gnb_ref, bd_ref, ltri_ref,
                      out_ref, sfin_ref, prev_scr, st_scr, y_scr, *, tt):
    c = pl.program_id(1)

    @pl.when(c == 0)
    def _():
        prev_scr[...] = jnp.zeros_like(prev_scr)
        st_scr[...] = jnp.zeros_like(st_scr)

    rp = rp_ref[...]
    rolled = pltpu.roll(rp, 1, 0)
    rowi = lax.broadcasted_iota(jnp.int32, (tt, 1), 0)
    prev = jnp.where(rowi == 0, prev_scr[...], rolled)
    prev_scr[...] = rp[tt - 1:tt, :]
    bd = bd_ref[...]
    r, ld, k2, v, kk, a, g = _rwkv_prep(rp, prev, mu_ref[...], w0_ref[...], wlu_ref[...],
                                        a0_ref[...], alu_ref[...], glu_ref[...], kk_ref[...],
                                        ka_ref[...], bd)
    cum = _split_dot(ltri_ref[...], ld, 3)
    ecum = jnp.exp(cum)
    einv = jnp.exp(-cum)
    at = -kk * jnp.exp(cum - ld)
    bh = kk * a * einv
    kh = k2 * einv
    rt = r * ecum

    ii = lax.broadcasted_iota(jnp.int32, (CHUNK, CHUNK), 0)
    jj = lax.broadcasted_iota(jnp.int32, (CHUNK, CHUNK), 1)
    strict = ii > jj
    incl = ii >= jj
    eye = (ii == jj).astype(F32)

    nsub = tt // CHUNK
    units = [(s, h) for s in range(nsub) for h in range(RWKV_HEADS)]
    rows = lambda s: slice(s * CHUNK, (s + 1) * CHUNK)
    cols = lambda h: slice(h * HEAD_DIM, (h + 1) * HEAD_DIM)
    cut = lambda z, u: z[rows(u[0]), cols(u[1])]
    kka = kk * a
    p_last, kh_end, bh_end = [], [], []
    for s in range(nsub):
        c_last = cum[(s + 1) * CHUNK - 1:(s + 1) * CHUNK, :]
        p_last.append(jnp.exp(c_last))
        tail = jnp.exp(c_last - cum[rows(s)])
        kh_end.append(k2[rows(s)] * tail)
        bh_end.append(kka[rows(s)] * tail)
    amat = {u: _bdot_nt(jnp.concatenate([cut(at, u), cut(rt, u)], axis=0),
                        jnp.concatenate([cut(bh, u), cut(kh, u)], axis=0)) for u in units}
    nmat = {u: jnp.where(strict, amat[u][0:CHUNK, 0:CHUNK], 0.0) for u in units}
    a_ak = {u: jnp.where(strict, amat[u][0:CHUNK, CHUNK:], 0.0) for u in units}
    a_rb = {u: jnp.where(incl, amat[u][CHUNK:, 0:CHUNK], 0.0) for u in units}
    a_rk = {u: jnp.where(incl, amat[u][CHUNK:, CHUNK:], 0.0) for u in units}
    tmat = {u: eye + nmat[u] for u in units}
    npow = nmat
    for _ in range(5):
        npow = {u: _bdot(npow[u], npow[u]) for u in units}
        tmat = {u: tmat[u] + _bdot(npow[u], tmat[u]) for u in units}
    av = {u: _bdot(a_ak[u], cut(v, u)) for u in units}
    tw = {u: _bdot(tmat[u], jnp.concatenate([cut(at, u), av[u]], axis=1)) for u in units}
    rb = {u: _bdot(a_rb[u], tw[u]) for u in units}
    rprime = {u: cut(rt, u) + rb[u][:, 0:HEAD_DIM] for u in units}
    y0 = {u: _bdot(a_rk[u], cut(v, u)) + rb[u][:, HEAD_DIM:] for u in units}
    hmat = {u: _bdot_tn(jnp.concatenate([cut(v, u), tw[u][:, HEAD_DIM:]], axis=0),
                        jnp.concatenate([kh_end[u[0]][:, cols(u[1])], bh_end[u[0]][:, cols(u[1])]],
                                        axis=0)) for u in units}
    g0 = {u: _bdot_tn(tw[u][:, 0:HEAD_DIM], bh_end[u[0]][:, cols(u[1])]) for u in units}

    states = [st_scr[h] for h in range(RWKV_HEADS)]
    for s in range(nsub):
        for h in range(RWKV_HEADS):
            u = (s, h)
            st = states[h]
            y_scr[rows(s), cols(h)] = _bdot_nt(rprime[u], st) + y0[u]
            states[h] = st * p_last[s][:, cols(h)] + _bdot(st, g0[u]) + hmat[u]
    for h in range(RWKV_HEADS):
        st_scr[h] = states[h]
        sfin_ref[0, h] = states[h]
    out_ref[...] = _rwkv_finish(y_scr[...], r, k2, v, g, rk_ref[...], gng_ref[...], gnb_ref[...], bd)


def _chunk_tril(tt):
    i = jnp.arange(tt)
    same = (i[:, None] // CHUNK) == (i[None, :] // CHUNK)
    return (same & (i[:, None] >= i[None, :])).astype(BF16)


def _head_blockdiag():
    i = jnp.arange(RWKV_WIDTH) // HEAD_DIM
    return (i[:, None] == i[None, :]).astype(BF16)


def _rwkv_prompt(rp, prm, nbatch, seq, tt):
    nc = seq // tt
    fixed = lambda b, c: (0, 0)
    row = lambda b, c: (b * nc + c, 0)
    vec = lambda n: pl.BlockSpec((1, n), fixed)
    return pl.pallas_call(
        functools.partial(_rwkv_prompt_body, tt=tt),
        grid=(nbatch, nc),
        in_specs=[pl.BlockSpec((tt, RWKV_PROJ), row),
                  vec(RWKV_PROJ), vec(RWKV_WIDTH),
                  pl.BlockSpec((LORA_W, RWKV_WIDTH), fixed), vec(RWKV_WIDTH),
                  pl.BlockSpec((LORA_A, RWKV_WIDTH), fixed),
                  pl.BlockSpec((LORA_G, RWKV_WIDTH), fixed),
                  vec(RWKV_WIDTH), vec(RWKV_WIDTH), vec(RWKV_WIDTH), vec(RWKV_WIDTH),
                  vec(RWKV_WIDTH),
                  pl.BlockSpec((RWKV_WIDTH, RWKV_WIDTH), fixed),
                  pl.BlockSpec((tt, tt), fixed)],
        out_specs=[pl.BlockSpec((tt, RWKV_WIDTH), row),
                   pl.BlockSpec((1, RWKV_HEADS, HEAD_DIM, HEAD_DIM), lambda b, c: (b, 0, 0, 0))],
        out_shape=[jax.ShapeDtypeStruct((nbatch * seq, RWKV_WIDTH), F32),
                   jax.ShapeDtypeStruct((nbatch, RWKV_HEADS, HEAD_DIM, HEAD_DIM), F32)],
        scratch_shapes=[pltpu.VMEM((1, RWKV_PROJ), F32),
                        pltpu.VMEM((RWKV_HEADS, HEAD_DIM, HEAD_DIM), F32),
                        pltpu.VMEM((tt, RWKV_WIDTH), F32)],
        compiler_params=_cparams(("parallel", "arbitrary")),
        name="rwkv_prompt",
    )(rp, prm["mu"], prm["w0"], prm["wlu"], prm["a0"], prm["alu"], prm["glu"], prm["k_k"],
      prm["k_a"], prm["r_k"], prm["gn_g"], prm["gn_b"], _head_blockdiag(), _chunk_tril(tt))


def _rwkv_sample_prep_body(rp_ref, prev_ref, mu_ref, w0_ref, wlu_ref, a0_ref, alu_ref, glu_ref,
                           kk_ref, ka_ref, bd_ref, o_ref):
    r, ld, k2, v, kk, a, g = _rwkv_prep(rp_ref[...], prev_ref[...], mu_ref[...], w0_ref[...],
                                        wlu_ref[...], a0_ref[...], alu_ref[...], glu_ref[...],
                                        kk_ref[...], ka_ref[...], bd_ref[...])
    for i, z in enumerate((r, jnp.exp(ld), k2, v, kk, a, g)):
        o_ref[i] = z


def _rwkv_sample_prep(rp, prev, prm):
    nb = rp.shape[0]
    return pl.pallas_call(
        _rwkv_sample_prep_body,
        out_shape=jax.ShapeDtypeStruct((7, nb, RWKV_WIDTH), F32),
        compiler_params=pltpu.CompilerParams(vmem_limit_bytes=VMEM_LIMIT),
        name="rwkv_sample_prep",
    )(rp, prev, prm["mu"], prm["w0"], prm["wlu"], prm["a0"], prm["alu"], prm["glu"], prm["k_k"],
      prm["k_a"], _head_blockdiag())


def _rwkv_sample_step_body(p_ref, s_ref, rk_ref, gng_ref, gnb_ref, so_ref, o_ref, *, bb):
    ii = lax.broadcasted_iota(jnp.int32, (HEAD_DIM, HEAD_DIM), 0)
    jj = lax.broadcasted_iota(jnp.int32, (HEAD_DIM, HEAD_DIM), 1)
    eye = ii == jj
    for b in range(bb):
        for h in range(RWKV_HEADS):
            row = lambda i: p_ref[i, b, h:h + 1, :]
            r_r, w_r, k_r, v_r, kk_r, a_r = (row(i) for i in range(6))
            st = s_ref[b, h]
            sa = jnp.sum(st * kk_r, axis=1, keepdims=True)
            v_col = jnp.sum(jnp.where(eye, v_r, 0.0), axis=1, keepdims=True)
            st = st * w_r - sa * (kk_r * a_r) + v_col * k_r
            so_ref[b, h] = st
            y_col = jnp.sum(st * r_r, axis=1, keepdims=True)
            o_ref[b, h:h + 1, :] = jnp.sum(jnp.where(eye, y_col, 0.0), axis=0, keepdims=True)
    y = o_ref[...]
    r, k2, v, g = p_ref[0], p_ref[2], p_ref[3], p_ref[6]
    mu = jnp.mean(y, axis=-1, keepdims=True)
    yc = y - mu
    var = jnp.mean(yc * yc, axis=-1, keepdims=True)
    yn = yc * lax.rsqrt(var + GN_EPS) * gng_ref[...] + gnb_ref[...]
    bonus = jnp.sum(r * k2 * rk_ref[...], axis=-1, keepdims=True) * v
    o_ref[...] = (yn + bonus) * g


def _rwkv_sample_step(p4, state, r_k, gn_g, gn_b, bb=8):
    nb = state.shape[0]
    hv = pl.BlockSpec((RWKV_HEADS, HEAD_DIM), lambda i: (0, 0))
    return pl.pallas_call(
        functools.partial(_rwkv_sample_step_body, bb=bb),
        grid=(nb // bb,),
        in_specs=[pl.BlockSpec((7, bb, RWKV_HEADS, HEAD_DIM), lambda i: (0, i, 0, 0)),
                  pl.BlockSpec((bb, RWKV_HEADS, HEAD_DIM, HEAD_DIM), lambda i: (i, 0, 0, 0)),
                  hv, hv, hv],
        out_specs=[pl.BlockSpec((bb, RWKV_HEADS, HEAD_DIM, HEAD_DIM), lambda i: (i, 0, 0, 0)),
                   pl.BlockSpec((bb, RWKV_HEADS, HEAD_DIM), lambda i: (i, 0, 0))],
        out_shape=[jax.ShapeDtypeStruct(state.shape, F32),
                   jax.ShapeDtypeStruct((nb, RWKV_HEADS, HEAD_DIM), F32)],
        compiler_params=_cparams(("parallel",)),
        name="rwkv_sample_step",
    )(p4, state, r_k, gn_g, gn_b)


def _mix_router_body(att_ref, rw_ref, xn_ref, wo_ref, g_ref, b_ref, wrh_ref, wrl_ref, br_ref,
                     base_ref, x1_ref, info_ref, gate_ref, stat_ref, cnt_ref, base_scr, *, tm):
    i = pl.program_id(0)

    @pl.when(i == 0)
    def _():
        base_scr[...] = base_ref[...]

    mixed = (jnp.dot(att_ref[...].astype(BF16), wo_ref[0:ATT_WIDTH, :], preferred_element_type=F32)
             + jnp.dot(rw_ref[...].astype(BF16), wo_ref[ATT_WIDTH:, :], preferred_element_type=F32))
    x1 = _layer_norm(DEEPNORM_ALPHA * xn_ref[...] + mixed, g_ref[...], b_ref[...])
    x1_ref[...] = x1

    x1h = x1.astype(BF16)
    x1l = (x1 - x1h.astype(F32)).astype(BF16)
    logits = (jnp.dot(x1h, wrh_ref[...], preferred_element_type=F32)
              + jnp.dot(x1l, wrh_ref[...], preferred_element_type=F32)
              + jnp.dot(x1h, wrl_ref[...], preferred_element_type=F32)) + br_ref[...]
    lane = lax.broadcasted_iota(jnp.int32, (tm, LANES), 1)
    lanef = lane.astype(F32)
    vals, idxs, hots = [], [], []
    cur = logits
    for _ in range(TOP_K):
        m = jnp.max(cur, axis=1, keepdims=True)
        idx = jnp.min(jnp.where(cur == m, lanef, float(LANES)), axis=1, keepdims=True)
        hot = lanef == idx
        cur = jnp.where(hot, -jnp.inf, cur)
        vals.append(m)
        idxs.append(idx)
        hots.append(hot)
    es = [jnp.exp(vk - vals[0]) for vk in vals]
    den = es[0] + es[1] + es[2] + es[3]
    multi = jnp.zeros((tm, LANES), F32)
    for hot in hots:
        multi = multi + hot.astype(F32)
    ti = lax.broadcasted_iota(jnp.int32, (tm, tm), 0)
    tj = lax.broadcasted_iota(jnp.int32, (tm, tm), 1)
    before = jnp.dot((ti > tj).astype(BF16), multi.astype(BF16), preferred_element_type=F32)
    cnt_t = jnp.sum(multi, axis=0, keepdims=True)
    run = jnp.floor((cnt_t + (SLOT_CHUNK - 1)) * (1.0 / SLOT_CHUNK)) * SLOT_CHUNK
    ei = lax.broadcasted_iota(jnp.int32, (LANES, LANES), 0)
    ej = lax.broadcasted_iota(jnp.int32, (LANES, LANES), 1)
    run8 = jnp.broadcast_to(run, (8, LANES))
    start8 = None
    for _ in range(3):
        hi = run8.astype(BF16)
        part = jnp.dot(hi, (ei < ej).astype(BF16), preferred_element_type=F32)
        start8 = part if start8 is None else start8 + part
        run8 = run8 - hi.astype(F32)
    start_t = start8[0:1, :]
    info = jnp.zeros((tm, LANES), jnp.int32)
    gates = jnp.zeros((tm, LANES), F32)
    for k in range(TOP_K):
        lpos = jnp.sum(jnp.where(hots[k], before + start_t, 0.0), axis=1, keepdims=True)
        info = jnp.where(lane == k, idxs[k].astype(jnp.int32), info)
        info = jnp.where(lane == TOP_K + k, lpos.astype(jnp.int32), info)
        gates = jnp.where(lane == k, es[k] / den, gates)
    info_ref[...] = info
    gate_ref[...] = gates
    srow = lax.broadcasted_iota(jnp.int32, (8, LANES), 0)
    stat = jnp.where(srow == 0, cnt_t, jnp.where(srow == 1, base_scr[...],
                                                 jnp.where(srow == 2, start_t, 0.0)))
    stat_ref[...] = stat.astype(jnp.int32)
    base_scr[...] = base_scr[...] + cnt_t
    cnt_ref[...] = base_scr[...]


def _mix_router(att, rw, xn, wo_bf16, g, b, wr_hi, wr_lo, br_pad, base, tm):
    t = att.shape[0]
    row = lambda i: (i, 0)
    fixed = lambda i: (0, 0)
    return pl.pallas_call(
        functools.partial(_mix_router_body, tm=tm),
        grid=(t // tm,),
        in_specs=[pl.BlockSpec((tm, ATT_WIDTH), row),
                  pl.BlockSpec((tm, RWKV_WIDTH), row),
                  pl.BlockSpec((tm, D_MODEL), row),
                  pl.BlockSpec((D_MODEL, D_MODEL), fixed),
                  pl.BlockSpec((1, D_MODEL), fixed),
                  pl.BlockSpec((1, D_MODEL), fixed),
                  pl.BlockSpec((D_MODEL, LANES), fixed),
                  pl.BlockSpec((D_MODEL, LANES), fixed),
                  pl.BlockSpec((1, LANES), fixed),
                  pl.BlockSpec((1, LANES), fixed)],
        out_specs=[pl.BlockSpec((tm, D_MODEL), row),
                   pl.BlockSpec((tm, LANES), row),
                   pl.BlockSpec((tm, LANES), row),
                   pl.BlockSpec((8, LANES), row),
                   pl.BlockSpec((1, LANES), fixed)],
        out_shape=[jax.ShapeDtypeStruct((t, D_MODEL), F32),
                   jax.ShapeDtypeStruct((t, LANES), jnp.int32),
                   jax.ShapeDtypeStruct((t, LANES), F32),
                   jax.ShapeDtypeStruct((t // tm * 8, LANES), jnp.int32),
                   jax.ShapeDtypeStruct((1, LANES), F32)],
        scratch_shapes=[pltpu.VMEM((1, LANES), F32)],
        compiler_params=_cparams(("arbitrary",)),
        name="mix_router",
    )(att, rw, xn, wo_bf16, g, b, wr_hi, wr_lo, br_pad, base)


def _local_rows(tm):
    return TOP_K * tm + N_EXPERTS * SLOT_CHUNK


def _run_copies(cnt_ref, cst_ref, dst_ref, tile, make_copy, op):
    def per_expert(e, carry):
        n = cnt_ref[tile * N_EXPERTS + e]
        s = cst_ref[tile * N_EXPERTS + e]
        d = dst_ref[tile * N_EXPERTS + e]

        def piece(j, c2):
            op(make_copy(s + j * SLOT_CHUNK, d + j * SLOT_CHUNK))
            return c2

        lax.fori_loop(0, (n + (SLOT_CHUNK - 1)) // SLOT_CHUNK, piece, 0)
        return carry

    lax.fori_loop(0, N_EXPERTS, per_expert, 0)


def _pick_matrix(info, tm, weights):
    col = lax.broadcasted_iota(jnp.int32, (tm, _local_rows(tm)), 1)
    out = jnp.zeros((tm, _local_rows(tm)), F32)
    for k in range(TOP_K):
        out = jnp.where(col == info[:, TOP_K + k:TOP_K + k + 1], weights[k], out)
    return out


def _dispatch_body(cnt_ref, cst_ref, dst_ref, x_ref, info_ref, xs_in_ref, xs_ref, buf, sem, *, tm):
    del xs_in_ref
    tile = pl.program_id(0)
    pick = _pick_matrix(info_ref[...], tm, [1.0] * TOP_K).astype(BF16)
    ordered = _bdot_tn(pick, x_ref[...])
    for c in range(D_MODEL // LANES):
        buf[:, c, :] = ordered[:, c * LANES:(c + 1) * LANES]

    def make_copy(src, dst):
        return pltpu.make_async_copy(buf.at[pl.ds(src, SLOT_CHUNK)],
                                     xs_ref.at[pl.ds(dst, SLOT_CHUNK)], sem)

    _run_copies(cnt_ref, cst_ref, dst_ref, tile, make_copy, lambda cp: cp.start())
    _run_copies(cnt_ref, cst_ref, dst_ref, tile, make_copy, lambda cp: cp.wait())


def _dispatch(runs, x1, info, xs, tm):
    t = x1.shape[0]
    row = lambda i, *_: (i, 0)
    return pl.pallas_call(
        functools.partial(_dispatch_body, tm=tm),
        grid_spec=pltpu.PrefetchScalarGridSpec(
            num_scalar_prefetch=3,
            grid=(t // tm,),
            in_specs=[pl.BlockSpec((tm, D_MODEL), row),
                      pl.BlockSpec((tm, LANES), row),
                      pl.BlockSpec(memory_space=pl.ANY)],
            out_specs=pl.BlockSpec(memory_space=pl.ANY),
            scratch_shapes=[pltpu.VMEM((_local_rows(tm), D_MODEL // LANES, LANES), F32),
                            pltpu.SemaphoreType.DMA]),
        out_shape=jax.ShapeDtypeStruct(xs.shape, xs.dtype),
        input_output_aliases={5: 0},
        compiler_params=_cparams(("arbitrary",)),
        name="moe_dispatch",
    )(*runs, x1, info, xs)


def _experts_body(te_ref, nu_ref, xs_ref, wgu_ref, wd_ref, bg_ref, bu_ref, bd_ref, sel_ref, ys_ref,
                  wg_scr, wu_scr, wd_scr):
    i = pl.program_id(0)
    new_expert = (i == 0) | (te_ref[i] != te_ref[jnp.maximum(i - 1, 0)])

    @pl.when(new_expert)
    def _():
        wd_scr[...] = wd_ref[...].astype(BF16)
        for m in range(D_FF // LANES):
            pair = wgu_ref[:, 2 * m * LANES:2 * (m + 1) * LANES].astype(BF16)
            split = jnp.dot(pair, sel_ref[...], preferred_element_type=F32)
            wg_scr[:, m * LANES:(m + 1) * LANES] = split[:, 0:LANES].astype(BF16)
            wu_scr[:, m * LANES:(m + 1) * LANES] = split[:, LANES:].astype(BF16)

    @pl.when(i < nu_ref[0])
    def _():
        x = jnp.concatenate([xs_ref[:, c, :] for c in range(D_MODEL // LANES)], axis=1).astype(BF16)
        gate = jnp.dot(x, wg_scr[...], preferred_element_type=F32) + bg_ref[...]
        up = jnp.dot(x, wu_scr[...], preferred_element_type=F32) + bu_ref[...]
        gate = jnp.minimum(gate, SWIGLU_LIMIT)
        up = jnp.clip(up, -SWIGLU_LIMIT, SWIGLU_LIMIT)
        act = (up + 1.0) * gate * _sigmoid(SWIGLU_ALPHA * gate)
        out = jnp.dot(act.astype(BF16), wd_scr[...], preferred_element_type=F32) + bd_ref[...]
        for c in range(D_MODEL // LANES):
            ys_ref[:, c, :] = out[:, c * LANES:(c + 1) * LANES]

    @pl.when(i >= nu_ref[0])
    def _():
        ys_ref[...] = jnp.zeros_like(ys_ref)


def _gate_up_selector():
    i = jnp.arange(2 * LANES)
    src = jnp.where(i < LANES, 2 * i, 2 * (i - LANES) + 1)
    return (i[:, None] == src[None, :]).astype(BF16)


def _experts(tile_e, n_used, xs, wgu, wd, bg, bu, bd):
    ns = xs.shape[0]
    tm = EXPERT_TILE
    nc = D_MODEL // LANES
    wspec = lambda a, b: pl.BlockSpec((None, a, b), lambda i, te, nu: (te[i], 0, 0))
    return pl.pallas_call(
        _experts_body,
        grid_spec=pltpu.PrefetchScalarGridSpec(
            num_scalar_prefetch=2,
            grid=(ns // tm,),
            in_specs=[pl.BlockSpec((tm, nc, LANES), lambda i, te, nu: (i, 0, 0)),
                      wspec(D_MODEL, 2 * D_FF), wspec(D_FF, D_MODEL),
                      wspec(1, D_FF), wspec(1, D_FF), wspec(1, D_MODEL),
                      pl.BlockSpec((2 * LANES, 2 * LANES), lambda i, te, nu: (0, 0))],
            out_specs=pl.BlockSpec((tm, nc, LANES), lambda i, te, nu: (i, 0, 0)),
            scratch_shapes=[pltpu.VMEM((D_MODEL, D_FF), BF16), pltpu.VMEM((D_MODEL, D_FF), BF16),
                            pltpu.VMEM((D_FF, D_MODEL), BF16)]),
        out_shape=jax.ShapeDtypeStruct((ns, nc, LANES), F32),
        compiler_params=pltpu.CompilerParams(dimension_semantics=("arbitrary",),
                                             vmem_limit_bytes=EXPERTS_VMEM_LIMIT),
        name="moe_experts",
    )(tile_e, n_used, xs, wgu, wd, bg, bu, bd, _gate_up_selector())


def _combine_body(cnt_ref, cst_ref, dst_ref, ys_ref, info_ref, gate_ref, x1_ref, pe_ref, g_ref,
                  b_ref, wple_ref, wpg_ref, o_ref, buf, sem, *, tm):
    tile = pl.program_id(0)

    @pl.when(tile == 0)
    def _():
        buf[...] = jnp.zeros_like(buf)

    def make_copy(loc, slot):
        return pltpu.make_async_copy(ys_ref.at[pl.ds(slot, SLOT_CHUNK)],
                                     buf.at[pl.ds(loc, SLOT_CHUNK)], sem)

    _run_copies(cnt_ref, cst_ref, dst_ref, tile, make_copy, lambda cp: cp.start())
    _run_copies(cnt_ref, cst_ref, dst_ref, tile, make_copy, lambda cp: cp.wait())
    gates = gate_ref[...]
    pick = _pick_matrix(info_ref[...], tm, [gates[:, k:k + 1] for k in range(TOP_K)])
    pick_hi = pick.astype(BF16)
    pick_lo = (pick - pick_hi.astype(F32)).astype(BF16)
    parts = []
    for c in range(D_MODEL // LANES):
        rows = buf[:, c, :]
        rows_hi = rows.astype(BF16)
        rows_lo = (rows - rows_hi.astype(F32)).astype(BF16)
        parts.append(jnp.dot(pick_hi, rows_hi, preferred_element_type=F32)
                     + jnp.dot(pick_lo, rows_hi, preferred_element_type=F32)
                     + jnp.dot(pick_hi, rows_lo, preferred_element_type=F32))
    ffn = jnp.concatenate(parts, axis=1)
    x2 = _layer_norm(DEEPNORM_ALPHA * x1_ref[...] + ffn, g_ref[...], b_ref[...])
    gate = _sigmoid(jnp.dot(x2.astype(BF16), wpg_ref[...], preferred_element_type=F32))
    emb = jnp.dot(pe_ref[...].astype(BF16), wple_ref[...], preferred_element_type=F32)
    o_ref[...] = x2 + gate * emb


def _combine(runs, ys, info, gates, x1, pe, g, b, wple_bf16, wpg_bf16, tm):
    t = x1.shape[0]
    row = lambda i, *_: (i, 0)
    fixed = lambda i, *_: (0, 0)
    return pl.pallas_call(
        functools.partial(_combine_body, tm=tm),
        grid_spec=pltpu.PrefetchScalarGridSpec(
            num_scalar_prefetch=3,
            grid=(t // tm,),
            in_specs=[pl.BlockSpec(memory_space=pl.ANY),
                      pl.BlockSpec((tm, LANES), row),
                      pl.BlockSpec((tm, LANES), row),
                      pl.BlockSpec((tm, D_MODEL), row),
                      pl.BlockSpec((tm, PLE_DIM), row),
                      pl.BlockSpec((1, D_MODEL), fixed),
                      pl.BlockSpec((1, D_MODEL), fixed),
                      pl.BlockSpec((PLE_DIM, D_MODEL), fixed),
                      pl.BlockSpec((D_MODEL, D_MODEL), fixed)],
            out_specs=pl.BlockSpec((tm, D_MODEL), row),
            scratch_shapes=[pltpu.VMEM((_local_rows(tm), D_MODEL // LANES, LANES), F32),
                            pltpu.SemaphoreType.DMA]),
        out_shape=jax.ShapeDtypeStruct((t, D_MODEL), F32),
        compiler_params=_cparams(("arbitrary",)),
        name="moe_combine",
    )(*runs, ys, info, gates, x1, pe, g, b, wple_bf16, wpg_bf16)


def kernel(x_prompt, x_sample, cache_k, cache_v, state_shift, state_wkv, p_prompt, p_sample,
           ln_emb_g, ln_emb_b, w_in, attn_sinks, rwkv_mu, rwkv_w0, rwkv_w_lora_up, rwkv_a0,
           rwkv_a_lora_up, rwkv_g_lora_up, rwkv_k_k, rwkv_k_a, rwkv_r_k, rwkv_gn_g, rwkv_gn_b,
           w_out, ln1_g, ln1_b, w_router, b_router, w_gate_up, b_gate_up, w_down, b_down,
           ln2_g, ln2_b, w_ple, w_ple_gate):
    assert w_in.shape[0] == DEPTH == 1
    nbp, seq, _ = x_prompt.shape
    nbs, dec_seq, _ = x_sample.shape
    assert dec_seq == 1
    tp, ts = nbp * seq, nbs
    w_buf = cache_k.shape[2]
    rowv = lambda z: z.reshape(1, -1)

    w_in_b = w_in[0].astype(BF16)
    w_out_b = w_out[0].astype(BF16)
    w_ple_b = w_ple[0].astype(BF16)
    w_pg_b = w_ple_gate[0].astype(BF16)
    sinks = attn_sinks[0]
    prm = dict(mu=rowv(rwkv_mu[0]), w0=rowv(rwkv_w0[0]), wlu=rwkv_w_lora_up[0],
               a0=rowv(rwkv_a0[0]), alu=rwkv_a_lora_up[0], glu=rwkv_g_lora_up[0],
               k_k=rowv(rwkv_k_k[0]), k_a=rowv(rwkv_k_a[0]), r_k=rowv(rwkv_r_k[0]),
               gn_g=rowv(rwkv_gn_g[0]), gn_b=rowv(rwkv_gn_b[0]))
    ge, be = rowv(ln_emb_g), rowv(ln_emb_b)

    xn_p, q_p, kv_p, rp_p = _inproj(x_prompt.reshape(tp, D_MODEL), ge, be, w_in_b, 512)
    att_p = _swa_prompt(sinks, q_p, kv_p, nbp, seq)
    rw_p, wkv_p = _rwkv_prompt(rp_p, prm, nbp, seq, 256)

    xn_s, q_s, kv_s, rp_s = _inproj(x_sample.reshape(ts, D_MODEL), ge, be, w_in_b, ts)
    ck3 = cache_k[0].reshape(nbs, w_buf, KV_COLS)
    cv3 = cache_v[0].reshape(nbs, w_buf, KV_COLS)
    att_s = _swa_sample(sinks, q_s.reshape(nbs, ATT_HEADS, HEAD_DIM),
                        kv_s.reshape(nbs, 2 * KV_HEADS, HEAD_DIM), ck3, cv3)
    att_s = att_s.reshape(ts, ATT_WIDTH)
    p7 = _rwkv_sample_prep(rp_s, state_shift[0], prm)
    hv = lambda z: z.reshape(RWKV_HEADS, HEAD_DIM)
    wkv_s, rw_s = _rwkv_sample_step(p7.reshape(7, nbs, RWKV_HEADS, HEAD_DIM), state_wkv[0],
                                    hv(rwkv_r_k[0]), hv(rwkv_gn_g[0]), hv(rwkv_gn_b[0]))
    rw_s = rw_s.reshape(ts, RWKV_WIDTH)

    wr_pad = jnp.zeros((D_MODEL, LANES), F32).at[:, :N_EXPERTS].set(w_router[0])
    wr_hi = wr_pad.astype(BF16)
    wr_lo = (wr_pad - wr_hi.astype(F32)).astype(BF16)
    br_pad = jnp.full((1, LANES), -jnp.inf, F32).at[0, :N_EXPERTS].set(b_router[0])
    g1, b1 = rowv(ln1_g[0]), rowv(ln1_b[0])
    tmp = 256
    x1_p, info_p, gate_p, stat_p, cnt_p = _mix_router(att_p, rw_p, xn_p, w_out_b, g1, b1, wr_hi, wr_lo,
                                                      br_pad, jnp.zeros((1, LANES), F32), tmp)
    x1_s, info_s, gate_s, stat_s, cnt = _mix_router(att_s, rw_s, xn_s, w_out_b, g1, b1, wr_hi, wr_lo,
                                                    br_pad, cnt_p, ts)

    counts = cnt[0, :N_EXPERTS].astype(jnp.int32)
    padded = ((counts + SLOT_CHUNK + EXPERT_TILE - 1) // EXPERT_TILE) * EXPERT_TILE
    ends = jnp.cumsum(padded)
    offs = ends - padded
    n_slots = -(-((tp + ts) * TOP_K + N_EXPERTS * (EXPERT_TILE + SLOT_CHUNK)) // EXPERT_TILE) * EXPERT_TILE
    n_tiles = n_slots // EXPERT_TILE
    n_used = (ends[-1] // EXPERT_TILE).reshape(1).astype(jnp.int32)
    tile_start = jnp.arange(n_tiles, dtype=jnp.int32) * EXPERT_TILE
    tile_e = jnp.sum(ends[None, :] <= tile_start[:, None], axis=1).astype(jnp.int32)
    last_e = jnp.max(jnp.where(padded > 0, jnp.arange(N_EXPERTS), 0)).astype(jnp.int32)
    tile_e = jnp.minimum(tile_e, last_e)

    def runs(stat):
        st = stat.reshape(-1, 8, LANES)[:, :, :N_EXPERTS]
        return (st[:, 0].reshape(-1), st[:, 2].reshape(-1), (st[:, 1] + offs[None, :]).reshape(-1))

    runs_p, runs_s = runs(stat_p), runs(stat_s)

    xs = jnp.zeros((n_slots, D_MODEL // LANES, LANES), F32)
    xs = _dispatch(runs_p, x1_p, info_p, xs, tmp)
    xs = _dispatch(runs_s, x1_s, info_s, xs, ts)
    bgu = b_gate_up[0]
    bg = bgu[:, 0::2].reshape(N_EXPERTS, 1, D_FF)
    bu = bgu[:, 1::2].reshape(N_EXPERTS, 1, D_FF)
    bdn = b_down[0].reshape(N_EXPERTS, 1, D_MODEL)
    ys = _experts(tile_e, n_used, xs, w_gate_up[0], w_down[0], bg, bu, bdn)

    g2, b2 = rowv(ln2_g[0]), rowv(ln2_b[0])
    y_p = _combine(runs_p, ys, info_p, gate_p, x1_p, p_prompt[0].reshape(tp, PLE_DIM), g2, b2,
                   w_ple_b, w_pg_b, tmp)
    y_s = _combine(runs_s, ys, info_s, gate_s, x1_s, p_sample[0].reshape(ts, PLE_DIM), g2, b2,
                   w_ple_b, w_pg_b, ts)

    w_keep = min(WINDOW, seq)
    kv_p3 = kv_p.reshape(nbp, seq, 2 * KV_COLS)[:, seq - w_keep:]
    k_win_p = kv_p3[:, :, 0:KV_COLS].reshape(1, nbp, w_keep, KV_HEADS, HEAD_DIM)
    v_win_p = kv_p3[:, :, KV_COLS:].reshape(1, nbp, w_keep, KV_HEADS, HEAD_DIM)
    shift_p = rp_p.reshape(nbp, seq, RWKV_PROJ)[:, seq - 1][None]
    k_new = kv_s[:, 0:KV_COLS].reshape(nbs, 1, KV_HEADS, HEAD_DIM)
    v_new = kv_s[:, KV_COLS:].reshape(nbs, 1, KV_HEADS, HEAD_DIM)
    k_win_s = jnp.concatenate([cache_k[0], k_new], axis=1)[:, 1:][None]
    v_win_s = jnp.concatenate([cache_v[0], v_new], axis=1)[:, 1:][None]
    return (y_p.reshape(nbp, seq, D_MODEL), y_s.reshape(nbs, 1, D_MODEL),
            k_win_p, v_win_p, shift_p, wkv_p[None],
            k_win_s, v_win_s, rp_s[None], wkv_s[None])
```

```python
import functools

import jax
import jax.numpy as jnp
from jax import lax
from jax.experimental import pallas as pl
from jax.experimental.pallas import tpu as pltpu

F32 = jnp.float32
BF16 = jnp.bfloat16
HIGHEST = lax.Precision.HIGHEST

D_MODEL = 1024
HEAD_DIM = 64
ATT_HEADS = 8
KV_HEADS = 2
GROUP = ATT_HEADS // KV_HEADS
ATT_WIDTH = ATT_HEADS * HEAD_DIM
KV_COLS = KV_HEADS * HEAD_DIM
RWKV_HEADS = 8
RWKV_WIDTH = RWKV_HEADS * HEAD_DIM
LORA_W = 64
LORA_A = 64
LORA_G = 128
RWKV_PROJ = 3 * RWKV_WIDTH + LORA_W + LORA_A + LORA_G
WINDOW = 128
BLOCK = 128
PLE_DIM = 256
N_EXPERTS = 32
TOP_K = 4
D_FF = 1024
SWIGLU_LIMIT = 7.0
SWIGLU_ALPHA = 1.702
LN_EPS = 1e-5
GN_EPS = 64e-5
DEPTH = 1
DEEPNORM_ALPHA = (2 * DEPTH) ** 0.25

LANES = 128
CHUNK = 64
EXPERT_TILE = 512
ROW_DMA_UNROLL = 8
VMEM_LIMIT = 48 * 1024 * 1024
EXPERTS_VMEM_LIMIT = 58 * 1024 * 1024


def _cparams(sem):
    return pltpu.CompilerParams(dimension_semantics=sem, vmem_limit_bytes=VMEM_LIMIT)


def _bdot(a, b):
    return jnp.dot(a.astype(BF16), b.astype(BF16), preferred_element_type=F32)


def _bdot_nt(a, b):
    return lax.dot_general(a.astype(BF16), b.astype(BF16), (((1,), (1,)), ((), ())),
                           preferred_element_type=F32)


def _bdot_tn(a, b):
    return lax.dot_general(a.astype(BF16), b.astype(BF16), (((0,), (0,)), ((), ())),
                           preferred_element_type=F32)


def _split_dot(m_bf16, x, passes):
    acc = None
    rem = x
    for _ in range(passes):
        hi = rem.astype(BF16)
        part = jnp.dot(m_bf16, hi, preferred_element_type=F32)
        acc = part if acc is None else acc + part
        rem = rem - hi.astype(F32)
    return acc


def _head_sum(x, bd_bf16):
    acc = None
    rem = x
    for _ in range(3):
        hi = rem.astype(BF16)
        part = jnp.dot(hi, bd_bf16, preferred_element_type=F32)
        acc = part if acc is None else acc + part
        rem = rem - hi.astype(F32)
    return acc


def _sigmoid(x):
    return 1.0 / (1.0 + jnp.exp(-x))


def _layer_norm(x, g, b):
    mu = jnp.mean(x, axis=-1, keepdims=True)
    xc = x - mu
    var = jnp.mean(xc * xc, axis=-1, keepdims=True)
    return xc * lax.rsqrt(var + LN_EPS) * g + b


def _inproj_body(x_ref, g_ref, b_ref, w_ref, xn_ref, q_ref, kv_ref, rp_ref):
    xn = _layer_norm(x_ref[...], g_ref[...], b_ref[...])
    xn_ref[...] = xn
    xb = xn.astype(BF16)
    q_ref[...] = jnp.dot(xb, w_ref[:, 0:ATT_WIDTH], preferred_element_type=F32)
    kv_ref[...] = jnp.dot(xb, w_ref[:, ATT_WIDTH:ATT_WIDTH + 2 * KV_COLS],
                          preferred_element_type=F32)
    rp_ref[...] = jnp.dot(xb, w_ref[:, ATT_WIDTH + 2 * KV_COLS:], preferred_element_type=F32)


def _inproj(x2d, g, b, w_bf16, tm):
    t = x2d.shape[0]
    in_proj = w_bf16.shape[1]
    row = lambda i: (i, 0)
    fixed = lambda i: (0, 0)
    return pl.pallas_call(
        _inproj_body,
        grid=(t // tm,),
        in_specs=[pl.BlockSpec((tm, D_MODEL), row),
                  pl.BlockSpec((1, D_MODEL), fixed),
                  pl.BlockSpec((1, D_MODEL), fixed),
                  pl.BlockSpec((D_MODEL, in_proj), fixed)],
        out_specs=[pl.BlockSpec((tm, D_MODEL), row),
                   pl.BlockSpec((tm, ATT_WIDTH), row),
                   pl.BlockSpec((tm, 2 * KV_COLS), row),
                   pl.BlockSpec((tm, RWKV_PROJ), row)],
        out_shape=[jax.ShapeDtypeStruct((t, D_MODEL), F32),
                   jax.ShapeDtypeStruct((t, ATT_WIDTH), F32),
                   jax.ShapeDtypeStruct((t, 2 * KV_COLS), F32),
                   jax.ShapeDtypeStruct((t, RWKV_PROJ), F32)],
        compiler_params=_cparams(("parallel",)),
        name="inproj",
    )(x2d, g, b, w_bf16)


def _alibi_slope(h):
    return 2.0 ** (-8.0 * (h + 1) / ATT_HEADS)


def _swa_prompt_body(sink_ref, q_ref, kvc_ref, kvp_ref, o_ref):
    n = pl.program_id(1)
    q = (q_ref[...] * (HEAD_DIM ** -0.5)).astype(BF16)
    kvc = kvc_ref[...].astype(BF16)
    kvp = kvp_ref[...].astype(BF16)
    row = lax.broadcasted_iota(jnp.int32, (BLOCK, 2 * BLOCK), 0)
    col = lax.broadcasted_iota(jnp.int32, (BLOCK, 2 * BLOCK), 1)
    dist = row + BLOCK - col
    valid = (dist >= 0) & (dist <= WINDOW) & ((col >= BLOCK) | (n > 0))
    distf = dist.astype(F32)
    outs = []
    for g in range(KV_HEADS):
        ks = slice(g * HEAD_DIM, (g + 1) * HEAD_DIM)
        vs = slice(KV_COLS + g * HEAD_DIM, KV_COLS + (g + 1) * HEAD_DIM)
        kband = jnp.concatenate([kvp[:, ks], kvc[:, ks]], axis=0)
        vband = jnp.concatenate([kvp[:, vs], kvc[:, vs]], axis=0)
        for j in range(GROUP):
            h = g * GROUP + j
            qh = q[:, h * HEAD_DIM:(h + 1) * HEAD_DIM]
            s = lax.dot_general(qh, kband, (((1,), (1,)), ((), ())), preferred_element_type=F32)
            s = jnp.where(valid, s - _alibi_slope(h) * distf, -jnp.inf)
            sink = sink_ref[h]
            m = jnp.maximum(jnp.max(s, axis=1, keepdims=True), sink)
            e = jnp.exp(s - m)
            den = jnp.sum(e, axis=1, keepdims=True) + jnp.exp(sink - m)
            p = e / den
            outs.append(jnp.dot(p.astype(BF16), vband, preferred_element_type=F32))
    o_ref[...] = jnp.concatenate(outs, axis=1)


def _swa_prompt(sinks, q, kv, nbatch, seq):
    nb = seq // BLOCK
    cur = lambda b, n: (b * nb + n, 0)
    prv = lambda b, n: (b * nb + jnp.maximum(n - 1, 0), 0)
    return pl.pallas_call(
        _swa_prompt_body,
        grid=(nbatch, nb),
        in_specs=[pl.BlockSpec(memory_space=pltpu.SMEM),
                  pl.BlockSpec((BLOCK, ATT_WIDTH), cur),
                  pl.BlockSpec((BLOCK, 2 * KV_COLS), cur),
                  pl.BlockSpec((BLOCK, 2 * KV_COLS), prv)],
        out_specs=pl.BlockSpec((BLOCK, ATT_WIDTH), cur),
        out_shape=jax.ShapeDtypeStruct((nbatch * seq, ATT_WIDTH), F32),
        compiler_params=_cparams(("parallel", "parallel")),
        name="swa_prompt",
    )(sinks, q, kv, kv)


def _swa_sample_body(sink_ref, q_ref, kvn_ref, ck_ref, cv_ref, o_ref, *, bb, w_buf):
    hrow = lax.broadcasted_iota(jnp.int32, (ATT_HEADS, 1), 0)
    slope = jnp.zeros((ATT_HEADS, 1), F32)
    sink = jnp.zeros((ATT_HEADS, 1), F32)
    for h in range(ATT_HEADS):
        slope = jnp.where(hrow == h, _alibi_slope(h), slope)
        sink = jnp.where(hrow == h, sink_ref[h], sink)
    jcol = lax.broadcasted_iota(jnp.int32, (ATT_HEADS, w_buf), 1)
    bias = -slope * (w_buf - jcol).astype(F32)
    lower = hrow < GROUP
    for b in range(bb):
        q2 = q_ref[b] * (HEAD_DIM ** -0.5)
        kvn = kvn_ref[b]
        ck = ck_ref[b].astype(BF16)
        cv = cv_ref[b].astype(BF16)
        q2b = q2.astype(BF16)
        s0 = _bdot_nt(q2b, ck[:, 0:HEAD_DIM])
        s1 = _bdot_nt(q2b, ck[:, HEAD_DIM:2 * HEAD_DIM])
        s = jnp.where(lower, s0, s1) + bias
        knew = jnp.where(lower, kvn[0:1, :], kvn[1:2, :])
        vnew = jnp.where(lower, kvn[2:3, :], kvn[3:4, :])
        q2r = q2b.astype(F32)
        snew = jnp.sum(q2r * knew.astype(BF16).astype(F32), axis=1, keepdims=True)
        m = jnp.maximum(jnp.maximum(jnp.max(s, axis=1, keepdims=True), snew), sink)
        e = jnp.exp(s - m)
        enew = jnp.exp(snew - m)
        den = jnp.sum(e, axis=1, keepdims=True) + enew + jnp.exp(sink - m)
        p = (e / den).astype(BF16)
        pnew = (enew / den).astype(BF16).astype(F32)
        o0 = jnp.dot(p, cv[:, 0:HEAD_DIM], preferred_element_type=F32)
        o1 = jnp.dot(p, cv[:, HEAD_DIM:2 * HEAD_DIM], preferred_element_type=F32)
        o = jnp.where(lower, o0, o1) + pnew * vnew.astype(BF16).astype(F32)
        o_ref[b] = o


def _swa_sample(sinks, q3, kvn3, ck3, cv3, bb=8):
    nb, w_buf = ck3.shape[0], ck3.shape[1]
    blk = lambda i: (i, 0, 0)
    return pl.pallas_call(
        functools.partial(_swa_sample_body, bb=bb, w_buf=w_buf),
        grid=(nb // bb,),
        in_specs=[pl.BlockSpec(memory_space=pltpu.SMEM),
                  pl.BlockSpec((bb, ATT_HEADS, HEAD_DIM), blk),
                  pl.BlockSpec((bb, 2 * KV_HEADS, HEAD_DIM), blk),
                  pl.BlockSpec((bb, w_buf, KV_COLS), blk),
                  pl.BlockSpec((bb, w_buf, KV_COLS), blk)],
        out_specs=pl.BlockSpec((bb, ATT_HEADS, HEAD_DIM), blk),
        out_shape=jax.ShapeDtypeStruct((nb, ATT_HEADS, HEAD_DIM), F32),
        compiler_params=_cparams(("parallel",)),
        name="swa_sample",
    )(sinks, q3, kvn3, ck3, cv3)


def _rwkv_prep(rp, prev, mu, w0, wlu, a0, alu, glu, k_k, k_a, bd):
    xs = rp + mu * (prev - rp)
    r = xs[:, 0:RWKV_WIDTH]
    k = xs[:, RWKV_WIDTH:2 * RWKV_WIDTH]
    v = xs[:, 2 * RWKV_WIDTH:3 * RWKV_WIDTH]
    o = 3 * RWKV_WIDTH
    wd = xs[:, o:o + LORA_W]
    ad = xs[:, o + LORA_W:o + LORA_W + LORA_A]
    gd = xs[:, o + LORA_W + LORA_A:]
    z = -(w0 + _bdot(jnp.tanh(wd), wlu))
    softplus = jnp.maximum(z, 0.0) + jnp.log(1.0 + jnp.exp(-jnp.abs(z)))
    log_decay = -jnp.exp(-softplus - 0.5)
    a = _sigmoid(a0 + _bdot(ad, alu))
    g = _bdot(_sigmoid(gd), glu)
    kkr = k * k_k
    kk = kkr / jnp.maximum(jnp.sqrt(_head_sum(kkr * kkr, bd)), 1e-12)
    k2 = k * (1.0 + (a - 1.0) * k_a)
    return r, log_decay, k2, v, kk, a, g


def _rwkv_finish(y, r, k2, v, g, r_k, gn_g, gn_b, bd):
    inv = 1.0 / HEAD_DIM
    mu = _head_sum(y, bd) * inv
    yc = y - mu
    var = _head_sum(yc * yc, bd) * inv
    yn = yc * lax.rsqrt(var + GN_EPS) * gn_g + gn_b
    bonus = _head_sum(r * k2 * r_k, bd) * v
    return (yn + bonus) * g


def _rwkv_prompt_body(rp_ref, mu_ref, w0_ref, wlu_ref, a0_ref, alu_ref, glu_ref, kk_ref, ka_ref,
                      rk_ref, gng_ref, gnb_ref, bd_ref, ltri_ref,
                      out_ref, sfin_ref, prev_scr, st_scr, y_scr, *, tt):
    c = pl.program_id(1)

    @pl.when(c == 0)
    def _():
        prev_scr[...] = jnp.zeros_like(prev_scr)
        st_scr[...] = jnp.zeros_like(st_scr)

    rp = rp_ref[...]
    rolled = pltpu.roll(rp, 1, 0)
    rowi = lax.broadcasted_iota(jnp.int32, (tt, 1), 0)
    prev = jnp.where(rowi == 0, prev_scr[...], rolled)
    prev_scr[...] = rp[tt - 1:tt, :]
    bd = bd_ref[...]
    r, ld, k2, v, kk, a, g = _rwkv_prep(rp, prev, mu_ref[...], w0_ref[...], wlu_ref[...],
                                        a0_ref[...], alu_ref[...], glu_ref[...], kk_ref[...],
                                        ka_ref[...], bd)
    cum = _split_dot(ltri_ref[...], ld, 3)
    ecum = jnp.exp(cum)
    einv = jnp.exp(-cum)
    at = -kk * jnp.exp(cum - ld)
    bh = kk * a * einv
    kh = k2 * einv
    rt = r * ecum

    ii = lax.broadcasted_iota(jnp.int32, (CHUNK, CHUNK), 0)
    jj = lax.broadcasted_iota(jnp.int32, (CHUNK, CHUNK), 1)
    strict = ii > jj
    incl = ii >= jj
    eye = (ii == jj).astype(F32)

    nsub = tt // CHUNK
    units = [(s, h) for s in range(nsub) for h in range(RWKV_HEADS)]
    rows = lambda s: slice(s * CHUNK, (s + 1) * CHUNK)
    cols = lambda h: slice(h * HEAD_DIM, (h + 1) * HEAD_DIM)
    cut = lambda z, u: z[rows(u[0]), cols(u[1])]
    kka = kk * a
    p_last, kh_end, bh_end = [], [], []
    for s in range(nsub):
        c_last = cum[(s + 1) * CHUNK - 1:(s + 1) * CHUNK, :]
        p_last.append(jnp.exp(c_last))
        tail = jnp.exp(c_last - cum[rows(s)])
        kh_end.append(k2[rows(s)] * tail)
        bh_end.append(kka[rows(s)] * tail)
    amat = {u: _bdot_nt(jnp.concatenate([cut(at, u), cut(rt, u)], axis=0),
                        jnp.concatenate([cut(bh, u), cut(kh, u)], axis=0)) for u in units}
    nmat = {u: jnp.where(strict, amat[u][0:CHUNK, 0:CHUNK], 0.0) for u in units}
    a_ak = {u: jnp.where(strict, amat[u][0:CHUNK, CHUNK:], 0.0) for u in units}
    a_rb = {u: jnp.where(incl, amat[u][CHUNK:, 0:CHUNK], 0.0) for u in units}
    a_rk = {u: jnp.where(incl, amat[u][CHUNK:, CHUNK:], 0.0) for u in units}
    tmat = {u: eye + nmat[u] for u in units}
    npow = nmat
    for _ in range(5):
        npow = {u: _bdot(npow[u], npow[u]) for u in units}
        tmat = {u: tmat[u] + _bdot(npow[u], tmat[u]) for u in units}
    av = {u: _bdot(a_ak[u], cut(v, u)) for u in units}
    tw = {u: _bdot(tmat[u], jnp.concatenate([cut(at, u), av[u]], axis=1)) for u in units}
    rb = {u: _bdot(a_rb[u], tw[u]) for u in units}
    rprime = {u: cut(rt, u) + rb[u][:, 0:HEAD_DIM] for u in units}
    y0 = {u: _bdot(a_rk[u], cut(v, u)) + rb[u][:, HEAD_DIM:] for u in units}
    hmat = {u: _bdot_tn(jnp.concatenate([cut(v, u), tw[u][:, HEAD_DIM:]], axis=0),
                        jnp.concatenate([kh_end[u[0]][:, cols(u[1])], bh_end[u[0]][:, cols(u[1])]],
                                        axis=0)) for u in units}
    g0 = {u: _bdot_tn(tw[u][:, 0:HEAD_DIM], bh_end[u[0]][:, cols(u[1])]) for u in units}

    states = [st_scr[h] for h in range(RWKV_HEADS)]
    for s in range(nsub):
        for h in range(RWKV_HEADS):
            u = (s, h)
            st = states[h]
            y_scr[rows(s), cols(h)] = _bdot_nt(rprime[u], st) + y0[u]
            states[h] = st * p_last[s][:, cols(h)] + _bdot(st, g0[u]) + hmat[u]
    for h in range(RWKV_HEADS):
        st_scr[h] = states[h]
        sfin_ref[0, h] = states[h]
    out_ref[...] = _rwkv_finish(y_scr[...], r, k2, v, g, rk_ref[...], gng_ref[...], gnb_ref[...], bd)


def _chunk_tril(tt):
    i = jnp.arange(tt)
    same = (i[:, None] // CHUNK) == (i[None, :] // CHUNK)
    return (same & (i[:, None] >= i[None, :])).astype(BF16)


def _head_blockdiag():
    i = jnp.arange(RWKV_WIDTH) // HEAD_DIM
    return (i[:, None] == i[None, :]).astype(BF16)


def _rwkv_prompt(rp, prm, nbatch, seq, tt):
    nc = seq // tt
    fixed = lambda b, c: (0, 0)
    row = lambda b, c: (b * nc + c, 0)
    vec = lambda n: pl.BlockSpec((1, n), fixed)
    return pl.pallas_call(
        functools.partial(_rwkv_prompt_body, tt=tt),
        grid=(nbatch, nc),
        in_specs=[pl.BlockSpec((tt, RWKV_PROJ), row),
                  vec(RWKV_PROJ), vec(RWKV_WIDTH),
                  pl.BlockSpec((LORA_W, RWKV_WIDTH), fixed), vec(RWKV_WIDTH),
                  pl.BlockSpec((LORA_A, RWKV_WIDTH), fixed),
                  pl.BlockSpec((LORA_G, RWKV_WIDTH), fixed),
                  vec(RWKV_WIDTH), vec(RWKV_WIDTH), vec(RWKV_WIDTH), vec(RWKV_WIDTH),
                  vec(RWKV_WIDTH),
                  pl.BlockSpec((RWKV_WIDTH, RWKV_WIDTH), fixed),
                  pl.BlockSpec((tt, tt), fixed)],
        out_specs=[pl.BlockSpec((tt, RWKV_WIDTH), row),
                   pl.BlockSpec((1, RWKV_HEADS, HEAD_DIM, HEAD_DIM), lambda b, c: (b, 0, 0, 0))],
        out_shape=[jax.ShapeDtypeStruct((nbatch * seq, RWKV_WIDTH), F32),
                   jax.ShapeDtypeStruct((nbatch, RWKV_HEADS, HEAD_DIM, HEAD_DIM), F32)],
        scratch_shapes=[pltpu.VMEM((1, RWKV_PROJ), F32),
                        pltpu.VMEM((RWKV_HEADS, HEAD_DIM, HEAD_DIM), F32),
                        pltpu.VMEM((tt, RWKV_WIDTH), F32)],
        compiler_params=_cparams(("parallel", "arbitrary")),
        name="rwkv_prompt",
    )(rp, prm["mu"], prm["w0"], prm["wlu"], prm["a0"], prm["alu"], prm["glu"], prm["k_k"],
      prm["k_a"], prm["r_k"], prm["gn_g"], prm["gn_b"], _head_blockdiag(), _chunk_tril(tt))


def _rwkv_sample_prep_body(rp_ref, prev_ref, mu_ref, w0_ref, wlu_ref, a0_ref, alu_ref, glu_ref,
                           kk_ref, ka_ref, bd_ref, o_ref):
    r, ld, k2, v, kk, a, g = _rwkv_prep(rp_ref[...], prev_ref[...], mu_ref[...], w0_ref[...],
                                        wlu_ref[...], a0_ref[...], alu_ref[...], glu_ref[...],
                                        kk_ref[...], ka_ref[...], bd_ref[...])
    for i, z in enumerate((r, jnp.exp(ld), k2, v, kk, a, g)):
        o_ref[i] = z


def _rwkv_sample_prep(rp, prev, prm):
    nb = rp.shape[0]
    return pl.pallas_call(
        _rwkv_sample_prep_body,
        out_shape=jax.ShapeDtypeStruct((7, nb, RWKV_WIDTH), F32),
        compiler_params=pltpu.CompilerParams(vmem_limit_bytes=VMEM_LIMIT),
        name="rwkv_sample_prep",
    )(rp, prev, prm["mu"], prm["w0"], prm["wlu"], prm["a0"], prm["alu"], prm["glu"], prm["k_k"],
      prm["k_a"], _head_blockdiag())


def _rwkv_sample_step_body(p_ref, s_ref, rk_ref, gng_ref, gnb_ref, so_ref, o_ref, *, bb):
    ii = lax.broadcasted_iota(jnp.int32, (HEAD_DIM, HEAD_DIM), 0)
    jj = lax.broadcasted_iota(jnp.int32, (HEAD_DIM, HEAD_DIM), 1)
    eye = ii == jj
    for b in range(bb):
        for h in range(RWKV_HEADS):
            row = lambda i: p_ref[i, b, h:h + 1, :]
            r_r, w_r, k_r, v_r, kk_r, a_r = (row(i) for i in range(6))
            st = s_ref[b, h]
            sa = jnp.sum(st * kk_r, axis=1, keepdims=True)
            v_col = jnp.sum(jnp.where(eye, v_r, 0.0), axis=1, keepdims=True)
            st = st * w_r - sa * (kk_r * a_r) + v_col * k_r
            so_ref[b, h] = st
            y_col = jnp.sum(st * r_r, axis=1, keepdims=True)
            o_ref[b, h:h + 1, :] = jnp.sum(jnp.where(eye, y_col, 0.0), axis=0, keepdims=True)
    y = o_ref[...]
    r, k2, v, g = p_ref[0], p_ref[2], p_ref[3], p_ref[6]
    mu = jnp.mean(y, axis=-1, keepdims=True)
    yc = y - mu
    var = jnp.mean(yc * yc, axis=-1, keepdims=True)
    yn = yc * lax.rsqrt(var + GN_EPS) * gng_ref[...] + gnb_ref[...]
    bonus = jnp.sum(r * k2 * rk_ref[...], axis=-1, keepdims=True) * v
    o_ref[...] = (yn + bonus) * g


def _rwkv_sample_step(p4, state, r_k, gn_g, gn_b, bb=8):
    nb = state.shape[0]
    hv = pl.BlockSpec((RWKV_HEADS, HEAD_DIM), lambda i: (0, 0))
    return pl.pallas_call(
        functools.partial(_rwkv_sample_step_body, bb=bb),
        grid=(nb // bb,),
        in_specs=[pl.BlockSpec((7, bb, RWKV_HEADS, HEAD_DIM), lambda i: (0, i, 0, 0)),
                  pl.BlockSpec((bb, RWKV_HEADS, HEAD_DIM, HEAD_DIM), lambda i: (i, 0, 0, 0)),
                  hv, hv, hv],
        out_specs=[pl.BlockSpec((bb, RWKV_HEADS, HEAD_DIM, HEAD_DIM), lambda i: (i, 0, 0, 0)),
                   pl.BlockSpec((bb, RWKV_HEADS, HEAD_DIM), lambda i: (i, 0, 0))],
        out_shape=[jax.ShapeDtypeStruct(state.shape, F32),
                   jax.ShapeDtypeStruct((nb, RWKV_HEADS, HEAD_DIM), F32)],
        compiler_params=_cparams(("parallel",)),
        name="rwkv_sample_step",
    )(p4, state, r_k, gn_g, gn_b)


def _mix_router_body(att_ref, rw_ref, xn_ref, wo_ref, g_ref, b_ref, wrh_ref, wrl_ref, br_ref,
                     base_ref, x1_ref, info_ref, gate_ref, cnt_ref, base_scr, *, tm):
    i = pl.program_id(0)

    @pl.when(i == 0)
    def _():
        base_scr[...] = base_ref[...]

    mixed = (jnp.dot(att_ref[...].astype(BF16), wo_ref[0:ATT_WIDTH, :], preferred_element_type=F32)
             + jnp.dot(rw_ref[...].astype(BF16), wo_ref[ATT_WIDTH:, :], preferred_element_type=F32))
    x1 = _layer_norm(DEEPNORM_ALPHA * xn_ref[...] + mixed, g_ref[...], b_ref[...])
    x1_ref[...] = x1

    x1h = x1.astype(BF16)
    x1l = (x1 - x1h.astype(F32)).astype(BF16)
    logits = (jnp.dot(x1h, wrh_ref[...], preferred_element_type=F32)
              + jnp.dot(x1l, wrh_ref[...], preferred_element_type=F32)
              + jnp.dot(x1h, wrl_ref[...], preferred_element_type=F32)) + br_ref[...]
    lane = lax.broadcasted_iota(jnp.int32, (tm, LANES), 1)
    lanef = lane.astype(F32)
    vals, idxs, hots = [], [], []
    cur = logits
    for _ in range(TOP_K):
        m = jnp.max(cur, axis=1, keepdims=True)
        idx = jnp.min(jnp.where(cur == m, lanef, float(LANES)), axis=1, keepdims=True)
        hot = lanef == idx
        cur = jnp.where(hot, -jnp.inf, cur)
        vals.append(m)
        idxs.append(idx)
        hots.append(hot)
    es = [jnp.exp(vk - vals[0]) for vk in vals]
    den = es[0] + es[1] + es[2] + es[3]
    multi = jnp.zeros((tm, LANES), F32)
    for hot in hots:
        multi = multi + hot.astype(F32)
    ti = lax.broadcasted_iota(jnp.int32, (tm, tm), 0)
    tj = lax.broadcasted_iota(jnp.int32, (tm, tm), 1)
    before = jnp.dot((ti > tj).astype(BF16), multi.astype(BF16), preferred_element_type=F32)
    before = before + base_scr[...]
    info = jnp.zeros((tm, LANES), jnp.int32)
    gates = jnp.zeros((tm, LANES), F32)
    for k in range(TOP_K):
        rank = jnp.sum(jnp.where(hots[k], before, 0.0), axis=1, keepdims=True)
        info = jnp.where(lane == k, idxs[k].astype(jnp.int32), info)
        info = jnp.where(lane == TOP_K + k, rank.astype(jnp.int32), info)
        gates = jnp.where(lane == k, es[k] / den, gates)
    info_ref[...] = info
    gate_ref[...] = gates
    base_scr[...] = base_scr[...] + jnp.sum(multi, axis=0, keepdims=True)
    cnt_ref[...] = base_scr[...]


def _mix_router(att, rw, xn, wo_bf16, g, b, wr_hi, wr_lo, br_pad, base, tm):
    t = att.shape[0]
    row = lambda i: (i, 0)
    fixed = lambda i: (0, 0)
    return pl.pallas_call(
        functools.partial(_mix_router_body, tm=tm),
        grid=(t // tm,),
        in_specs=[pl.BlockSpec((tm, ATT_WIDTH), row),
                  pl.BlockSpec((tm, RWKV_WIDTH), row),
                  pl.BlockSpec((tm, D_MODEL), row),
                  pl.BlockSpec((D_MODEL, D_MODEL), fixed),
                  pl.BlockSpec((1, D_MODEL), fixed),
                  pl.BlockSpec((1, D_MODEL), fixed),
                  pl.BlockSpec((D_MODEL, LANES), fixed),
                  pl.BlockSpec((D_MODEL, LANES), fixed),
                  pl.BlockSpec((1, LANES), fixed),
                  pl.BlockSpec((1, LANES), fixed)],
        out_specs=[pl.BlockSpec((tm, D_MODEL), row),
                   pl.BlockSpec((tm, LANES), row),
                   pl.BlockSpec((tm, LANES), row),
                   pl.BlockSpec((1, LANES), fixed)],
        out_shape=[jax.ShapeDtypeStruct((t, D_MODEL), F32),
                   jax.ShapeDtypeStruct((t, LANES), jnp.int32),
                   jax.ShapeDtypeStruct((t, LANES), F32),
                   jax.ShapeDtypeStruct((1, LANES), F32)],
        scratch_shapes=[pltpu.VMEM((1, LANES), F32)],
        compiler_params=_cparams(("arbitrary",)),
        name="mix_router",
    )(att, rw, xn, wo_bf16, g, b, wr_hi, wr_lo, br_pad, base)


def _row_copies(pos_ref, base, tm, make_copy):
    def start(r, carry):
        for k in range(TOP_K):
            make_copy(r, k, pos_ref[(base + r) * TOP_K + k]).start()
        return carry

    lax.fori_loop(0, tm, start, 0, unroll=ROW_DMA_UNROLL)


def _dispatch_body(pos_ref, x_ref, xs_in_ref, xs_ref, sem, *, tm):
    del xs_in_ref
    base = pl.program_id(0) * tm

    def make_copy(r, k, p):
        return pltpu.make_async_copy(x_ref.at[pl.ds(r, 1)], xs_ref.at[pl.ds(p, 1)], sem)

    _row_copies(pos_ref, base, tm, make_copy)
    for _ in range(TOP_K):
        pltpu.make_async_copy(x_ref, xs_ref.at[pl.ds(0, tm)], sem).wait()


def _dispatch(pos_flat, x1, xs, tm):
    t = x1.shape[0]
    return pl.pallas_call(
        functools.partial(_dispatch_body, tm=tm),
        grid_spec=pltpu.PrefetchScalarGridSpec(
            num_scalar_prefetch=1,
            grid=(t // tm,),
            in_specs=[pl.BlockSpec((tm, D_MODEL), lambda i, pos: (i, 0)),
                      pl.BlockSpec(memory_space=pl.ANY)],
            out_specs=pl.BlockSpec(memory_space=pl.ANY),
            scratch_shapes=[pltpu.SemaphoreType.DMA]),
        out_shape=jax.ShapeDtypeStruct(xs.shape, xs.dtype),
        input_output_aliases={2: 0},
        compiler_params=_cparams(("arbitrary",)),
        name="moe_dispatch",
    )(pos_flat, x1, xs)


def _experts_body(te_ref, nu_ref, xs_ref, wgu_ref, wd_ref, bg_ref, bu_ref, bd_ref, sel_ref, ys_ref,
                  wg_scr, wu_scr, wd_scr):
    i = pl.program_id(0)
    new_expert = (i == 0) | (te_ref[i] != te_ref[jnp.maximum(i - 1, 0)])

    @pl.when(new_expert)
    def _():
        wd_scr[...] = wd_ref[...].astype(BF16)
        for m in range(D_FF // LANES):
            pair = wgu_ref[:, 2 * m * LANES:2 * (m + 1) * LANES].astype(BF16)
            split = jnp.dot(pair, sel_ref[...], preferred_element_type=F32)
            wg_scr[:, m * LANES:(m + 1) * LANES] = split[:, 0:LANES].astype(BF16)
            wu_scr[:, m * LANES:(m + 1) * LANES] = split[:, LANES:].astype(BF16)

    @pl.when(i < nu_ref[0])
    def _():
        x = xs_ref[...].astype(BF16)
        gate = jnp.dot(x, wg_scr[...], preferred_element_type=F32) + bg_ref[...]
        up = jnp.dot(x, wu_scr[...], preferred_element_type=F32) + bu_ref[...]
        gate = jnp.minimum(gate, SWIGLU_LIMIT)
        up = jnp.clip(up, -SWIGLU_LIMIT, SWIGLU_LIMIT)
        act = (up + 1.0) * gate * _sigmoid(SWIGLU_ALPHA * gate)
        ys_ref[...] = jnp.dot(act.astype(BF16), wd_scr[...], preferred_element_type=F32) + bd_ref[...]

    @pl.when(i >= nu_ref[0])
    def _():
        ys_ref[...] = jnp.zeros_like(ys_ref)


def _gate_up_selector():
    i = jnp.arange(2 * LANES)
    src = jnp.where(i < LANES, 2 * i, 2 * (i - LANES) + 1)
    return (i[:, None] == src[None, :]).astype(BF16)


def _experts(tile_e, n_used, xs, wgu, wd, bg, bu, bd):
    ns = xs.shape[0]
    tm = EXPERT_TILE
    wspec = lambda a, b: pl.BlockSpec((None, a, b), lambda i, te, nu: (te[i], 0, 0))
    return pl.pallas_call(
        _experts_body,
        grid_spec=pltpu.PrefetchScalarGridSpec(
            num_scalar_prefetch=2,
            grid=(ns // tm,),
            in_specs=[pl.BlockSpec((tm, D_MODEL), lambda i, te, nu: (i, 0)),
                      wspec(D_MODEL, 2 * D_FF), wspec(D_FF, D_MODEL),
                      wspec(1, D_FF), wspec(1, D_FF), wspec(1, D_MODEL),
                      pl.BlockSpec((2 * LANES, 2 * LANES), lambda i, te, nu: (0, 0))],
            out_specs=pl.BlockSpec((tm, D_MODEL), lambda i, te, nu: (i, 0)),
            scratch_shapes=[pltpu.VMEM((D_MODEL, D_FF), BF16), pltpu.VMEM((D_MODEL, D_FF), BF16),
                            pltpu.VMEM((D_FF, D_MODEL), BF16)]),
        out_shape=jax.ShapeDtypeStruct((ns, D_MODEL), F32),
        compiler_params=pltpu.CompilerParams(dimension_semantics=("arbitrary",),
                                             vmem_limit_bytes=EXPERTS_VMEM_LIMIT),
        name="moe_experts",
    )(tile_e, n_used, xs, wgu, wd, bg, bu, bd, _gate_up_selector())


def _combine_body(pos_ref, ys_ref, gate_ref, x1_ref, pe_ref, g_ref, b_ref, wple_ref, wpg_ref,
                  o_ref, buf, sem, *, tm):
    i = pl.program_id(0)
    cur = i % 2

    def gather(tile, slot):
        def make_copy(r, k, p):
            return pltpu.make_async_copy(ys_ref.at[pl.ds(p, 1)], buf.at[slot, k, pl.ds(r, 1)],
                                         sem.at[slot])
        _row_copies(pos_ref, tile * tm, tm, make_copy)

    @pl.when(i == 0)
    def _():
        gather(0, 0)

    @pl.when(i + 1 < pl.num_programs(0))
    def _():
        gather(i + 1, 1 - cur)

    for k in range(TOP_K):
        pltpu.make_async_copy(ys_ref.at[pl.ds(0, tm)], buf.at[cur, k], sem.at[cur]).wait()
    gates = gate_ref[...]
    ffn = gates[:, 0:1] * buf[cur, 0]
    for k in range(1, TOP_K):
        ffn = ffn + gates[:, k:k + 1] * buf[cur, k]
    x2 = _layer_norm(DEEPNORM_ALPHA * x1_ref[...] + ffn, g_ref[...], b_ref[...])
    gate = _sigmoid(jnp.dot(x2.astype(BF16), wpg_ref[...], preferred_element_type=F32))
    emb = jnp.dot(pe_ref[...].astype(BF16), wple_ref[...], preferred_element_type=F32)
    o_ref[...] = x2 + gate * emb


def _combine(pos_flat, ys, gates, x1, pe, g, b, wple_bf16, wpg_bf16, tm):
    t = x1.shape[0]
    row = lambda i, *_: (i, 0)
    fixed = lambda i, *_: (0, 0)
    return pl.pallas_call(
        functools.partial(_combine_body, tm=tm),
        grid_spec=pltpu.PrefetchScalarGridSpec(
            num_scalar_prefetch=1,
            grid=(t // tm,),
            in_specs=[pl.BlockSpec(memory_space=pl.ANY),
                      pl.BlockSpec((tm, LANES), row),
                      pl.BlockSpec((tm, D_MODEL), row),
                      pl.BlockSpec((tm, PLE_DIM), row),
                      pl.BlockSpec((1, D_MODEL), fixed),
                      pl.BlockSpec((1, D_MODEL), fixed),
                      pl.BlockSpec((PLE_DIM, D_MODEL), fixed),
                      pl.BlockSpec((D_MODEL, D_MODEL), fixed)],
            out_specs=pl.BlockSpec((tm, D_MODEL), row),
            scratch_shapes=[pltpu.VMEM((2, TOP_K, tm, D_MODEL), F32),
                            pltpu.SemaphoreType.DMA((2,))]),
        out_shape=jax.ShapeDtypeStruct((t, D_MODEL), F32),
        compiler_params=_cparams(("arbitrary",)),
        name="moe_combine",
    )(pos_flat, ys, gates, x1, pe, g, b, wple_bf16, wpg_bf16)


def kernel(x_prompt, x_sample, cache_k, cache_v, state_shift, state_wkv, p_prompt, p_sample,
           ln_emb_g, ln_emb_b, w_in, attn_sinks, rwkv_mu, rwkv_w0, rwkv_w_lora_up, rwkv_a0,
           rwkv_a_lora_up, rwkv_g_lora_up, rwkv_k_k, rwkv_k_a, rwkv_r_k, rwkv_gn_g, rwkv_gn_b,
           w_out, ln1_g, ln1_b, w_router, b_router, w_gate_up, b_gate_up, w_down, b_down,
           ln2_g, ln2_b, w_ple, w_ple_gate):
    assert w_in.shape[0] == DEPTH == 1
    nbp, seq, _ = x_prompt.shape
    nbs, dec_seq, _ = x_sample.shape
    assert dec_seq == 1
    tp, ts = nbp * seq, nbs
    w_buf = cache_k.shape[2]
    rowv = lambda z: z.reshape(1, -1)

    w_in_b = w_in[0].astype(BF16)
    w_out_b = w_out[0].astype(BF16)
    w_ple_b = w_ple[0].astype(BF16)
    w_pg_b = w_ple_gate[0].astype(BF16)
    sinks = attn_sinks[0]
    prm = dict(mu=rowv(rwkv_mu[0]), w0=rowv(rwkv_w0[0]), wlu=rwkv_w_lora_up[0],
               a0=rowv(rwkv_a0[0]), alu=rwkv_a_lora_up[0], glu=rwkv_g_lora_up[0],
               k_k=rowv(rwkv_k_k[0]), k_a=rowv(rwkv_k_a[0]), r_k=rowv(rwkv_r_k[0]),
               gn_g=rowv(rwkv_gn_g[0]), gn_b=rowv(rwkv_gn_b[0]))
    ge, be = rowv(ln_emb_g), rowv(ln_emb_b)

    xn_p, q_p, kv_p, rp_p = _inproj(x_prompt.reshape(tp, D_MODEL), ge, be, w_in_b, 512)
    att_p = _swa_prompt(sinks, q_p, kv_p, nbp, seq)
    rw_p, wkv_p = _rwkv_prompt(rp_p, prm, nbp, seq, 256)

    xn_s, q_s, kv_s, rp_s = _inproj(x_sample.reshape(ts, D_MODEL), ge, be, w_in_b, ts)
    ck3 = cache_k[0].reshape(nbs, w_buf, KV_COLS)
    cv3 = cache_v[0].reshape(nbs, w_buf, KV_COLS)
    att_s = _swa_sample(sinks, q_s.reshape(nbs, ATT_HEADS, HEAD_DIM),
                        kv_s.reshape(nbs, 2 * KV_HEADS, HEAD_DIM), ck3, cv3)
    att_s = att_s.reshape(ts, ATT_WIDTH)
    p7 = _rwkv_sample_prep(rp_s, state_shift[0], prm)
    hv = lambda z: z.reshape(RWKV_HEADS, HEAD_DIM)
    wkv_s, rw_s = _rwkv_sample_step(p7.reshape(7, nbs, RWKV_HEADS, HEAD_DIM), state_wkv[0],
                                    hv(rwkv_r_k[0]), hv(rwkv_gn_g[0]), hv(rwkv_gn_b[0]))
    rw_s = rw_s.reshape(ts, RWKV_WIDTH)

    wr_pad = jnp.zeros((D_MODEL, LANES), F32).at[:, :N_EXPERTS].set(w_router[0])
    wr_hi = wr_pad.astype(BF16)
    wr_lo = (wr_pad - wr_hi.astype(F32)).astype(BF16)
    br_pad = jnp.full((1, LANES), -jnp.inf, F32).at[0, :N_EXPERTS].set(b_router[0])
    g1, b1 = rowv(ln1_g[0]), rowv(ln1_b[0])
    tmp = 256
    x1_p, info_p, gate_p, cnt_p = _mix_router(att_p, rw_p, xn_p, w_out_b, g1, b1, wr_hi, wr_lo,
                                              br_pad, jnp.zeros((1, LANES), F32), tmp)
    x1_s, info_s, gate_s, cnt = _mix_router(att_s, rw_s, xn_s, w_out_b, g1, b1, wr_hi, wr_lo,
                                            br_pad, cnt_p, ts)

    counts = cnt[0, :N_EXPERTS].astype(jnp.int32)
    padded = ((counts + EXPERT_TILE - 1) // EXPERT_TILE) * EXPERT_TILE
    ends = jnp.cumsum(padded)
    offs = ends - padded
    n_slots = (tp + ts) * TOP_K + N_EXPERTS * EXPERT_TILE
    n_tiles = n_slots // EXPERT_TILE
    n_used = (ends[-1] // EXPERT_TILE).reshape(1).astype(jnp.int32)
    tile_start = jnp.arange(n_tiles, dtype=jnp.int32) * EXPERT_TILE
    tile_e = jnp.sum(ends[None, :] <= tile_start[:, None], axis=1).astype(jnp.int32)
    last_e = jnp.max(jnp.where(padded > 0, jnp.arange(N_EXPERTS), 0)).astype(jnp.int32)
    tile_e = jnp.minimum(tile_e, last_e)

    def slots(info):
        return (offs[info[:, 0:TOP_K]] + info[:, TOP_K:2 * TOP_K]).reshape(-1).astype(jnp.int32)

    pos_p, pos_s = slots(info_p), slots(info_s)

    xs = jnp.zeros((n_slots, D_MODEL), F32)
    xs = _dispatch(pos_p, x1_p, xs, tmp)
    xs = _dispatch(pos_s, x1_s, xs, ts)
    bgu = b_gate_up[0]
    bg = bgu[:, 0::2].reshape(N_EXPERTS, 1, D_FF)
    bu = bgu[:, 1::2].reshape(N_EXPERTS, 1, D_FF)
    bdn = b_down[0].reshape(N_EXPERTS, 1, D_MODEL)
    ys = _experts(tile_e, n_used, xs, w_gate_up[0], w_down[0], bg, bu, bdn)

    g2, b2 = rowv(ln2_g[0]), rowv(ln2_b[0])
    y_p = _combine(pos_p, ys, gate_p, x1_p, p_prompt[0].reshape(tp, PLE_DIM), g2, b2,
                   w_ple_b, w_pg_b, tmp)
    y_s = _combine(pos_s, ys, gate_s, x1_s, p_sample[0].reshape(ts, PLE_DIM), g2, b2,
                   w_ple_b, w_pg_b, ts)

    w_keep = min(WINDOW, seq)
    kv_p3 = kv_p.reshape(nbp, seq, 2 * KV_COLS)[:, seq - w_keep:]
    k_win_p = kv_p3[:, :, 0:KV_COLS].reshape(1, nbp, w_keep, KV_HEADS, HEAD_DIM)
    v_win_p = kv_p3[:, :, KV_COLS:].reshape(1, nbp, w_keep, KV_HEADS, HEAD_DIM)
    shift_p = rp_p.reshape(nbp, seq, RWKV_PROJ)[:, seq - 1][None]
    k_new = kv_s[:, 0:KV_COLS].reshape(nbs, 1, KV_HEADS, HEAD_DIM)
    v_new = kv_s[:, KV_COLS:].reshape(nbs, 1, KV_HEADS, HEAD_DIM)
    k_win_s = jnp.concatenate([cache_k[0], k_new], axis=1)[:, 1:][None]
    v_win_s = jnp.concatenate([cache_v[0], v_new], axis=1)[:, 1:][None]
    return (y_p.reshape(nbp, seq, D_MODEL), y_s.reshape(nbs, 1, D_MODEL),
            k_win_p, v_win_p, shift_p, wkv_p[None],
            k_win_s, v_win_s, rp_s[None], wkv_s[None])
```

```python
import functools

import jax
import jax.numpy as jnp
from jax import lax
from jax.experimental import pallas as pl
from jax.experimental.pallas import tpu as pltpu

F32 = jnp.float32
BF16 = jnp.bfloat16
HIGHEST = lax.Precision.HIGHEST

D_MODEL = 1024
HEAD_DIM = 64
ATT_HEADS = 8
KV_HEADS = 2
GROUP = ATT_HEADS // KV_HEADS
ATT_WIDTH = ATT_HEADS * HEAD_DIM
KV_COLS = KV_HEADS * HEAD_DIM
RWKV_HEADS = 8
RWKV_WIDTH = RWKV_HEADS * HEAD_DIM
LORA_W = 64
LORA_A = 64
LORA_G = 128
RWKV_PROJ = 3 * RWKV_WIDTH + LORA_W + LORA_A + LORA_G
WINDOW = 128
BLOCK = 128
PLE_DIM = 256
N_EXPERTS = 32
TOP_K = 4
D_FF = 1024
SWIGLU_LIMIT = 7.0
SWIGLU_ALPHA = 1.702
LN_EPS = 1e-5
GN_EPS = 64e-5
DEPTH = 1
DEEPNORM_ALPHA = (2 * DEPTH) ** 0.25

LANES = 128
CHUNK = 64
EXPERT_TILE = 512
ROW_DMA_UNROLL = 8
VMEM_LIMIT = 48 * 1024 * 1024
EXPERTS_VMEM_LIMIT = 58 * 1024 * 1024


def _cparams(sem):
    return pltpu.CompilerParams(dimension_semantics=sem, vmem_limit_bytes=VMEM_LIMIT)


def _bdot(a, b):
    return jnp.dot(a.astype(BF16), b.astype(BF16), preferred_element_type=F32)


def _bdot_nt(a, b):
    return lax.dot_general(a.astype(BF16), b.astype(BF16), (((1,), (1,)), ((), ())),
                           preferred_element_type=F32)


def _bdot_tn(a, b):
    return lax.dot_general(a.astype(BF16), b.astype(BF16), (((0,), (0,)), ((), ())),
                           preferred_element_type=F32)


def _split_dot(m_bf16, x, passes):
    acc = None
    rem = x
    for _ in range(passes):
        hi = rem.astype(BF16)
        part = jnp.dot(m_bf16, hi, preferred_element_type=F32)
        acc = part if acc is None else acc + part
        rem = rem - hi.astype(F32)
    return acc


def _head_sum(x, bd_bf16):
    acc = None
    rem = x
    for _ in range(3):
        hi = rem.astype(BF16)
        part = jnp.dot(hi, bd_bf16, preferred_element_type=F32)
        acc = part if acc is None else acc + part
        rem = rem - hi.astype(F32)
    return acc


def _sigmoid(x):
    return 1.0 / (1.0 + jnp.exp(-x))


def _layer_norm(x, g, b):
    mu = jnp.mean(x, axis=-1, keepdims=True)
    xc = x - mu
    var = jnp.mean(xc * xc, axis=-1, keepdims=True)
    return xc * lax.rsqrt(var + LN_EPS) * g + b


def _inproj_body(x_ref, g_ref, b_ref, w_ref, xn_ref, q_ref, kv_ref, rp_ref):
    xn = _layer_norm(x_ref[...], g_ref[...], b_ref[...])
    xn_ref[...] = xn
    xb = xn.astype(BF16)
    q_ref[...] = jnp.dot(xb, w_ref[:, 0:ATT_WIDTH], preferred_element_type=F32)
    kv_ref[...] = jnp.dot(xb, w_ref[:, ATT_WIDTH:ATT_WIDTH + 2 * KV_COLS],
                          preferred_element_type=F32)
    rp_ref[...] = jnp.dot(xb, w_ref[:, ATT_WIDTH + 2 * KV_COLS:], preferred_element_type=F32)


def _inproj(x2d, g, b, w_bf16, tm):
    t = x2d.shape[0]
    in_proj = w_bf16.shape[1]
    row = lambda i: (i, 0)
    fixed = lambda i: (0, 0)
    return pl.pallas_call(
        _inproj_body,
        grid=(t // tm,),
        in_specs=[pl.BlockSpec((tm, D_MODEL), row),
                  pl.BlockSpec((1, D_MODEL), fixed),
                  pl.BlockSpec((1, D_MODEL), fixed),
                  pl.BlockSpec((D_MODEL, in_proj), fixed)],
        out_specs=[pl.BlockSpec((tm, D_MODEL), row),
                   pl.BlockSpec((tm, ATT_WIDTH), row),
                   pl.BlockSpec((tm, 2 * KV_COLS), row),
                   pl.BlockSpec((tm, RWKV_PROJ), row)],
        out_shape=[jax.ShapeDtypeStruct((t, D_MODEL), F32),
                   jax.ShapeDtypeStruct((t, ATT_WIDTH), F32),
                   jax.ShapeDtypeStruct((t, 2 * KV_COLS), F32),
                   jax.ShapeDtypeStruct((t, RWKV_PROJ), F32)],
        compiler_params=_cparams(("parallel",)),
        name="inproj",
    )(x2d, g, b, w_bf16)


def _alibi_slope(h):
    return 2.0 ** (-8.0 * (h + 1) / ATT_HEADS)


def _swa_prompt_body(sink_ref, q_ref, kvc_ref, kvp_ref, o_ref):
    n = pl.program_id(1)
    q = (q_ref[...] * (HEAD_DIM ** -0.5)).astype(BF16)
    kvc = kvc_ref[...].astype(BF16)
    kvp = kvp_ref[...].astype(BF16)
    row = lax.broadcasted_iota(jnp.int32, (BLOCK, 2 * BLOCK), 0)
    col = lax.broadcasted_iota(jnp.int32, (BLOCK, 2 * BLOCK), 1)
    dist = row + BLOCK - col
    valid = (dist >= 0) & (dist <= WINDOW) & ((col >= BLOCK) | (n > 0))
    distf = dist.astype(F32)
    kbands, vbands = [], []
    for g in range(KV_HEADS):
        ks = slice(g * HEAD_DIM, (g + 1) * HEAD_DIM)
        vs = slice(KV_COLS + g * HEAD_DIM, KV_COLS + (g + 1) * HEAD_DIM)
        kbands.append(jnp.concatenate([kvp[:, ks], kvc[:, ks]], axis=0))
        vbands.append(jnp.concatenate([kvp[:, vs], kvc[:, vs]], axis=0))
    scores = [lax.dot_general(q[:, h * HEAD_DIM:(h + 1) * HEAD_DIM], kbands[h // GROUP],
                              (((1,), (1,)), ((), ())), preferred_element_type=F32)
              for h in range(ATT_HEADS)]
    probs = []
    for h in range(ATT_HEADS):
        s = jnp.where(valid, scores[h] - _alibi_slope(h) * distf, -jnp.inf)
        sink = sink_ref[h]
        m = jnp.maximum(jnp.max(s, axis=1, keepdims=True), sink)
        e = jnp.exp(s - m)
        den = jnp.sum(e, axis=1, keepdims=True) + jnp.exp(sink - m)
        probs.append((e / den).astype(BF16))
    outs = [jnp.dot(probs[h], vbands[h // GROUP], preferred_element_type=F32)
            for h in range(ATT_HEADS)]
    o_ref[...] = jnp.concatenate(outs, axis=1)


def _swa_prompt(sinks, q, kv, nbatch, seq):
    nb = seq // BLOCK
    cur = lambda b, n: (b * nb + n, 0)
    prv = lambda b, n: (b * nb + jnp.maximum(n - 1, 0), 0)
    return pl.pallas_call(
        _swa_prompt_body,
        grid=(nbatch, nb),
        in_specs=[pl.BlockSpec(memory_space=pltpu.SMEM),
                  pl.BlockSpec((BLOCK, ATT_WIDTH), cur),
                  pl.BlockSpec((BLOCK, 2 * KV_COLS), cur),
                  pl.BlockSpec((BLOCK, 2 * KV_COLS), prv)],
        out_specs=pl.BlockSpec((BLOCK, ATT_WIDTH), cur),
        out_shape=jax.ShapeDtypeStruct((nbatch * seq, ATT_WIDTH), F32),
        compiler_params=_cparams(("parallel", "parallel")),
        name="swa_prompt",
    )(sinks, q, kv, kv)


def _swa_sample_body(sink_ref, q_ref, kvn_ref, ck_ref, cv_ref, o_ref, *, bb, w_buf):
    hrow = lax.broadcasted_iota(jnp.int32, (ATT_HEADS, 1), 0)
    slope = jnp.zeros((ATT_HEADS, 1), F32)
    sink = jnp.zeros((ATT_HEADS, 1), F32)
    for h in range(ATT_HEADS):
        slope = jnp.where(hrow == h, _alibi_slope(h), slope)
        sink = jnp.where(hrow == h, sink_ref[h], sink)
    jcol = lax.broadcasted_iota(jnp.int32, (ATT_HEADS, w_buf), 1)
    bias = -slope * (w_buf - jcol).astype(F32)
    lower = hrow < GROUP
    q2b = [(q_ref[b] * (HEAD_DIM ** -0.5)).astype(BF16) for b in range(bb)]
    s01 = []
    for b in range(bb):
        ck = ck_ref[b].astype(BF16)
        s01.append((_bdot_nt(q2b[b], ck[:, 0:HEAD_DIM]),
                    _bdot_nt(q2b[b], ck[:, HEAD_DIM:2 * HEAD_DIM])))
    ps = []
    for b in range(bb):
        kvn = kvn_ref[b]
        s = jnp.where(lower, s01[b][0], s01[b][1]) + bias
        knew = jnp.where(lower, kvn[0:1, :], kvn[1:2, :])
        vnew = jnp.where(lower, kvn[2:3, :], kvn[3:4, :])
        snew = jnp.sum(q2b[b].astype(F32) * knew.astype(BF16).astype(F32), axis=1, keepdims=True)
        m = jnp.maximum(jnp.maximum(jnp.max(s, axis=1, keepdims=True), snew), sink)
        e = jnp.exp(s - m)
        enew = jnp.exp(snew - m)
        den = jnp.sum(e, axis=1, keepdims=True) + enew + jnp.exp(sink - m)
        pnew = (enew / den).astype(BF16).astype(F32)
        ps.append(((e / den).astype(BF16), pnew * vnew.astype(BF16).astype(F32)))
    for b in range(bb):
        cv = cv_ref[b].astype(BF16)
        o0 = jnp.dot(ps[b][0], cv[:, 0:HEAD_DIM], preferred_element_type=F32)
        o1 = jnp.dot(ps[b][0], cv[:, HEAD_DIM:2 * HEAD_DIM], preferred_element_type=F32)
        o_ref[b] = jnp.where(lower, o0, o1) + ps[b][1]


def _swa_sample(sinks, q3, kvn3, ck3, cv3, bb=8):
    nb, w_buf = ck3.shape[0], ck3.shape[1]
    blk = lambda i: (i, 0, 0)
    return pl.pallas_call(
        functools.partial(_swa_sample_body, bb=bb, w_buf=w_buf),
        grid=(nb // bb,),
        in_specs=[pl.BlockSpec(memory_space=pltpu.SMEM),
                  pl.BlockSpec((bb, ATT_HEADS, HEAD_DIM), blk),
                  pl.BlockSpec((bb, 2 * KV_HEADS, HEAD_DIM), blk),
                  pl.BlockSpec((bb, w_buf, KV_COLS), blk),
                  pl.BlockSpec((bb, w_buf, KV_COLS), blk)],
        out_specs=pl.BlockSpec((bb, ATT_HEADS, HEAD_DIM), blk),
        out_shape=jax.ShapeDtypeStruct((nb, ATT_HEADS, HEAD_DIM), F32),
        compiler_params=_cparams(("parallel",)),
        name="swa_sample",
    )(sinks, q3, kvn3, ck3, cv3)


def _rwkv_prep(rp, prev, mu, w0, wlu, a0, alu, glu, k_k, k_a, bd):
    xs = rp + mu * (prev - rp)
    r = xs[:, 0:RWKV_WIDTH]
    k = xs[:, RWKV_WIDTH:2 * RWKV_WIDTH]
    v = xs[:, 2 * RWKV_WIDTH:3 * RWKV_WIDTH]
    o = 3 * RWKV_WIDTH
    wd = xs[:, o:o + LORA_W]
    ad = xs[:, o + LORA_W:o + LORA_W + LORA_A]
    gd = xs[:, o + LORA_W + LORA_A:]
    z = -(w0 + _bdot(jnp.tanh(wd), wlu))
    softplus = jnp.maximum(z, 0.0) + jnp.log(1.0 + jnp.exp(-jnp.abs(z)))
    log_decay = -jnp.exp(-softplus - 0.5)
    a = _sigmoid(a0 + _bdot(ad, alu))
    g = _bdot(_sigmoid(gd), glu)
    kkr = k * k_k
    kk = kkr / jnp.maximum(jnp.sqrt(_head_sum(kkr * kkr, bd)), 1e-12)
    k2 = k * (1.0 + (a - 1.0) * k_a)
    return r, log_decay, k2, v, kk, a, g


def _rwkv_finish(y, r, k2, v, g, r_k, gn_g, gn_b, bd):
    inv = 1.0 / HEAD_DIM
    mu = _head_sum(y, bd) * inv
    yc = y - mu
    var = _head_sum(yc * yc, bd) * inv
    yn = yc * lax.rsqrt(var + GN_EPS) * gn_g + gn_b
    bonus = _head_sum(r * k2 * r_k, bd) * v
    return (yn + bonus) * g


def _rwkv_prompt_body(rp_ref, mu_ref, w0_ref, wlu_ref, a0_ref, alu_ref, glu_ref, kk_ref, ka_ref,
                      rk_ref, gng_ref, gnb_ref, bd_ref, ltri_ref,
                      out_ref, sfin_ref, prev_scr, st_scr, y_scr, *, tt):
    c = pl.program_id(1)

    @pl.when(c == 0)
    def _():
        prev_scr[...] = jnp.zeros_like(prev_scr)
        st_scr[...] = jnp.zeros_like(st_scr)

    rp = rp_ref[...]
    rolled = pltpu.roll(rp, 1, 0)
    rowi = lax.broadcasted_iota(jnp.int32, (tt, 1), 0)
    prev = jnp.where(rowi == 0, prev_scr[...], rolled)
    prev_scr[...] = rp[tt - 1:tt, :]
    bd = bd_ref[...]
    r, ld, k2, v, kk, a, g = _rwkv_prep(rp, prev, mu_ref[...], w0_ref[...], wlu_ref[...],
                                        a0_ref[...], alu_ref[...], glu_ref[...], kk_ref[...],
                                        ka_ref[...], bd)
    cum = _split_dot(ltri_ref[...], ld, 3)
    ecum = jnp.exp(cum)
    einv = jnp.exp(-cum)
    at = -kk * jnp.exp(cum - ld)
    bh = kk * a * einv
    kh = k2 * einv
    rt = r * ecum

    ii = lax.broadcasted_iota(jnp.int32, (CHUNK, CHUNK), 0)
    jj = lax.broadcasted_iota(jnp.int32, (CHUNK, CHUNK), 1)
    strict = ii > jj
    incl = ii >= jj
    eye = (ii == jj).astype(F32)

    nsub = tt // CHUNK
    units = [(s, h) for s in range(nsub) for h in range(RWKV_HEADS)]
    rows = lambda s: slice(s * CHUNK, (s + 1) * CHUNK)
    cols = lambda h: slice(h * HEAD_DIM, (h + 1) * HEAD_DIM)
    cut = lambda z, u: z[rows(u[0]), cols(u[1])]
    kka = kk * a
    p_last, kh_end, bh_end = [], [], []
    for s in range(nsub):
        c_last = cum[(s + 1) * CHUNK - 1:(s + 1) * CHUNK, :]
        p_last.append(jnp.exp(c_last))
        tail = jnp.exp(c_last - cum[rows(s)])
        kh_end.append(k2[rows(s)] * tail)
        bh_end.append(kka[rows(s)] * tail)
    amat = {u: _bdot_nt(jnp.concatenate([cut(at, u), cut(rt, u)], axis=0),
                        jnp.concatenate([cut(bh, u), cut(kh, u)], axis=0)) for u in units}
    nmat = {u: jnp.where(strict, amat[u][0:CHUNK, 0:CHUNK], 0.0) for u in units}
    a_ak = {u: jnp.where(strict, amat[u][0:CHUNK, CHUNK:], 0.0) for u in units}
    a_rb = {u: jnp.where(incl, amat[u][CHUNK:, 0:CHUNK], 0.0) for u in units}
    a_rk = {u: jnp.where(incl, amat[u][CHUNK:, CHUNK:], 0.0) for u in units}
    tmat = {u: eye + nmat[u] for u in units}
    npow = nmat
    for _ in range(5):
        npow = {u: _bdot(npow[u], npow[u]) for u in units}
        tmat = {u: tmat[u] + _bdot(npow[u], tmat[u]) for u in units}
    av = {u: _bdot(a_ak[u], cut(v, u)) for u in units}
    tw = {u: _bdot(tmat[u], jnp.concatenate([cut(at, u), av[u]], axis=1)) for u in units}
    rb = {u: _bdot(a_rb[u], tw[u]) for u in units}
    rprime = {u: cut(rt, u) + rb[u][:, 0:HEAD_DIM] for u in units}
    y0 = {u: _bdot(a_rk[u], cut(v, u)) + rb[u][:, HEAD_DIM:] for u in units}
    hmat = {u: _bdot_tn(jnp.concatenate([cut(v, u), tw[u][:, HEAD_DIM:]], axis=0),
                        jnp.concatenate([kh_end[u[0]][:, cols(u[1])], bh_end[u[0]][:, cols(u[1])]],
                                        axis=0)) for u in units}
    g0 = {u: _bdot_tn(tw[u][:, 0:HEAD_DIM], bh_end[u[0]][:, cols(u[1])]) for u in units}

    states = [st_scr[h] for h in range(RWKV_HEADS)]
    for s in range(nsub):
        for h in range(RWKV_HEADS):
            u = (s, h)
            st = states[h]
            y_scr[rows(s), cols(h)] = _bdot_nt(rprime[u], st) + y0[u]
            states[h] = st * p_last[s][:, cols(h)] + _bdot(st, g0[u]) + hmat[u]
    for h in range(RWKV_HEADS):
        st_scr[h] = states[h]
        sfin_ref[0, h] = states[h]
    out_ref[...] = _rwkv_finish(y_scr[...], r, k2, v, g, rk_ref[...], gng_ref[...], gnb_ref[...], bd)


def _chunk_tril(tt):
    i = jnp.arange(tt)
    same = (i[:, None] // CHUNK) == (i[None, :] // CHUNK)
    return (same & (i[:, None] >= i[None, :])).astype(BF16)


def _head_blockdiag():
    i = jnp.arange(RWKV_WIDTH) // HEAD_DIM
    return (i[:, None] == i[None, :]).astype(BF16)


def _rwkv_prompt(rp, prm, nbatch, seq, tt):
    nc = seq // tt
    fixed = lambda b, c: (0, 0)
    row = lambda b, c: (b * nc + c, 0)
    vec = lambda n: pl.BlockSpec((1, n), fixed)
    return pl.pallas_call(
        functools.partial(_rwkv_prompt_body, tt=tt),
        grid=(nbatch, nc),
        in_specs=[pl.BlockSpec((tt, RWKV_PROJ), row),
                  vec(RWKV_PROJ), vec(RWKV_WIDTH),
                  pl.BlockSpec((LORA_W, RWKV_WIDTH), fixed), vec(RWKV_WIDTH),
                  pl.BlockSpec((LORA_A, RWKV_WIDTH), fixed),
                  pl.BlockSpec((LORA_G, RWKV_WIDTH), fixed),
                  vec(RWKV_WIDTH), vec(RWKV_WIDTH), vec(RWKV_WIDTH), vec(RWKV_WIDTH),
                  vec(RWKV_WIDTH),
                  pl.BlockSpec((RWKV_WIDTH, RWKV_WIDTH), fixed),
                  pl.BlockSpec((tt, tt), fixed)],
        out_specs=[pl.BlockSpec((tt, RWKV_WIDTH), row),
                   pl.BlockSpec((1, RWKV_HEADS, HEAD_DIM, HEAD_DIM), lambda b, c: (b, 0, 0, 0))],
        out_shape=[jax.ShapeDtypeStruct((nbatch * seq, RWKV_WIDTH), F32),
                   jax.ShapeDtypeStruct((nbatch, RWKV_HEADS, HEAD_DIM, HEAD_DIM), F32)],
        scratch_shapes=[pltpu.VMEM((1, RWKV_PROJ), F32),
                        pltpu.VMEM((RWKV_HEADS, HEAD_DIM, HEAD_DIM), F32),
                        pltpu.VMEM((tt, RWKV_WIDTH), F32)],
        compiler_params=_cparams(("parallel", "arbitrary")),
        name="rwkv_prompt",
    )(rp, prm["mu"], prm["w0"], prm["wlu"], prm["a0"], prm["alu"], prm["glu"], prm["k_k"],
      prm["k_a"], prm["r_k"], prm["gn_g"], prm["gn_b"], _head_blockdiag(), _chunk_tril(tt))


def _rwkv_sample_prep_body(rp_ref, prev_ref, mu_ref, w0_ref, wlu_ref, a0_ref, alu_ref, glu_ref,
                           kk_ref, ka_ref, bd_ref, o_ref):
    r, ld, k2, v, kk, a, g = _rwkv_prep(rp_ref[...], prev_ref[...], mu_ref[...], w0_ref[...],
                                        wlu_ref[...], a0_ref[...], alu_ref[...], glu_ref[...],
                                        kk_ref[...], ka_ref[...], bd_ref[...])
    for i, z in enumerate((r, jnp.exp(ld), k2, v, kk, a, g)):
        o_ref[i] = z


def _rwkv_sample_prep(rp, prev, prm):
    nb = rp.shape[0]
    return pl.pallas_call(
        _rwkv_sample_prep_body,
        out_shape=jax.ShapeDtypeStruct((7, nb, RWKV_WIDTH), F32),
        compiler_params=pltpu.CompilerParams(vmem_limit_bytes=VMEM_LIMIT),
        name="rwkv_sample_prep",
    )(rp, prev, prm["mu"], prm["w0"], prm["wlu"], prm["a0"], prm["alu"], prm["glu"], prm["k_k"],
      prm["k_a"], _head_blockdiag())


def _rwkv_sample_step_body(p_ref, s_ref, rk_ref, gng_ref, gnb_ref, so_ref, o_ref, *, bb):
    ii = lax.broadcasted_iota(jnp.int32, (HEAD_DIM, HEAD_DIM), 0)
    jj = lax.broadcasted_iota(jnp.int32, (HEAD_DIM, HEAD_DIM), 1)
    eye = ii == jj
    for b in range(bb):
        for h in range(RWKV_HEADS):
            row = lambda i: p_ref[i, b, h:h + 1, :]
            r_r, w_r, k_r, v_r, kk_r, a_r = (row(i) for i in range(6))
            st = s_ref[b, h]
            sa = jnp.sum(st * kk_r, axis=1, keepdims=True)
            v_col = jnp.sum(jnp.where(eye, v_r, 0.0), axis=1, keepdims=True)
            st = st * w_r - sa * (kk_r * a_r) + v_col * k_r
            so_ref[b, h] = st
            y_col = jnp.sum(st * r_r, axis=1, keepdims=True)
            o_ref[b, h:h + 1, :] = jnp.sum(jnp.where(eye, y_col, 0.0), axis=0, keepdims=True)
    y = o_ref[...]
    r, k2, v, g = p_ref[0], p_ref[2], p_ref[3], p_ref[6]
    mu = jnp.mean(y, axis=-1, keepdims=True)
    yc = y - mu
    var = jnp.mean(yc * yc, axis=-1, keepdims=True)
    yn = yc * lax.rsqrt(var + GN_EPS) * gng_ref[...] + gnb_ref[...]
    bonus = jnp.sum(r * k2 * rk_ref[...], axis=-1, keepdims=True) * v
    o_ref[...] = (yn + bonus) * g


def _rwkv_sample_step(p4, state, r_k, gn_g, gn_b, bb=8):
    nb = state.shape[0]
    hv = pl.BlockSpec((RWKV_HEADS, HEAD_DIM), lambda i: (0, 0))
    return pl.pallas_call(
        functools.partial(_rwkv_sample_step_body, bb=bb),
        grid=(nb // bb,),
        in_specs=[pl.BlockSpec((7, bb, RWKV_HEADS, HEAD_DIM), lambda i: (0, i, 0, 0)),
                  pl.BlockSpec((bb, RWKV_HEADS, HEAD_DIM, HEAD_DIM), lambda i: (i, 0, 0, 0)),
                  hv, hv, hv],
        out_specs=[pl.BlockSpec((bb, RWKV_HEADS, HEAD_DIM, HEAD_DIM), lambda i: (i, 0, 0, 0)),
                   pl.BlockSpec((bb, RWKV_HEADS, HEAD_DIM), lambda i: (i, 0, 0))],
        out_shape=[jax.ShapeDtypeStruct(state.shape, F32),
                   jax.ShapeDtypeStruct((nb, RWKV_HEADS, HEAD_DIM), F32)],
        compiler_params=_cparams(("parallel",)),
        name="rwkv_sample_step",
    )(p4, state, r_k, gn_g, gn_b)


def _mix_router_body(att_ref, rw_ref, xn_ref, wo_ref, g_ref, b_ref, wrh_ref, wrl_ref, br_ref,
                     base_ref, x1_ref, info_ref, gate_ref, cnt_ref, base_scr, *, tm):
    i = pl.program_id(0)

    @pl.when(i == 0)
    def _():
        base_scr[...] = base_ref[...]

    mixed = (jnp.dot(att_ref[...].astype(BF16), wo_ref[0:ATT_WIDTH, :], preferred_element_type=F32)
             + jnp.dot(rw_ref[...].astype(BF16), wo_ref[ATT_WIDTH:, :], preferred_element_type=F32))
    x1 = _layer_norm(DEEPNORM_ALPHA * xn_ref[...] + mixed, g_ref[...], b_ref[...])
    x1_ref[...] = x1

    x1h = x1.astype(BF16)
    x1l = (x1 - x1h.astype(F32)).astype(BF16)
    logits = (jnp.dot(x1h, wrh_ref[...], preferred_element_type=F32)
              + jnp.dot(x1l, wrh_ref[...], preferred_element_type=F32)
              + jnp.dot(x1h, wrl_ref[...], preferred_element_type=F32)) + br_ref[...]
    lane = lax.broadcasted_iota(jnp.int32, (tm, LANES), 1)
    lanef = lane.astype(F32)
    vals, idxs, hots = [], [], []
    cur = logits
    for _ in range(TOP_K):
        m = jnp.max(cur, axis=1, keepdims=True)
        idx = jnp.min(jnp.where(cur == m, lanef, float(LANES)), axis=1, keepdims=True)
        hot = lanef == idx
        cur = jnp.where(hot, -jnp.inf, cur)
        vals.append(m)
        idxs.append(idx)
        hots.append(hot)
    es = [jnp.exp(vk - vals[0]) for vk in vals]
    den = es[0] + es[1] + es[2] + es[3]
    multi = jnp.zeros((tm, LANES), F32)
    for hot in hots:
        multi = multi + hot.astype(F32)
    ti = lax.broadcasted_iota(jnp.int32, (tm, tm), 0)
    tj = lax.broadcasted_iota(jnp.int32, (tm, tm), 1)
    before = jnp.dot((ti > tj).astype(BF16), multi.astype(BF16), preferred_element_type=F32)
    before = before + base_scr[...]
    info = jnp.zeros((tm, LANES), jnp.int32)
    gates = jnp.zeros((tm, LANES), F32)
    for k in range(TOP_K):
        rank = jnp.sum(jnp.where(hots[k], before, 0.0), axis=1, keepdims=True)
        info = jnp.where(lane == k, idxs[k].astype(jnp.int32), info)
        info = jnp.where(lane == TOP_K + k, rank.astype(jnp.int32), info)
        gates = jnp.where(lane == k, es[k] / den, gates)
    info_ref[...] = info
    gate_ref[...] = gates
    base_scr[...] = base_scr[...] + jnp.sum(multi, axis=0, keepdims=True)
    cnt_ref[...] = base_scr[...]


def _mix_router(att, rw, xn, wo_bf16, g, b, wr_hi, wr_lo, br_pad, base, tm):
    t = att.shape[0]
    row = lambda i: (i, 0)
    fixed = lambda i: (0, 0)
    return pl.pallas_call(
        functools.partial(_mix_router_body, tm=tm),
        grid=(t // tm,),
        in_specs=[pl.BlockSpec((tm, ATT_WIDTH), row),
                  pl.BlockSpec((tm, RWKV_WIDTH), row),
                  pl.BlockSpec((tm, D_MODEL), row),
                  pl.BlockSpec((D_MODEL, D_MODEL), fixed),
                  pl.BlockSpec((1, D_MODEL), fixed),
                  pl.BlockSpec((1, D_MODEL), fixed),
                  pl.BlockSpec((D_MODEL, LANES), fixed),
                  pl.BlockSpec((D_MODEL, LANES), fixed),
                  pl.BlockSpec((1, LANES), fixed),
                  pl.BlockSpec((1, LANES), fixed)],
        out_specs=[pl.BlockSpec((tm, D_MODEL), row),
                   pl.BlockSpec((tm, LANES), row),
                   pl.BlockSpec((tm, LANES), row),
                   pl.BlockSpec((1, LANES), fixed)],
        out_shape=[jax.ShapeDtypeStruct((t, D_MODEL), F32),
                   jax.ShapeDtypeStruct((t, LANES), jnp.int32),
                   jax.ShapeDtypeStruct((t, LANES), F32),
                   jax.ShapeDtypeStruct((1, LANES), F32)],
        scratch_shapes=[pltpu.VMEM((1, LANES), F32)],
        compiler_params=_cparams(("arbitrary",)),
        name="mix_router",
    )(att, rw, xn, wo_bf16, g, b, wr_hi, wr_lo, br_pad, base)


def _row_copies(pos_ref, base, tm, make_copy):
    def start(r, carry):
        for k in range(TOP_K):
            make_copy(r, k, pos_ref[(base + r) * TOP_K + k]).start()
        return carry

    lax.fori_loop(0, tm, start, 0, unroll=ROW_DMA_UNROLL)


def _dispatch_rows(pos_ref, base, x_ref, xs_ref, sem, rows):
    def make_copy(r, k, p):
        return pltpu.make_async_copy(x_ref.at[pl.ds(r, 1)], xs_ref.at[pl.ds(p, 1)], sem)

    _row_copies(pos_ref, base, rows, make_copy)
    for _ in range(TOP_K):
        pltpu.make_async_copy(x_ref, xs_ref.at[pl.ds(0, rows)], sem).wait()


def _dispatch_body(pos_a_ref, pos_b_ref, fill_ref, xa_ref, xb_ref, xs_ref, zero_scr, sem, fill_sem,
                   *, tm, n_tiles):
    i = pl.program_id(0)
    last = pl.num_programs(0) - 1

    @pl.when(i == 0)
    def _():
        zero_scr[...] = jnp.zeros_like(zero_scr)

        def tile_fill(start):
            return pltpu.make_async_copy(
                zero_scr, xs_ref.at[pl.ds(pl.multiple_of(start, EXPERT_TILE), EXPERT_TILE)], fill_sem)

        fills = [tile_fill(fill_ref[e]) for e in range(N_EXPERTS)]
        for cp in fills:
            cp.start()
        first_unused = fill_ref[N_EXPERTS]
        lax.fori_loop(first_unused, n_tiles,
                      lambda t, c: (tile_fill(t * EXPERT_TILE).start(), c)[1], 0)
        for cp in fills:
            cp.wait()
        lax.fori_loop(first_unused, n_tiles,
                      lambda t, c: (tile_fill(t * EXPERT_TILE).wait(), c)[1], 0)

    @pl.when(i < last)
    def _():
        _dispatch_rows(pos_a_ref, i * tm, xa_ref, xs_ref, sem, tm)

    @pl.when(i == last)
    def _():
        _dispatch_rows(pos_b_ref, 0, xb_ref, xs_ref, sem, xb_ref.shape[0])


def _dispatch(pos_a, pos_b, fill_start, x_a, x_b, n_slots, tm):
    nta = x_a.shape[0] // tm
    return pl.pallas_call(
        functools.partial(_dispatch_body, tm=tm, n_tiles=n_slots // EXPERT_TILE),
        grid_spec=pltpu.PrefetchScalarGridSpec(
            num_scalar_prefetch=3,
            grid=(nta + 1,),
            in_specs=[pl.BlockSpec((tm, D_MODEL), lambda i, *_: (jnp.minimum(i, nta - 1), 0)),
                      pl.BlockSpec(x_b.shape, lambda i, *_: (0, 0))],
            out_specs=pl.BlockSpec(memory_space=pl.ANY),
            scratch_shapes=[pltpu.VMEM((EXPERT_TILE, D_MODEL), F32),
                            pltpu.SemaphoreType.DMA, pltpu.SemaphoreType.DMA]),
        out_shape=jax.ShapeDtypeStruct((n_slots, D_MODEL), F32),
        compiler_params=_cparams(("arbitrary",)),
        name="moe_dispatch",
    )(pos_a, pos_b, fill_start, x_a, x_b)


def _experts_body(te_ref, nu_ref, xs_ref, wgu_ref, wd_ref, bg_ref, bu_ref, bd_ref, sel_ref, ys_ref,
                  wg_scr, wu_scr, wd_scr):
    i = pl.program_id(0)
    new_expert = (i == 0) | (te_ref[i] != te_ref[jnp.maximum(i - 1, 0)])

    @pl.when(new_expert)
    def _():
        wd_scr[...] = wd_ref[...].astype(BF16)
        for m in range(D_FF // LANES):
            pair = wgu_ref[:, 2 * m * LANES:2 * (m + 1) * LANES].astype(BF16)
            split = jnp.dot(pair, sel_ref[...], preferred_element_type=F32)
            wg_scr[:, m * LANES:(m + 1) * LANES] = split[:, 0:LANES].astype(BF16)
            wu_scr[:, m * LANES:(m + 1) * LANES] = split[:, LANES:].astype(BF16)

    @pl.when(i < nu_ref[0])
    def _():
        x = xs_ref[...].astype(BF16)
        gate = jnp.dot(x, wg_scr[...], preferred_element_type=F32) + bg_ref[...]
        up = jnp.dot(x, wu_scr[...], preferred_element_type=F32) + bu_ref[...]
        gate = jnp.minimum(gate, SWIGLU_LIMIT)
        up = jnp.clip(up, -SWIGLU_LIMIT, SWIGLU_LIMIT)
        act = (up + 1.0) * gate * _sigmoid(SWIGLU_ALPHA * gate)
        ys_ref[...] = jnp.dot(act.astype(BF16), wd_scr[...], preferred_element_type=F32) + bd_ref[...]

    @pl.when(i >= nu_ref[0])
    def _():
        ys_ref[...] = jnp.zeros_like(ys_ref)


def _gate_up_selector():
    i = jnp.arange(2 * LANES)
    src = jnp.where(i < LANES, 2 * i, 2 * (i - LANES) + 1)
    return (i[:, None] == src[None, :]).astype(BF16)


def _experts(tile_e, n_used, xs, wgu, wd, bg, bu, bd):
    ns = xs.shape[0]
    tm = EXPERT_TILE
    wspec = lambda a, b: pl.BlockSpec((None, a, b), lambda i, te, nu: (te[i], 0, 0))
    return pl.pallas_call(
        _experts_body,
        grid_spec=pltpu.PrefetchScalarGridSpec(
            num_scalar_prefetch=2,
            grid=(ns // tm,),
            in_specs=[pl.BlockSpec((tm, D_MODEL), lambda i, te, nu: (jnp.minimum(i, nu[0] - 1), 0)),
                      wspec(D_MODEL, 2 * D_FF), wspec(D_FF, D_MODEL),
                      wspec(1, D_FF), wspec(1, D_FF), wspec(1, D_MODEL),
                      pl.BlockSpec((2 * LANES, 2 * LANES), lambda i, te, nu: (0, 0))],
            out_specs=pl.BlockSpec((tm, D_MODEL), lambda i, te, nu: (i, 0)),
            scratch_shapes=[pltpu.VMEM((D_MODEL, D_FF), BF16), pltpu.VMEM((D_MODEL, D_FF), BF16),
                            pltpu.VMEM((D_FF, D_MODEL), BF16)]),
        out_shape=jax.ShapeDtypeStruct((ns, D_MODEL), F32),
        compiler_params=pltpu.CompilerParams(dimension_semantics=("arbitrary",),
                                             vmem_limit_bytes=EXPERTS_VMEM_LIMIT),
        name="moe_experts",
    )(tile_e, n_used, xs, wgu, wd, bg, bu, bd, _gate_up_selector())


def _combine_body(pos_ref, ys_ref, gate_ref, x1_ref, pe_ref, g_ref, b_ref, wple_ref, wpg_ref,
                  o_ref, buf, sem, *, tm):
    i = pl.program_id(0)
    cur = i % 2

    def gather(tile, slot):
        def make_copy(r, k, p):
            return pltpu.make_async_copy(ys_ref.at[pl.ds(p, 1)], buf.at[slot, k, pl.ds(r, 1)],
                                         sem.at[slot])
        _row_copies(pos_ref, tile * tm, tm, make_copy)

    @pl.when(i == 0)
    def _():
        gather(0, 0)

    @pl.when(i + 1 < pl.num_programs(0))
    def _():
        gather(i + 1, 1 - cur)

    for k in range(TOP_K):
        pltpu.make_async_copy(ys_ref.at[pl.ds(0, tm)], buf.at[cur, k], sem.at[cur]).wait()
    gates = gate_ref[...]
    ffn = gates[:, 0:1] * buf[cur, 0]
    for k in range(1, TOP_K):
        ffn = ffn + gates[:, k:k + 1] * buf[cur, k]
    x2 = _layer_norm(DEEPNORM_ALPHA * x1_ref[...] + ffn, g_ref[...], b_ref[...])
    gate = _sigmoid(jnp.dot(x2.astype(BF16), wpg_ref[...], preferred_element_type=F32))
    emb = jnp.dot(pe_ref[...].astype(BF16), wple_ref[...], preferred_element_type=F32)
    o_ref[...] = x2 + gate * emb


def _combine(pos_flat, ys, gates, x1, pe, g, b, wple_bf16, wpg_bf16, tm):
    t = x1.shape[0]
    row = lambda i, *_: (i, 0)
    fixed = lambda i, *_: (0, 0)
    return pl.pallas_call(
        functools.partial(_combine_body, tm=tm),
        grid_spec=pltpu.PrefetchScalarGridSpec(
            num_scalar_prefetch=1,
            grid=(t // tm,),
            in_specs=[pl.BlockSpec(memory_space=pl.ANY),
                      pl.BlockSpec((tm, LANES), row),
                      pl.BlockSpec((tm, D_MODEL), row),
                      pl.BlockSpec((tm, PLE_DIM), row),
                      pl.BlockSpec((1, D_MODEL), fixed),
                      pl.BlockSpec((1, D_MODEL), fixed),
                      pl.BlockSpec((PLE_DIM, D_MODEL), fixed),
                      pl.BlockSpec((D_MODEL, D_MODEL), fixed)],
            out_specs=pl.BlockSpec((tm, D_MODEL), row),
            scratch_shapes=[pltpu.VMEM((2, TOP_K, tm, D_MODEL), F32),
                            pltpu.SemaphoreType.DMA((2,))]),
        out_shape=jax.ShapeDtypeStruct((t, D_MODEL), F32),
        compiler_params=_cparams(("arbitrary",)),
        name="moe_combine",
    )(pos_flat, ys, gates, x1, pe, g, b, wple_bf16, wpg_bf16)


def kernel(x_prompt, x_sample, cache_k, cache_v, state_shift, state_wkv, p_prompt, p_sample,
           ln_emb_g, ln_emb_b, w_in, attn_sinks, rwkv_mu, rwkv_w0, rwkv_w_lora_up, rwkv_a0,
           rwkv_a_lora_up, rwkv_g_lora_up, rwkv_k_k, rwkv_k_a, rwkv_r_k, rwkv_gn_g, rwkv_gn_b,
           w_out, ln1_g, ln1_b, w_router, b_router, w_gate_up, b_gate_up, w_down, b_down,
           ln2_g, ln2_b, w_ple, w_ple_gate):
    assert w_in.shape[0] == DEPTH == 1
    nbp, seq, _ = x_prompt.shape
    nbs, dec_seq, _ = x_sample.shape
    assert dec_seq == 1
    tp, ts = nbp * seq, nbs
    w_buf = cache_k.shape[2]
    rowv = lambda z: z.reshape(1, -1)

    w_in_b = w_in[0].astype(BF16)
    w_out_b = w_out[0].astype(BF16)
    w_ple_b = w_ple[0].astype(BF16)
    w_pg_b = w_ple_gate[0].astype(BF16)
    sinks = attn_sinks[0]
    prm = dict(mu=rowv(rwkv_mu[0]), w0=rowv(rwkv_w0[0]), wlu=rwkv_w_lora_up[0],
               a0=rowv(rwkv_a0[0]), alu=rwkv_a_lora_up[0], glu=rwkv_g_lora_up[0],
               k_k=rowv(rwkv_k_k[0]), k_a=rowv(rwkv_k_a[0]), r_k=rowv(rwkv_r_k[0]),
               gn_g=rowv(rwkv_gn_g[0]), gn_b=rowv(rwkv_gn_b[0]))
    ge, be = rowv(ln_emb_g), rowv(ln_emb_b)

    xn_p, q_p, kv_p, rp_p = _inproj(x_prompt.reshape(tp, D_MODEL), ge, be, w_in_b, 512)
    att_p = _swa_prompt(sinks, q_p, kv_p, nbp, seq)
    rw_p, wkv_p = _rwkv_prompt(rp_p, prm, nbp, seq, 256)

    xn_s, q_s, kv_s, rp_s = _inproj(x_sample.reshape(ts, D_MODEL), ge, be, w_in_b, ts)
    ck3 = cache_k[0].reshape(nbs, w_buf, KV_COLS)
    cv3 = cache_v[0].reshape(nbs, w_buf, KV_COLS)
    att_s = _swa_sample(sinks, q_s.reshape(nbs, ATT_HEADS, HEAD_DIM),
                        kv_s.reshape(nbs, 2 * KV_HEADS, HEAD_DIM), ck3, cv3)
    att_s = att_s.reshape(ts, ATT_WIDTH)
    p7 = _rwkv_sample_prep(rp_s, state_shift[0], prm)
    hv = lambda z: z.reshape(RWKV_HEADS, HEAD_DIM)
    wkv_s, rw_s = _rwkv_sample_step(p7.reshape(7, nbs, RWKV_HEADS, HEAD_DIM), state_wkv[0],
                                    hv(rwkv_r_k[0]), hv(rwkv_gn_g[0]), hv(rwkv_gn_b[0]))
    rw_s = rw_s.reshape(ts, RWKV_WIDTH)

    wr_pad = jnp.zeros((D_MODEL, LANES), F32).at[:, :N_EXPERTS].set(w_router[0])
    wr_hi = wr_pad.astype(BF16)
    wr_lo = (wr_pad - wr_hi.astype(F32)).astype(BF16)
    br_pad = jnp.full((1, LANES), -jnp.inf, F32).at[0, :N_EXPERTS].set(b_router[0])
    g1, b1 = rowv(ln1_g[0]), rowv(ln1_b[0])
    tmp, tmd = 512, 256
    x1_p, info_p, gate_p, cnt_p = _mix_router(att_p, rw_p, xn_p, w_out_b, g1, b1, wr_hi, wr_lo,
                                              br_pad, jnp.zeros((1, LANES), F32), tmp)
    x1_s, info_s, gate_s, cnt = _mix_router(att_s, rw_s, xn_s, w_out_b, g1, b1, wr_hi, wr_lo,
                                            br_pad, cnt_p, ts)

    counts = cnt[0, :N_EXPERTS].astype(jnp.int32)
    padded = ((counts + EXPERT_TILE - 1) // EXPERT_TILE) * EXPERT_TILE
    ends = jnp.cumsum(padded)
    offs = ends - padded
    n_slots = (tp + ts) * TOP_K + N_EXPERTS * EXPERT_TILE
    n_tiles = n_slots // EXPERT_TILE
    n_used = (ends[-1] // EXPERT_TILE).reshape(1).astype(jnp.int32)
    tile_start = jnp.arange(n_tiles, dtype=jnp.int32) * EXPERT_TILE
    tile_e = jnp.sum(ends[None, :] <= tile_start[:, None], axis=1).astype(jnp.int32)
    last_e = jnp.max(jnp.where(padded > 0, jnp.arange(N_EXPERTS), 0)).astype(jnp.int32)
    tile_e = jnp.minimum(tile_e, last_e)

    def slots(info):
        return (offs[info[:, 0:TOP_K]] + info[:, TOP_K:2 * TOP_K]).reshape(-1).astype(jnp.int32)

    pos_p, pos_s = slots(info_p), slots(info_s)

    fill_start = jnp.concatenate([jnp.clip(ends - EXPERT_TILE, 0, n_slots - EXPERT_TILE),
                                  n_used]).astype(jnp.int32)
    xs = _dispatch(pos_p, pos_s, fill_start, x1_p, x1_s, n_slots, tmd)
    bgu = b_gate_up[0]
    bg = bgu[:, 0::2].reshape(N_EXPERTS, 1, D_FF)
    bu = bgu[:, 1::2].reshape(N_EXPERTS, 1, D_FF)
    bdn = b_down[0].reshape(N_EXPERTS, 1, D_MODEL)
    ys = _experts(tile_e, n_used, xs, w_gate_up[0], w_down[0], bg, bu, bdn)

    g2, b2 = rowv(ln2_g[0]), rowv(ln2_b[0])
    y_p = _combine(pos_p, ys, gate_p, x1_p, p_prompt[0].reshape(tp, PLE_DIM), g2, b2,
                   w_ple_b, w_pg_b, tmd)
    y_s = _combine(pos_s, ys, gate_s, x1_s, p_sample[0].reshape(ts, PLE_DIM), g2, b2,
                   w_ple_b, w_pg_b, ts)

    w_keep = min(WINDOW, seq)
    kv_p3 = kv_p.reshape(nbp, seq, 2 * KV_COLS)[:, seq - w_keep:]
    k_win_p = kv_p3[:, :, 0:KV_COLS].reshape(1, nbp, w_keep, KV_HEADS, HEAD_DIM)
    v_win_p = kv_p3[:, :, KV_COLS:].reshape(1, nbp, w_keep, KV_HEADS, HEAD_DIM)
    shift_p = rp_p.reshape(nbp, seq, RWKV_PROJ)[:, seq - 1][None]
    k_new = kv_s[:, 0:KV_COLS].reshape(nbs, 1, KV_HEADS, HEAD_DIM)
    v_new = kv_s[:, KV_COLS:].reshape(nbs, 1, KV_HEADS, HEAD_DIM)
    k_win_s = jnp.concatenate([cache_k[0], k_new], axis=1)[:, 1:][None]
    v_win_s = jnp.concatenate([cache_v[0], v_new], axis=1)[:, 1:][None]
    return (y_p.reshape(nbp, seq, D_MODEL), y_s.reshape(nbs, 1, D_MODEL),
            k_win_p, v_win_p, shift_p, wkv_p[None],
            k_win_s, v_win_s, rp_s[None], wkv_s[None])
```

```python
import functools

import jax
import jax.numpy as jnp
from jax import lax
from jax.experimental import pallas as pl
from jax.experimental.pallas import tpu as pltpu

F32 = jnp.float32
BF16 = jnp.bfloat16
HIGHEST = lax.Precision.HIGHEST

D_MODEL = 1024
HEAD_DIM = 64
ATT_HEADS = 8
KV_HEADS = 2
GROUP = ATT_HEADS // KV_HEADS
ATT_WIDTH = ATT_HEADS * HEAD_DIM
KV_COLS = KV_HEADS * HEAD_DIM
RWKV_HEADS = 8
RWKV_WIDTH = RWKV_HEADS * HEAD_DIM
LORA_W = 64
LORA_A = 64
LORA_G = 128
RWKV_PROJ = 3 * RWKV_WIDTH + LORA_W + LORA_A + LORA_G
WINDOW = 128
BLOCK = 128
PLE_DIM = 256
N_EXPERTS = 32
TOP_K = 4
D_FF = 1024
SWIGLU_LIMIT = 7.0
SWIGLU_ALPHA = 1.702
LN_EPS = 1e-5
GN_EPS = 64e-5
DEPTH = 1
DEEPNORM_ALPHA = (2 * DEPTH) ** 0.25

LANES = 128
CHUNK = 64
HEAD_QUAD = 4
EXPERT_TILE = 512
ROW_DMA_UNROLL = 8
VMEM_LIMIT = 48 * 1024 * 1024
EXPERTS_VMEM_LIMIT = 58 * 1024 * 1024


def _cparams(sem):
    return pltpu.CompilerParams(dimension_semantics=sem, vmem_limit_bytes=VMEM_LIMIT)


def _bdot(a, b):
    return jnp.dot(a.astype(BF16), b.astype(BF16), preferred_element_type=F32)


def _bdot_nt(a, b):
    return lax.dot_general(a.astype(BF16), b.astype(BF16), (((1,), (1,)), ((), ())),
                           preferred_element_type=F32)


def _bdot_tn(a, b):
    return lax.dot_general(a.astype(BF16), b.astype(BF16), (((0,), (0,)), ((), ())),
                           preferred_element_type=F32)


def _split_dot(m_bf16, x, passes):
    acc = None
    rem = x
    for _ in range(passes):
        hi = rem.astype(BF16)
        part = jnp.dot(m_bf16, hi, preferred_element_type=F32)
        acc = part if acc is None else acc + part
        rem = rem - hi.astype(F32)
    return acc


def _head_sum(x, bd_bf16):
    acc = None
    rem = x
    for _ in range(2):
        hi = rem.astype(BF16)
        part = jnp.dot(hi, bd_bf16, preferred_element_type=F32)
        acc = part if acc is None else acc + part
        rem = rem - hi.astype(F32)
    return acc


def _sigmoid(x):
    return 1.0 / (1.0 + jnp.exp(-x))


def _layer_norm(x, g, b):
    mu = jnp.mean(x, axis=-1, keepdims=True)
    xc = x - mu
    var = jnp.mean(xc * xc, axis=-1, keepdims=True)
    return xc * lax.rsqrt(var + LN_EPS) * g + b


def _inproj_body(x_ref, g_ref, b_ref, w_ref, xn_ref, q_ref, kv_ref, rp_ref):
    xn = _layer_norm(x_ref[...], g_ref[...], b_ref[...])
    xn_ref[...] = xn
    xb = xn.astype(BF16)
    q_ref[...] = jnp.dot(xb, w_ref[:, 0:ATT_WIDTH], preferred_element_type=F32)
    kv_ref[...] = jnp.dot(xb, w_ref[:, ATT_WIDTH:ATT_WIDTH + 2 * KV_COLS],
                          preferred_element_type=F32)
    rp_ref[...] = jnp.dot(xb, w_ref[:, ATT_WIDTH + 2 * KV_COLS:], preferred_element_type=F32)


def _inproj(x2d, g, b, w_bf16, tm):
    t = x2d.shape[0]
    in_proj = w_bf16.shape[1]
    row = lambda i: (i, 0)
    fixed = lambda i: (0, 0)
    return pl.pallas_call(
        _inproj_body,
        grid=(t // tm,),
        in_specs=[pl.BlockSpec((tm, D_MODEL), row),
                  pl.BlockSpec((1, D_MODEL), fixed),
                  pl.BlockSpec((1, D_MODEL), fixed),
                  pl.BlockSpec((D_MODEL, in_proj), fixed)],
        out_specs=[pl.BlockSpec((tm, D_MODEL), row),
                   pl.BlockSpec((tm, ATT_WIDTH), row),
                   pl.BlockSpec((tm, 2 * KV_COLS), row),
                   pl.BlockSpec((tm, RWKV_PROJ), row)],
        out_shape=[jax.ShapeDtypeStruct((t, D_MODEL), F32),
                   jax.ShapeDtypeStruct((t, ATT_WIDTH), F32),
                   jax.ShapeDtypeStruct((t, 2 * KV_COLS), F32),
                   jax.ShapeDtypeStruct((t, RWKV_PROJ), F32)],
        compiler_params=_cparams(("parallel",)),
        name="inproj",
    )(x2d, g, b, w_bf16)


def _alibi_slope(h):
    return 2.0 ** (-8.0 * (h + 1) / ATT_HEADS)


def _swa_prompt_body(sink_ref, q_ref, kvc_ref, kvp_ref, o_ref):
    n = pl.program_id(1)
    q = (q_ref[...] * (HEAD_DIM ** -0.5)).astype(BF16)
    kvc = kvc_ref[...].astype(BF16)
    kvp = kvp_ref[...].astype(BF16)
    row = lax.broadcasted_iota(jnp.int32, (BLOCK, 2 * BLOCK), 0)
    col = lax.broadcasted_iota(jnp.int32, (BLOCK, 2 * BLOCK), 1)
    dist = row + BLOCK - col
    valid = (dist >= 0) & (dist <= WINDOW) & ((col >= BLOCK) | (n > 0))
    distf = dist.astype(F32)
    kbands, vbands = [], []
    for g in range(KV_HEADS):
        ks = slice(g * HEAD_DIM, (g + 1) * HEAD_DIM)
        vs = slice(KV_COLS + g * HEAD_DIM, KV_COLS + (g + 1) * HEAD_DIM)
        kbands.append(jnp.concatenate([kvp[:, ks], kvc[:, ks]], axis=0))
        vbands.append(jnp.concatenate([kvp[:, vs], kvc[:, vs]], axis=0))
    scores = [lax.dot_general(q[:, h * HEAD_DIM:(h + 1) * HEAD_DIM], kbands[h // GROUP],
                              (((1,), (1,)), ((), ())), preferred_element_type=F32)
              for h in range(ATT_HEADS)]
    probs = []
    for h in range(ATT_HEADS):
        s = jnp.where(valid, scores[h] - _alibi_slope(h) * distf, -jnp.inf)
        sink = sink_ref[h]
        m = jnp.maximum(jnp.max(s, axis=1, keepdims=True), sink)
        e = jnp.exp(s - m)
        den = jnp.sum(e, axis=1, keepdims=True) + jnp.exp(sink - m)
        probs.append((e / den).astype(BF16))
    outs = [jnp.dot(probs[h], vbands[h // GROUP], preferred_element_type=F32)
            for h in range(ATT_HEADS)]
    o_ref[...] = jnp.concatenate(outs, axis=1)


def _swa_prompt(sinks, q, kv, nbatch, seq):
    nb = seq // BLOCK
    cur = lambda b, n: (b * nb + n, 0)
    prv = lambda b, n: (b * nb + jnp.maximum(n - 1, 0), 0)
    return pl.pallas_call(
        _swa_prompt_body,
        grid=(nbatch, nb),
        in_specs=[pl.BlockSpec(memory_space=pltpu.SMEM),
                  pl.BlockSpec((BLOCK, ATT_WIDTH), cur),
                  pl.BlockSpec((BLOCK, 2 * KV_COLS), cur),
                  pl.BlockSpec((BLOCK, 2 * KV_COLS), prv)],
        out_specs=pl.BlockSpec((BLOCK, ATT_WIDTH), cur),
        out_shape=jax.ShapeDtypeStruct((nbatch * seq, ATT_WIDTH), F32),
        compiler_params=_cparams(("parallel", "parallel")),
        name="swa_prompt",
    )(sinks, q, kv, kv)


def _swa_sample_body(sink_ref, q_ref, kvn_ref, ck_ref, cv_ref, o_ref, *, bb, w_buf):
    hrow = lax.broadcasted_iota(jnp.int32, (ATT_HEADS, 1), 0)
    slope = jnp.zeros((ATT_HEADS, 1), F32)
    sink = jnp.zeros((ATT_HEADS, 1), F32)
    for h in range(ATT_HEADS):
        slope = jnp.where(hrow == h, _alibi_slope(h), slope)
        sink = jnp.where(hrow == h, sink_ref[h], sink)
    jcol = lax.broadcasted_iota(jnp.int32, (ATT_HEADS, w_buf), 1)
    bias = -slope * (w_buf - jcol).astype(F32)
    lower = hrow < GROUP
    q2b = [(q_ref[b] * (HEAD_DIM ** -0.5)).astype(BF16) for b in range(bb)]
    s01 = []
    for b in range(bb):
        ck = ck_ref[b].astype(BF16)
        s01.append((_bdot_nt(q2b[b], ck[:, 0:HEAD_DIM]),
                    _bdot_nt(q2b[b], ck[:, HEAD_DIM:2 * HEAD_DIM])))
    ps = []
    for b in range(bb):
        kvn = kvn_ref[b]
        s = jnp.where(lower, s01[b][0], s01[b][1]) + bias
        knew = jnp.where(lower, kvn[0:1, :], kvn[1:2, :])
        vnew = jnp.where(lower, kvn[2:3, :], kvn[3:4, :])
        snew = jnp.sum(q2b[b].astype(F32) * knew.astype(BF16).astype(F32), axis=1, keepdims=True)
        m = jnp.maximum(jnp.maximum(jnp.max(s, axis=1, keepdims=True), snew), sink)
        e = jnp.exp(s - m)
        enew = jnp.exp(snew - m)
        den = jnp.sum(e, axis=1, keepdims=True) + enew + jnp.exp(sink - m)
        pnew = (enew / den).astype(BF16).astype(F32)
        ps.append(((e / den).astype(BF16), pnew * vnew.astype(BF16).astype(F32)))
    for b in range(bb):
        cv = cv_ref[b].astype(BF16)
        o0 = jnp.dot(ps[b][0], cv[:, 0:HEAD_DIM], preferred_element_type=F32)
        o1 = jnp.dot(ps[b][0], cv[:, HEAD_DIM:2 * HEAD_DIM], preferred_element_type=F32)
        o_ref[b] = jnp.where(lower, o0, o1) + ps[b][1]


def _swa_sample(sinks, q3, kvn3, ck3, cv3, bb=8):
    nb, w_buf = ck3.shape[0], ck3.shape[1]
    blk = lambda i: (i, 0, 0)
    return pl.pallas_call(
        functools.partial(_swa_sample_body, bb=bb, w_buf=w_buf),
        grid=(nb // bb,),
        in_specs=[pl.BlockSpec(memory_space=pltpu.SMEM),
                  pl.BlockSpec((bb, ATT_HEADS, HEAD_DIM), blk),
                  pl.BlockSpec((bb, 2 * KV_HEADS, HEAD_DIM), blk),
                  pl.BlockSpec((bb, w_buf, KV_COLS), blk),
                  pl.BlockSpec((bb, w_buf, KV_COLS), blk)],
        out_specs=pl.BlockSpec((bb, ATT_HEADS, HEAD_DIM), blk),
        out_shape=jax.ShapeDtypeStruct((nb, ATT_HEADS, HEAD_DIM), F32),
        compiler_params=_cparams(("parallel",)),
        name="swa_sample",
    )(sinks, q3, kvn3, ck3, cv3)


def _rwkv_prep(rp, prev, mu, w0, wlu, a0, alu, glu, k_k, k_a, bd):
    xs = rp + mu * (prev - rp)
    r = xs[:, 0:RWKV_WIDTH]
    k = xs[:, RWKV_WIDTH:2 * RWKV_WIDTH]
    v = xs[:, 2 * RWKV_WIDTH:3 * RWKV_WIDTH]
    o = 3 * RWKV_WIDTH
    wd = xs[:, o:o + LORA_W]
    ad = xs[:, o + LORA_W:o + LORA_W + LORA_A]
    gd = xs[:, o + LORA_W + LORA_A:]
    z = -(w0 + _bdot(jnp.tanh(wd), wlu))
    softplus = jnp.maximum(z, 0.0) + jnp.log(1.0 + jnp.exp(-jnp.abs(z)))
    log_decay = -jnp.exp(-softplus - 0.5)
    a = _sigmoid(a0 + _bdot(ad, alu))
    g = _bdot(_sigmoid(gd), glu)
    kkr = k * k_k
    kk = kkr / jnp.maximum(jnp.sqrt(_head_sum(kkr * kkr, bd)), 1e-12)
    k2 = k * (1.0 + (a - 1.0) * k_a)
    return r, log_decay, k2, v, kk, a, g


def _rwkv_finish(y, r, k2, v, g, r_k, gn_g, gn_b, bd):
    inv = 1.0 / HEAD_DIM
    mu = _head_sum(y, bd) * inv
    yc = y - mu
    var = _head_sum(yc * yc, bd) * inv
    yn = yc * lax.rsqrt(var + GN_EPS) * gn_g + gn_b
    bonus = _head_sum(r * k2 * r_k, bd) * v
    return (yn + bonus) * g


def _rwkv_prompt_body(rp_ref, mu_ref, w0_ref, wlu_ref, a0_ref, alu_ref, glu_ref, kk_ref, ka_ref,
                      rk_ref, gng_ref, gnb_ref, bd_ref, ltri_ref,
                      out_ref, sfin_ref, prev_scr, st_scr, y_scr, *, tt):
    c = pl.program_id(1)

    @pl.when(c == 0)
    def _():
        prev_scr[...] = jnp.zeros_like(prev_scr)
        st_scr[...] = jnp.zeros_like(st_scr)

    rp = rp_ref[...]
    rolled = pltpu.roll(rp, 1, 0)
    rowi = lax.broadcasted_iota(jnp.int32, (tt, 1), 0)
    prev = jnp.where(rowi == 0, prev_scr[...], rolled)
    prev_scr[...] = rp[tt - 1:tt, :]
    bd = bd_ref[...]
    r, ld, k2, v, kk, a, g = _rwkv_prep(rp, prev, mu_ref[...], w0_ref[...], wlu_ref[...],
                                        a0_ref[...], alu_ref[...], glu_ref[...], kk_ref[...],
                                        ka_ref[...], bd)
    cum = _split_dot(ltri_ref[...], ld, 2)
    ecum = jnp.exp(cum)
    einv = jnp.exp(-cum)
    at = -kk * jnp.exp(cum - ld)
    bh = kk * a * einv
    kh = k2 * einv
    rt = r * ecum

    qw = HEAD_QUAD * HEAD_DIM
    ii = lax.broadcasted_iota(jnp.int32, (CHUNK, qw), 0)
    jl = lax.broadcasted_iota(jnp.int32, (CHUNK, qw), 1) % CHUNK
    strict = ii > jl
    incl = ii >= jl
    eye = (ii == jl).astype(F32)
    same_head = (lax.broadcasted_iota(jnp.int32, (qw, qw), 0) // HEAD_DIM
                 == lax.broadcasted_iota(jnp.int32, (qw, qw), 1) // HEAD_DIM)

    def blockdiag(m):
        mb = m.astype(BF16)
        return jnp.where(same_head, jnp.concatenate([mb] * HEAD_QUAD, axis=0), jnp.zeros((), BF16))

    def diag_blocks(full):
        out = None
        for hh in range(HEAD_QUAD):
            rs = slice(hh * HEAD_DIM, (hh + 1) * HEAD_DIM)
            blk = jnp.where(same_head[rs], full[rs], 0.0)
            out = blk if out is None else out + blk
        return out

    mm = lambda x, w_bf16: jnp.dot(x.astype(BF16), w_bf16, preferred_element_type=F32)
    mm_nt = lambda x, w_bf16: lax.dot_general(x.astype(BF16), w_bf16, (((1,), (1,)), ((), ())),
                                              preferred_element_type=F32)
    nsub = tt // CHUNK
    units = [(s, q) for s in range(nsub) for q in range(RWKV_HEADS // HEAD_QUAD)]
    rows = lambda s: slice(s * CHUNK, (s + 1) * CHUNK)
    cols = lambda q: slice(q * qw, (q + 1) * qw)
    cut = lambda z, u: z[rows(u[0]), cols(u[1])]
    kka = kk * a
    p_last, kh_end, bh_end = [], [], []
    for s in range(nsub):
        c_last = cum[(s + 1) * CHUNK - 1:(s + 1) * CHUNK, :]
        p_last.append(jnp.exp(c_last))
        tail = jnp.exp(c_last - cum[rows(s)])
        kh_end.append(k2[rows(s)] * tail)
        bh_end.append(kka[rows(s)] * tail)
    x2 = {u: jnp.concatenate([cut(at, u), cut(rt, u)], axis=0) for u in units}
    xb = {u: mm_nt(x2[u], blockdiag(cut(bh, u))) for u in units}
    xk = {u: mm_nt(x2[u], blockdiag(cut(kh, u))) for u in units}
    nmat = {u: jnp.where(strict, xb[u][0:CHUNK], 0.0) for u in units}
    a_rb = {u: jnp.where(incl, xb[u][CHUNK:], 0.0) for u in units}
    a_ak = {u: jnp.where(strict, xk[u][0:CHUNK], 0.0) for u in units}
    a_rk = {u: jnp.where(incl, xk[u][CHUNK:], 0.0) for u in units}
    tmat = {u: eye + nmat[u] for u in units}
    npow = nmat
    for _ in range(5):
        npow = {u: mm(npow[u], blockdiag(npow[u])) for u in units}
        tmat = {u: tmat[u] + mm(npow[u], blockdiag(tmat[u])) for u in units}
    vbd = {u: blockdiag(cut(v, u)) for u in units}
    av = {u: mm(a_ak[u], vbd[u]) for u in units}
    aprime = {u: mm(tmat[u], blockdiag(cut(at, u))) for u in units}
    wmat = {u: mm(tmat[u], blockdiag(av[u])) for u in units}
    rprime = {u: cut(rt, u) + mm(a_rb[u], blockdiag(aprime[u])) for u in units}
    y0 = {u: mm(a_rk[u], vbd[u]) + mm(a_rb[u], blockdiag(wmat[u])) for u in units}
    hmat = {u: diag_blocks(_bdot_tn(jnp.concatenate([cut(v, u), wmat[u]], axis=0),
                                    jnp.concatenate([cut(kh_end[u[0]], (0, u[1])),
                                                     cut(bh_end[u[0]], (0, u[1]))], axis=0)))
            for u in units}
    g0 = {u: diag_blocks(_bdot_tn(aprime[u], cut(bh_end[u[0]], (0, u[1])))) for u in units}

    states = [st_scr[q] for q in range(RWKV_HEADS // HEAD_QUAD)]
    for s in range(nsub):
        for q in range(RWKV_HEADS // HEAD_QUAD):
            u = (s, q)
            st = states[q]
            y_scr[rows(s), cols(q)] = mm_nt(rprime[u], blockdiag(st)) + y0[u]
            states[q] = st * p_last[s][:, cols(q)] + mm(st, blockdiag(g0[u])) + hmat[u]
    for q in range(RWKV_HEADS // HEAD_QUAD):
        st_scr[q] = states[q]
        for hh in range(HEAD_QUAD):
            sfin_ref[0, q * HEAD_QUAD + hh] = states[q][:, hh * HEAD_DIM:(hh + 1) * HEAD_DIM]
    out_ref[...] = _rwkv_finish(y_scr[...], r, k2, v, g, rk_ref[...], gng_ref[...], gnb_ref[...], bd)


def _chunk_tril(tt):
    i = jnp.arange(tt)
    same = (i[:, None] // CHUNK) == (i[None, :] // CHUNK)
    return (same & (i[:, None] >= i[None, :])).astype(BF16)


def _head_blockdiag():
    i = jnp.arange(RWKV_WIDTH) // HEAD_DIM
    return (i[:, None] == i[None, :]).astype(BF16)


def _rwkv_prompt(rp, prm, nbatch, seq, tt):
    nc = seq // tt
    fixed = lambda b, c: (0, 0)
    row = lambda b, c: (b * nc + c, 0)
    vec = lambda n: pl.BlockSpec((1, n), fixed)
    return pl.pallas_call(
        functools.partial(_rwkv_prompt_body, tt=tt),
        grid=(nbatch, nc),
        in_specs=[pl.BlockSpec((tt, RWKV_PROJ), row),
                  vec(RWKV_PROJ), vec(RWKV_WIDTH),
                  pl.BlockSpec((LORA_W, RWKV_WIDTH), fixed), vec(RWKV_WIDTH),
                  pl.BlockSpec((LORA_A, RWKV_WIDTH), fixed),
                  pl.BlockSpec((LORA_G, RWKV_WIDTH), fixed),
                  vec(RWKV_WIDTH), vec(RWKV_WIDTH), vec(RWKV_WIDTH), vec(RWKV_WIDTH),
                  vec(RWKV_WIDTH),
                  pl.BlockSpec((RWKV_WIDTH, RWKV_WIDTH), fixed),
                  pl.BlockSpec((tt, tt), fixed)],
        out_specs=[pl.BlockSpec((tt, RWKV_WIDTH), row),
                   pl.BlockSpec((1, RWKV_HEADS, HEAD_DIM, HEAD_DIM), lambda b, c: (b, 0, 0, 0))],
        out_shape=[jax.ShapeDtypeStruct((nbatch * seq, RWKV_WIDTH), F32),
                   jax.ShapeDtypeStruct((nbatch, RWKV_HEADS, HEAD_DIM, HEAD_DIM), F32)],
        scratch_shapes=[pltpu.VMEM((1, RWKV_PROJ), F32),
                        pltpu.VMEM((RWKV_HEADS // HEAD_QUAD, HEAD_DIM, HEAD_QUAD * HEAD_DIM), F32),
                        pltpu.VMEM((tt, RWKV_WIDTH), F32)],
        compiler_params=_cparams(("parallel", "arbitrary")),
        name="rwkv_prompt",
    )(rp, prm["mu"], prm["w0"], prm["wlu"], prm["a0"], prm["alu"], prm["glu"], prm["k_k"],
      prm["k_a"], prm["r_k"], prm["gn_g"], prm["gn_b"], _head_blockdiag(), _chunk_tril(tt))


def _rwkv_sample_prep_body(rp_ref, prev_ref, mu_ref, w0_ref, wlu_ref, a0_ref, alu_ref, glu_ref,
                           kk_ref, ka_ref, bd_ref, o_ref):
    r, ld, k2, v, kk, a, g = _rwkv_prep(rp_ref[...], prev_ref[...], mu_ref[...], w0_ref[...],
                                        wlu_ref[...], a0_ref[...], alu_ref[...], glu_ref[...],
                                        kk_ref[...], ka_ref[...], bd_ref[...])
    for i, z in enumerate((r, jnp.exp(ld), k2, v, kk, a, g)):
        o_ref[i] = z


def _rwkv_sample_prep(rp, prev, prm):
    nb = rp.shape[0]
    return pl.pallas_call(
        _rwkv_sample_prep_body,
        out_shape=jax.ShapeDtypeStruct((7, nb, RWKV_WIDTH), F32),
        compiler_params=pltpu.CompilerParams(vmem_limit_bytes=VMEM_LIMIT),
        name="rwkv_sample_prep",
    )(rp, prev, prm["mu"], prm["w0"], prm["wlu"], prm["a0"], prm["alu"], prm["glu"], prm["k_k"],
      prm["k_a"], _head_blockdiag())


def _rwkv_sample_step_body(p_ref, s_ref, rk_ref, gng_ref, gnb_ref, so_ref, o_ref, *, bb):
    ii = lax.broadcasted_iota(jnp.int32, (HEAD_DIM, HEAD_DIM), 0)
    jj = lax.broadcasted_iota(jnp.int32, (HEAD_DIM, HEAD_DIM), 1)
    eye = ii == jj
    for b in range(bb):
        for h in range(RWKV_HEADS):
            row = lambda i: p_ref[i, b, h:h + 1, :]
            r_r, w_r, k_r, v_r, kk_r, a_r = (row(i) for i in range(6))
            st = s_ref[b, h]
            sa = jnp.sum(st * kk_r, axis=1, keepdims=True)
            v_col = jnp.sum(jnp.where(eye, v_r, 0.0), axis=1, keepdims=True)
            st = st * w_r - sa * (kk_r * a_r) + v_col * k_r
            so_ref[b, h] = st
            y_col = jnp.sum(st * r_r, axis=1, keepdims=True)
            o_ref[b, h:h + 1, :] = jnp.sum(jnp.where(eye, y_col, 0.0), axis=0, keepdims=True)
    y = o_ref[...]
    r, k2, v, g = p_ref[0], p_ref[2], p_ref[3], p_ref[6]
    mu = jnp.mean(y, axis=-1, keepdims=True)
    yc = y - mu
    var = jnp.mean(yc * yc, axis=-1, keepdims=True)
    yn = yc * lax.rsqrt(var + GN_EPS) * gng_ref[...] + gnb_ref[...]
    bonus = jnp.sum(r * k2 * rk_ref[...], axis=-1, keepdims=True) * v
    o_ref[...] = (yn + bonus) * g


def _rwkv_sample_step(p4, state, r_k, gn_g, gn_b, bb=8):
    nb = state.shape[0]
    hv = pl.BlockSpec((RWKV_HEADS, HEAD_DIM), lambda i: (0, 0))
    return pl.pallas_call(
        functools.partial(_rwkv_sample_step_body, bb=bb),
        grid=(nb // bb,),
        in_specs=[pl.BlockSpec((7, bb, RWKV_HEADS, HEAD_DIM), lambda i: (0, i, 0, 0)),
                  pl.BlockSpec((bb, RWKV_HEADS, HEAD_DIM, HEAD_DIM), lambda i: (i, 0, 0, 0)),
                  hv, hv, hv],
        out_specs=[pl.BlockSpec((bb, RWKV_HEADS, HEAD_DIM, HEAD_DIM), lambda i: (i, 0, 0, 0)),
                   pl.BlockSpec((bb, RWKV_HEADS, HEAD_DIM), lambda i: (i, 0, 0))],
        out_shape=[jax.ShapeDtypeStruct(state.shape, F32),
                   jax.ShapeDtypeStruct((nb, RWKV_HEADS, HEAD_DIM), F32)],
        compiler_params=_cparams(("parallel",)),
        name="rwkv_sample_step",
    )(p4, state, r_k, gn_g, gn_b)


def _mix_router_body(att_ref, rw_ref, xn_ref, wo_ref, g_ref, b_ref, wrh_ref, wrl_ref, br_ref,
                     base_ref, x1_ref, info_ref, gate_ref, cnt_ref, base_scr, *, tm):
    i = pl.program_id(0)

    @pl.when(i == 0)
    def _():
        base_scr[...] = base_ref[...]

    mixed = (jnp.dot(att_ref[...].astype(BF16), wo_ref[0:ATT_WIDTH, :], preferred_element_type=F32)
             + jnp.dot(rw_ref[...].astype(BF16), wo_ref[ATT_WIDTH:, :], preferred_element_type=F32))
    x1 = _layer_norm(DEEPNORM_ALPHA * xn_ref[...] + mixed, g_ref[...], b_ref[...])
    x1_ref[...] = x1

    x1h = x1.astype(BF16)
    x1l = (x1 - x1h.astype(F32)).astype(BF16)
    logits = (jnp.dot(x1h, wrh_ref[...], preferred_element_type=F32)
              + jnp.dot(x1l, wrh_ref[...], preferred_element_type=F32)
              + jnp.dot(x1h, wrl_ref[...], preferred_element_type=F32)) + br_ref[...]
    lane = lax.broadcasted_iota(jnp.int32, (tm, LANES), 1)
    lanef = lane.astype(F32)
    vals, idxs, hots = [], [], []
    cur = logits
    for _ in range(TOP_K):
        m = jnp.max(cur, axis=1, keepdims=True)
        idx = jnp.min(jnp.where(cur == m, lanef, float(LANES)), axis=1, keepdims=True)
        hot = lanef == idx
        cur = jnp.where(hot, -jnp.inf, cur)
        vals.append(m)
        idxs.append(idx)
        hots.append(hot)
    es = [jnp.exp(vk - vals[0]) for vk in vals]
    den = es[0] + es[1] + es[2] + es[3]
    multi = jnp.zeros((tm, LANES), F32)
    for hot in hots:
        multi = multi + hot.astype(F32)
    ti = lax.broadcasted_iota(jnp.int32, (tm, tm), 0)
    tj = lax.broadcasted_iota(jnp.int32, (tm, tm), 1)
    before = jnp.dot((ti > tj).astype(BF16), multi.astype(BF16), preferred_element_type=F32)
    before = before + base_scr[...]
    info = jnp.zeros((tm, LANES), jnp.int32)
    gates = jnp.zeros((tm, LANES), F32)
    for k in range(TOP_K):
        rank = jnp.sum(jnp.where(hots[k], before, 0.0), axis=1, keepdims=True)
        info = jnp.where(lane == k, idxs[k].astype(jnp.int32), info)
        info = jnp.where(lane == TOP_K + k, rank.astype(jnp.int32), info)
        gates = jnp.where(lane == k, es[k] / den, gates)
    info_ref[...] = info
    gate_ref[...] = gates
    base_scr[...] = base_scr[...] + jnp.sum(multi, axis=0, keepdims=True)
    cnt_ref[...] = base_scr[...]


def _mix_router(att, rw, xn, wo_bf16, g, b, wr_hi, wr_lo, br_pad, base, tm):
    t = att.shape[0]
    row = lambda i: (i, 0)
    fixed = lambda i: (0, 0)
    return pl.pallas_call(
        functools.partial(_mix_router_body, tm=tm),
        grid=(t // tm,),
        in_specs=[pl.BlockSpec((tm, ATT_WIDTH), row),
                  pl.BlockSpec((tm, RWKV_WIDTH), row),
                  pl.BlockSpec((tm, D_MODEL), row),
                  pl.BlockSpec((D_MODEL, D_MODEL), fixed),
                  pl.BlockSpec((1, D_MODEL), fixed),
                  pl.BlockSpec((1, D_MODEL), fixed),
                  pl.BlockSpec((D_MODEL, LANES), fixed),
                  pl.BlockSpec((D_MODEL, LANES), fixed),
                  pl.BlockSpec((1, LANES), fixed),
                  pl.BlockSpec((1, LANES), fixed)],
        out_specs=[pl.BlockSpec((tm, D_MODEL), row),
                   pl.BlockSpec((tm, LANES), row),
                   pl.BlockSpec((tm, LANES), row),
                   pl.BlockSpec((1, LANES), fixed)],
        out_shape=[jax.ShapeDtypeStruct((t, D_MODEL), F32),
                   jax.ShapeDtypeStruct((t, LANES), jnp.int32),
                   jax.ShapeDtypeStruct((t, LANES), F32),
                   jax.ShapeDtypeStruct((1, LANES), F32)],
        scratch_shapes=[pltpu.VMEM((1, LANES), F32)],
        compiler_params=_cparams(("arbitrary",)),
        name="mix_router",
    )(att, rw, xn, wo_bf16, g, b, wr_hi, wr_lo, br_pad, base)


def _row_copies(pos_ref, base, tm, make_copy):
    def group(gi, carry):
        r0 = pl.multiple_of(gi * ROW_DMA_UNROLL, ROW_DMA_UNROLL)
        p0 = (base + r0) * TOP_K
        for j in range(ROW_DMA_UNROLL):
            for k in range(TOP_K):
                make_copy(r0 + j, k, pos_ref[p0 + j * TOP_K + k]).start()
        return carry

    lax.fori_loop(0, tm // ROW_DMA_UNROLL, group, 0)


def _dispatch_rows(pos_ref, base, x_ref, xs_ref, sem, rows):
    def make_copy(r, k, p):
        return pltpu.make_async_copy(x_ref.at[pl.ds(r, 1)], xs_ref.at[pl.ds(p, 1)], sem)

    _row_copies(pos_ref, base, rows, make_copy)
    for _ in range(TOP_K):
        pltpu.make_async_copy(x_ref, xs_ref.at[pl.ds(0, rows)], sem).wait()


def _dispatch_body(pos_a_ref, pos_b_ref, fill_ref, xa_ref, xb_ref, xs_ref, zero_scr, sem, fill_sem,
                   *, tm, n_tiles):
    i = pl.program_id(0)
    last = pl.num_programs(0) - 1

    @pl.when(i == 0)
    def _():
        zero_scr[...] = jnp.zeros_like(zero_scr)

        def tile_fill(start):
            return pltpu.make_async_copy(
                zero_scr, xs_ref.at[pl.ds(pl.multiple_of(start, EXPERT_TILE), EXPERT_TILE)], fill_sem)

        fills = [tile_fill(fill_ref[e]) for e in range(N_EXPERTS)]
        for cp in fills:
            cp.start()
        first_unused = fill_ref[N_EXPERTS]
        lax.fori_loop(first_unused, n_tiles,
                      lambda t, c: (tile_fill(t * EXPERT_TILE).start(), c)[1], 0)
        for cp in fills:
            cp.wait()
        lax.fori_loop(first_unused, n_tiles,
                      lambda t, c: (tile_fill(t * EXPERT_TILE).wait(), c)[1], 0)

    @pl.when(i < last)
    def _():
        _dispatch_rows(pos_a_ref, i * tm, xa_ref, xs_ref, sem, tm)

    @pl.when(i == last)
    def _():
        _dispatch_rows(pos_b_ref, 0, xb_ref, xs_ref, sem, xb_ref.shape[0])


def _dispatch(pos_a, pos_b, fill_start, x_a, x_b, n_slots, tm):
    nta = x_a.shape[0] // tm
    return pl.pallas_call(
        functools.partial(_dispatch_body, tm=tm, n_tiles=n_slots // EXPERT_TILE),
        grid_spec=pltpu.PrefetchScalarGridSpec(
            num_scalar_prefetch=3,
            grid=(nta + 1,),
            in_specs=[pl.BlockSpec((tm, D_MODEL), lambda i, *_: (jnp.minimum(i, nta - 1), 0)),
                      pl.BlockSpec(x_b.shape, lambda i, *_: (0, 0))],
            out_specs=pl.BlockSpec(memory_space=pl.ANY),
            scratch_shapes=[pltpu.VMEM((EXPERT_TILE, D_MODEL), F32),
                            pltpu.SemaphoreType.DMA, pltpu.SemaphoreType.DMA]),
        out_shape=jax.ShapeDtypeStruct((n_slots, D_MODEL), F32),
        compiler_params=_cparams(("arbitrary",)),
        name="moe_dispatch",
    )(pos_a, pos_b, fill_start, x_a, x_b)


def _experts_body(te_ref, nu_ref, xs_ref, wgu_ref, wd_ref, bg_ref, bu_ref, bd_ref, sel_ref, ys_ref,
                  wg_scr, wu_scr, wd_scr):
    i = pl.program_id(0)
    new_expert = (i == 0) | (te_ref[i] != te_ref[jnp.maximum(i - 1, 0)])

    @pl.when(new_expert)
    def _():
        wd_scr[...] = wd_ref[...].astype(BF16)
        for m in range(D_FF // LANES):
            pair = wgu_ref[:, 2 * m * LANES:2 * (m + 1) * LANES].astype(BF16)
            split = jnp.dot(pair, sel_ref[...], preferred_element_type=F32)
            wg_scr[:, m * LANES:(m + 1) * LANES] = split[:, 0:LANES].astype(BF16)
            wu_scr[:, m * LANES:(m + 1) * LANES] = split[:, LANES:].astype(BF16)

    @pl.when(i < nu_ref[0])
    def _():
        x = xs_ref[...].astype(BF16)
        gate = jnp.dot(x, wg_scr[...], preferred_element_type=F32) + bg_ref[...]
        up = jnp.dot(x, wu_scr[...], preferred_element_type=F32) + bu_ref[...]
        gate = jnp.minimum(gate, SWIGLU_LIMIT)
        up = jnp.clip(up, -SWIGLU_LIMIT, SWIGLU_LIMIT)
        act = (up + 1.0) * gate * _sigmoid(SWIGLU_ALPHA * gate)
        ys_ref[...] = jnp.dot(act.astype(BF16), wd_scr[...], preferred_element_type=F32) + bd_ref[...]

    @pl.when(i >= nu_ref[0])
    def _():
        ys_ref[...] = jnp.zeros_like(ys_ref)


def _gate_up_selector():
    i = jnp.arange(2 * LANES)
    src = jnp.where(i < LANES, 2 * i, 2 * (i - LANES) + 1)
    return (i[:, None] == src[None, :]).astype(BF16)


def _experts(tile_e, n_used, xs, wgu, wd, bg, bu, bd):
    ns = xs.shape[0]
    tm = EXPERT_TILE
    wspec = lambda a, b: pl.BlockSpec((None, a, b), lambda i, te, nu: (te[i], 0, 0))
    return pl.pallas_call(
        _experts_body,
        grid_spec=pltpu.PrefetchScalarGridSpec(
            num_scalar_prefetch=2,
            grid=(ns // tm,),
            in_specs=[pl.BlockSpec((tm, D_MODEL), lambda i, te, nu: (jnp.minimum(i, nu[0] - 1), 0)),
                      wspec(D_MODEL, 2 * D_FF), wspec(D_FF, D_MODEL),
                      wspec(1, D_FF), wspec(1, D_FF), wspec(1, D_MODEL),
                      pl.BlockSpec((2 * LANES, 2 * LANES), lambda i, te, nu: (0, 0))],
            out_specs=pl.BlockSpec((tm, D_MODEL), lambda i, te, nu: (i, 0)),
            scratch_shapes=[pltpu.VMEM((D_MODEL, D_FF), BF16), pltpu.VMEM((D_MODEL, D_FF), BF16),
                            pltpu.VMEM((D_FF, D_MODEL), BF16)]),
        out_shape=jax.ShapeDtypeStruct((ns, D_MODEL), F32),
        compiler_params=pltpu.CompilerParams(dimension_semantics=("arbitrary",),
                                             vmem_limit_bytes=EXPERTS_VMEM_LIMIT),
        name="moe_experts",
    )(tile_e, n_used, xs, wgu, wd, bg, bu, bd, _gate_up_selector())


def _combine_body(pos_ref, ys_ref, gate_ref, x1_ref, pe_ref, g_ref, b_ref, wple_ref, wpg_ref,
                  o_ref, buf, sem, *, tm):
    i = pl.program_id(0)
    cur = i % 2

    def gather(tile, slot):
        def make_copy(r, k, p):
            return pltpu.make_async_copy(ys_ref.at[pl.ds(p, 1)], buf.at[slot, k, pl.ds(r, 1)],
                                         sem.at[slot])
        _row_copies(pos_ref, tile * tm, tm, make_copy)

    @pl.when(i == 0)
    def _():
        gather(0, 0)

    @pl.when(i + 1 < pl.num_programs(0))
    def _():
        gather(i + 1, 1 - cur)

    for k in range(TOP_K):
        pltpu.make_async_copy(ys_ref.at[pl.ds(0, tm)], buf.at[cur, k], sem.at[cur]).wait()
    gates = gate_ref[...]
    ffn = gates[:, 0:1] * buf[cur, 0]
    for k in range(1, TOP_K):
        ffn = ffn + gates[:, k:k + 1] * buf[cur, k]
    x2 = _layer_norm(DEEPNORM_ALPHA * x1_ref[...] + ffn, g_ref[...], b_ref[...])
    gate = _sigmoid(jnp.dot(x2.astype(BF16), wpg_ref[...], preferred_element_type=F32))
    emb = jnp.dot(pe_ref[...].astype(BF16), wple_ref[...], preferred_element_type=F32)
    o_ref[...] = x2 + gate * emb


def _combine(pos_flat, ys, gates, x1, pe, g, b, wple_bf16, wpg_bf16, tm):
    t = x1.shape[0]
    row = lambda i, *_: (i, 0)
    fixed = lambda i, *_: (0, 0)
    return pl.pallas_call(
        functools.partial(_combine_body, tm=tm),
        grid_spec=pltpu.PrefetchScalarGridSpec(
            num_scalar_prefetch=1,
            grid=(t // tm,),
            in_specs=[pl.BlockSpec(memory_space=pl.ANY),
                      pl.BlockSpec((tm, LANES), row),
                      pl.BlockSpec((tm, D_MODEL), row),
                      pl.BlockSpec((tm, PLE_DIM), row),
                      pl.BlockSpec((1, D_MODEL), fixed),
                      pl.BlockSpec((1, D_MODEL), fixed),
                      pl.BlockSpec((PLE_DIM, D_MODEL), fixed),
                      pl.BlockSpec((D_MODEL, D_MODEL), fixed)],
            out_specs=pl.BlockSpec((tm, D_MODEL), row),
            scratch_shapes=[pltpu.VMEM((2, TOP_K, tm, D_MODEL), F32),
                            pltpu.SemaphoreType.DMA((2,))]),
        out_shape=jax.ShapeDtypeStruct((t, D_MODEL), F32),
        compiler_params=_cparams(("arbitrary",)),
        name="moe_combine",
    )(pos_flat, ys, gates, x1, pe, g, b, wple_bf16, wpg_bf16)


def kernel(x_prompt, x_sample, cache_k, cache_v, state_shift, state_wkv, p_prompt, p_sample,
           ln_emb_g, ln_emb_b, w_in, attn_sinks, rwkv_mu, rwkv_w0, rwkv_w_lora_up, rwkv_a0,
           rwkv_a_lora_up, rwkv_g_lora_up, rwkv_k_k, rwkv_k_a, rwkv_r_k, rwkv_gn_g, rwkv_gn_b,
           w_out, ln1_g, ln1_b, w_router, b_router, w_gate_up, b_gate_up, w_down, b_down,
           ln2_g, ln2_b, w_ple, w_ple_gate):
    assert w_in.shape[0] == DEPTH == 1
    nbp, seq, _ = x_prompt.shape
    nbs, dec_seq, _ = x_sample.shape
    assert dec_seq == 1
    tp, ts = nbp * seq, nbs
    w_buf = cache_k.shape[2]
    rowv = lambda z: z.reshape(1, -1)

    w_in_b = w_in[0].astype(BF16)
    w_out_b = w_out[0].astype(BF16)
    w_ple_b = w_ple[0].astype(BF16)
    w_pg_b = w_ple_gate[0].astype(BF16)
    sinks = attn_sinks[0]
    prm = dict(mu=rowv(rwkv_mu[0]), w0=rowv(rwkv_w0[0]), wlu=rwkv_w_lora_up[0],
               a0=rowv(rwkv_a0[0]), alu=rwkv_a_lora_up[0], glu=rwkv_g_lora_up[0],
               k_k=rowv(rwkv_k_k[0]), k_a=rowv(rwkv_k_a[0]), r_k=rowv(rwkv_r_k[0]),
               gn_g=rowv(rwkv_gn_g[0]), gn_b=rowv(rwkv_gn_b[0]))
    ge, be = rowv(ln_emb_g), rowv(ln_emb_b)

    xn_p, q_p, kv_p, rp_p = _inproj(x_prompt.reshape(tp, D_MODEL), ge, be, w_in_b, 512)
    att_p = _swa_prompt(sinks, q_p, kv_p, nbp, seq)
    rw_p, wkv_p = _rwkv_prompt(rp_p, prm, nbp, seq, 256)

    xn_s, q_s, kv_s, rp_s = _inproj(x_sample.reshape(ts, D_MODEL), ge, be, w_in_b, ts)
    ck3 = cache_k[0].reshape(nbs, w_buf, KV_COLS)
    cv3 = cache_v[0].reshape(nbs, w_buf, KV_COLS)
    att_s = _swa_sample(sinks, q_s.reshape(nbs, ATT_HEADS, HEAD_DIM),
                        kv_s.reshape(nbs, 2 * KV_HEADS, HEAD_DIM), ck3, cv3)
    att_s = att_s.reshape(ts, ATT_WIDTH)
    p7 = _rwkv_sample_prep(rp_s, state_shift[0], prm)
    hv = lambda z: z.reshape(RWKV_HEADS, HEAD_DIM)
    wkv_s, rw_s = _rwkv_sample_step(p7.reshape(7, nbs, RWKV_HEADS, HEAD_DIM), state_wkv[0],
                                    hv(rwkv_r_k[0]), hv(rwkv_gn_g[0]), hv(rwkv_gn_b[0]))
    rw_s = rw_s.reshape(ts, RWKV_WIDTH)

    wr_pad = jnp.zeros((D_MODEL, LANES), F32).at[:, :N_EXPERTS].set(w_router[0])
    wr_hi = wr_pad.astype(BF16)
    wr_lo = (wr_pad - wr_hi.astype(F32)).astype(BF16)
    br_pad = jnp.full((1, LANES), -jnp.inf, F32).at[0, :N_EXPERTS].set(b_router[0])
    g1, b1 = rowv(ln1_g[0]), rowv(ln1_b[0])
    tmp, tmd = 512, 256
    x1_p, info_p, gate_p, cnt_p = _mix_router(att_p, rw_p, xn_p, w_out_b, g1, b1, wr_hi, wr_lo,
                                              br_pad, jnp.zeros((1, LANES), F32), tmp)
    x1_s, info_s, gate_s, cnt = _mix_router(att_s, rw_s, xn_s, w_out_b, g1, b1, wr_hi, wr_lo,
                                            br_pad, cnt_p, ts)

    counts = cnt[0, :N_EXPERTS].astype(jnp.int32)
    padded = ((counts + EXPERT_TILE - 1) // EXPERT_TILE) * EXPERT_TILE
    ends = jnp.cumsum(padded)
    offs = ends - padded
    n_slots = (tp + ts) * TOP_K + N_EXPERTS * EXPERT_TILE
    n_tiles = n_slots // EXPERT_TILE
    n_used = (ends[-1] // EXPERT_TILE).reshape(1).astype(jnp.int32)
    tile_start = jnp.arange(n_tiles, dtype=jnp.int32) * EXPERT_TILE
    tile_e = jnp.sum(ends[None, :] <= tile_start[:, None], axis=1).astype(jnp.int32)
    last_e = jnp.max(jnp.where(padded > 0, jnp.arange(N_EXPERTS), 0)).astype(jnp.int32)
    tile_e = jnp.minimum(tile_e, last_e)

    def slots(info):
        return (offs[info[:, 0:TOP_K]] + info[:, TOP_K:2 * TOP_K]).reshape(-1).astype(jnp.int32)

    pos_p, pos_s = slots(info_p), slots(info_s)

    fill_start = jnp.concatenate([jnp.clip(ends - EXPERT_TILE, 0, n_slots - EXPERT_TILE),
                                  n_used]).astype(jnp.int32)
    xs = _dispatch(pos_p, pos_s, fill_start, x1_p, x1_s, n_slots, tmd)
    bgu = b_gate_up[0]
    bg = bgu[:, 0::2].reshape(N_EXPERTS, 1, D_FF)
    bu = bgu[:, 1::2].reshape(N_EXPERTS, 1, D_FF)
    bdn = b_down[0].reshape(N_EXPERTS, 1, D_MODEL)
    ys = _experts(tile_e, n_used, xs, w_gate_up[0], w_down[0], bg, bu, bdn)

    g2, b2 = rowv(ln2_g[0]), rowv(ln2_b[0])
    y_p = _combine(pos_p, ys, gate_p, x1_p, p_prompt[0].reshape(tp, PLE_DIM), g2, b2,
                   w_ple_b, w_pg_b, tmd)
    y_s = _combine(pos_s, ys, gate_s, x1_s, p_sample[0].reshape(ts, PLE_DIM), g2, b2,
                   w_ple_b, w_pg_b, ts)

    w_keep = min(WINDOW, seq)
    kv_p3 = kv_p.reshape(nbp, seq, 2 * KV_COLS)[:, seq - w_keep:]
    k_win_p = kv_p3[:, :, 0:KV_COLS].reshape(1, nbp, w_keep, KV_HEADS, HEAD_DIM)
    v_win_p = kv_p3[:, :, KV_COLS:].reshape(1, nbp, w_keep, KV_HEADS, HEAD_DIM)
    shift_p = rp_p.reshape(nbp, seq, RWKV_PROJ)[:, seq - 1][None]
    k_new = kv_s[:, 0:KV_COLS].reshape(nbs, 1, KV_HEADS, HEAD_DIM)
    v_new = kv_s[:, KV_COLS:].reshape(nbs, 1, KV_HEADS, HEAD_DIM)
    k_win_s = jnp.concatenate([cache_k[0], k_new], axis=1)[:, 1:][None]
    v_win_s = jnp.concatenate([cache_v[0], v_new], axis=1)[:, 1:][None]
    return (y_p.reshape(nbp, seq, D_MODEL), y_s.reshape(nbs, 1, D_MODEL),
            k_win_p, v_win_p, shift_p, wkv_p[None],
            k_win_s, v_win_s, rp_s[None], wkv_s[None])
```

```python
import functools

import jax
import jax.numpy as jnp
from jax import lax
from jax.experimental import pallas as pl
from jax.experimental.pallas import tpu as pltpu

F32 = jnp.float32
BF16 = jnp.bfloat16
HIGHEST = lax.Precision.HIGHEST

D_MODEL = 1024
HEAD_DIM = 64
ATT_HEADS = 8
KV_HEADS = 2
GROUP = ATT_HEADS // KV_HEADS
ATT_WIDTH = ATT_HEADS * HEAD_DIM
KV_COLS = KV_HEADS * HEAD_DIM
RWKV_HEADS = 8
RWKV_WIDTH = RWKV_HEADS * HEAD_DIM
LORA_W = 64
LORA_A = 64
LORA_G = 128
RWKV_PROJ = 3 * RWKV_WIDTH + LORA_W + LORA_A + LORA_G
WINDOW = 128
BLOCK = 128
PLE_DIM = 256
N_EXPERTS = 32
TOP_K = 4
D_FF = 1024
SWIGLU_LIMIT = 7.0
SWIGLU_ALPHA = 1.702
LN_EPS = 1e-5
GN_EPS = 64e-5
DEPTH = 1
DEEPNORM_ALPHA = (2 * DEPTH) ** 0.25

LANES = 128
CHUNK = 64
HEAD_QUAD = 4
EXPERT_TILE = 512
ROW_DMA_UNROLL = 8
VMEM_LIMIT = 48 * 1024 * 1024
EXPERTS_VMEM_LIMIT = 58 * 1024 * 1024


def _cparams(sem):
    return pltpu.CompilerParams(dimension_semantics=sem, vmem_limit_bytes=VMEM_LIMIT)


def _bdot(a, b):
    return jnp.dot(a.astype(BF16), b.astype(BF16), preferred_element_type=F32)


def _bdot_nt(a, b):
    return lax.dot_general(a.astype(BF16), b.astype(BF16), (((1,), (1,)), ((), ())),
                           preferred_element_type=F32)


def _bdot_tn(a, b):
    return lax.dot_general(a.astype(BF16), b.astype(BF16), (((0,), (0,)), ((), ())),
                           preferred_element_type=F32)


def _split_dot(m_bf16, x, passes):
    acc = None
    rem = x
    for _ in range(passes):
        hi = rem.astype(BF16)
        part = jnp.dot(m_bf16, hi, preferred_element_type=F32)
        acc = part if acc is None else acc + part
        rem = rem - hi.astype(F32)
    return acc


def _head_sum(x, bd_bf16):
    acc = None
    rem = x
    for _ in range(2):
        hi = rem.astype(BF16)
        part = jnp.dot(hi, bd_bf16, preferred_element_type=F32)
        acc = part if acc is None else acc + part
        rem = rem - hi.astype(F32)
    return acc


def _sigmoid(x):
    return 1.0 / (1.0 + jnp.exp(-x))


def _layer_norm(x, g, b):
    mu = jnp.mean(x, axis=-1, keepdims=True)
    xc = x - mu
    var = jnp.mean(xc * xc, axis=-1, keepdims=True)
    return xc * lax.rsqrt(var + LN_EPS) * g + b


def _inproj_body(x_ref, g_ref, b_ref, w_ref, xn_ref, q_ref, kv_ref, rp_ref):
    xn = _layer_norm(x_ref[...], g_ref[...], b_ref[...])
    xn_ref[...] = xn
    xb = xn.astype(BF16)
    q_ref[...] = jnp.dot(xb, w_ref[:, 0:ATT_WIDTH], preferred_element_type=F32)
    kv_ref[...] = jnp.dot(xb, w_ref[:, ATT_WIDTH:ATT_WIDTH + 2 * KV_COLS],
                          preferred_element_type=F32)
    rp_ref[...] = jnp.dot(xb, w_ref[:, ATT_WIDTH + 2 * KV_COLS:], preferred_element_type=F32)


def _inproj(x2d, g, b, w_bf16, tm):
    t = x2d.shape[0]
    in_proj = w_bf16.shape[1]
    row = lambda i: (i, 0)
    fixed = lambda i: (0, 0)
    return pl.pallas_call(
        _inproj_body,
        grid=(t // tm,),
        in_specs=[pl.BlockSpec((tm, D_MODEL), row),
                  pl.BlockSpec((1, D_MODEL), fixed),
                  pl.BlockSpec((1, D_MODEL), fixed),
                  pl.BlockSpec((D_MODEL, in_proj), fixed)],
        out_specs=[pl.BlockSpec((tm, D_MODEL), row),
                   pl.BlockSpec((tm, ATT_WIDTH), row),
                   pl.BlockSpec((tm, 2 * KV_COLS), row),
                   pl.BlockSpec((tm, RWKV_PROJ), row)],
        out_shape=[jax.ShapeDtypeStruct((t, D_MODEL), F32),
                   jax.ShapeDtypeStruct((t, ATT_WIDTH), F32),
                   jax.ShapeDtypeStruct((t, 2 * KV_COLS), F32),
                   jax.ShapeDtypeStruct((t, RWKV_PROJ), F32)],
        compiler_params=_cparams(("parallel",)),
        name="inproj",
    )(x2d, g, b, w_bf16)


def _alibi_slope(h):
    return 2.0 ** (-8.0 * (h + 1) / ATT_HEADS)


def _swa_prompt_body(sink_ref, q_ref, kvc_ref, kvp_ref, o_ref):
    n = pl.program_id(1)
    q = (q_ref[...] * (HEAD_DIM ** -0.5)).astype(BF16)
    kvc = kvc_ref[...].astype(BF16)
    kvp = kvp_ref[...].astype(BF16)
    row = lax.broadcasted_iota(jnp.int32, (BLOCK, 2 * BLOCK), 0)
    col = lax.broadcasted_iota(jnp.int32, (BLOCK, 2 * BLOCK), 1)
    dist = row + BLOCK - col
    valid = (dist >= 0) & (dist <= WINDOW) & ((col >= BLOCK) | (n > 0))
    distf = dist.astype(F32)
    kbands, vbands = [], []
    for g in range(KV_HEADS):
        ks = slice(g * HEAD_DIM, (g + 1) * HEAD_DIM)
        vs = slice(KV_COLS + g * HEAD_DIM, KV_COLS + (g + 1) * HEAD_DIM)
        kbands.append(jnp.concatenate([kvp[:, ks], kvc[:, ks]], axis=0))
        vbands.append(jnp.concatenate([kvp[:, vs], kvc[:, vs]], axis=0))
    scores = [lax.dot_general(q[:, h * HEAD_DIM:(h + 1) * HEAD_DIM], kbands[h // GROUP],
                              (((1,), (1,)), ((), ())), preferred_element_type=F32)
              for h in range(ATT_HEADS)]
    probs = []
    for h in range(ATT_HEADS):
        s = jnp.where(valid, scores[h] - _alibi_slope(h) * distf, -jnp.inf)
        sink = sink_ref[h]
        m = jnp.maximum(jnp.max(s, axis=1, keepdims=True), sink)
        e = jnp.exp(s - m)
        den = jnp.sum(e, axis=1, keepdims=True) + jnp.exp(sink - m)
        probs.append((e / den).astype(BF16))
    outs = [jnp.dot(probs[h], vbands[h // GROUP], preferred_element_type=F32)
            for h in range(ATT_HEADS)]
    o_ref[...] = jnp.concatenate(outs, axis=1)


def _swa_prompt(sinks, q, kv, nbatch, seq):
    nb = seq // BLOCK
    cur = lambda b, n: (b * nb + n, 0)
    prv = lambda b, n: (b * nb + jnp.maximum(n - 1, 0), 0)
    return pl.pallas_call(
        _swa_prompt_body,
        grid=(nbatch, nb),
        in_specs=[pl.BlockSpec(memory_space=pltpu.SMEM),
                  pl.BlockSpec((BLOCK, ATT_WIDTH), cur),
                  pl.BlockSpec((BLOCK, 2 * KV_COLS), cur),
                  pl.BlockSpec((BLOCK, 2 * KV_COLS), prv)],
        out_specs=pl.BlockSpec((BLOCK, ATT_WIDTH), cur),
        out_shape=jax.ShapeDtypeStruct((nbatch * seq, ATT_WIDTH), F32),
        compiler_params=_cparams(("parallel", "parallel")),
        name="swa_prompt",
    )(sinks, q, kv, kv)


def _swa_sample_body(sink_ref, q_ref, kvn_ref, ck_ref, cv_ref, o_ref, *, bb, w_buf):
    hrow = lax.broadcasted_iota(jnp.int32, (ATT_HEADS, 1), 0)
    slope = jnp.zeros((ATT_HEADS, 1), F32)
    sink = jnp.zeros((ATT_HEADS, 1), F32)
    for h in range(ATT_HEADS):
        slope = jnp.where(hrow == h, _alibi_slope(h), slope)
        sink = jnp.where(hrow == h, sink_ref[h], sink)
    jcol = lax.broadcasted_iota(jnp.int32, (ATT_HEADS, w_buf), 1)
    bias = -slope * (w_buf - jcol).astype(F32)
    lower = hrow < GROUP
    q2b = [(q_ref[b] * (HEAD_DIM ** -0.5)).astype(BF16) for b in range(bb)]
    s01 = []
    for b in range(bb):
        ck = ck_ref[b].astype(BF16)
        s01.append((_bdot_nt(q2b[b], ck[:, 0:HEAD_DIM]),
                    _bdot_nt(q2b[b], ck[:, HEAD_DIM:2 * HEAD_DIM])))
    ps = []
    for b in range(bb):
        kvn = kvn_ref[b]
        s = jnp.where(lower, s01[b][0], s01[b][1]) + bias
        knew = jnp.where(lower, kvn[0:1, :], kvn[1:2, :])
        vnew = jnp.where(lower, kvn[2:3, :], kvn[3:4, :])
        snew = jnp.sum(q2b[b].astype(F32) * knew.astype(BF16).astype(F32), axis=1, keepdims=True)
        m = jnp.maximum(jnp.maximum(jnp.max(s, axis=1, keepdims=True), snew), sink)
        e = jnp.exp(s - m)
        enew = jnp.exp(snew - m)
        den = jnp.sum(e, axis=1, keepdims=True) + enew + jnp.exp(sink - m)
        pnew = (enew / den).astype(BF16).astype(F32)
        ps.append(((e / den).astype(BF16), pnew * vnew.astype(BF16).astype(F32)))
    for b in range(bb):
        cv = cv_ref[b].astype(BF16)
        o0 = jnp.dot(ps[b][0], cv[:, 0:HEAD_DIM], preferred_element_type=F32)
        o1 = jnp.dot(ps[b][0], cv[:, HEAD_DIM:2 * HEAD_DIM], preferred_element_type=F32)
        o_ref[b] = jnp.where(lower, o0, o1) + ps[b][1]


def _swa_sample(sinks, q3, kvn3, ck3, cv3, bb=8):
    nb, w_buf = ck3.shape[0], ck3.shape[1]
    blk = lambda i: (i, 0, 0)
    return pl.pallas_call(
        functools.partial(_swa_sample_body, bb=bb, w_buf=w_buf),
        grid=(nb // bb,),
        in_specs=[pl.BlockSpec(memory_space=pltpu.SMEM),
                  pl.BlockSpec((bb, ATT_HEADS, HEAD_DIM), blk),
                  pl.BlockSpec((bb, 2 * KV_HEADS, HEAD_DIM), blk),
                  pl.BlockSpec((bb, w_buf, KV_COLS), blk),
                  pl.BlockSpec((bb, w_buf, KV_COLS), blk)],
        out_specs=pl.BlockSpec((bb, ATT_HEADS, HEAD_DIM), blk),
        out_shape=jax.ShapeDtypeStruct((nb, ATT_HEADS, HEAD_DIM), F32),
        compiler_params=_cparams(("parallel",)),
        name="swa_sample",
    )(sinks, q3, kvn3, ck3, cv3)


def _rwkv_prep(rp, prev, mu, w0, wlu, a0, alu, glu, k_k, k_a, bd):
    xs = rp + mu * (prev - rp)
    r = xs[:, 0:RWKV_WIDTH]
    k = xs[:, RWKV_WIDTH:2 * RWKV_WIDTH]
    v = xs[:, 2 * RWKV_WIDTH:3 * RWKV_WIDTH]
    o = 3 * RWKV_WIDTH
    wd = xs[:, o:o + LORA_W]
    ad = xs[:, o + LORA_W:o + LORA_W + LORA_A]
    gd = xs[:, o + LORA_W + LORA_A:]
    z = -(w0 + _bdot(jnp.tanh(wd), wlu))
    softplus = jnp.maximum(z, 0.0) + jnp.log(1.0 + jnp.exp(-jnp.abs(z)))
    log_decay = -jnp.exp(-softplus - 0.5)
    a = _sigmoid(a0 + _bdot(ad, alu))
    g = _bdot(_sigmoid(gd), glu)
    kkr = k * k_k
    kk = kkr / jnp.maximum(jnp.sqrt(_head_sum(kkr * kkr, bd)), 1e-12)
    k2 = k * (1.0 + (a - 1.0) * k_a)
    return r, log_decay, k2, v, kk, a, g


def _rwkv_finish(y, r, k2, v, g, r_k, gn_g, gn_b, bd):
    inv = 1.0 / HEAD_DIM
    mu = _head_sum(y, bd) * inv
    yc = y - mu
    var = _head_sum(yc * yc, bd) * inv
    yn = yc * lax.rsqrt(var + GN_EPS) * gn_g + gn_b
    bonus = _head_sum(r * k2 * r_k, bd) * v
    return (yn + bonus) * g


def _rwkv_prompt_body(rp_ref, mu_ref, w0_ref, wlu_ref, a0_ref, alu_ref, glu_ref, kk_ref, ka_ref,
                      rk_ref, gng_ref, gnb_ref, bd_ref, ltri_ref,
                      out_ref, sfin_ref, prev_scr, st_scr, y_scr, *, tt):
    c = pl.program_id(1)

    @pl.when(c == 0)
    def _():
        prev_scr[...] = jnp.zeros_like(prev_scr)
        st_scr[...] = jnp.zeros_like(st_scr)

    rp = rp_ref[...]
    rolled = pltpu.roll(rp, 1, 0)
    rowi = lax.broadcasted_iota(jnp.int32, (tt, 1), 0)
    prev = jnp.where(rowi == 0, prev_scr[...], rolled)
    prev_scr[...] = rp[tt - 1:tt, :]
    bd = bd_ref[...]
    r, ld, k2, v, kk, a, g = _rwkv_prep(rp, prev, mu_ref[...], w0_ref[...], wlu_ref[...],
                                        a0_ref[...], alu_ref[...], glu_ref[...], kk_ref[...],
                                        ka_ref[...], bd)
    cum = _split_dot(ltri_ref[...], ld, 2)
    ecum = jnp.exp(cum)
    einv = jnp.exp(-cum)
    at = -kk * jnp.exp(cum - ld)
    bh = kk * a * einv
    kh = k2 * einv
    rt = r * ecum

    qw = HEAD_QUAD * HEAD_DIM
    ii = lax.broadcasted_iota(jnp.int32, (CHUNK, qw), 0)
    jl = lax.broadcasted_iota(jnp.int32, (CHUNK, qw), 1) % CHUNK
    strict = ii > jl
    incl = ii >= jl
    eye = (ii == jl).astype(F32)
    same_head = (lax.broadcasted_iota(jnp.int32, (qw, qw), 0) // HEAD_DIM
                 == lax.broadcasted_iota(jnp.int32, (qw, qw), 1) // HEAD_DIM)

    def blockdiag(m):
        mb = m.astype(BF16)
        return jnp.where(same_head, jnp.concatenate([mb] * HEAD_QUAD, axis=0), jnp.zeros((), BF16))

    def diag_blocks(full):
        out = None
        for hh in range(HEAD_QUAD):
            rs = slice(hh * HEAD_DIM, (hh + 1) * HEAD_DIM)
            blk = jnp.where(same_head[rs], full[rs], 0.0)
            out = blk if out is None else out + blk
        return out

    mm = lambda x, w_bf16: jnp.dot(x.astype(BF16), w_bf16, preferred_element_type=F32)
    mm_nt = lambda x, w_bf16: lax.dot_general(x.astype(BF16), w_bf16, (((1,), (1,)), ((), ())),
                                              preferred_element_type=F32)
    nsub = tt // CHUNK
    units = [(s, q) for s in range(nsub) for q in range(RWKV_HEADS // HEAD_QUAD)]
    rows = lambda s: slice(s * CHUNK, (s + 1) * CHUNK)
    cols = lambda q: slice(q * qw, (q + 1) * qw)
    cut = lambda z, u: z[rows(u[0]), cols(u[1])]
    kka = kk * a
    p_last, kh_end, bh_end = [], [], []
    for s in range(nsub):
        c_last = cum[(s + 1) * CHUNK - 1:(s + 1) * CHUNK, :]
        p_last.append(jnp.exp(c_last))
        tail = jnp.exp(c_last - cum[rows(s)])
        kh_end.append(k2[rows(s)] * tail)
        bh_end.append(kka[rows(s)] * tail)
    x2 = {u: jnp.concatenate([cut(at, u), cut(rt, u)], axis=0) for u in units}
    xb = {u: mm_nt(x2[u], blockdiag(cut(bh, u))) for u in units}
    xk = {u: mm_nt(x2[u], blockdiag(cut(kh, u))) for u in units}
    nmat = {u: jnp.where(strict, xb[u][0:CHUNK], 0.0) for u in units}
    a_rb = {u: jnp.where(incl, xb[u][CHUNK:], 0.0) for u in units}
    a_ak = {u: jnp.where(strict, xk[u][0:CHUNK], 0.0) for u in units}
    a_rk = {u: jnp.where(incl, xk[u][CHUNK:], 0.0) for u in units}
    tmat = {u: eye + nmat[u] for u in units}
    npow = nmat
    for _ in range(5):
        npow = {u: mm(npow[u], blockdiag(npow[u])) for u in units}
        tmat = {u: tmat[u] + mm(npow[u], blockdiag(tmat[u])) for u in units}
    vbd = {u: blockdiag(cut(v, u)) for u in units}
    av = {u: mm(a_ak[u], vbd[u]) for u in units}
    aprime = {u: mm(tmat[u], blockdiag(cut(at, u))) for u in units}
    wmat = {u: mm(tmat[u], blockdiag(av[u])) for u in units}
    rprime = {u: cut(rt, u) + mm(a_rb[u], blockdiag(aprime[u])) for u in units}
    y0 = {u: mm(a_rk[u], vbd[u]) + mm(a_rb[u], blockdiag(wmat[u])) for u in units}
    hmat = {u: diag_blocks(_bdot_tn(jnp.concatenate([cut(v, u), wmat[u]], axis=0),
                                    jnp.concatenate([cut(kh_end[u[0]], (0, u[1])),
                                                     cut(bh_end[u[0]], (0, u[1]))], axis=0)))
            for u in units}
    g0 = {u: diag_blocks(_bdot_tn(aprime[u], cut(bh_end[u[0]], (0, u[1])))) for u in units}

    states = [st_scr[q] for q in range(RWKV_HEADS // HEAD_QUAD)]
    for s in range(nsub):
        for q in range(RWKV_HEADS // HEAD_QUAD):
            u = (s, q)
            st = states[q]
            y_scr[rows(s), cols(q)] = mm_nt(rprime[u], blockdiag(st)) + y0[u]
            states[q] = st * p_last[s][:, cols(q)] + mm(st, blockdiag(g0[u])) + hmat[u]
    for q in range(RWKV_HEADS // HEAD_QUAD):
        st_scr[q] = states[q]
        for hh in range(HEAD_QUAD):
            sfin_ref[0, q * HEAD_QUAD + hh] = states[q][:, hh * HEAD_DIM:(hh + 1) * HEAD_DIM]
    out_ref[...] = _rwkv_finish(y_scr[...], r, k2, v, g, rk_ref[...], gng_ref[...], gnb_ref[...], bd)


def _chunk_tril(tt):
    i = jnp.arange(tt)
    same = (i[:, None] // CHUNK) == (i[None, :] // CHUNK)
    return (same & (i[:, None] >= i[None, :])).astype(BF16)


def _head_blockdiag():
    i = jnp.arange(RWKV_WIDTH) // HEAD_DIM
    return (i[:, None] == i[None, :]).astype(BF16)


def _rwkv_prompt(rp, prm, nbatch, seq, tt):
    nc = seq // tt
    fixed = lambda b, c: (0, 0)
    row = lambda b, c: (b * nc + c, 0)
    vec = lambda n: pl.BlockSpec((1, n), fixed)
    return pl.pallas_call(
        functools.partial(_rwkv_prompt_body, tt=tt),
        grid=(nbatch, nc),
        in_specs=[pl.BlockSpec((tt, RWKV_PROJ), row),
                  vec(RWKV_PROJ), vec(RWKV_WIDTH),
                  pl.BlockSpec((LORA_W, RWKV_WIDTH), fixed), vec(RWKV_WIDTH),
                  pl.BlockSpec((LORA_A, RWKV_WIDTH), fixed),
                  pl.BlockSpec((LORA_G, RWKV_WIDTH), fixed),
                  vec(RWKV_WIDTH), vec(RWKV_WIDTH), vec(RWKV_WIDTH), vec(RWKV_WIDTH),
                  vec(RWKV_WIDTH),
                  pl.BlockSpec((RWKV_WIDTH, RWKV_WIDTH), fixed),
                  pl.BlockSpec((tt, tt), fixed)],
        out_specs=[pl.BlockSpec((tt, RWKV_WIDTH), row),
                   pl.BlockSpec((1, RWKV_HEADS, HEAD_DIM, HEAD_DIM), lambda b, c: (b, 0, 0, 0))],
        out_shape=[jax.ShapeDtypeStruct((nbatch * seq, RWKV_WIDTH), F32),
                   jax.ShapeDtypeStruct((nbatch, RWKV_HEADS, HEAD_DIM, HEAD_DIM), F32)],
        scratch_shapes=[pltpu.VMEM((1, RWKV_PROJ), F32),
                        pltpu.VMEM((RWKV_HEADS // HEAD_QUAD, HEAD_DIM, HEAD_QUAD * HEAD_DIM), F32),
                        pltpu.VMEM((tt, RWKV_WIDTH), F32)],
        compiler_params=_cparams(("parallel", "arbitrary")),
        name="rwkv_prompt",
    )(rp, prm["mu"], prm["w0"], prm["wlu"], prm["a0"], prm["alu"], prm["glu"], prm["k_k"],
      prm["k_a"], prm["r_k"], prm["gn_g"], prm["gn_b"], _head_blockdiag(), _chunk_tril(tt))


def _rwkv_sample_prep_body(rp_ref, prev_ref, mu_ref, w0_ref, wlu_ref, a0_ref, alu_ref, glu_ref,
                           kk_ref, ka_ref, bd_ref, o_ref):
    r, ld, k2, v, kk, a, g = _rwkv_prep(rp_ref[...], prev_ref[...], mu_ref[...], w0_ref[...],
                                        wlu_ref[...], a0_ref[...], alu_ref[...], glu_ref[...],
                                        kk_ref[...], ka_ref[...], bd_ref[...])
    for i, z in enumerate((r, jnp.exp(ld), k2, v, kk, a, g)):
        o_ref[i] = z


def _rwkv_sample_prep(rp, prev, prm):
    nb = rp.shape[0]
    return pl.pallas_call(
        _rwkv_sample_prep_body,
        out_shape=jax.ShapeDtypeStruct((7, nb, RWKV_WIDTH), F32),
        compiler_params=pltpu.CompilerParams(vmem_limit_bytes=VMEM_LIMIT),
        name="rwkv_sample_prep",
    )(rp, prev, prm["mu"], prm["w0"], prm["wlu"], prm["a0"], prm["alu"], prm["glu"], prm["k_k"],
      prm["k_a"], _head_blockdiag())


def _rwkv_sample_step_body(p_ref, s_ref, rk_ref, gng_ref, gnb_ref, so_ref, o_ref, *, bb):
    ii = lax.broadcasted_iota(jnp.int32, (HEAD_DIM, HEAD_DIM), 0)
    jj = lax.broadcasted_iota(jnp.int32, (HEAD_DIM, HEAD_DIM), 1)
    eye = ii == jj
    for b in range(bb):
        for h in range(RWKV_HEADS):
            row = lambda i: p_ref[i, b, h:h + 1, :]
            r_r, w_r, k_r, v_r, kk_r, a_r = (row(i) for i in range(6))
            st = s_ref[b, h]
            sa = jnp.sum(st * kk_r, axis=1, keepdims=True)
            v_col = jnp.sum(jnp.where(eye, v_r, 0.0), axis=1, keepdims=True)
            st = st * w_r - sa * (kk_r * a_r) + v_col * k_r
            so_ref[b, h] = st
            y_col = jnp.sum(st * r_r, axis=1, keepdims=True)
            o_ref[b, h:h + 1, :] = jnp.sum(jnp.where(eye, y_col, 0.0), axis=0, keepdims=True)
    y = o_ref[...]
    r, k2, v, g = p_ref[0], p_ref[2], p_ref[3], p_ref[6]
    mu = jnp.mean(y, axis=-1, keepdims=True)
    yc = y - mu
    var = jnp.mean(yc * yc, axis=-1, keepdims=True)
    yn = yc * lax.rsqrt(var + GN_EPS) * gng_ref[...] + gnb_ref[...]
    bonus = jnp.sum(r * k2 * rk_ref[...], axis=-1, keepdims=True) * v
    o_ref[...] = (yn + bonus) * g


def _rwkv_sample_step(p4, state, r_k, gn_g, gn_b, bb=8):
    nb = state.shape[0]
    hv = pl.BlockSpec((RWKV_HEADS, HEAD_DIM), lambda i: (0, 0))
    return pl.pallas_call(
        functools.partial(_rwkv_sample_step_body, bb=bb),
        grid=(nb // bb,),
        in_specs=[pl.BlockSpec((7, bb, RWKV_HEADS, HEAD_DIM), lambda i: (0, i, 0, 0)),
                  pl.BlockSpec((bb, RWKV_HEADS, HEAD_DIM, HEAD_DIM), lambda i: (i, 0, 0, 0)),
                  hv, hv, hv],
        out_specs=[pl.BlockSpec((bb, RWKV_HEADS, HEAD_DIM, HEAD_DIM), lambda i: (i, 0, 0, 0)),
                   pl.BlockSpec((bb, RWKV_HEADS, HEAD_DIM), lambda i: (i, 0, 0))],
        out_shape=[jax.ShapeDtypeStruct(state.shape, F32),
                   jax.ShapeDtypeStruct((nb, RWKV_HEADS, HEAD_DIM), F32)],
        compiler_params=_cparams(("parallel",)),
        name="rwkv_sample_step",
    )(p4, state, r_k, gn_g, gn_b)


def _mix_router_body(att_ref, rw_ref, xn_ref, wo_ref, g_ref, b_ref, wrh_ref, wrl_ref, br_ref,
                     base_ref, x1_ref, info_ref, gate_ref, cnt_ref, base_scr, *, tm):
    i = pl.program_id(0)

    @pl.when(i == 0)
    def _():
        base_scr[...] = base_ref[...]

    mixed = (jnp.dot(att_ref[...].astype(BF16), wo_ref[0:ATT_WIDTH, :], preferred_element_type=F32)
             + jnp.dot(rw_ref[...].astype(BF16), wo_ref[ATT_WIDTH:, :], preferred_element_type=F32))
    x1 = _layer_norm(DEEPNORM_ALPHA * xn_ref[...] + mixed, g_ref[...], b_ref[...])
    x1_ref[...] = x1

    x1h = x1.astype(BF16)
    x1l = (x1 - x1h.astype(F32)).astype(BF16)
    logits = (jnp.dot(x1h, wrh_ref[...], preferred_element_type=F32)
              + jnp.dot(x1l, wrh_ref[...], preferred_element_type=F32)
              + jnp.dot(x1h, wrl_ref[...], preferred_element_type=F32)) + br_ref[...]
    lane = lax.broadcasted_iota(jnp.int32, (tm, LANES), 1)
    lanef = lane.astype(F32)
    vals, idxs, hots = [], [], []
    cur = logits
    for _ in range(TOP_K):
        m = jnp.max(cur, axis=1, keepdims=True)
        idx = jnp.min(jnp.where(cur == m, lanef, float(LANES)), axis=1, keepdims=True)
        hot = lanef == idx
        cur = jnp.where(hot, -jnp.inf, cur)
        vals.append(m)
        idxs.append(idx)
        hots.append(hot)
    es = [jnp.exp(vk - vals[0]) for vk in vals]
    den = es[0] + es[1] + es[2] + es[3]
    multi = jnp.zeros((tm, LANES), F32)
    for hot in hots:
        multi = multi + hot.astype(F32)
    ti = lax.broadcasted_iota(jnp.int32, (tm, tm), 0)
    tj = lax.broadcasted_iota(jnp.int32, (tm, tm), 1)
    before = jnp.dot((ti > tj).astype(BF16), multi.astype(BF16), preferred_element_type=F32)
    before = before + base_scr[...]
    info = jnp.zeros((tm, LANES), jnp.int32)
    gates = jnp.zeros((tm, LANES), F32)
    for k in range(TOP_K):
        rank = jnp.sum(jnp.where(hots[k], before, 0.0), axis=1, keepdims=True)
        info = jnp.where(lane == k, idxs[k].astype(jnp.int32), info)
        info = jnp.where(lane == TOP_K + k, rank.astype(jnp.int32), info)
        gates = jnp.where(lane == k, es[k] / den, gates)
    info_ref[...] = info
    gate_ref[...] = gates
    base_scr[...] = base_scr[...] + jnp.sum(multi, axis=0, keepdims=True)
    cnt_ref[...] = base_scr[...]


def _mix_router(att, rw, xn, wo_bf16, g, b, wr_hi, wr_lo, br_pad, base, tm):
    t = att.shape[0]
    row = lambda i: (i, 0)
    fixed = lambda i: (0, 0)
    return pl.pallas_call(
        functools.partial(_mix_router_body, tm=tm),
        grid=(t // tm,),
        in_specs=[pl.BlockSpec((tm, ATT_WIDTH), row),
                  pl.BlockSpec((tm, RWKV_WIDTH), row),
                  pl.BlockSpec((tm, D_MODEL), row),
                  pl.BlockSpec((D_MODEL, D_MODEL), fixed),
                  pl.BlockSpec((1, D_MODEL), fixed),
                  pl.BlockSpec((1, D_MODEL), fixed),
                  pl.BlockSpec((D_MODEL, LANES), fixed),
                  pl.BlockSpec((D_MODEL, LANES), fixed),
                  pl.BlockSpec((1, LANES), fixed),
                  pl.BlockSpec((1, LANES), fixed)],
        out_specs=[pl.BlockSpec((tm, D_MODEL), row),
                   pl.BlockSpec((tm, LANES), row),
                   pl.BlockSpec((tm, LANES), row),
                   pl.BlockSpec((1, LANES), fixed)],
        out_shape=[jax.ShapeDtypeStruct((t, D_MODEL), F32),
                   jax.ShapeDtypeStruct((t, LANES), jnp.int32),
                   jax.ShapeDtypeStruct((t, LANES), F32),
                   jax.ShapeDtypeStruct((1, LANES), F32)],
        scratch_shapes=[pltpu.VMEM((1, LANES), F32)],
        compiler_params=_cparams(("arbitrary",)),
        name="mix_router",
    )(att, rw, xn, wo_bf16, g, b, wr_hi, wr_lo, br_pad, base)


def _row_copies(pos_ref, base, tm, make_copy):
    def group(gi, carry):
        r0 = pl.multiple_of(gi * ROW_DMA_UNROLL, ROW_DMA_UNROLL)
        p0 = (base + r0) * TOP_K
        for j in range(ROW_DMA_UNROLL):
            for k in range(TOP_K):
                make_copy(r0 + j, k, pos_ref[p0 + j * TOP_K + k]).start(priority=k % 2)
        return carry

    lax.fori_loop(0, tm // ROW_DMA_UNROLL, group, 0)


def _dispatch_rows(pos_ref, base, x_ref, xs_ref, sem, rows):
    def make_copy(r, k, p):
        return pltpu.make_async_copy(x_ref.at[pl.ds(r, 1)], xs_ref.at[pl.ds(p, 1)], sem)

    _row_copies(pos_ref, base, rows, make_copy)
    for _ in range(TOP_K):
        pltpu.make_async_copy(x_ref, xs_ref.at[pl.ds(0, rows)], sem).wait()


def _dispatch_body(pos_a_ref, pos_b_ref, fill_ref, xa_ref, xb_ref, xs_ref, zero_scr, sem, fill_sem,
                   *, tm, n_tiles):
    i = pl.program_id(0)
    last = pl.num_programs(0) - 1

    @pl.when(i == 0)
    def _():
        zero_scr[...] = jnp.zeros_like(zero_scr)

        def tile_fill(start):
            return pltpu.make_async_copy(
                zero_scr, xs_ref.at[pl.ds(pl.multiple_of(start, EXPERT_TILE), EXPERT_TILE)], fill_sem)

        fills = [tile_fill(fill_ref[e]) for e in range(N_EXPERTS)]
        for cp in fills:
            cp.start()
        first_unused = fill_ref[N_EXPERTS]
        lax.fori_loop(first_unused, n_tiles,
                      lambda t, c: (tile_fill(t * EXPERT_TILE).start(), c)[1], 0)
        for cp in fills:
            cp.wait()
        lax.fori_loop(first_unused, n_tiles,
                      lambda t, c: (tile_fill(t * EXPERT_TILE).wait(), c)[1], 0)

    @pl.when(i < last)
    def _():
        _dispatch_rows(pos_a_ref, i * tm, xa_ref, xs_ref, sem, tm)

    @pl.when(i == last)
    def _():
        _dispatch_rows(pos_b_ref, 0, xb_ref, xs_ref, sem, xb_ref.shape[0])


def _dispatch(pos_a, pos_b, fill_start, x_a, x_b, n_slots, tm):
    nta = x_a.shape[0] // tm
    return pl.pallas_call(
        functools.partial(_dispatch_body, tm=tm, n_tiles=n_slots // EXPERT_TILE),
        grid_spec=pltpu.PrefetchScalarGridSpec(
            num_scalar_prefetch=3,
            grid=(nta + 1,),
            in_specs=[pl.BlockSpec((tm, D_MODEL), lambda i, *_: (jnp.minimum(i, nta - 1), 0)),
                      pl.BlockSpec(x_b.shape, lambda i, *_: (0, 0))],
            out_specs=pl.BlockSpec(memory_space=pl.ANY),
            scratch_shapes=[pltpu.VMEM((EXPERT_TILE, D_MODEL), F32),
                            pltpu.SemaphoreType.DMA, pltpu.SemaphoreType.DMA]),
        out_shape=jax.ShapeDtypeStruct((n_slots, D_MODEL), F32),
        compiler_params=_cparams(("arbitrary",)),
        name="moe_dispatch",
    )(pos_a, pos_b, fill_start, x_a, x_b)


def _experts_body(te_ref, nu_ref, xs_ref, wgu_ref, wd_ref, bg_ref, bu_ref, bd_ref, sel_ref, ys_ref,
                  wg_scr, wu_scr, wd_scr):
    i = pl.program_id(0)
    new_expert = (i == 0) | (te_ref[i] != te_ref[jnp.maximum(i - 1, 0)])

    @pl.when(new_expert)
    def _():
        wd_scr[...] = wd_ref[...].astype(BF16)
        for m in range(D_FF // LANES):
            pair = wgu_ref[:, 2 * m * LANES:2 * (m + 1) * LANES].astype(BF16)
            split = jnp.dot(pair, sel_ref[...], preferred_element_type=F32)
            wg_scr[:, m * LANES:(m + 1) * LANES] = split[:, 0:LANES].astype(BF16)
            wu_scr[:, m * LANES:(m + 1) * LANES] = split[:, LANES:].astype(BF16)

    @pl.when(i < nu_ref[0])
    def _():
        x = xs_ref[...].astype(BF16)
        gate = jnp.dot(x, wg_scr[...], preferred_element_type=F32) + bg_ref[...]
        up = jnp.dot(x, wu_scr[...], preferred_element_type=F32) + bu_ref[...]
        gate = jnp.minimum(gate, SWIGLU_LIMIT)
        up = jnp.clip(up, -SWIGLU_LIMIT, SWIGLU_LIMIT)
        act = (up + 1.0) * gate * _sigmoid(SWIGLU_ALPHA * gate)
        ys_ref[...] = jnp.dot(act.astype(BF16), wd_scr[...], preferred_element_type=F32) + bd_ref[...]

    @pl.when(i >= nu_ref[0])
    def _():
        ys_ref[...] = jnp.zeros_like(ys_ref)


def _gate_up_selector():
    i = jnp.arange(2 * LANES)
    src = jnp.where(i < LANES, 2 * i, 2 * (i - LANES) + 1)
    return (i[:, None] == src[None, :]).astype(BF16)


def _experts(tile_e, n_used, xs, wgu, wd, bg, bu, bd):
    ns = xs.shape[0]
    tm = EXPERT_TILE
    wspec = lambda a, b: pl.BlockSpec((None, a, b), lambda i, te, nu: (te[i], 0, 0))
    return pl.pallas_call(
        _experts_body,
        grid_spec=pltpu.PrefetchScalarGridSpec(
            num_scalar_prefetch=2,
            grid=(ns // tm,),
            in_specs=[pl.BlockSpec((tm, D_MODEL), lambda i, te, nu: (jnp.minimum(i, nu[0] - 1), 0)),
                      wspec(D_MODEL, 2 * D_FF), wspec(D_FF, D_MODEL),
                      wspec(1, D_FF), wspec(1, D_FF), wspec(1, D_MODEL),
                      pl.BlockSpec((2 * LANES, 2 * LANES), lambda i, te, nu: (0, 0))],
            out_specs=pl.BlockSpec((tm, D_MODEL), lambda i, te, nu: (i, 0)),
            scratch_shapes=[pltpu.VMEM((D_MODEL, D_FF), BF16), pltpu.VMEM((D_MODEL, D_FF), BF16),
                            pltpu.VMEM((D_FF, D_MODEL), BF16)]),
        out_shape=jax.ShapeDtypeStruct((ns, D_MODEL), F32),
        compiler_params=pltpu.CompilerParams(dimension_semantics=("arbitrary",),
                                             vmem_limit_bytes=EXPERTS_VMEM_LIMIT),
        name="moe_experts",
    )(tile_e, n_used, xs, wgu, wd, bg, bu, bd, _gate_up_selector())


def _combine_body(pos_ref, ys_ref, gate_ref, x1_ref, pe_ref, g_ref, b_ref, wple_ref, wpg_ref,
                  o_ref, buf, sem, *, tm):
    i = pl.program_id(0)
    cur = i % 2

    def gather(tile, slot):
        def make_copy(r, k, p):
            return pltpu.make_async_copy(ys_ref.at[pl.ds(p, 1)], buf.at[slot, k, pl.ds(r, 1)],
                                         sem.at[slot])
        _row_copies(pos_ref, tile * tm, tm, make_copy)

    @pl.when(i == 0)
    def _():
        gather(0, 0)

    @pl.when(i + 1 < pl.num_programs(0))
    def _():
        gather(i + 1, 1 - cur)

    for k in range(TOP_K):
        pltpu.make_async_copy(ys_ref.at[pl.ds(0, tm)], buf.at[cur, k], sem.at[cur]).wait()
    gates = gate_ref[...]
    ffn = gates[:, 0:1] * buf[cur, 0]
    for k in range(1, TOP_K):
        ffn = ffn + gates[:, k:k + 1] * buf[cur, k]
    x2 = _layer_norm(DEEPNORM_ALPHA * x1_ref[...] + ffn, g_ref[...], b_ref[...])
    gate = _sigmoid(jnp.dot(x2.astype(BF16), wpg_ref[...], preferred_element_type=F32))
    emb = jnp.dot(pe_ref[...].astype(BF16), wple_ref[...], preferred_element_type=F32)
    o_ref[...] = x2 + gate * emb


def _combine(pos_flat, ys, gates, x1, pe, g, b, wple_bf16, wpg_bf16, tm):
    t = x1.shape[0]
    row = lambda i, *_: (i, 0)
    fixed = lambda i, *_: (0, 0)
    return pl.pallas_call(
        functools.partial(_combine_body, tm=tm),
        grid_spec=pltpu.PrefetchScalarGridSpec(
            num_scalar_prefetch=1,
            grid=(t // tm,),
            in_specs=[pl.BlockSpec(memory_space=pl.ANY),
                      pl.BlockSpec((tm, LANES), row),
                      pl.BlockSpec((tm, D_MODEL), row),
                      pl.BlockSpec((tm, PLE_DIM), row),
                      pl.BlockSpec((1, D_MODEL), fixed),
                      pl.BlockSpec((1, D_MODEL), fixed),
                      pl.BlockSpec((PLE_DIM, D_MODEL), fixed),
                      pl.BlockSpec((D_MODEL, D_MODEL), fixed)],
            out_specs=pl.BlockSpec((tm, D_MODEL), row),
            scratch_shapes=[pltpu.VMEM((2, TOP_K, tm, D_MODEL), F32),
                            pltpu.SemaphoreType.DMA((2,))]),
        out_shape=jax.ShapeDtypeStruct((t, D_MODEL), F32),
        compiler_params=_cparams(("arbitrary",)),
        name="moe_combine",
    )(pos_flat, ys, gates, x1, pe, g, b, wple_bf16, wpg_bf16)


def kernel(x_prompt, x_sample, cache_k, cache_v, state_shift, state_wkv, p_prompt, p_sample,
           ln_emb_g, ln_emb_b, w_in, attn_sinks, rwkv_mu, rwkv_w0, rwkv_w_lora_up, rwkv_a0,
           rwkv_a_lora_up, rwkv_g_lora_up, rwkv_k_k, rwkv_k_a, rwkv_r_k, rwkv_gn_g, rwkv_gn_b,
           w_out, ln1_g, ln1_b, w_router, b_router, w_gate_up, b_gate_up, w_down, b_down,
           ln2_g, ln2_b, w_ple, w_ple_gate):
    assert w_in.shape[0] == DEPTH == 1
    nbp, seq, _ = x_prompt.shape
    nbs, dec_seq, _ = x_sample.shape
    assert dec_seq == 1
    tp, ts = nbp * seq, nbs
    w_buf = cache_k.shape[2]
    rowv = lambda z: z.reshape(1, -1)

    w_in_b = w_in[0].astype(BF16)
    w_out_b = w_out[0].astype(BF16)
    w_ple_b = w_ple[0].astype(BF16)
    w_pg_b = w_ple_gate[0].astype(BF16)
    sinks = attn_sinks[0]
    prm = dict(mu=rowv(rwkv_mu[0]), w0=rowv(rwkv_w0[0]), wlu=rwkv_w_lora_up[0],
               a0=rowv(rwkv_a0[0]), alu=rwkv_a_lora_up[0], glu=rwkv_g_lora_up[0],
               k_k=rowv(rwkv_k_k[0]), k_a=rowv(rwkv_k_a[0]), r_k=rowv(rwkv_r_k[0]),
               gn_g=rowv(rwkv_gn_g[0]), gn_b=rowv(rwkv_gn_b[0]))
    ge, be = rowv(ln_emb_g), rowv(ln_emb_b)

    xn_p, q_p, kv_p, rp_p = _inproj(x_prompt.reshape(tp, D_MODEL), ge, be, w_in_b, 512)
    att_p = _swa_prompt(sinks, q_p, kv_p, nbp, seq)
    rw_p, wkv_p = _rwkv_prompt(rp_p, prm, nbp, seq, 256)

    xn_s, q_s, kv_s, rp_s = _inproj(x_sample.reshape(ts, D_MODEL), ge, be, w_in_b, ts)
    ck3 = cache_k[0].reshape(nbs, w_buf, KV_COLS)
    cv3 = cache_v[0].reshape(nbs, w_buf, KV_COLS)
    att_s = _swa_sample(sinks, q_s.reshape(nbs, ATT_HEADS, HEAD_DIM),
                        kv_s.reshape(nbs, 2 * KV_HEADS, HEAD_DIM), ck3, cv3)
    att_s = att_s.reshape(ts, ATT_WIDTH)
    p7 = _rwkv_sample_prep(rp_s, state_shift[0], prm)
    hv = lambda z: z.reshape(RWKV_HEADS, HEAD_DIM)
    wkv_s, rw_s = _rwkv_sample_step(p7.reshape(7, nbs, RWKV_HEADS, HEAD_DIM), state_wkv[0],
                                    hv(rwkv_r_k[0]), hv(rwkv_gn_g[0]), hv(rwkv_gn_b[0]))
    rw_s = rw_s.reshape(ts, RWKV_WIDTH)

    wr_pad = jnp.zeros((D_MODEL, LANES), F32).at[:, :N_EXPERTS].set(w_router[0])
    wr_hi = wr_pad.astype(BF16)
    wr_lo = (wr_pad - wr_hi.astype(F32)).astype(BF16)
    br_pad = jnp.full((1, LANES), -jnp.inf, F32).at[0, :N_EXPERTS].set(b_router[0])
    g1, b1 = rowv(ln1_g[0]), rowv(ln1_b[0])
    tmp, tmd = 512, 256
    x1_p, info_p, gate_p, cnt_p = _mix_router(att_p, rw_p, xn_p, w_out_b, g1, b1, wr_hi, wr_lo,
                                              br_pad, jnp.zeros((1, LANES), F32), tmp)
    x1_s, info_s, gate_s, cnt = _mix_router(att_s, rw_s, xn_s, w_out_b, g1, b1, wr_hi, wr_lo,
                                            br_pad, cnt_p, ts)

    counts = cnt[0, :N_EXPERTS].astype(jnp.int32)
    padded = ((counts + EXPERT_TILE - 1) // EXPERT_TILE) * EXPERT_TILE
    ends = jnp.cumsum(padded)
    offs = ends - padded
    n_slots = (tp + ts) * TOP_K + N_EXPERTS * EXPERT_TILE
    n_tiles = n_slots // EXPERT_TILE
    n_used = (ends[-1] // EXPERT_TILE).reshape(1).astype(jnp.int32)
    tile_start = jnp.arange(n_tiles, dtype=jnp.int32) * EXPERT_TILE
    tile_e = jnp.sum(ends[None, :] <= tile_start[:, None], axis=1).astype(jnp.int32)
    last_e = jnp.max(jnp.where(padded > 0, jnp.arange(N_EXPERTS), 0)).astype(jnp.int32)
    tile_e = jnp.minimum(tile_e, last_e)

    def slots(info):
        return (offs[info[:, 0:TOP_K]] + info[:, TOP_K:2 * TOP_K]).reshape(-1).astype(jnp.int32)

    pos_p, pos_s = slots(info_p), slots(info_s)

    fill_start = jnp.concatenate([jnp.clip(ends - EXPERT_TILE, 0, n_slots - EXPERT_TILE),
                                  n_used]).astype(jnp.int32)
    xs = _dispatch(pos_p, pos_s, fill_start, x1_p, x1_s, n_slots, tmd)
    bgu = b_gate_up[0]
    bg = bgu[:, 0::2].reshape(N_EXPERTS, 1, D_FF)
    bu = bgu[:, 1::2].reshape(N_EXPERTS, 1, D_FF)
    bdn = b_down[0].reshape(N_EXPERTS, 1, D_MODEL)
    ys = _experts(tile_e, n_used, xs, w_gate_up[0], w_down[0], bg, bu, bdn)

    g2, b2 = rowv(ln2_g[0]), rowv(ln2_b[0])
    y_p = _combine(pos_p, ys, gate_p, x1_p, p_prompt[0].reshape(tp, PLE_DIM), g2, b2,
                   w_ple_b, w_pg_b, tmd)
    y_s = _combine(pos_s, ys, gate_s, x1_s, p_sample[0].reshape(ts, PLE_DIM), g2, b2,
                   w_ple_b, w_pg_b, ts)

    w_keep = min(WINDOW, seq)
    kv_p3 = kv_p.reshape(nbp, seq, 2 * KV_COLS)[:, seq - w_keep:]
    k_win_p = kv_p3[:, :, 0:KV_COLS].reshape(1, nbp, w_keep, KV_HEADS, HEAD_DIM)
    v_win_p = kv_p3[:, :, KV_COLS:].reshape(1, nbp, w_keep, KV_HEADS, HEAD_DIM)
    shift_p = rp_p.reshape(nbp, seq, RWKV_PROJ)[:, seq - 1][None]
    k_new = kv_s[:, 0:KV_COLS].reshape(nbs, 1, KV_HEADS, HEAD_DIM)
    v_new = kv_s[:, KV_COLS:].reshape(nbs, 1, KV_HEADS, HEAD_DIM)
    k_win_s = jnp.concatenate([cache_k[0], k_new], axis=1)[:, 1:][None]
    v_win_s = jnp.concatenate([cache_v[0], v_new], axis=1)[:, 1:][None]
    return (y_p.reshape(nbp, seq, D_MODEL), y_s.reshape(nbs, 1, D_MODEL),
            k_win_p, v_win_p, shift_p, wkv_p[None],
            k_win_s, v_win_s, rp_s[None], wkv_s[None])
```

```python
import functools

import jax
import jax.numpy as jnp
from jax import lax
from jax.experimental import pallas as pl
from jax.experimental.pallas import tpu as pltpu

F32 = jnp.float32
BF16 = jnp.bfloat16
HIGHEST = lax.Precision.HIGHEST

D_MODEL = 1024
HEAD_DIM = 64
ATT_HEADS = 8
KV_HEADS = 2
GROUP = ATT_HEADS // KV_HEADS
ATT_WIDTH = ATT_HEADS * HEAD_DIM
KV_COLS = KV_HEADS * HEAD_DIM
RWKV_HEADS = 8
RWKV_WIDTH = RWKV_HEADS * HEAD_DIM
LORA_W = 64
LORA_A = 64
LORA_G = 128
RWKV_PROJ = 3 * RWKV_WIDTH + LORA_W + LORA_A + LORA_G
WINDOW = 128
BLOCK = 128
PLE_DIM = 256
N_EXPERTS = 32
TOP_K = 4
D_FF = 1024
SWIGLU_LIMIT = 7.0
SWIGLU_ALPHA = 1.702
LN_EPS = 1e-5
GN_EPS = 64e-5
DEPTH = 1
DEEPNORM_ALPHA = (2 * DEPTH) ** 0.25

LANES = 128
CHUNK = 64
HEAD_QUAD = 4
EXPERT_TILE = 512
ROW_DMA_UNROLL = 8
VMEM_LIMIT = 48 * 1024 * 1024
EXPERTS_VMEM_LIMIT = 58 * 1024 * 1024


def _cparams(sem):
    return pltpu.CompilerParams(dimension_semantics=sem, vmem_limit_bytes=VMEM_LIMIT)


def _bdot(a, b):
    return jnp.dot(a.astype(BF16), b.astype(BF16), preferred_element_type=F32)


def _bdot_nt(a, b):
    return lax.dot_general(a.astype(BF16), b.astype(BF16), (((1,), (1,)), ((), ())),
                           preferred_element_type=F32)


def _bdot_tn(a, b):
    return lax.dot_general(a.astype(BF16), b.astype(BF16), (((0,), (0,)), ((), ())),
                           preferred_element_type=F32)


def _split_dot(m_bf16, x, passes):
    acc = None
    rem = x
    for _ in range(passes):
        hi = rem.astype(BF16)
        part = jnp.dot(m_bf16, hi, preferred_element_type=F32)
        acc = part if acc is None else acc + part
        rem = rem - hi.astype(F32)
    return acc


def _head_sum(x, bd_bf16):
    acc = None
    rem = x
    for _ in range(2):
        hi = rem.astype(BF16)
        part = jnp.dot(hi, bd_bf16, preferred_element_type=F32)
        acc = part if acc is None else acc + part
        rem = rem - hi.astype(F32)
    return acc


def _sigmoid(x):
    return 1.0 / (1.0 + jnp.exp(-x))


def _layer_norm(x, g, b):
    mu = jnp.mean(x, axis=-1, keepdims=True)
    xc = x - mu
    var = jnp.mean(xc * xc, axis=-1, keepdims=True)
    return xc * lax.rsqrt(var + LN_EPS) * g + b


def _inproj_body(x_ref, g_ref, b_ref, w_ref, xn_ref, q_ref, kv_ref, rp_ref):
    xn = _layer_norm(x_ref[...], g_ref[...], b_ref[...])
    xn_ref[...] = xn
    xb = xn.astype(BF16)
    q_ref[...] = jnp.dot(xb, w_ref[:, 0:ATT_WIDTH], preferred_element_type=F32)
    kv_ref[...] = jnp.dot(xb, w_ref[:, ATT_WIDTH:ATT_WIDTH + 2 * KV_COLS],
                          preferred_element_type=F32)
    rp_ref[...] = jnp.dot(xb, w_ref[:, ATT_WIDTH + 2 * KV_COLS:], preferred_element_type=F32)


def _inproj(x2d, g, b, w_bf16, tm):
    t = x2d.shape[0]
    in_proj = w_bf16.shape[1]
    row = lambda i: (i, 0)
    fixed = lambda i: (0, 0)
    return pl.pallas_call(
        _inproj_body,
        grid=(t // tm,),
        in_specs=[pl.BlockSpec((tm, D_MODEL), row),
                  pl.BlockSpec((1, D_MODEL), fixed),
                  pl.BlockSpec((1, D_MODEL), fixed),
                  pl.BlockSpec((D_MODEL, in_proj), fixed)],
        out_specs=[pl.BlockSpec((tm, D_MODEL), row),
                   pl.BlockSpec((tm, ATT_WIDTH), row),
                   pl.BlockSpec((tm, 2 * KV_COLS), row),
                   pl.BlockSpec((tm, RWKV_PROJ), row)],
        out_shape=[jax.ShapeDtypeStruct((t, D_MODEL), F32),
                   jax.ShapeDtypeStruct((t, ATT_WIDTH), F32),
                   jax.ShapeDtypeStruct((t, 2 * KV_COLS), F32),
                   jax.ShapeDtypeStruct((t, RWKV_PROJ), F32)],
        compiler_params=_cparams(("parallel",)),
        name="inproj",
    )(x2d, g, b, w_bf16)


def _alibi_slope(h):
    return 2.0 ** (-8.0 * (h + 1) / ATT_HEADS)


def _swa_prompt_body(sink_ref, q_ref, kvc_ref, kvp_ref, o_ref):
    n = pl.program_id(1)
    q = (q_ref[...] * (HEAD_DIM ** -0.5)).astype(BF16)
    kvc = kvc_ref[...].astype(BF16)
    kvp = kvp_ref[...].astype(BF16)
    row = lax.broadcasted_iota(jnp.int32, (BLOCK, 2 * BLOCK), 0)
    col = lax.broadcasted_iota(jnp.int32, (BLOCK, 2 * BLOCK), 1)
    dist = row + BLOCK - col
    valid = (dist >= 0) & (dist <= WINDOW) & ((col >= BLOCK) | (n > 0))
    distf = dist.astype(F32)
    kbands, vbands = [], []
    for g in range(KV_HEADS):
        ks = slice(g * HEAD_DIM, (g + 1) * HEAD_DIM)
        vs = slice(KV_COLS + g * HEAD_DIM, KV_COLS + (g + 1) * HEAD_DIM)
        kbands.append(jnp.concatenate([kvp[:, ks], kvc[:, ks]], axis=0))
        vbands.append(jnp.concatenate([kvp[:, vs], kvc[:, vs]], axis=0))
    scores = [lax.dot_general(q[:, h * HEAD_DIM:(h + 1) * HEAD_DIM], kbands[h // GROUP],
                              (((1,), (1,)), ((), ())), preferred_element_type=F32)
              for h in range(ATT_HEADS)]
    probs = []
    for h in range(ATT_HEADS):
        s = jnp.where(valid, scores[h] - _alibi_slope(h) * distf, -jnp.inf)
        sink = sink_ref[h]
        m = jnp.maximum(jnp.max(s, axis=1, keepdims=True), sink)
        e = jnp.exp(s - m)
        den = jnp.sum(e, axis=1, keepdims=True) + jnp.exp(sink - m)
        probs.append((e / den).astype(BF16))
    outs = [jnp.dot(probs[h], vbands[h // GROUP], preferred_element_type=F32)
            for h in range(ATT_HEADS)]
    o_ref[...] = jnp.concatenate(outs, axis=1)


def _swa_prompt(sinks, q, kv, nbatch, seq):
    nb = seq // BLOCK
    cur = lambda b, n: (b * nb + n, 0)
    prv = lambda b, n: (b * nb + jnp.maximum(n - 1, 0), 0)
    return pl.pallas_call(
        _swa_prompt_body,
        grid=(nbatch, nb),
        in_specs=[pl.BlockSpec(memory_space=pltpu.SMEM),
                  pl.BlockSpec((BLOCK, ATT_WIDTH), cur),
                  pl.BlockSpec((BLOCK, 2 * KV_COLS), cur),
                  pl.BlockSpec((BLOCK, 2 * KV_COLS), prv)],
        out_specs=pl.BlockSpec((BLOCK, ATT_WIDTH), cur),
        out_shape=jax.ShapeDtypeStruct((nbatch * seq, ATT_WIDTH), F32),
        compiler_params=_cparams(("parallel", "parallel")),
        name="swa_prompt",
    )(sinks, q, kv, kv)


def _swa_sample_body(sink_ref, q_ref, kvn_ref, ck_ref, cv_ref, o_ref, *, bb, w_buf):
    hrow = lax.broadcasted_iota(jnp.int32, (ATT_HEADS, 1), 0)
    slope = jnp.zeros((ATT_HEADS, 1), F32)
    sink = jnp.zeros((ATT_HEADS, 1), F32)
    for h in range(ATT_HEADS):
        slope = jnp.where(hrow == h, _alibi_slope(h), slope)
        sink = jnp.where(hrow == h, sink_ref[h], sink)
    jcol = lax.broadcasted_iota(jnp.int32, (ATT_HEADS, w_buf), 1)
    bias = -slope * (w_buf - jcol).astype(F32)
    lower = hrow < GROUP
    q2b = [(q_ref[b] * (HEAD_DIM ** -0.5)).astype(BF16) for b in range(bb)]
    s01 = []
    for b in range(bb):
        ck = ck_ref[b].astype(BF16)
        s01.append((_bdot_nt(q2b[b], ck[:, 0:HEAD_DIM]),
                    _bdot_nt(q2b[b], ck[:, HEAD_DIM:2 * HEAD_DIM])))
    ps = []
    for b in range(bb):
        kvn = kvn_ref[b]
        s = jnp.where(lower, s01[b][0], s01[b][1]) + bias
        knew = jnp.where(lower, kvn[0:1, :], kvn[1:2, :])
        vnew = jnp.where(lower, kvn[2:3, :], kvn[3:4, :])
        snew = jnp.sum(q2b[b].astype(F32) * knew.astype(BF16).astype(F32), axis=1, keepdims=True)
        m = jnp.maximum(jnp.maximum(jnp.max(s, axis=1, keepdims=True), snew), sink)
        e = jnp.exp(s - m)
        enew = jnp.exp(snew - m)
        den = jnp.sum(e, axis=1, keepdims=True) + enew + jnp.exp(sink - m)
        pnew = (enew / den).astype(BF16).astype(F32)
        ps.append(((e / den).astype(BF16), pnew * vnew.astype(BF16).astype(F32)))
    for b in range(bb):
        cv = cv_ref[b].astype(BF16)
        o0 = jnp.dot(ps[b][0], cv[:, 0:HEAD_DIM], preferred_element_type=F32)
        o1 = jnp.dot(ps[b][0], cv[:, HEAD_DIM:2 * HEAD_DIM], preferred_element_type=F32)
        o_ref[b] = jnp.where(lower, o0, o1) + ps[b][1]


def _swa_sample(sinks, q3, kvn3, ck3, cv3, bb=8):
    nb, w_buf = ck3.shape[0], ck3.shape[1]
    blk = lambda i: (i, 0, 0)
    return pl.pallas_call(
        functools.partial(_swa_sample_body, bb=bb, w_buf=w_buf),
        grid=(nb // bb,),
        in_specs=[pl.BlockSpec(memory_space=pltpu.SMEM),
                  pl.BlockSpec((bb, ATT_HEADS, HEAD_DIM), blk),
                  pl.BlockSpec((bb, 2 * KV_HEADS, HEAD_DIM), blk),
                  pl.BlockSpec((bb, w_buf, KV_COLS), blk),
                  pl.BlockSpec((bb, w_buf, KV_COLS), blk)],
        out_specs=pl.BlockSpec((bb, ATT_HEADS, HEAD_DIM), blk),
        out_shape=jax.ShapeDtypeStruct((nb, ATT_HEADS, HEAD_DIM), F32),
        compiler_params=_cparams(("parallel",)),
        name="swa_sample",
    )(sinks, q3, kvn3, ck3, cv3)


def _rwkv_prep(rp, prev, mu, w0, wlu, a0, alu, glu, k_k, k_a, bd):
    xs = rp + mu * (prev - rp)
    r = xs[:, 0:RWKV_WIDTH]
    k = xs[:, RWKV_WIDTH:2 * RWKV_WIDTH]
    v = xs[:, 2 * RWKV_WIDTH:3 * RWKV_WIDTH]
    o = 3 * RWKV_WIDTH
    wd = xs[:, o:o + LORA_W]
    ad = xs[:, o + LORA_W:o + LORA_W + LORA_A]
    gd = xs[:, o + LORA_W + LORA_A:]
    z = -(w0 + _bdot(jnp.tanh(wd), wlu))
    softplus = jnp.maximum(z, 0.0) + jnp.log(1.0 + jnp.exp(-jnp.abs(z)))
    log_decay = -jnp.exp(-softplus - 0.5)
    a = _sigmoid(a0 + _bdot(ad, alu))
    g = _bdot(_sigmoid(gd), glu)
    kkr = k * k_k
    kk = kkr / jnp.maximum(jnp.sqrt(_head_sum(kkr * kkr, bd)), 1e-12)
    k2 = k * (1.0 + (a - 1.0) * k_a)
    return r, log_decay, k2, v, kk, a, g


def _rwkv_finish(y, r, k2, v, g, r_k, gn_g, gn_b, bd):
    inv = 1.0 / HEAD_DIM
    mu = _head_sum(y, bd) * inv
    yc = y - mu
    var = _head_sum(yc * yc, bd) * inv
    yn = yc * lax.rsqrt(var + GN_EPS) * gn_g + gn_b
    bonus = _head_sum(r * k2 * r_k, bd) * v
    return (yn + bonus) * g


def _rwkv_prompt_body(rp_ref, mu_ref, w0_ref, wlu_ref, a0_ref, alu_ref, glu_ref, kk_ref, ka_ref,
                      rk_ref, gng_ref, gnb_ref, bd_ref, ltri_ref,
                      out_ref, sfin_ref, prev_scr, st_scr, y_scr, *, tt):
    c = pl.program_id(1)

    @pl.when(c == 0)
    def _():
        prev_scr[...] = jnp.zeros_like(prev_scr)
        st_scr[...] = jnp.zeros_like(st_scr)

    rp = rp_ref[...]
    rolled = pltpu.roll(rp, 1, 0)
    rowi = lax.broadcasted_iota(jnp.int32, (tt, 1), 0)
    prev = jnp.where(rowi == 0, prev_scr[...], rolled)
    prev_scr[...] = rp[tt - 1:tt, :]
    bd = bd_ref[...]
    r, ld, k2, v, kk, a, g = _rwkv_prep(rp, prev, mu_ref[...], w0_ref[...], wlu_ref[...],
                                        a0_ref[...], alu_ref[...], glu_ref[...], kk_ref[...],
                                        ka_ref[...], bd)
    cum = _split_dot(ltri_ref[...], ld, 2)
    ecum = jnp.exp(cum)
    einv = jnp.exp(-cum)
    at = -kk * jnp.exp(cum - ld)
    bh = kk * a * einv
    kh = k2 * einv
    rt = r * ecum

    qw = HEAD_QUAD * HEAD_DIM
    ii = lax.broadcasted_iota(jnp.int32, (CHUNK, qw), 0)
    jl = lax.broadcasted_iota(jnp.int32, (CHUNK, qw), 1) % CHUNK
    strict = ii > jl
    incl = ii >= jl
    eye = (ii == jl).astype(F32)
    same_head = (lax.broadcasted_iota(jnp.int32, (qw, qw), 0) // HEAD_DIM
                 == lax.broadcasted_iota(jnp.int32, (qw, qw), 1) // HEAD_DIM)

    def blockdiag(m):
        mb = m.astype(BF16)
        return jnp.where(same_head, jnp.concatenate([mb] * HEAD_QUAD, axis=0), jnp.zeros((), BF16))

    def diag_blocks(full):
        out = None
        for hh in range(HEAD_QUAD):
            rs = slice(hh * HEAD_DIM, (hh + 1) * HEAD_DIM)
            blk = jnp.where(same_head[rs], full[rs], 0.0)
            out = blk if out is None else out + blk
        return out

    mm = lambda x, w_bf16: jnp.dot(x.astype(BF16), w_bf16, preferred_element_type=F32)
    mm_nt = lambda x, w_bf16: lax.dot_general(x.astype(BF16), w_bf16, (((1,), (1,)), ((), ())),
                                              preferred_element_type=F32)
    nsub = tt // CHUNK
    units = [(s, q) for s in range(nsub) for q in range(RWKV_HEADS // HEAD_QUAD)]
    rows = lambda s: slice(s * CHUNK, (s + 1) * CHUNK)
    cols = lambda q: slice(q * qw, (q + 1) * qw)
    cut = lambda z, u: z[rows(u[0]), cols(u[1])]
    kka = kk * a
    p_last, kh_end, bh_end = [], [], []
    for s in range(nsub):
        c_last = cum[(s + 1) * CHUNK - 1:(s + 1) * CHUNK, :]
        p_last.append(jnp.exp(c_last))
        tail = jnp.exp(c_last - cum[rows(s)])
        kh_end.append(k2[rows(s)] * tail)
        bh_end.append(kka[rows(s)] * tail)
    x2 = {u: jnp.concatenate([cut(at, u), cut(rt, u)], axis=0) for u in units}
    xb = {u: mm_nt(x2[u], blockdiag(cut(bh, u))) for u in units}
    xk = {u: mm_nt(x2[u], blockdiag(cut(kh, u))) for u in units}
    nmat = {u: jnp.where(strict, xb[u][0:CHUNK], 0.0) for u in units}
    a_rb = {u: jnp.where(incl, xb[u][CHUNK:], 0.0) for u in units}
    a_ak = {u: jnp.where(strict, xk[u][0:CHUNK], 0.0) for u in units}
    a_rk = {u: jnp.where(incl, xk[u][CHUNK:], 0.0) for u in units}
    tmat = {u: eye + nmat[u] for u in units}
    npow = nmat
    for _ in range(5):
        npow = {u: mm(npow[u], blockdiag(npow[u])) for u in units}
        tmat = {u: tmat[u] + mm(npow[u], blockdiag(tmat[u])) for u in units}
    vbd = {u: blockdiag(cut(v, u)) for u in units}
    av = {u: mm(a_ak[u], vbd[u]) for u in units}
    aprime = {u: mm(tmat[u], blockdiag(cut(at, u))) for u in units}
    wmat = {u: mm(tmat[u], blockdiag(av[u])) for u in units}
    rprime = {u: cut(rt, u) + mm(a_rb[u], blockdiag(aprime[u])) for u in units}
    y0 = {u: mm(a_rk[u], vbd[u]) + mm(a_rb[u], blockdiag(wmat[u])) for u in units}
    hmat = {u: diag_blocks(_bdot_tn(jnp.concatenate([cut(v, u), wmat[u]], axis=0),
                                    jnp.concatenate([cut(kh_end[u[0]], (0, u[1])),
                                                     cut(bh_end[u[0]], (0, u[1]))], axis=0)))
            for u in units}
    g0 = {u: diag_blocks(_bdot_tn(aprime[u], cut(bh_end[u[0]], (0, u[1])))) for u in units}

    states = [st_scr[q] for q in range(RWKV_HEADS // HEAD_QUAD)]
    for s in range(nsub):
        for q in range(RWKV_HEADS // HEAD_QUAD):
            u = (s, q)
            st = states[q]
            y_scr[rows(s), cols(q)] = mm_nt(rprime[u], blockdiag(st)) + y0[u]
            states[q] = st * p_last[s][:, cols(q)] + mm(st, blockdiag(g0[u])) + hmat[u]
    for q in range(RWKV_HEADS // HEAD_QUAD):
        st_scr[q] = states[q]
        for hh in range(HEAD_QUAD):
            sfin_ref[0, q * HEAD_QUAD + hh] = states[q][:, hh * HEAD_DIM:(hh + 1) * HEAD_DIM]
    out_ref[...] = _rwkv_finish(y_scr[...], r, k2, v, g, rk_ref[...], gng_ref[...], gnb_ref[...], bd)


def _chunk_tril(tt):
    i = jnp.arange(tt)
    same = (i[:, None] // CHUNK) == (i[None, :] // CHUNK)
    return (same & (i[:, None] >= i[None, :])).astype(BF16)


def _head_blockdiag():
    i = jnp.arange(RWKV_WIDTH) // HEAD_DIM
    return (i[:, None] == i[None, :]).astype(BF16)


def _rwkv_prompt(rp, prm, nbatch, seq, tt):
    nc = seq // tt
    fixed = lambda b, c: (0, 0)
    row = lambda b, c: (b * nc + c, 0)
    vec = lambda n: pl.BlockSpec((1, n), fixed)
    return pl.pallas_call(
        functools.partial(_rwkv_prompt_body, tt=tt),
        grid=(nbatch, nc),
        in_specs=[pl.BlockSpec((tt, RWKV_PROJ), row),
                  vec(RWKV_PROJ), vec(RWKV_WIDTH),
                  pl.BlockSpec((LORA_W, RWKV_WIDTH), fixed), vec(RWKV_WIDTH),
                  pl.BlockSpec((LORA_A, RWKV_WIDTH), fixed),
                  pl.BlockSpec((LORA_G, RWKV_WIDTH), fixed),
                  vec(RWKV_WIDTH), vec(RWKV_WIDTH), vec(RWKV_WIDTH), vec(RWKV_WIDTH),
                  vec(RWKV_WIDTH),
                  pl.BlockSpec((RWKV_WIDTH, RWKV_WIDTH), fixed),
                  pl.BlockSpec((tt, tt), fixed)],
        out_specs=[pl.BlockSpec((tt, RWKV_WIDTH), row),
                   pl.BlockSpec((1, RWKV_HEADS, HEAD_DIM, HEAD_DIM), lambda b, c: (b, 0, 0, 0))],
        out_shape=[jax.ShapeDtypeStruct((nbatch * seq, RWKV_WIDTH), F32),
                   jax.ShapeDtypeStruct((nbatch, RWKV_HEADS, HEAD_DIM, HEAD_DIM), F32)],
        scratch_shapes=[pltpu.VMEM((1, RWKV_PROJ), F32),
                        pltpu.VMEM((RWKV_HEADS // HEAD_QUAD, HEAD_DIM, HEAD_QUAD * HEAD_DIM), F32),
                        pltpu.VMEM((tt, RWKV_WIDTH), F32)],
        compiler_params=_cparams(("parallel", "arbitrary")),
        name="rwkv_prompt",
    )(rp, prm["mu"], prm["w0"], prm["wlu"], prm["a0"], prm["alu"], prm["glu"], prm["k_k"],
      prm["k_a"], prm["r_k"], prm["gn_g"], prm["gn_b"], _head_blockdiag(), _chunk_tril(tt))


def _rwkv_sample_prep_body(rp_ref, prev_ref, mu_ref, w0_ref, wlu_ref, a0_ref, alu_ref, glu_ref,
                           kk_ref, ka_ref, bd_ref, o_ref):
    r, ld, k2, v, kk, a, g = _rwkv_prep(rp_ref[...], prev_ref[...], mu_ref[...], w0_ref[...],
                                        wlu_ref[...], a0_ref[...], alu_ref[...], glu_ref[...],
                                        kk_ref[...], ka_ref[...], bd_ref[...])
    for i, z in enumerate((r, jnp.exp(ld), k2, v, kk, a, g)):
        o_ref[i] = z


def _rwkv_sample_prep(rp, prev, prm):
    nb = rp.shape[0]
    return pl.pallas_call(
        _rwkv_sample_prep_body,
        out_shape=jax.ShapeDtypeStruct((7, nb, RWKV_WIDTH), F32),
        compiler_params=pltpu.CompilerParams(vmem_limit_bytes=VMEM_LIMIT),
        name="rwkv_sample_prep",
    )(rp, prev, prm["mu"], prm["w0"], prm["wlu"], prm["a0"], prm["alu"], prm["glu"], prm["k_k"],
      prm["k_a"], _head_blockdiag())


def _rwkv_sample_step_body(p_ref, s_ref, rk_ref, gng_ref, gnb_ref, so_ref, o_ref, *, bb):
    ii = lax.broadcasted_iota(jnp.int32, (HEAD_DIM, HEAD_DIM), 0)
    jj = lax.broadcasted_iota(jnp.int32, (HEAD_DIM, HEAD_DIM), 1)
    eye = ii == jj
    for b in range(bb):
        for h in range(RWKV_HEADS):
            row = lambda i: p_ref[i, b, h:h + 1, :]
            r_r, w_r, k_r, v_r, kk_r, a_r = (row(i) for i in range(6))
            st = s_ref[b, h]
            sa = jnp.sum(st * kk_r, axis=1, keepdims=True)
            v_col = jnp.sum(jnp.where(eye, v_r, 0.0), axis=1, keepdims=True)
            st = st * w_r - sa * (kk_r * a_r) + v_col * k_r
            so_ref[b, h] = st
            y_col = jnp.sum(st * r_r, axis=1, keepdims=True)
            o_ref[b, h:h + 1, :] = jnp.sum(jnp.where(eye, y_col, 0.0), axis=0, keepdims=True)
    y = o_ref[...]
    r, k2, v, g = p_ref[0], p_ref[2], p_ref[3], p_ref[6]
    mu = jnp.mean(y, axis=-1, keepdims=True)
    yc = y - mu
    var = jnp.mean(yc * yc, axis=-1, keepdims=True)
    yn = yc * lax.rsqrt(var + GN_EPS) * gng_ref[...] + gnb_ref[...]
    bonus = jnp.sum(r * k2 * rk_ref[...], axis=-1, keepdims=True) * v
    o_ref[...] = (yn + bonus) * g


def _rwkv_sample_step(p4, state, r_k, gn_g, gn_b, bb=8):
    nb = state.shape[0]
    hv = pl.BlockSpec((RWKV_HEADS, HEAD_DIM), lambda i: (0, 0))
    return pl.pallas_call(
        functools.partial(_rwkv_sample_step_body, bb=bb),
        grid=(nb // bb,),
        in_specs=[pl.BlockSpec((7, bb, RWKV_HEADS, HEAD_DIM), lambda i: (0, i, 0, 0)),
                  pl.BlockSpec((bb, RWKV_HEADS, HEAD_DIM, HEAD_DIM), lambda i: (i, 0, 0, 0)),
                  hv, hv, hv],
        out_specs=[pl.BlockSpec((bb, RWKV_HEADS, HEAD_DIM, HEAD_DIM), lambda i: (i, 0, 0, 0)),
                   pl.BlockSpec((bb, RWKV_HEADS, HEAD_DIM), lambda i: (i, 0, 0))],
        out_shape=[jax.ShapeDtypeStruct(state.shape, F32),
                   jax.ShapeDtypeStruct((nb, RWKV_HEADS, HEAD_DIM), F32)],
        compiler_params=_cparams(("parallel",)),
        name="rwkv_sample_step",
    )(p4, state, r_k, gn_g, gn_b)


def _mix_router_body(att_ref, rw_ref, xn_ref, wo_ref, g_ref, b_ref, wrh_ref, wrl_ref, br_ref,
                     base_ref, x1_ref, info_ref, gate_ref, cnt_ref, base_scr, *, tm):
    i = pl.program_id(0)

    @pl.when(i == 0)
    def _():
        base_scr[...] = base_ref[...]

    mixed = (jnp.dot(att_ref[...].astype(BF16), wo_ref[0:ATT_WIDTH, :], preferred_element_type=F32)
             + jnp.dot(rw_ref[...].astype(BF16), wo_ref[ATT_WIDTH:, :], preferred_element_type=F32))
    x1 = _layer_norm(DEEPNORM_ALPHA * xn_ref[...] + mixed, g_ref[...], b_ref[...])
    x1_ref[...] = x1

    x1h = x1.astype(BF16)
    x1l = (x1 - x1h.astype(F32)).astype(BF16)
    logits = (jnp.dot(x1h, wrh_ref[...], preferred_element_type=F32)
              + jnp.dot(x1l, wrh_ref[...], preferred_element_type=F32)
              + jnp.dot(x1h, wrl_ref[...], preferred_element_type=F32)) + br_ref[...]
    lane = lax.broadcasted_iota(jnp.int32, (tm, LANES), 1)
    lanef = lane.astype(F32)
    vals, idxs, hots = [], [], []
    cur = logits
    for _ in range(TOP_K):
        m = jnp.max(cur, axis=1, keepdims=True)
        idx = jnp.min(jnp.where(cur == m, lanef, float(LANES)), axis=1, keepdims=True)
        hot = lanef == idx
        cur = jnp.where(hot, -jnp.inf, cur)
        vals.append(m)
        idxs.append(idx)
        hots.append(hot)
    es = [jnp.exp(vk - vals[0]) for vk in vals]
    den = es[0] + es[1] + es[2] + es[3]
    multi = jnp.zeros((tm, LANES), F32)
    for hot in hots:
        multi = multi + hot.astype(F32)
    ti = lax.broadcasted_iota(jnp.int32, (tm, tm), 0)
    tj = lax.broadcasted_iota(jnp.int32, (tm, tm), 1)
    before = jnp.dot((ti > tj).astype(BF16), multi.astype(BF16), preferred_element_type=F32)
    before = before + base_scr[...]
    info = jnp.zeros((tm, LANES), F32)
    gates = jnp.zeros((tm, LANES), F32)
    for k in range(TOP_K):
        rank = jnp.sum(jnp.where(hots[k], before, 0.0), axis=1, keepdims=True)
        info = jnp.where(lane == k, idxs[k], info)
        info = jnp.where(lane == TOP_K + k, rank, info)
        gates = jnp.where(lane == k, es[k] / den, gates)
    info_ref[...] = jnp.transpose(info)[0:2 * TOP_K, :].astype(jnp.int32)
    gate_ref[...] = gates
    base_scr[...] = base_scr[...] + jnp.sum(multi, axis=0, keepdims=True)
    cnt_ref[...] = base_scr[...]


def _mix_router(att, rw, xn, wo_bf16, g, b, wr_hi, wr_lo, br_pad, base, tm):
    t = att.shape[0]
    row = lambda i: (i, 0)
    fixed = lambda i: (0, 0)
    return pl.pallas_call(
        functools.partial(_mix_router_body, tm=tm),
        grid=(t // tm,),
        in_specs=[pl.BlockSpec((tm, ATT_WIDTH), row),
                  pl.BlockSpec((tm, RWKV_WIDTH), row),
                  pl.BlockSpec((tm, D_MODEL), row),
                  pl.BlockSpec((D_MODEL, D_MODEL), fixed),
                  pl.BlockSpec((1, D_MODEL), fixed),
                  pl.BlockSpec((1, D_MODEL), fixed),
                  pl.BlockSpec((D_MODEL, LANES), fixed),
                  pl.BlockSpec((D_MODEL, LANES), fixed),
                  pl.BlockSpec((1, LANES), fixed),
                  pl.BlockSpec((1, LANES), fixed)],
        out_specs=[pl.BlockSpec((tm, D_MODEL), row),
                   pl.BlockSpec((2 * TOP_K, tm), lambda i: (0, i)),
                   pl.BlockSpec((tm, LANES), row),
                   pl.BlockSpec((1, LANES), fixed)],
        out_shape=[jax.ShapeDtypeStruct((t, D_MODEL), F32),
                   jax.ShapeDtypeStruct((2 * TOP_K, t), jnp.int32),
                   jax.ShapeDtypeStruct((t, LANES), F32),
                   jax.ShapeDtypeStruct((1, LANES), F32)],
        scratch_shapes=[pltpu.VMEM((1, LANES), F32)],
        compiler_params=_cparams(("arbitrary",)),
        name="mix_router",
    )(att, rw, xn, wo_bf16, g, b, wr_hi, wr_lo, br_pad, base)


def _row_copies(pos_ref, base, tm, make_copy):
    tokens = pos_ref.shape[0] // TOP_K

    def group(gi, carry):
        r0 = pl.multiple_of(gi * ROW_DMA_UNROLL, ROW_DMA_UNROLL)
        for j in range(ROW_DMA_UNROLL):
            for k in range(TOP_K):
                make_copy(r0 + j, k, pos_ref[k * tokens + base + r0 + j]).start(priority=k % 2)
        return carry

    lax.fori_loop(0, tm // ROW_DMA_UNROLL, group, 0)


def _dispatch_rows(pos_ref, base, x_ref, xs_ref, sem, rows):
    def make_copy(r, k, p):
        return pltpu.make_async_copy(x_ref.at[pl.ds(r, 1)], xs_ref.at[pl.ds(p, 1)], sem)

    _row_copies(pos_ref, base, rows, make_copy)
    for _ in range(TOP_K):
        pltpu.make_async_copy(x_ref, xs_ref.at[pl.ds(0, rows)], sem).wait()


def _dispatch_body(pos_a_ref, pos_b_ref, fill_ref, xa_ref, xb_ref, xs_ref, zero_scr, sem, fill_sem,
                   *, tm, n_tiles):
    i = pl.program_id(0)
    last = pl.num_programs(0) - 1

    @pl.when(i == 0)
    def _():
        zero_scr[...] = jnp.zeros_like(zero_scr)

        def tile_fill(start):
            return pltpu.make_async_copy(
                zero_scr, xs_ref.at[pl.ds(pl.multiple_of(start, EXPERT_TILE), EXPERT_TILE)], fill_sem)

        fills = [tile_fill(fill_ref[e]) for e in range(N_EXPERTS)]
        for cp in fills:
            cp.start()
        first_unused = fill_ref[N_EXPERTS]
        lax.fori_loop(first_unused, n_tiles,
                      lambda t, c: (tile_fill(t * EXPERT_TILE).start(), c)[1], 0)
        for cp in fills:
            cp.wait()
        lax.fori_loop(first_unused, n_tiles,
                      lambda t, c: (tile_fill(t * EXPERT_TILE).wait(), c)[1], 0)

    @pl.when(i < last)
    def _():
        _dispatch_rows(pos_a_ref, i * tm, xa_ref, xs_ref, sem, tm)

    @pl.when(i == last)
    def _():
        _dispatch_rows(pos_b_ref, 0, xb_ref, xs_ref, sem, xb_ref.shape[0])


def _dispatch(pos_a, pos_b, fill_start, x_a, x_b, n_slots, tm):
    nta = x_a.shape[0] // tm
    return pl.pallas_call(
        functools.partial(_dispatch_body, tm=tm, n_tiles=n_slots // EXPERT_TILE),
        grid_spec=pltpu.PrefetchScalarGridSpec(
            num_scalar_prefetch=3,
            grid=(nta + 1,),
            in_specs=[pl.BlockSpec((tm, D_MODEL), lambda i, *_: (jnp.minimum(i, nta - 1), 0)),
                      pl.BlockSpec(x_b.shape, lambda i, *_: (0, 0))],
            out_specs=pl.BlockSpec(memory_space=pl.ANY),
            scratch_shapes=[pltpu.VMEM((EXPERT_TILE, D_MODEL), F32),
                            pltpu.SemaphoreType.DMA, pltpu.SemaphoreType.DMA]),
        out_shape=jax.ShapeDtypeStruct((n_slots, D_MODEL), F32),
        compiler_params=_cparams(("arbitrary",)),
        name="moe_dispatch",
    )(pos_a, pos_b, fill_start, x_a, x_b)


def _experts_body(te_ref, nu_ref, xs_ref, wgu_ref, wd_ref, bg_ref, bu_ref, bd_ref, sel_ref, ys_ref,
                  wg_scr, wu_scr, wd_scr):
    i = pl.program_id(0)
    new_expert = (i == 0) | (te_ref[i] != te_ref[jnp.maximum(i - 1, 0)])

    @pl.when(new_expert)
    def _():
        wd_scr[...] = wd_ref[...].astype(BF16)
        for m in range(D_FF // LANES):
            pair = wgu_ref[:, 2 * m * LANES:2 * (m + 1) * LANES].astype(BF16)
            split = jnp.dot(pair, sel_ref[...], preferred_element_type=F32)
            wg_scr[:, m * LANES:(m + 1) * LANES] = split[:, 0:LANES].astype(BF16)
            wu_scr[:, m * LANES:(m + 1) * LANES] = split[:, LANES:].astype(BF16)

    @pl.when(i < nu_ref[0])
    def _():
        x = xs_ref[...].astype(BF16)
        gate = jnp.dot(x, wg_scr[...], preferred_element_type=F32) + bg_ref[...]
        up = jnp.dot(x, wu_scr[...], preferred_element_type=F32) + bu_ref[...]
        gate = jnp.minimum(gate, SWIGLU_LIMIT)
        up = jnp.clip(up, -SWIGLU_LIMIT, SWIGLU_LIMIT)
        act = (up + 1.0) * gate * _sigmoid(SWIGLU_ALPHA * gate)
        ys_ref[...] = jnp.dot(act.astype(BF16), wd_scr[...], preferred_element_type=F32) + bd_ref[...]

    @pl.when(i >= nu_ref[0])
    def _():
        ys_ref[...] = jnp.zeros_like(ys_ref)


def _gate_up_selector():
    i = jnp.arange(2 * LANES)
    src = jnp.where(i < LANES, 2 * i, 2 * (i - LANES) + 1)
    return (i[:, None] == src[None, :]).astype(BF16)


def _experts(tile_e, n_used, xs, wgu, wd, bg, bu, bd):
    ns = xs.shape[0]
    tm = EXPERT_TILE
    wspec = lambda a, b: pl.BlockSpec((None, a, b), lambda i, te, nu: (te[i], 0, 0))
    return pl.pallas_call(
        _experts_body,
        grid_spec=pltpu.PrefetchScalarGridSpec(
            num_scalar_prefetch=2,
            grid=(ns // tm,),
            in_specs=[pl.BlockSpec((tm, D_MODEL), lambda i, te, nu: (jnp.minimum(i, nu[0] - 1), 0)),
                      wspec(D_MODEL, 2 * D_FF), wspec(D_FF, D_MODEL),
                      wspec(1, D_FF), wspec(1, D_FF), wspec(1, D_MODEL),
                      pl.BlockSpec((2 * LANES, 2 * LANES), lambda i, te, nu: (0, 0))],
            out_specs=pl.BlockSpec((tm, D_MODEL), lambda i, te, nu: (i, 0)),
            scratch_shapes=[pltpu.VMEM((D_MODEL, D_FF), BF16), pltpu.VMEM((D_MODEL, D_FF), BF16),
                            pltpu.VMEM((D_FF, D_MODEL), BF16)]),
        out_shape=jax.ShapeDtypeStruct((ns, D_MODEL), F32),
        compiler_params=pltpu.CompilerParams(dimension_semantics=("arbitrary",),
                                             vmem_limit_bytes=EXPERTS_VMEM_LIMIT),
        name="moe_experts",
    )(tile_e, n_used, xs, wgu, wd, bg, bu, bd, _gate_up_selector())


def _combine_body(pos_ref, ys_ref, gate_ref, x1_ref, pe_ref, g_ref, b_ref, wple_ref, wpg_ref,
                  o_ref, buf, sem, *, tm):
    i = pl.program_id(0)
    cur = i % 2

    def gather(tile, slot):
        def make_copy(r, k, p):
            return pltpu.make_async_copy(ys_ref.at[pl.ds(p, 1)], buf.at[slot, k, pl.ds(r, 1)],
                                         sem.at[slot])
        _row_copies(pos_ref, tile * tm, tm, make_copy)

    @pl.when(i == 0)
    def _():
        gather(0, 0)

    for slot in range(2):
        @pl.when((i + 1 < pl.num_programs(0)) & (cur != slot))
        def _():
            gather(i + 1, slot)

    for k in range(TOP_K):
        pltpu.make_async_copy(ys_ref.at[pl.ds(0, tm)], buf.at[cur, k], sem.at[cur]).wait()
    gates = gate_ref[...]
    ffn = gates[:, 0:1] * buf[cur, 0]
    for k in range(1, TOP_K):
        ffn = ffn + gates[:, k:k + 1] * buf[cur, k]
    x2 = _layer_norm(DEEPNORM_ALPHA * x1_ref[...] + ffn, g_ref[...], b_ref[...])
    gate = _sigmoid(jnp.dot(x2.astype(BF16), wpg_ref[...], preferred_element_type=F32))
    emb = jnp.dot(pe_ref[...].astype(BF16), wple_ref[...], preferred_element_type=F32)
    o_ref[...] = x2 + gate * emb


def _combine(pos_flat, ys, gates, x1, pe, g, b, wple_bf16, wpg_bf16, tm):
    t = x1.shape[0]
    row = lambda i, *_: (i, 0)
    fixed = lambda i, *_: (0, 0)
    return pl.pallas_call(
        functools.partial(_combine_body, tm=tm),
        grid_spec=pltpu.PrefetchScalarGridSpec(
            num_scalar_prefetch=1,
            grid=(t // tm,),
            in_specs=[pl.BlockSpec(memory_space=pl.ANY),
                      pl.BlockSpec((tm, LANES), row),
                      pl.BlockSpec((tm, D_MODEL), row),
                      pl.BlockSpec((tm, PLE_DIM), row),
                      pl.BlockSpec((1, D_MODEL), fixed),
                      pl.BlockSpec((1, D_MODEL), fixed),
                      pl.BlockSpec((PLE_DIM, D_MODEL), fixed),
                      pl.BlockSpec((D_MODEL, D_MODEL), fixed)],
            out_specs=pl.BlockSpec((tm, D_MODEL), row),
            scratch_shapes=[pltpu.VMEM((2, TOP_K, tm, D_MODEL), F32),
                            pltpu.SemaphoreType.DMA((2,))]),
        out_shape=jax.ShapeDtypeStruct((t, D_MODEL), F32),
        compiler_params=_cparams(("arbitrary",)),
        name="moe_combine",
    )(pos_flat, ys, gates, x1, pe, g, b, wple_bf16, wpg_bf16)


def kernel(x_prompt, x_sample, cache_k, cache_v, state_shift, state_wkv, p_prompt, p_sample,
           ln_emb_g, ln_emb_b, w_in, attn_sinks, rwkv_mu, rwkv_w0, rwkv_w_lora_up, rwkv_a0,
           rwkv_a_lora_up, rwkv_g_lora_up, rwkv_k_k, rwkv_k_a, rwkv_r_k, rwkv_gn_g, rwkv_gn_b,
           w_out, ln1_g, ln1_b, w_router, b_router, w_gate_up, b_gate_up, w_down, b_down,
           ln2_g, ln2_b, w_ple, w_ple_gate):
    assert w_in.shape[0] == DEPTH == 1
    nbp, seq, _ = x_prompt.shape
    nbs, dec_seq, _ = x_sample.shape
    assert dec_seq == 1
    tp, ts = nbp * seq, nbs
    w_buf = cache_k.shape[2]
    rowv = lambda z: z.reshape(1, -1)

    w_in_b = w_in[0].astype(BF16)
    w_out_b = w_out[0].astype(BF16)
    w_ple_b = w_ple[0].astype(BF16)
    w_pg_b = w_ple_gate[0].astype(BF16)
    sinks = attn_sinks[0]
    prm = dict(mu=rowv(rwkv_mu[0]), w0=rowv(rwkv_w0[0]), wlu=rwkv_w_lora_up[0],
               a0=rowv(rwkv_a0[0]), alu=rwkv_a_lora_up[0], glu=rwkv_g_lora_up[0],
               k_k=rowv(rwkv_k_k[0]), k_a=rowv(rwkv_k_a[0]), r_k=rowv(rwkv_r_k[0]),
               gn_g=rowv(rwkv_gn_g[0]), gn_b=rowv(rwkv_gn_b[0]))
    ge, be = rowv(ln_emb_g), rowv(ln_emb_b)

    xn_p, q_p, kv_p, rp_p = _inproj(x_prompt.reshape(tp, D_MODEL), ge, be, w_in_b, 512)
    att_p = _swa_prompt(sinks, q_p, kv_p, nbp, seq)
    rw_p, wkv_p = _rwkv_prompt(rp_p, prm, nbp, seq, 256)

    xn_s, q_s, kv_s, rp_s = _inproj(x_sample.reshape(ts, D_MODEL), ge, be, w_in_b, ts)
    ck3 = cache_k[0].reshape(nbs, w_buf, KV_COLS)
    cv3 = cache_v[0].reshape(nbs, w_buf, KV_COLS)
    att_s = _swa_sample(sinks, q_s.reshape(nbs, ATT_HEADS, HEAD_DIM),
                        kv_s.reshape(nbs, 2 * KV_HEADS, HEAD_DIM), ck3, cv3)
    att_s = att_s.reshape(ts, ATT_WIDTH)
    p7 = _rwkv_sample_prep(rp_s, state_shift[0], prm)
    hv = lambda z: z.reshape(RWKV_HEADS, HEAD_DIM)
    wkv_s, rw_s = _rwkv_sample_step(p7.reshape(7, nbs, RWKV_HEADS, HEAD_DIM), state_wkv[0],
                                    hv(rwkv_r_k[0]), hv(rwkv_gn_g[0]), hv(rwkv_gn_b[0]))
    rw_s = rw_s.reshape(ts, RWKV_WIDTH)

    wr_pad = jnp.zeros((D_MODEL, LANES), F32).at[:, :N_EXPERTS].set(w_router[0])
    wr_hi = wr_pad.astype(BF16)
    wr_lo = (wr_pad - wr_hi.astype(F32)).astype(BF16)
    br_pad = jnp.full((1, LANES), -jnp.inf, F32).at[0, :N_EXPERTS].set(b_router[0])
    g1, b1 = rowv(ln1_g[0]), rowv(ln1_b[0])
    tmp, tmd = 512, 256
    x1_p, info_p, gate_p, cnt_p = _mix_router(att_p, rw_p, xn_p, w_out_b, g1, b1, wr_hi, wr_lo,
                                              br_pad, jnp.zeros((1, LANES), F32), tmp)
    x1_s, info_s, gate_s, cnt = _mix_router(att_s, rw_s, xn_s, w_out_b, g1, b1, wr_hi, wr_lo,
                                            br_pad, cnt_p, ts)

    counts = cnt[0, :N_EXPERTS].astype(jnp.int32)
    padded = ((counts + EXPERT_TILE - 1) // EXPERT_TILE) * EXPERT_TILE
    ends = jnp.cumsum(padded)
    offs = ends - padded
    n_slots = (tp + ts) * TOP_K + N_EXPERTS * EXPERT_TILE
    n_tiles = n_slots // EXPERT_TILE
    n_used = (ends[-1] // EXPERT_TILE).reshape(1).astype(jnp.int32)
    tile_start = jnp.arange(n_tiles, dtype=jnp.int32) * EXPERT_TILE
    tile_e = jnp.sum(ends[None, :] <= tile_start[:, None], axis=1).astype(jnp.int32)
    last_e = jnp.max(jnp.where(padded > 0, jnp.arange(N_EXPERTS), 0)).astype(jnp.int32)
    tile_e = jnp.minimum(tile_e, last_e)

    def slots(info):
        return (offs[info[0:TOP_K]] + info[TOP_K:2 * TOP_K]).reshape(-1).astype(jnp.int32)

    pos_p, pos_s = slots(info_p), slots(info_s)

    fill_start = jnp.concatenate([jnp.clip(ends - EXPERT_TILE, 0, n_slots - EXPERT_TILE),
                                  n_used]).astype(jnp.int32)
    xs = _dispatch(pos_p, pos_s, fill_start, x1_p, x1_s, n_slots, tmd)
    bgu = b_gate_up[0]
    bg = bgu[:, 0::2].reshape(N_EXPERTS, 1, D_FF)
    bu = bgu[:, 1::2].reshape(N_EXPERTS, 1, D_FF)
    bdn = b_down[0].reshape(N_EXPERTS, 1, D_MODEL)
    ys = _experts(tile_e, n_used, xs, w_gate_up[0], w_down[0], bg, bu, bdn)

    g2, b2 = rowv(ln2_g[0]), rowv(ln2_b[0])
    y_p = _combine(pos_p, ys, gate_p, x1_p, p_prompt[0].reshape(tp, PLE_DIM), g2, b2,
                   w_ple_b, w_pg_b, tmd)
    y_s = _combine(pos_s, ys, gate_s, x1_s, p_sample[0].reshape(ts, PLE_DIM), g2, b2,
                   w_ple_b, w_pg_b, ts)

    w_keep = min(WINDOW, seq)
    kv_p3 = kv_p.reshape(nbp, seq, 2 * KV_COLS)[:, seq - w_keep:]
    k_win_p = kv_p3[:, :, 0:KV_COLS].reshape(1, nbp, w_keep, KV_HEADS, HEAD_DIM)
    v_win_p = kv_p3[:, :, KV_COLS:].reshape(1, nbp, w_keep, KV_HEADS, HEAD_DIM)
    shift_p = rp_p.reshape(nbp, seq, RWKV_PROJ)[:, seq - 1][None]
    k_new = kv_s[:, 0:KV_COLS].reshape(nbs, 1, KV_HEADS, HEAD_DIM)
    v_new = kv_s[:, KV_COLS:].reshape(nbs, 1, KV_HEADS, HEAD_DIM)
    k_win_s = jnp.concatenate([cache_k[0], k_new], axis=1)[:, 1:][None]
    v_win_s = jnp.concatenate([cache_v[0], v_new], axis=1)[:, 1:][None]
    return (y_p.reshape(nbp, seq, D_MODEL), y_s.reshape(nbs, 1, D_MODEL),
            k_win_p, v_win_p, shift_p, wkv_p[None],
            k_win_s, v_win_s, rp_s[None], wkv_s[None])
```

```python
import functools

import jax
import jax.numpy as jnp
from jax import lax
from jax.experimental import pallas as pl
from jax.experimental.pallas import tpu as pltpu

F32 = jnp.float32
BF16 = jnp.bfloat16
HIGHEST = lax.Precision.HIGHEST

D_MODEL = 1024
HEAD_DIM = 64
ATT_HEADS = 8
KV_HEADS = 2
GROUP = ATT_HEADS // KV_HEADS
ATT_WIDTH = ATT_HEADS * HEAD_DIM
KV_COLS = KV_HEADS * HEAD_DIM
RWKV_HEADS = 8
RWKV_WIDTH = RWKV_HEADS * HEAD_DIM
LORA_W = 64
LORA_A = 64
LORA_G = 128
RWKV_PROJ = 3 * RWKV_WIDTH + LORA_W + LORA_A + LORA_G
WINDOW = 128
BLOCK = 128
PLE_DIM = 256
N_EXPERTS = 32
TOP_K = 4
D_FF = 1024
SWIGLU_LIMIT = 7.0
SWIGLU_ALPHA = 1.702
LN_EPS = 1e-5
GN_EPS = 64e-5
DEPTH = 1
DEEPNORM_ALPHA = (2 * DEPTH) ** 0.25

LANES = 128
CHUNK = 64
HEAD_QUAD = 4
EXPERT_TILE = 512
ROW_DMA_UNROLL = 8
VMEM_LIMIT = 48 * 1024 * 1024
EXPERTS_VMEM_LIMIT = 58 * 1024 * 1024


def _cparams(sem):
    return pltpu.CompilerParams(dimension_semantics=sem, vmem_limit_bytes=VMEM_LIMIT)


def _bdot(a, b):
    return jnp.dot(a.astype(BF16), b.astype(BF16), preferred_element_type=F32)


def _bdot_nt(a, b):
    return lax.dot_general(a.astype(BF16), b.astype(BF16), (((1,), (1,)), ((), ())),
                           preferred_element_type=F32)


def _bdot_tn(a, b):
    return lax.dot_general(a.astype(BF16), b.astype(BF16), (((0,), (0,)), ((), ())),
                           preferred_element_type=F32)


def _split_dot(m_bf16, x, passes):
    acc = None
    rem = x
    for _ in range(passes):
        hi = rem.astype(BF16)
        part = jnp.dot(m_bf16, hi, preferred_element_type=F32)
        acc = part if acc is None else acc + part
        rem = rem - hi.astype(F32)
    return acc


def _head_sum(x, bd_bf16):
    acc = None
    rem = x
    for _ in range(2):
        hi = rem.astype(BF16)
        part = jnp.dot(hi, bd_bf16, preferred_element_type=F32)
        acc = part if acc is None else acc + part
        rem = rem - hi.astype(F32)
    return acc


def _sigmoid(x):
    return 1.0 / (1.0 + jnp.exp(-x))


def _layer_norm(x, g, b):
    mu = jnp.mean(x, axis=-1, keepdims=True)
    xc = x - mu
    var = jnp.mean(xc * xc, axis=-1, keepdims=True)
    return xc * lax.rsqrt(var + LN_EPS) * g + b


def _inproj_body(x_ref, g_ref, b_ref, w_ref, xn_ref, q_ref, kv_ref, rp_ref):
    xn = _layer_norm(x_ref[...], g_ref[...], b_ref[...])
    xn_ref[...] = xn
    xb = xn.astype(BF16)
    q_ref[...] = jnp.dot(xb, w_ref[:, 0:ATT_WIDTH], preferred_element_type=F32)
    kv_ref[...] = jnp.dot(xb, w_ref[:, ATT_WIDTH:ATT_WIDTH + 2 * KV_COLS],
                          preferred_element_type=F32)
    rp_ref[...] = jnp.dot(xb, w_ref[:, ATT_WIDTH + 2 * KV_COLS:], preferred_element_type=F32)


def _inproj(x2d, g, b, w_bf16, tm):
    t = x2d.shape[0]
    in_proj = w_bf16.shape[1]
    row = lambda i: (i, 0)
    fixed = lambda i: (0, 0)
    return pl.pallas_call(
        _inproj_body,
        grid=(t // tm,),
        in_specs=[pl.BlockSpec((tm, D_MODEL), row),
                  pl.BlockSpec((1, D_MODEL), fixed),
                  pl.BlockSpec((1, D_MODEL), fixed),
                  pl.BlockSpec((D_MODEL, in_proj), fixed)],
        out_specs=[pl.BlockSpec((tm, D_MODEL), row),
                   pl.BlockSpec((tm, ATT_WIDTH), row),
                   pl.BlockSpec((tm, 2 * KV_COLS), row),
                   pl.BlockSpec((tm, RWKV_PROJ), row)],
        out_shape=[jax.ShapeDtypeStruct((t, D_MODEL), F32),
                   jax.ShapeDtypeStruct((t, ATT_WIDTH), F32),
                   jax.ShapeDtypeStruct((t, 2 * KV_COLS), F32),
                   jax.ShapeDtypeStruct((t, RWKV_PROJ), F32)],
        compiler_params=_cparams(("parallel",)),
        name="inproj",
    )(x2d, g, b, w_bf16)


def _alibi_slope(h):
    return 2.0 ** (-8.0 * (h + 1) / ATT_HEADS)


def _swa_prompt_body(sink_ref, q_ref, kvc_ref, kvp_ref, o_ref):
    n = pl.program_id(1)
    q = (q_ref[...] * (HEAD_DIM ** -0.5)).astype(BF16)
    kvc = kvc_ref[...].astype(BF16)
    kvp = kvp_ref[...].astype(BF16)
    row = lax.broadcasted_iota(jnp.int32, (BLOCK, 2 * BLOCK), 0)
    col = lax.broadcasted_iota(jnp.int32, (BLOCK, 2 * BLOCK), 1)
    dist = row + BLOCK - col
    valid = (dist >= 0) & (dist <= WINDOW) & ((col >= BLOCK) | (n > 0))
    distf = dist.astype(F32)
    kbands, vbands = [], []
    for g in range(KV_HEADS):
        ks = slice(g * HEAD_DIM, (g + 1) * HEAD_DIM)
        vs = slice(KV_COLS + g * HEAD_DIM, KV_COLS + (g + 1) * HEAD_DIM)
        kbands.append(jnp.concatenate([kvp[:, ks], kvc[:, ks]], axis=0))
        vbands.append(jnp.concatenate([kvp[:, vs], kvc[:, vs]], axis=0))
    scores = [lax.dot_general(q[:, h * HEAD_DIM:(h + 1) * HEAD_DIM], kbands[h // GROUP],
                              (((1,), (1,)), ((), ())), preferred_element_type=F32)
              for h in range(ATT_HEADS)]
    probs = []
    for h in range(ATT_HEADS):
        s = jnp.where(valid, scores[h] - _alibi_slope(h) * distf, -jnp.inf)
        sink = sink_ref[h]
        m = jnp.maximum(jnp.max(s, axis=1, keepdims=True), sink)
        e = jnp.exp(s - m)
        den = jnp.sum(e, axis=1, keepdims=True) + jnp.exp(sink - m)
        probs.append((e / den).astype(BF16))
    outs = [jnp.dot(probs[h], vbands[h // GROUP], preferred_element_type=F32)
            for h in range(ATT_HEADS)]
    o_ref[...] = jnp.concatenate(outs, axis=1)


def _swa_prompt(sinks, q, kv, nbatch, seq):
    nb = seq // BLOCK
    cur = lambda b, n: (b * nb + n, 0)
    prv = lambda b, n: (b * nb + jnp.maximum(n - 1, 0), 0)
    return pl.pallas_call(
        _swa_prompt_body,
        grid=(nbatch, nb),
        in_specs=[pl.BlockSpec(memory_space=pltpu.SMEM),
                  pl.BlockSpec((BLOCK, ATT_WIDTH), cur),
                  pl.BlockSpec((BLOCK, 2 * KV_COLS), cur),
                  pl.BlockSpec((BLOCK, 2 * KV_COLS), prv)],
        out_specs=pl.BlockSpec((BLOCK, ATT_WIDTH), cur),
        out_shape=jax.ShapeDtypeStruct((nbatch * seq, ATT_WIDTH), F32),
        compiler_params=_cparams(("parallel", "parallel")),
        name="swa_prompt",
    )(sinks, q, kv, kv)


def _swa_sample_body(sink_ref, q_ref, kvn_ref, ck_ref, cv_ref, o_ref, *, bb, w_buf):
    hrow = lax.broadcasted_iota(jnp.int32, (ATT_HEADS, 1), 0)
    slope = jnp.zeros((ATT_HEADS, 1), F32)
    sink = jnp.zeros((ATT_HEADS, 1), F32)
    for h in range(ATT_HEADS):
        slope = jnp.where(hrow == h, _alibi_slope(h), slope)
        sink = jnp.where(hrow == h, sink_ref[h], sink)
    jcol = lax.broadcasted_iota(jnp.int32, (ATT_HEADS, w_buf), 1)
    bias = -slope * (w_buf - jcol).astype(F32)
    lower = hrow < GROUP
    q2b = [(q_ref[b] * (HEAD_DIM ** -0.5)).astype(BF16) for b in range(bb)]
    s01 = []
    for b in range(bb):
        ck = ck_ref[b].astype(BF16)
        s01.append((_bdot_nt(q2b[b], ck[:, 0:HEAD_DIM]),
                    _bdot_nt(q2b[b], ck[:, HEAD_DIM:2 * HEAD_DIM])))
    ps = []
    for b in range(bb):
        kvn = kvn_ref[b]
        s = jnp.where(lower, s01[b][0], s01[b][1]) + bias
        knew = jnp.where(lower, kvn[0:1, :], kvn[1:2, :])
        vnew = jnp.where(lower, kvn[2:3, :], kvn[3:4, :])
        snew = jnp.sum(q2b[b].astype(F32) * knew.astype(BF16).astype(F32), axis=1, keepdims=True)
        m = jnp.maximum(jnp.maximum(jnp.max(s, axis=1, keepdims=True), snew), sink)
        e = jnp.exp(s - m)
        enew = jnp.exp(snew - m)
        den = jnp.sum(e, axis=1, keepdims=True) + enew + jnp.exp(sink - m)
        pnew = (enew / den).astype(BF16).astype(F32)
        ps.append(((e / den).astype(BF16), pnew * vnew.astype(BF16).astype(F32)))
    for b in range(bb):
        cv = cv_ref[b].astype(BF16)
        o0 = jnp.dot(ps[b][0], cv[:, 0:HEAD_DIM], preferred_element_type=F32)
        o1 = jnp.dot(ps[b][0], cv[:, HEAD_DIM:2 * HEAD_DIM], preferred_element_type=F32)
        o_ref[b] = jnp.where(lower, o0, o1) + ps[b][1]


def _swa_sample(sinks, q3, kvn3, ck3, cv3, bb=8):
    nb, w_buf = ck3.shape[0], ck3.shape[1]
    blk = lambda i: (i, 0, 0)
    return pl.pallas_call(
        functools.partial(_swa_sample_body, bb=bb, w_buf=w_buf),
        grid=(nb // bb,),
        in_specs=[pl.BlockSpec(memory_space=pltpu.SMEM),
                  pl.BlockSpec((bb, ATT_HEADS, HEAD_DIM), blk),
                  pl.BlockSpec((bb, 2 * KV_HEADS, HEAD_DIM), blk),
                  pl.BlockSpec((bb, w_buf, KV_COLS), blk),
                  pl.BlockSpec((bb, w_buf, KV_COLS), blk)],
        out_specs=pl.BlockSpec((bb, ATT_HEADS, HEAD_DIM), blk),
        out_shape=jax.ShapeDtypeStruct((nb, ATT_HEADS, HEAD_DIM), F32),
        compiler_params=_cparams(("parallel",)),
        name="swa_sample",
    )(sinks, q3, kvn3, ck3, cv3)


def _rwkv_prep(rp, prev, mu, w0, wlu, a0, alu, glu, k_k, k_a, bd):
    xs = rp + mu * (prev - rp)
    r = xs[:, 0:RWKV_WIDTH]
    k = xs[:, RWKV_WIDTH:2 * RWKV_WIDTH]
    v = xs[:, 2 * RWKV_WIDTH:3 * RWKV_WIDTH]
    o = 3 * RWKV_WIDTH
    wd = xs[:, o:o + LORA_W]
    ad = xs[:, o + LORA_W:o + LORA_W + LORA_A]
    gd = xs[:, o + LORA_W + LORA_A:]
    z = -(w0 + _bdot(jnp.tanh(wd), wlu))
    softplus = jnp.maximum(z, 0.0) + jnp.log(1.0 + jnp.exp(-jnp.abs(z)))
    log_decay = -jnp.exp(-softplus - 0.5)
    a = _sigmoid(a0 + _bdot(ad, alu))
    g = _bdot(_sigmoid(gd), glu)
    kkr = k * k_k
    kk = kkr / jnp.maximum(jnp.sqrt(_head_sum(kkr * kkr, bd)), 1e-12)
    k2 = k * (1.0 + (a - 1.0) * k_a)
    return r, log_decay, k2, v, kk, a, g


def _rwkv_finish(y, r, k2, v, g, r_k, gn_g, gn_b, bd):
    inv = 1.0 / HEAD_DIM
    mu = _head_sum(y, bd) * inv
    yc = y - mu
    var = _head_sum(yc * yc, bd) * inv
    yn = yc * lax.rsqrt(var + GN_EPS) * gn_g + gn_b
    bonus = _head_sum(r * k2 * r_k, bd) * v
    return (yn + bonus) * g


def _rwkv_prompt_body(rp_ref, mu_ref, w0_ref, wlu_ref, a0_ref, alu_ref, glu_ref, kk_ref, ka_ref,
                      rk_ref, gng_ref, gnb_ref, bd_ref, ltri_ref,
                      out_ref, sfin_ref, prev_scr, st_scr, y_scr, *, tt):
    c = pl.program_id(1)

    @pl.when(c == 0)
    def _():
        prev_scr[...] = jnp.zeros_like(prev_scr)
        st_scr[...] = jnp.zeros_like(st_scr)

    rp = rp_ref[...]
    rolled = pltpu.roll(rp, 1, 0)
    rowi = lax.broadcasted_iota(jnp.int32, (tt, 1), 0)
    prev = jnp.where(rowi == 0, prev_scr[...], rolled)
    prev_scr[...] = rp[tt - 1:tt, :]
    bd = bd_ref[...]
    r, ld, k2, v, kk, a, g = _rwkv_prep(rp, prev, mu_ref[...], w0_ref[...], wlu_ref[...],
                                        a0_ref[...], alu_ref[...], glu_ref[...], kk_ref[...],
                                        ka_ref[...], bd)
    cum = _split_dot(ltri_ref[...], ld, 2)
    ecum = jnp.exp(cum)
    einv = jnp.exp(-cum)
    at = -kk * jnp.exp(cum - ld)
    bh = kk * a * einv
    kh = k2 * einv
    rt = r * ecum

    qw = HEAD_QUAD * HEAD_DIM
    ii = lax.broadcasted_iota(jnp.int32, (CHUNK, qw), 0)
    jl = lax.broadcasted_iota(jnp.int32, (CHUNK, qw), 1) % CHUNK
    strict = ii > jl
    incl = ii >= jl
    eye = (ii == jl).astype(F32)
    same_head = (lax.broadcasted_iota(jnp.int32, (qw, qw), 0) // HEAD_DIM
                 == lax.broadcasted_iota(jnp.int32, (qw, qw), 1) // HEAD_DIM)

    def blockdiag(m):
        mb = m.astype(BF16)
        return jnp.where(same_head, jnp.concatenate([mb] * HEAD_QUAD, axis=0), jnp.zeros((), BF16))

    def diag_blocks(full):
        out = None
        for hh in range(HEAD_QUAD):
            rs = slice(hh * HEAD_DIM, (hh + 1) * HEAD_DIM)
            blk = jnp.where(same_head[rs], full[rs], 0.0)
            out = blk if out is None else out + blk
        return out

    mm = lambda x, w_bf16: jnp.dot(x.astype(BF16), w_bf16, preferred_element_type=F32)
    mm_nt = lambda x, w_bf16: lax.dot_general(x.astype(BF16), w_bf16, (((1,), (1,)), ((), ())),
                                              preferred_element_type=F32)
    nsub = tt // CHUNK
    units = [(s, q) for s in range(nsub) for q in range(RWKV_HEADS // HEAD_QUAD)]
    rows = lambda s: slice(s * CHUNK, (s + 1) * CHUNK)
    cols = lambda q: slice(q * qw, (q + 1) * qw)
    cut = lambda z, u: z[rows(u[0]), cols(u[1])]
    kka = kk * a
    p_last, kh_end, bh_end = [], [], []
    for s in range(nsub):
        c_last = cum[(s + 1) * CHUNK - 1:(s + 1) * CHUNK, :]
        p_last.append(jnp.exp(c_last))
        tail = jnp.exp(c_last - cum[rows(s)])
        kh_end.append(k2[rows(s)] * tail)
        bh_end.append(kka[rows(s)] * tail)
    x2 = {u: jnp.concatenate([cut(at, u), cut(rt, u)], axis=0) for u in units}
    xb = {u: mm_nt(x2[u], blockdiag(cut(bh, u))) for u in units}
    xk = {u: mm_nt(x2[u], blockdiag(cut(kh, u))) for u in units}
    nmat = {u: jnp.where(strict, xb[u][0:CHUNK], 0.0) for u in units}
    a_rb = {u: jnp.where(incl, xb[u][CHUNK:], 0.0) for u in units}
    a_ak = {u: jnp.where(strict, xk[u][0:CHUNK], 0.0) for u in units}
    a_rk = {u: jnp.where(incl, xk[u][CHUNK:], 0.0) for u in units}
    tmat = {u: eye + nmat[u] for u in units}
    npow = nmat
    for _ in range(5):
        npow = {u: mm(npow[u], blockdiag(npow[u])) for u in units}
        tmat = {u: tmat[u] + mm(npow[u], blockdiag(tmat[u])) for u in units}
    vbd = {u: blockdiag(cut(v, u)) for u in units}
    av = {u: mm(a_ak[u], vbd[u]) for u in units}
    aprime = {u: mm(tmat[u], blockdiag(cut(at, u))) for u in units}
    wmat = {u: mm(tmat[u], blockdiag(av[u])) for u in units}
    rprime = {u: cut(rt, u) + mm(a_rb[u], blockdiag(aprime[u])) for u in units}
    y0 = {u: mm(a_rk[u], vbd[u]) + mm(a_rb[u], blockdiag(wmat[u])) for u in units}
    hmat = {u: diag_blocks(_bdot_tn(jnp.concatenate([cut(v, u), wmat[u]], axis=0),
                                    jnp.concatenate([cut(kh_end[u[0]], (0, u[1])),
                                                     cut(bh_end[u[0]], (0, u[1]))], axis=0)))
            for u in units}
    g0 = {u: diag_blocks(_bdot_tn(aprime[u], cut(bh_end[u[0]], (0, u[1])))) for u in units}

    states = [st_scr[q] for q in range(RWKV_HEADS // HEAD_QUAD)]
    for s in range(nsub):
        for q in range(RWKV_HEADS // HEAD_QUAD):
            u = (s, q)
            st = states[q]
            y_scr[rows(s), cols(q)] = mm_nt(rprime[u], blockdiag(st)) + y0[u]
            states[q] = st * p_last[s][:, cols(q)] + mm(st, blockdiag(g0[u])) + hmat[u]
    for q in range(RWKV_HEADS // HEAD_QUAD):
        st_scr[q] = states[q]
        for hh in range(HEAD_QUAD):
            sfin_ref[0, q * HEAD_QUAD + hh] = states[q][:, hh * HEAD_DIM:(hh + 1) * HEAD_DIM]
    out_ref[...] = _rwkv_finish(y_scr[...], r, k2, v, g, rk_ref[...], gng_ref[...], gnb_ref[...], bd)


def _chunk_tril(tt):
    i = jnp.arange(tt)
    same = (i[:, None] // CHUNK) == (i[None, :] // CHUNK)
    return (same & (i[:, None] >= i[None, :])).astype(BF16)


def _head_blockdiag():
    i = jnp.arange(RWKV_WIDTH) // HEAD_DIM
    return (i[:, None] == i[None, :]).astype(BF16)


def _rwkv_prompt(rp, prm, nbatch, seq, tt):
    nc = seq // tt
    fixed = lambda b, c: (0, 0)
    row = lambda b, c: (b * nc + c, 0)
    vec = lambda n: pl.BlockSpec((1, n), fixed)
    return pl.pallas_call(
        functools.partial(_rwkv_prompt_body, tt=tt),
        grid=(nbatch, nc),
        in_specs=[pl.BlockSpec((tt, RWKV_PROJ), row),
                  vec(RWKV_PROJ), vec(RWKV_WIDTH),
                  pl.BlockSpec((LORA_W, RWKV_WIDTH), fixed), vec(RWKV_WIDTH),
                  pl.BlockSpec((LORA_A, RWKV_WIDTH), fixed),
                  pl.BlockSpec((LORA_G, RWKV_WIDTH), fixed),
                  vec(RWKV_WIDTH), vec(RWKV_WIDTH), vec(RWKV_WIDTH), vec(RWKV_WIDTH),
                  vec(RWKV_WIDTH),
                  pl.BlockSpec((RWKV_WIDTH, RWKV_WIDTH), fixed),
                  pl.BlockSpec((tt, tt), fixed)],
        out_specs=[pl.BlockSpec((tt, RWKV_WIDTH), row),
                   pl.BlockSpec((1, RWKV_HEADS, HEAD_DIM, HEAD_DIM), lambda b, c: (b, 0, 0, 0))],
        out_shape=[jax.ShapeDtypeStruct((nbatch * seq, RWKV_WIDTH), F32),
                   jax.ShapeDtypeStruct((nbatch, RWKV_HEADS, HEAD_DIM, HEAD_DIM), F32)],
        scratch_shapes=[pltpu.VMEM((1, RWKV_PROJ), F32),
                        pltpu.VMEM((RWKV_HEADS // HEAD_QUAD, HEAD_DIM, HEAD_QUAD * HEAD_DIM), F32),
                        pltpu.VMEM((tt, RWKV_WIDTH), F32)],
        compiler_params=_cparams(("parallel", "arbitrary")),
        name="rwkv_prompt",
    )(rp, prm["mu"], prm["w0"], prm["wlu"], prm["a0"], prm["alu"], prm["glu"], prm["k_k"],
      prm["k_a"], prm["r_k"], prm["gn_g"], prm["gn_b"], _head_blockdiag(), _chunk_tril(tt))


def _rwkv_sample_prep_body(rp_ref, prev_ref, mu_ref, w0_ref, wlu_ref, a0_ref, alu_ref, glu_ref,
                           kk_ref, ka_ref, bd_ref, o_ref):
    r, ld, k2, v, kk, a, g = _rwkv_prep(rp_ref[...], prev_ref[...], mu_ref[...], w0_ref[...],
                                        wlu_ref[...], a0_ref[...], alu_ref[...], glu_ref[...],
                                        kk_ref[...], ka_ref[...], bd_ref[...])
    for i, z in enumerate((r, jnp.exp(ld), k2, v, kk, a, g)):
        o_ref[i] = z


def _rwkv_sample_prep(rp, prev, prm):
    nb = rp.shape[0]
    return pl.pallas_call(
        _rwkv_sample_prep_body,
        out_shape=jax.ShapeDtypeStruct((7, nb, RWKV_WIDTH), F32),
        compiler_params=pltpu.CompilerParams(vmem_limit_bytes=VMEM_LIMIT),
        name="rwkv_sample_prep",
    )(rp, prev, prm["mu"], prm["w0"], prm["wlu"], prm["a0"], prm["alu"], prm["glu"], prm["k_k"],
      prm["k_a"], _head_blockdiag())


def _rwkv_sample_step_body(p_ref, s_ref, rk_ref, gng_ref, gnb_ref, so_ref, o_ref, *, bb):
    ii = lax.broadcasted_iota(jnp.int32, (HEAD_DIM, HEAD_DIM), 0)
    jj = lax.broadcasted_iota(jnp.int32, (HEAD_DIM, HEAD_DIM), 1)
    eye = ii == jj
    for b in range(bb):
        for h in range(RWKV_HEADS):
            row = lambda i: p_ref[i, b, h:h + 1, :]
            r_r, w_r, k_r, v_r, kk_r, a_r = (row(i) for i in range(6))
            st = s_ref[b, h]
            sa = jnp.sum(st * kk_r, axis=1, keepdims=True)
            v_col = jnp.sum(jnp.where(eye, v_r, 0.0), axis=1, keepdims=True)
            st = st * w_r - sa * (kk_r * a_r) + v_col * k_r
            so_ref[b, h] = st
            y_col = jnp.sum(st * r_r, axis=1, keepdims=True)
            o_ref[b, h:h + 1, :] = jnp.sum(jnp.where(eye, y_col, 0.0), axis=0, keepdims=True)
    y = o_ref[...]
    r, k2, v, g = p_ref[0], p_ref[2], p_ref[3], p_ref[6]
    mu = jnp.mean(y, axis=-1, keepdims=True)
    yc = y - mu
    var = jnp.mean(yc * yc, axis=-1, keepdims=True)
    yn = yc * lax.rsqrt(var + GN_EPS) * gng_ref[...] + gnb_ref[...]
    bonus = jnp.sum(r * k2 * rk_ref[...], axis=-1, keepdims=True) * v
    o_ref[...] = (yn + bonus) * g


def _rwkv_sample_step(p4, state, r_k, gn_g, gn_b, bb=8):
    nb = state.shape[0]
    hv = pl.BlockSpec((RWKV_HEADS, HEAD_DIM), lambda i: (0, 0))
    return pl.pallas_call(
        functools.partial(_rwkv_sample_step_body, bb=bb),
        grid=(nb // bb,),
        in_specs=[pl.BlockSpec((7, bb, RWKV_HEADS, HEAD_DIM), lambda i: (0, i, 0, 0)),
                  pl.BlockSpec((bb, RWKV_HEADS, HEAD_DIM, HEAD_DIM), lambda i: (i, 0, 0, 0)),
                  hv, hv, hv],
        out_specs=[pl.BlockSpec((bb, RWKV_HEADS, HEAD_DIM, HEAD_DIM), lambda i: (i, 0, 0, 0)),
                   pl.BlockSpec((bb, RWKV_HEADS, HEAD_DIM), lambda i: (i, 0, 0))],
        out_shape=[jax.ShapeDtypeStruct(state.shape, F32),
                   jax.ShapeDtypeStruct((nb, RWKV_HEADS, HEAD_DIM), F32)],
        compiler_params=_cparams(("parallel",)),
        name="rwkv_sample_step",
    )(p4, state, r_k, gn_g, gn_b)


def _mix_router_body(att_ref, rw_ref, xn_ref, wo_ref, g_ref, b_ref, wrh_ref, wrl_ref, br_ref,
                     base_ref, x1_ref, info_ref, gate_ref, cnt_ref, base_scr, *, tm):
    i = pl.program_id(0)

    @pl.when(i == 0)
    def _():
        base_scr[...] = base_ref[...]

    mixed = (jnp.dot(att_ref[...].astype(BF16), wo_ref[0:ATT_WIDTH, :], preferred_element_type=F32)
             + jnp.dot(rw_ref[...].astype(BF16), wo_ref[ATT_WIDTH:, :], preferred_element_type=F32))
    x1 = _layer_norm(DEEPNORM_ALPHA * xn_ref[...] + mixed, g_ref[...], b_ref[...])
    x1_ref[...] = x1

    x1h = x1.astype(BF16)
    x1l = (x1 - x1h.astype(F32)).astype(BF16)
    logits = (jnp.dot(x1h, wrh_ref[...], preferred_element_type=F32)
              + jnp.dot(x1l, wrh_ref[...], preferred_element_type=F32)
              + jnp.dot(x1h, wrl_ref[...], preferred_element_type=F32)) + br_ref[...]
    lane = lax.broadcasted_iota(jnp.int32, (tm, LANES), 1)
    lanef = lane.astype(F32)
    vals, idxs, hots = [], [], []
    cur = logits
    for _ in range(TOP_K):
        m = jnp.max(cur, axis=1, keepdims=True)
        idx = jnp.min(jnp.where(cur == m, lanef, float(LANES)), axis=1, keepdims=True)
        hot = lanef == idx
        cur = jnp.where(hot, -jnp.inf, cur)
        vals.append(m)
        idxs.append(idx)
        hots.append(hot)
    es = [jnp.exp(vk - vals[0]) for vk in vals]
    den = es[0] + es[1] + es[2] + es[3]
    multi = jnp.zeros((tm, LANES), F32)
    for hot in hots:
        multi = multi + hot.astype(F32)
    ti = lax.broadcasted_iota(jnp.int32, (tm, tm), 0)
    tj = lax.broadcasted_iota(jnp.int32, (tm, tm), 1)
    before = jnp.dot((ti > tj).astype(BF16), multi.astype(BF16), preferred_element_type=F32)
    before = before + base_scr[...]
    info = jnp.zeros((tm, LANES), F32)
    gates = jnp.zeros((tm, LANES), F32)
    for k in range(TOP_K):
        rank = jnp.sum(jnp.where(hots[k], before, 0.0), axis=1, keepdims=True)
        info = jnp.where(lane == k, idxs[k], info)
        info = jnp.where(lane == TOP_K + k, rank, info)
        gates = jnp.where(lane == k, es[k] / den, gates)
    info_ref[...] = jnp.transpose(info)[0:2 * TOP_K, :].astype(jnp.int32)
    gate_ref[...] = gates
    base_scr[...] = base_scr[...] + jnp.sum(multi, axis=0, keepdims=True)
    cnt_ref[...] = base_scr[...]


def _mix_router(att, rw, xn, wo_bf16, g, b, wr_hi, wr_lo, br_pad, base, tm):
    t = att.shape[0]
    row = lambda i: (i, 0)
    fixed = lambda i: (0, 0)
    return pl.pallas_call(
        functools.partial(_mix_router_body, tm=tm),
        grid=(t // tm,),
        in_specs=[pl.BlockSpec((tm, ATT_WIDTH), row),
                  pl.BlockSpec((tm, RWKV_WIDTH), row),
                  pl.BlockSpec((tm, D_MODEL), row),
                  pl.BlockSpec((D_MODEL, D_MODEL), fixed),
                  pl.BlockSpec((1, D_MODEL), fixed),
                  pl.BlockSpec((1, D_MODEL), fixed),
                  pl.BlockSpec((D_MODEL, LANES), fixed),
                  pl.BlockSpec((D_MODEL, LANES), fixed),
                  pl.BlockSpec((1, LANES), fixed),
                  pl.BlockSpec((1, LANES), fixed)],
        out_specs=[pl.BlockSpec((tm, D_MODEL), row),
                   pl.BlockSpec((2 * TOP_K, tm), lambda i: (0, i)),
                   pl.BlockSpec((tm, LANES), row),
                   pl.BlockSpec((1, LANES), fixed)],
        out_shape=[jax.ShapeDtypeStruct((t, D_MODEL), F32),
                   jax.ShapeDtypeStruct((2 * TOP_K, t), jnp.int32),
                   jax.ShapeDtypeStruct((t, LANES), F32),
                   jax.ShapeDtypeStruct((1, LANES), F32)],
        scratch_shapes=[pltpu.VMEM((1, LANES), F32)],
        compiler_params=_cparams(("arbitrary",)),
        name="mix_router",
    )(att, rw, xn, wo_bf16, g, b, wr_hi, wr_lo, br_pad, base)


def _row_copies(pos_ref, base, tm, make_copy):
    tokens = pos_ref.shape[0] // TOP_K

    def group(gi, carry):
        r0 = pl.multiple_of(gi * ROW_DMA_UNROLL, ROW_DMA_UNROLL)
        for j in range(ROW_DMA_UNROLL):
            for k in range(TOP_K):
                make_copy(r0 + j, k, pos_ref[k * tokens + base + r0 + j]).start(priority=k % 2)
        return carry

    lax.fori_loop(0, tm // ROW_DMA_UNROLL, group, 0)


def _dispatch_rows(pos_ref, base, x_ref, xs_ref, sem, rows):
    def make_copy(r, k, p):
        return pltpu.make_async_copy(x_ref.at[pl.ds(r, 1)], xs_ref.at[pl.ds(p, 1)], sem)

    _row_copies(pos_ref, base, rows, make_copy)
    for _ in range(TOP_K):
        pltpu.make_async_copy(x_ref, xs_ref.at[pl.ds(0, rows)], sem).wait()


def _dispatch_body(pos_a_ref, pos_b_ref, fill_ref, xa_ref, xb_ref, xs_ref, zero_scr, sem, fill_sem,
                   *, tm, n_tiles):
    i = pl.program_id(0)
    last = pl.num_programs(0) - 1

    @pl.when(i == 0)
    def _():
        zero_scr[...] = jnp.zeros_like(zero_scr)

        def tile_fill(start):
            return pltpu.make_async_copy(
                zero_scr, xs_ref.at[pl.ds(pl.multiple_of(start, EXPERT_TILE), EXPERT_TILE)], fill_sem)

        fills = [tile_fill(fill_ref[e]) for e in range(N_EXPERTS)]
        for cp in fills:
            cp.start()
        first_unused = fill_ref[N_EXPERTS]
        lax.fori_loop(first_unused, n_tiles,
                      lambda t, c: (tile_fill(t * EXPERT_TILE).start(), c)[1], 0)
        for cp in fills:
            cp.wait()
        lax.fori_loop(first_unused, n_tiles,
                      lambda t, c: (tile_fill(t * EXPERT_TILE).wait(), c)[1], 0)

    @pl.when(i < last)
    def _():
        _dispatch_rows(pos_a_ref, i * tm, xa_ref, xs_ref, sem, tm)

    @pl.when(i == last)
    def _():
        _dispatch_rows(pos_b_ref, 0, xb_ref, xs_ref, sem, xb_ref.shape[0])


def _dispatch(pos_a, pos_b, fill_start, x_a, x_b, n_slots, tm):
    nta = x_a.shape[0] // tm
    return pl.pallas_call(
        functools.partial(_dispatch_body, tm=tm, n_tiles=n_slots // EXPERT_TILE),
        grid_spec=pltpu.PrefetchScalarGridSpec(
            num_scalar_prefetch=3,
            grid=(nta + 1,),
            in_specs=[pl.BlockSpec((tm, D_MODEL), lambda i, *_: (jnp.minimum(i, nta - 1), 0)),
                      pl.BlockSpec(x_b.shape, lambda i, *_: (0, 0))],
            out_specs=pl.BlockSpec(memory_space=pl.ANY),
            scratch_shapes=[pltpu.VMEM((EXPERT_TILE, D_MODEL), F32),
                            pltpu.SemaphoreType.DMA, pltpu.SemaphoreType.DMA]),
        out_shape=jax.ShapeDtypeStruct((n_slots, D_MODEL), F32),
        compiler_params=_cparams(("arbitrary",)),
        name="moe_dispatch",
    )(pos_a, pos_b, fill_start, x_a, x_b)


def _experts_body(te_ref, nu_ref, xs_ref, wgu_ref, wd_ref, bg_ref, bu_ref, bd_ref, sel_ref, ys_ref,
                  wg_scr, wu_scr, wd_scr):
    i = pl.program_id(0)
    new_expert = (i == 0) | (te_ref[i] != te_ref[jnp.maximum(i - 1, 0)])

    @pl.when(new_expert)
    def _():
        wd_scr[...] = wd_ref[...].astype(BF16)
        for m in range(D_FF // LANES):
            pair = wgu_ref[:, 2 * m * LANES:2 * (m + 1) * LANES].astype(BF16)
            split = jnp.dot(pair, sel_ref[...], preferred_element_type=F32)
            wg_scr[:, m * LANES:(m + 1) * LANES] = split[:, 0:LANES].astype(BF16)
            wu_scr[:, m * LANES:(m + 1) * LANES] = split[:, LANES:].astype(BF16)

    @pl.when(i < nu_ref[0])
    def _():
        x = xs_ref[...].astype(BF16)
        gate = jnp.dot(x, wg_scr[...], preferred_element_type=F32) + bg_ref[...]
        up = jnp.dot(x, wu_scr[...], preferred_element_type=F32) + bu_ref[...]
        gate = jnp.minimum(gate, SWIGLU_LIMIT)
        up = jnp.clip(up, -SWIGLU_LIMIT, SWIGLU_LIMIT)
        act = (up + 1.0) * gate * _sigmoid(SWIGLU_ALPHA * gate)
        ys_ref[...] = jnp.dot(act.astype(BF16), wd_scr[...], preferred_element_type=F32) + bd_ref[...]

    @pl.when(i >= nu_ref[0])
    def _():
        ys_ref[...] = jnp.zeros_like(ys_ref)


def _gate_up_selector():
    i = jnp.arange(2 * LANES)
    src = jnp.where(i < LANES, 2 * i, 2 * (i - LANES) + 1)
    return (i[:, None] == src[None, :]).astype(BF16)


def _experts(tile_e, n_used, xs, wgu, wd, bg, bu, bd):
    ns = xs.shape[0]
    tm = EXPERT_TILE
    wspec = lambda a, b: pl.BlockSpec((None, a, b), lambda i, te, nu: (te[i], 0, 0))
    return pl.pallas_call(
        _experts_body,
        grid_spec=pltpu.PrefetchScalarGridSpec(
            num_scalar_prefetch=2,
            grid=(ns // tm,),
            in_specs=[pl.BlockSpec((tm, D_MODEL), lambda i, te, nu: (jnp.minimum(i, nu[0] - 1), 0)),
                      wspec(D_MODEL, 2 * D_FF), wspec(D_FF, D_MODEL),
                      wspec(1, D_FF), wspec(1, D_FF), wspec(1, D_MODEL),
                      pl.BlockSpec((2 * LANES, 2 * LANES), lambda i, te, nu: (0, 0))],
            out_specs=pl.BlockSpec((tm, D_MODEL), lambda i, te, nu: (i, 0)),
            scratch_shapes=[pltpu.VMEM((D_MODEL, D_FF), BF16), pltpu.VMEM((D_MODEL, D_FF), BF16),
                            pltpu.VMEM((D_FF, D_MODEL), BF16)]),
        out_shape=jax.ShapeDtypeStruct((ns, D_MODEL), F32),
        compiler_params=pltpu.CompilerParams(dimension_semantics=("arbitrary",),
                                             vmem_limit_bytes=EXPERTS_VMEM_LIMIT),
        name="moe_experts",
    )(tile_e, n_used, xs, wgu, wd, bg, bu, bd, _gate_up_selector())


def _combine_body(pos_ref, ys_ref, gate_ref, x1_ref, pe_ref, g_ref, b_ref, wple_ref, wpg_ref,
                  o_ref, buf, sem, *, tm):
    i = pl.program_id(0)
    cur = i % 2

    def gather(tile, slot):
        def make_copy(r, k, p):
            return pltpu.make_async_copy(ys_ref.at[pl.ds(p, 1)], buf.at[slot, k, pl.ds(r, 1)],
                                         sem.at[slot])
        _row_copies(pos_ref, tile * tm, tm, make_copy)

    @pl.when(i == 0)
    def _():
        gather(0, 0)

    for slot in range(2):
        @pl.when((i + 1 < pl.num_programs(0)) & (cur != slot))
        def _():
            gather(i + 1, slot)

    for k in range(TOP_K):
        pltpu.make_async_copy(ys_ref.at[pl.ds(0, tm)], buf.at[cur, k], sem.at[cur]).wait()
    gates = gate_ref[...]
    ffn = gates[:, 0:1] * buf[cur, 0]
    for k in range(1, TOP_K):
        ffn = ffn + gates[:, k:k + 1] * buf[cur, k]
    x2 = _layer_norm(DEEPNORM_ALPHA * x1_ref[...] + ffn, g_ref[...], b_ref[...])
    gate = _sigmoid(jnp.dot(x2.astype(BF16), wpg_ref[...], preferred_element_type=F32))
    emb = jnp.dot(pe_ref[...].astype(BF16), wple_ref[...], preferred_element_type=F32)
    o_ref[...] = x2 + gate * emb


def _combine(pos_flat, ys, gates, x1, pe, g, b, wple_bf16, wpg_bf16, tm):
    t = x1.shape[0]
    row = lambda i, *_: (i, 0)
    fixed = lambda i, *_: (0, 0)
    return pl.pallas_call(
        functools.partial(_combine_body, tm=tm),
        grid_spec=pltpu.PrefetchScalarGridSpec(
            num_scalar_prefetch=1,
            grid=(t // tm,),
            in_specs=[pl.BlockSpec(memory_space=pl.ANY),
                      pl.BlockSpec((tm, LANES), row),
                      pl.BlockSpec((tm, D_MODEL), row),
                      pl.BlockSpec((tm, PLE_DIM), row),
                      pl.BlockSpec((1, D_MODEL), fixed),
                      pl.BlockSpec((1, D_MODEL), fixed),
                      pl.BlockSpec((PLE_DIM, D_MODEL), fixed),
                      pl.BlockSpec((D_MODEL, D_MODEL), fixed)],
            out_specs=pl.BlockSpec((tm, D_MODEL), row),
            scratch_shapes=[pltpu.VMEM((2, TOP_K, tm, D_MODEL), F32),
                            pltpu.SemaphoreType.DMA((2,))]),
        out_shape=jax.ShapeDtypeStruct((t, D_MODEL), F32),
        compiler_params=_cparams(("arbitrary",)),
        name="moe_combine",
    )(pos_flat, ys, gates, x1, pe, g, b, wple_bf16, wpg_bf16)


def kernel(x_prompt, x_sample, cache_k, cache_v, state_shift, state_wkv, p_prompt, p_sample,
           ln_emb_g, ln_emb_b, w_in, attn_sinks, rwkv_mu, rwkv_w0, rwkv_w_lora_up, rwkv_a0,
           rwkv_a_lora_up, rwkv_g_lora_up, rwkv_k_k, rwkv_k_a, rwkv_r_k, rwkv_gn_g, rwkv_gn_b,
           w_out, ln1_g, ln1_b, w_router, b_router, w_gate_up, b_gate_up, w_down, b_down,
           ln2_g, ln2_b, w_ple, w_ple_gate):
    assert w_in.shape[0] == DEPTH == 1
    nbp, seq, _ = x_prompt.shape
    nbs, dec_seq, _ = x_sample.shape
    assert dec_seq == 1
    tp, ts = nbp * seq, nbs
    w_buf = cache_k.shape[2]
    rowv = lambda z: z.reshape(1, -1)

    w_in_b = w_in[0].astype(BF16)
    w_out_b = w_out[0].astype(BF16)
    w_ple_b = w_ple[0].astype(BF16)
    w_pg_b = w_ple_gate[0].astype(BF16)
    sinks = attn_sinks[0]
    prm = dict(mu=rowv(rwkv_mu[0]), w0=rowv(rwkv_w0[0]), wlu=rwkv_w_lora_up[0],
               a0=rowv(rwkv_a0[0]), alu=rwkv_a_lora_up[0], glu=rwkv_g_lora_up[0],
               k_k=rowv(rwkv_k_k[0]), k_a=rowv(rwkv_k_a[0]), r_k=rowv(rwkv_r_k[0]),
               gn_g=rowv(rwkv_gn_g[0]), gn_b=rowv(rwkv_gn_b[0]))
    ge, be = rowv(ln_emb_g), rowv(ln_emb_b)

    xn_p, q_p, kv_p, rp_p = _inproj(x_prompt.reshape(tp, D_MODEL), ge, be, w_in_b, 512)
    att_p = _swa_prompt(sinks, q_p, kv_p, nbp, seq)
    rw_p, wkv_p = _rwkv_prompt(rp_p, prm, nbp, seq, 256)

    xn_s, q_s, kv_s, rp_s = _inproj(x_sample.reshape(ts, D_MODEL), ge, be, w_in_b, ts)
    ck3 = cache_k[0].reshape(nbs, w_buf, KV_COLS)
    cv3 = cache_v[0].reshape(nbs, w_buf, KV_COLS)
    att_s = _swa_sample(sinks, q_s.reshape(nbs, ATT_HEADS, HEAD_DIM),
                        kv_s.reshape(nbs, 2 * KV_HEADS, HEAD_DIM), ck3, cv3)
    att_s = att_s.reshape(ts, ATT_WIDTH)
    p7 = _rwkv_sample_prep(rp_s, state_shift[0], prm)
    hv = lambda z: z.reshape(RWKV_HEADS, HEAD_DIM)
    wkv_s, rw_s = _rwkv_sample_step(p7.reshape(7, nbs, RWKV_HEADS, HEAD_DIM), state_wkv[0],
                                    hv(rwkv_r_k[0]), hv(rwkv_gn_g[0]), hv(rwkv_gn_b[0]))
    rw_s = rw_s.reshape(ts, RWKV_WIDTH)

    wr_pad = jnp.zeros((D_MODEL, LANES), F32).at[:, :N_EXPERTS].set(w_router[0])
    wr_hi = wr_pad.astype(BF16)
    wr_lo = (wr_pad - wr_hi.astype(F32)).astype(BF16)
    br_pad = jnp.full((1, LANES), -jnp.inf, F32).at[0, :N_EXPERTS].set(b_router[0])
    g1, b1 = rowv(ln1_g[0]), rowv(ln1_b[0])
    tmp, tmd = 512, 256
    x1_p, info_p, gate_p, cnt_p = _mix_router(att_p, rw_p, xn_p, w_out_b, g1, b1, wr_hi, wr_lo,
                                              br_pad, jnp.zeros((1, LANES), F32), tmp)
    x1_s, info_s, gate_s, cnt = _mix_router(att_s, rw_s, xn_s, w_out_b, g1, b1, wr_hi, wr_lo,
                                            br_pad, cnt_p, ts)

    counts = cnt[0, :N_EXPERTS].astype(jnp.int32)
    padded = ((counts + EXPERT_TILE - 1) // EXPERT_TILE) * EXPERT_TILE
    ends = jnp.cumsum(padded)
    offs = ends - padded
    n_slots = (tp + ts) * TOP_K + N_EXPERTS * EXPERT_TILE
    n_tiles = n_slots // EXPERT_TILE
    n_used = (ends[-1] // EXPERT_TILE).reshape(1).astype(jnp.int32)
    tile_start = jnp.arange(n_tiles, dtype=jnp.int32) * EXPERT_TILE
    tile_e = jnp.sum(ends[None, :] <= tile_start[:, None], axis=1).astype(jnp.int32)
    last_e = jnp.max(jnp.where(padded > 0, jnp.arange(N_EXPERTS), 0)).astype(jnp.int32)
    tile_e = jnp.minimum(tile_e, last_e)

    def slots(info):
        expert = jnp.arange(N_EXPERTS, dtype=jnp.int32)[:, None, None]
        first = jnp.sum(jnp.where(info[None, 0:TOP_K] == expert, offs[:, None, None], 0), axis=0)
        return (first + info[TOP_K:2 * TOP_K]).reshape(-1).astype(jnp.int32)

    pos_p, pos_s = slots(info_p), slots(info_s)

    fill_start = jnp.concatenate([jnp.clip(ends - EXPERT_TILE, 0, n_slots - EXPERT_TILE),
                                  n_used]).astype(jnp.int32)
    xs = _dispatch(pos_p, pos_s, fill_start, x1_p, x1_s, n_slots, tmd)
    bgu = b_gate_up[0]
    bg = bgu[:, 0::2].reshape(N_EXPERTS, 1, D_FF)
    bu = bgu[:, 1::2].reshape(N_EXPERTS, 1, D_FF)
    bdn = b_down[0].reshape(N_EXPERTS, 1, D_MODEL)
    ys = _experts(tile_e, n_used, xs, w_gate_up[0], w_down[0], bg, bu, bdn)

    g2, b2 = rowv(ln2_g[0]), rowv(ln2_b[0])
    y_p = _combine(pos_p, ys, gate_p, x1_p, p_prompt[0].reshape(tp, PLE_DIM), g2, b2,
                   w_ple_b, w_pg_b, tmd)
    y_s = _combine(pos_s, ys, gate_s, x1_s, p_sample[0].reshape(ts, PLE_DIM), g2, b2,
                   w_ple_b, w_pg_b, ts)

    w_keep = min(WINDOW, seq)
    kv_p3 = kv_p.reshape(nbp, seq, 2 * KV_COLS)[:, seq - w_keep:]
    k_win_p = kv_p3[:, :, 0:KV_COLS].reshape(1, nbp, w_keep, KV_HEADS, HEAD_DIM)
    v_win_p = kv_p3[:, :, KV_COLS:].reshape(1, nbp, w_keep, KV_HEADS, HEAD_DIM)
    shift_p = rp_p.reshape(nbp, seq, RWKV_PROJ)[:, seq - 1][None]
    k_new = kv_s[:, 0:KV_COLS].reshape(nbs, 1, KV_HEADS, HEAD_DIM)
    v_new = kv_s[:, KV_COLS:].reshape(nbs, 1, KV_HEADS, HEAD_DIM)
    k_win_s = jnp.concatenate([cache_k[0], k_new], axis=1)[:, 1:][None]
    v_win_s = jnp.concatenate([cache_v[0], v_new], axis=1)[:, 1:][None]
    return (y_p.reshape(nbp, seq, D_MODEL), y_s.reshape(nbs, 1, D_MODEL),
            k_win_p, v_win_p, shift_p, wkv_p[None],
            k_win_s, v_win_s, rp_s[None], wkv_s[None])
```

```python
import functools

import jax
import jax.numpy as jnp
from jax import lax
from jax.experimental import pallas as pl
from jax.experimental.pallas import tpu as pltpu

F32 = jnp.float32
BF16 = jnp.bfloat16
HIGHEST = lax.Precision.HIGHEST

D_MODEL = 1024
HEAD_DIM = 64
ATT_HEADS = 8
KV_HEADS = 2
GROUP = ATT_HEADS // KV_HEADS
ATT_WIDTH = ATT_HEADS * HEAD_DIM
KV_COLS = KV_HEADS * HEAD_DIM
RWKV_HEADS = 8
RWKV_WIDTH = RWKV_HEADS * HEAD_DIM
LORA_W = 64
LORA_A = 64
LORA_G = 128
RWKV_PROJ = 3 * RWKV_WIDTH + LORA_W + LORA_A + LORA_G
WINDOW = 128
BLOCK = 128
PLE_DIM = 256
N_EXPERTS = 32
TOP_K = 4
D_FF = 1024
SWIGLU_LIMIT = 7.0
SWIGLU_ALPHA = 1.702
LN_EPS = 1e-5
GN_EPS = 64e-5
DEPTH = 1
DEEPNORM_ALPHA = (2 * DEPTH) ** 0.25

LANES = 128
CHUNK = 64
HEAD_QUAD = 4
EXPERT_TILE = 512
ROW_DMA_UNROLL = 8
VMEM_LIMIT = 48 * 1024 * 1024
EXPERTS_VMEM_LIMIT = 58 * 1024 * 1024


def _cparams(sem):
    return pltpu.CompilerParams(dimension_semantics=sem, vmem_limit_bytes=VMEM_LIMIT)


def _bdot(a, b):
    return jnp.dot(a.astype(BF16), b.astype(BF16), preferred_element_type=F32)


def _bdot_nt(a, b):
    return lax.dot_general(a.astype(BF16), b.astype(BF16), (((1,), (1,)), ((), ())),
                           preferred_element_type=F32)


def _bdot_tn(a, b):
    return lax.dot_general(a.astype(BF16), b.astype(BF16), (((0,), (0,)), ((), ())),
                           preferred_element_type=F32)


def _split_dot(m_bf16, x, passes):
    acc = None
    rem = x
    for _ in range(passes):
        hi = rem.astype(BF16)
        part = jnp.dot(m_bf16, hi, preferred_element_type=F32)
        acc = part if acc is None else acc + part
        rem = rem - hi.astype(F32)
    return acc


def _head_sum(x, bd_bf16):
    acc = None
    rem = x
    for _ in range(2):
        hi = rem.astype(BF16)
        part = jnp.dot(hi, bd_bf16, preferred_element_type=F32)
        acc = part if acc is None else acc + part
        rem = rem - hi.astype(F32)
    return acc


def _sigmoid(x):
    return 1.0 / (1.0 + jnp.exp(-x))


def _layer_norm(x, g, b):
    mu = jnp.mean(x, axis=-1, keepdims=True)
    xc = x - mu
    var = jnp.mean(xc * xc, axis=-1, keepdims=True)
    return xc * lax.rsqrt(var + LN_EPS) * g + b


def _inproj_body(x_ref, g_ref, b_ref, w_ref, xn_ref, q_ref, kv_ref, rp_ref):
    xn = _layer_norm(x_ref[...], g_ref[...], b_ref[...])
    xn_ref[...] = xn
    xb = xn.astype(BF16)
    q_ref[...] = jnp.dot(xb, w_ref[:, 0:ATT_WIDTH], preferred_element_type=F32)
    kv_ref[...] = jnp.dot(xb, w_ref[:, ATT_WIDTH:ATT_WIDTH + 2 * KV_COLS],
                          preferred_element_type=F32)
    rp_ref[...] = jnp.dot(xb, w_ref[:, ATT_WIDTH + 2 * KV_COLS:], preferred_element_type=F32)


def _inproj(x2d, g, b, w_bf16, tm):
    t = x2d.shape[0]
    in_proj = w_bf16.shape[1]
    row = lambda i: (i, 0)
    fixed = lambda i: (0, 0)
    return pl.pallas_call(
        _inproj_body,
        grid=(t // tm,),
        in_specs=[pl.BlockSpec((tm, D_MODEL), row),
                  pl.BlockSpec((1, D_MODEL), fixed),
                  pl.BlockSpec((1, D_MODEL), fixed),
                  pl.BlockSpec((D_MODEL, in_proj), fixed)],
        out_specs=[pl.BlockSpec((tm, D_MODEL), row),
                   pl.BlockSpec((tm, ATT_WIDTH), row),
                   pl.BlockSpec((tm, 2 * KV_COLS), row),
                   pl.BlockSpec((tm, RWKV_PROJ), row)],
        out_shape=[jax.ShapeDtypeStruct((t, D_MODEL), F32),
                   jax.ShapeDtypeStruct((t, ATT_WIDTH), F32),
                   jax.ShapeDtypeStruct((t, 2 * KV_COLS), F32),
                   jax.ShapeDtypeStruct((t, RWKV_PROJ), F32)],
        compiler_params=_cparams(("parallel",)),
        name="inproj",
    )(x2d, g, b, w_bf16)


def _alibi_slope(h):
    return 2.0 ** (-8.0 * (h + 1) / ATT_HEADS)


def _swa_prompt_body(sink_ref, q_ref, kvc_ref, kvp_ref, o_ref):
    n = pl.program_id(1)
    q = (q_ref[...] * (HEAD_DIM ** -0.5)).astype(BF16)
    kvc = kvc_ref[...].astype(BF16)
    kvp = kvp_ref[...].astype(BF16)
    row = lax.broadcasted_iota(jnp.int32, (BLOCK, 2 * BLOCK), 0)
    col = lax.broadcasted_iota(jnp.int32, (BLOCK, 2 * BLOCK), 1)
    dist = row + BLOCK - col
    valid = (dist >= 0) & (dist <= WINDOW) & ((col >= BLOCK) | (n > 0))
    distf = dist.astype(F32)
    kbands, vbands = [], []
    for g in range(KV_HEADS):
        ks = slice(g * HEAD_DIM, (g + 1) * HEAD_DIM)
        vs = slice(KV_COLS + g * HEAD_DIM, KV_COLS + (g + 1) * HEAD_DIM)
        kbands.append(jnp.concatenate([kvp[:, ks], kvc[:, ks]], axis=0))
        vbands.append(jnp.concatenate([kvp[:, vs], kvc[:, vs]], axis=0))
    scores = [lax.dot_general(q[:, h * HEAD_DIM:(h + 1) * HEAD_DIM], kbands[h // GROUP],
                              (((1,), (1,)), ((), ())), preferred_element_type=F32)
              for h in range(ATT_HEADS)]
    probs = []
    for h in range(ATT_HEADS):
        s = jnp.where(valid, scores[h] - _alibi_slope(h) * distf, -jnp.inf)
        sink = sink_ref[h]
        m = jnp.maximum(jnp.max(s, axis=1, keepdims=True), sink)
        e = jnp.exp(s - m)
        den = jnp.sum(e, axis=1, keepdims=True) + jnp.exp(sink - m)
        probs.append((e / den).astype(BF16))
    outs = [jnp.dot(probs[h], vbands[h // GROUP], preferred_element_type=F32)
            for h in range(ATT_HEADS)]
    o_ref[...] = jnp.concatenate(outs, axis=1)


def _swa_prompt(sinks, q, kv, nbatch, seq):
    nb = seq // BLOCK
    cur = lambda b, n: (b * nb + n, 0)
    prv = lambda b, n: (b * nb + jnp.maximum(n - 1, 0), 0)
    return pl.pallas_call(
        _swa_prompt_body,
        grid=(nbatch, nb),
        in_specs=[pl.BlockSpec(memory_space=pltpu.SMEM),
                  pl.BlockSpec((BLOCK, ATT_WIDTH), cur),
                  pl.BlockSpec((BLOCK, 2 * KV_COLS), cur),
                  pl.BlockSpec((BLOCK, 2 * KV_COLS), prv)],
        out_specs=pl.BlockSpec((BLOCK, ATT_WIDTH), cur),
        out_shape=jax.ShapeDtypeStruct((nbatch * seq, ATT_WIDTH), F32),
        compiler_params=_cparams(("parallel", "parallel")),
        name="swa_prompt",
    )(sinks, q, kv, kv)


def _swa_sample_body(sink_ref, q_ref, kvn_ref, ck_ref, cv_ref, o_ref, *, bb, w_buf):
    hrow = lax.broadcasted_iota(jnp.int32, (ATT_HEADS, 1), 0)
    slope = jnp.zeros((ATT_HEADS, 1), F32)
    sink = jnp.zeros((ATT_HEADS, 1), F32)
    for h in range(ATT_HEADS):
        slope = jnp.where(hrow == h, _alibi_slope(h), slope)
        sink = jnp.where(hrow == h, sink_ref[h], sink)
    jcol = lax.broadcasted_iota(jnp.int32, (ATT_HEADS, w_buf), 1)
    bias = -slope * (w_buf - jcol).astype(F32)
    lower = hrow < GROUP
    q2b = [(q_ref[b] * (HEAD_DIM ** -0.5)).astype(BF16) for b in range(bb)]
    s01 = []
    for b in range(bb):
        ck = ck_ref[b].astype(BF16)
        s01.append((_bdot_nt(q2b[b], ck[:, 0:HEAD_DIM]),
                    _bdot_nt(q2b[b], ck[:, HEAD_DIM:2 * HEAD_DIM])))
    ps = []
    for b in range(bb):
        kvn = kvn_ref[b]
        s = jnp.where(lower, s01[b][0], s01[b][1]) + bias
        knew = jnp.where(lower, kvn[0:1, :], kvn[1:2, :])
        vnew = jnp.where(lower, kvn[2:3, :], kvn[3:4, :])
        snew = jnp.sum(q2b[b].astype(F32) * knew.astype(BF16).astype(F32), axis=1, keepdims=True)
        m = jnp.maximum(jnp.maximum(jnp.max(s, axis=1, keepdims=True), snew), sink)
        e = jnp.exp(s - m)
        enew = jnp.exp(snew - m)
        den = jnp.sum(e, axis=1, keepdims=True) + enew + jnp.exp(sink - m)
        pnew = (enew / den).astype(BF16).astype(F32)
        ps.append(((e / den).astype(BF16), pnew * vnew.astype(BF16).astype(F32)))
    for b in range(bb):
        cv = cv_ref[b].astype(BF16)
        o0 = jnp.dot(ps[b][0], cv[:, 0:HEAD_DIM], preferred_element_type=F32)
        o1 = jnp.dot(ps[b][0], cv[:, HEAD_DIM:2 * HEAD_DIM], preferred_element_type=F32)
        o_ref[b] = jnp.where(lower, o0, o1) + ps[b][1]


def _swa_sample(sinks, q3, kvn3, ck3, cv3, bb=8):
    nb, w_buf = ck3.shape[0], ck3.shape[1]
    blk = lambda i: (i, 0, 0)
    return pl.pallas_call(
        functools.partial(_swa_sample_body, bb=bb, w_buf=w_buf),
        grid=(nb // bb,),
        in_specs=[pl.BlockSpec(memory_space=pltpu.SMEM),
                  pl.BlockSpec((bb, ATT_HEADS, HEAD_DIM), blk),
                  pl.BlockSpec((bb, 2 * KV_HEADS, HEAD_DIM), blk),
                  pl.BlockSpec((bb, w_buf, KV_COLS), blk),
                  pl.BlockSpec((bb, w_buf, KV_COLS), blk)],
        out_specs=pl.BlockSpec((bb, ATT_HEADS, HEAD_DIM), blk),
        out_shape=jax.ShapeDtypeStruct((nb, ATT_HEADS, HEAD_DIM), F32),
        compiler_params=_cparams(("parallel",)),
        name="swa_sample",
    )(sinks, q3, kvn3, ck3, cv3)


def _rwkv_prep(rp, prev, mu, w0, wlu, a0, alu, glu, k_k, k_a, bd):
    xs = rp + mu * (prev - rp)
    r = xs[:, 0:RWKV_WIDTH]
    k = xs[:, RWKV_WIDTH:2 * RWKV_WIDTH]
    v = xs[:, 2 * RWKV_WIDTH:3 * RWKV_WIDTH]
    o = 3 * RWKV_WIDTH
    wd = xs[:, o:o + LORA_W]
    ad = xs[:, o + LORA_W:o + LORA_W + LORA_A]
    gd = xs[:, o + LORA_W + LORA_A:]
    z = -(w0 + _bdot(jnp.tanh(wd), wlu))
    softplus = jnp.maximum(z, 0.0) + jnp.log(1.0 + jnp.exp(-jnp.abs(z)))
    log_decay = -jnp.exp(-softplus - 0.5)
    a = _sigmoid(a0 + _bdot(ad, alu))
    g = _bdot(_sigmoid(gd), glu)
    kkr = k * k_k
    kk = kkr / jnp.maximum(jnp.sqrt(_head_sum(kkr * kkr, bd)), 1e-12)
    k2 = k * (1.0 + (a - 1.0) * k_a)
    return r, log_decay, k2, v, kk, a, g


def _rwkv_finish(y, r, k2, v, g, r_k, gn_g, gn_b, bd):
    inv = 1.0 / HEAD_DIM
    mu = _head_sum(y, bd) * inv
    yc = y - mu
    var = _head_sum(yc * yc, bd) * inv
    yn = yc * lax.rsqrt(var + GN_EPS) * gn_g + gn_b
    bonus = _head_sum(r * k2 * r_k, bd) * v
    return (yn + bonus) * g


def _rwkv_prompt_body(rp_ref, mu_ref, w0_ref, wlu_ref, a0_ref, alu_ref, glu_ref, kk_ref, ka_ref,
                      rk_ref, gng_ref, gnb_ref, bd_ref, ltri_ref,
                      out_ref, sfin_ref, prev_scr, st_scr, y_scr, *, tt):
    c = pl.program_id(1)

    @pl.when(c == 0)
    def _():
        prev_scr[...] = jnp.zeros_like(prev_scr)
        st_scr[...] = jnp.zeros_like(st_scr)

    rp = rp_ref[...]
    rolled = pltpu.roll(rp, 1, 0)
    rowi = lax.broadcasted_iota(jnp.int32, (tt, 1), 0)
    prev = jnp.where(rowi == 0, prev_scr[...], rolled)
    prev_scr[...] = rp[tt - 1:tt, :]
    bd = bd_ref[...]
    r, ld, k2, v, kk, a, g = _rwkv_prep(rp, prev, mu_ref[...], w0_ref[...], wlu_ref[...],
                                        a0_ref[...], alu_ref[...], glu_ref[...], kk_ref[...],
                                        ka_ref[...], bd)
    cum = _split_dot(ltri_ref[...], ld, 2)
    ecum = jnp.exp(cum)
    einv = jnp.exp(-cum)
    at = -kk * jnp.exp(cum - ld)
    bh = kk * a * einv
    kh = k2 * einv
    rt = r * ecum

    qw = HEAD_QUAD * HEAD_DIM
    ii = lax.broadcasted_iota(jnp.int32, (CHUNK, qw), 0)
    jl = lax.broadcasted_iota(jnp.int32, (CHUNK, qw), 1) % CHUNK
    strict = ii > jl
    incl = ii >= jl
    eye = (ii == jl).astype(F32)
    same_head = (lax.broadcasted_iota(jnp.int32, (qw, qw), 0) // HEAD_DIM
                 == lax.broadcasted_iota(jnp.int32, (qw, qw), 1) // HEAD_DIM)

    def blockdiag(m):
        mb = m.astype(BF16)
        return jnp.where(same_head, jnp.concatenate([mb] * HEAD_QUAD, axis=0), jnp.zeros((), BF16))

    def diag_blocks(full):
        out = None
        for hh in range(HEAD_QUAD):
            rs = slice(hh * HEAD_DIM, (hh + 1) * HEAD_DIM)
            blk = jnp.where(same_head[rs], full[rs], 0.0)
            out = blk if out is None else out + blk
        return out

    mm = lambda x, w_bf16: jnp.dot(x.astype(BF16), w_bf16, preferred_element_type=F32)
    mm_nt = lambda x, w_bf16: lax.dot_general(x.astype(BF16), w_bf16, (((1,), (1,)), ((), ())),
                                              preferred_element_type=F32)
    nsub = tt // CHUNK
    units = [(s, q) for s in range(nsub) for q in range(RWKV_HEADS // HEAD_QUAD)]
    rows = lambda s: slice(s * CHUNK, (s + 1) * CHUNK)
    cols = lambda q: slice(q * qw, (q + 1) * qw)
    cut = lambda z, u: z[rows(u[0]), cols(u[1])]
    kka = kk * a
    p_last, kh_end, bh_end = [], [], []
    for s in range(nsub):
        c_last = cum[(s + 1) * CHUNK - 1:(s + 1) * CHUNK, :]
        p_last.append(jnp.exp(c_last))
        tail = jnp.exp(c_last - cum[rows(s)])
        kh_end.append(k2[rows(s)] * tail)
        bh_end.append(kka[rows(s)] * tail)
    x2 = {u: jnp.concatenate([cut(at, u), cut(rt, u)], axis=0) for u in units}
    xb = {u: mm_nt(x2[u], blockdiag(cut(bh, u))) for u in units}
    xk = {u: mm_nt(x2[u], blockdiag(cut(kh, u))) for u in units}
    nmat = {u: jnp.where(strict, xb[u][0:CHUNK], 0.0) for u in units}
    a_rb = {u: jnp.where(incl, xb[u][CHUNK:], 0.0) for u in units}
    a_ak = {u: jnp.where(strict, xk[u][0:CHUNK], 0.0) for u in units}
    a_rk = {u: jnp.where(incl, xk[u][CHUNK:], 0.0) for u in units}
    tmat = {u: eye + nmat[u] for u in units}
    npow = nmat
    for _ in range(5):
        npow = {u: mm(npow[u], blockdiag(npow[u])) for u in units}
        tmat = {u: tmat[u] + mm(npow[u], blockdiag(tmat[u])) for u in units}
    vbd = {u: blockdiag(cut(v, u)) for u in units}
    av = {u: mm(a_ak[u], vbd[u]) for u in units}
    aprime = {u: mm(tmat[u], blockdiag(cut(at, u))) for u in units}
    wmat = {u: mm(tmat[u], blockdiag(av[u])) for u in units}
    rprime = {u: cut(rt, u) + mm(a_rb[u], blockdiag(aprime[u])) for u in units}
    y0 = {u: mm(a_rk[u], vbd[u]) + mm(a_rb[u], blockdiag(wmat[u])) for u in units}
    hmat = {u: diag_blocks(_bdot_tn(jnp.concatenate([cut(v, u), wmat[u]], axis=0),
                                    jnp.concatenate([cut(kh_end[u[0]], (0, u[1])),
                                                     cut(bh_end[u[0]], (0, u[1]))], axis=0)))
            for u in units}
    g0 = {u: diag_blocks(_bdot_tn(aprime[u], cut(bh_end[u[0]], (0, u[1])))) for u in units}

    states = [st_scr[q] for q in range(RWKV_HEADS // HEAD_QUAD)]
    for s in range(nsub):
        for q in range(RWKV_HEADS // HEAD_QUAD):
            u = (s, q)
            st = states[q]
            y_scr[rows(s), cols(q)] = mm_nt(rprime[u], blockdiag(st)) + y0[u]
            states[q] = st * p_last[s][:, cols(q)] + mm(st, blockdiag(g0[u])) + hmat[u]
    for q in range(RWKV_HEADS // HEAD_QUAD):
        st_scr[q] = states[q]
        for hh in range(HEAD_QUAD):
            sfin_ref[0, q * HEAD_QUAD + hh] = states[q][:, hh * HEAD_DIM:(hh + 1) * HEAD_DIM]
    out_ref[...] = _rwkv_finish(y_scr[...], r, k2, v, g, rk_ref[...], gng_ref[...], gnb_ref[...], bd)


def _chunk_tril(tt):
    i = jnp.arange(tt)
    same = (i[:, None] // CHUNK) == (i[None, :] // CHUNK)
    return (same & (i[:, None] >= i[None, :])).astype(BF16)


def _head_blockdiag():
    i = jnp.arange(RWKV_WIDTH) // HEAD_DIM
    return (i[:, None] == i[None, :]).astype(BF16)


def _rwkv_prompt(rp, prm, nbatch, seq, tt):
    nc = seq // tt
    fixed = lambda b, c: (0, 0)
    row = lambda b, c: (b * nc + c, 0)
    vec = lambda n: pl.BlockSpec((1, n), fixed)
    return pl.pallas_call(
        functools.partial(_rwkv_prompt_body, tt=tt),
        grid=(nbatch, nc),
        in_specs=[pl.BlockSpec((tt, RWKV_PROJ), row),
                  vec(RWKV_PROJ), vec(RWKV_WIDTH),
                  pl.BlockSpec((LORA_W, RWKV_WIDTH), fixed), vec(RWKV_WIDTH),
                  pl.BlockSpec((LORA_A, RWKV_WIDTH), fixed),
                  pl.BlockSpec((LORA_G, RWKV_WIDTH), fixed),
                  vec(RWKV_WIDTH), vec(RWKV_WIDTH), vec(RWKV_WIDTH), vec(RWKV_WIDTH),
                  vec(RWKV_WIDTH),
                  pl.BlockSpec((RWKV_WIDTH, RWKV_WIDTH), fixed),
                  pl.BlockSpec((tt, tt), fixed)],
        out_specs=[pl.BlockSpec((tt, RWKV_WIDTH), row),
                   pl.BlockSpec((1, RWKV_HEADS, HEAD_DIM, HEAD_DIM), lambda b, c: (b, 0, 0, 0))],
        out_shape=[jax.ShapeDtypeStruct((nbatch * seq, RWKV_WIDTH), F32),
                   jax.ShapeDtypeStruct((nbatch, RWKV_HEADS, HEAD_DIM, HEAD_DIM), F32)],
        scratch_shapes=[pltpu.VMEM((1, RWKV_PROJ), F32),
                        pltpu.VMEM((RWKV_HEADS // HEAD_QUAD, HEAD_DIM, HEAD_QUAD * HEAD_DIM), F32),
                        pltpu.VMEM((tt, RWKV_WIDTH), F32)],
        compiler_params=_cparams(("parallel", "arbitrary")),
        name="rwkv_prompt",
    )(rp, prm["mu"], prm["w0"], prm["wlu"], prm["a0"], prm["alu"], prm["glu"], prm["k_k"],
      prm["k_a"], prm["r_k"], prm["gn_g"], prm["gn_b"], _head_blockdiag(), _chunk_tril(tt))


def _rwkv_sample_prep_body(rp_ref, prev_ref, mu_ref, w0_ref, wlu_ref, a0_ref, alu_ref, glu_ref,
                           kk_ref, ka_ref, bd_ref, o_ref):
    r, ld, k2, v, kk, a, g = _rwkv_prep(rp_ref[...], prev_ref[...], mu_ref[...], w0_ref[...],
                                        wlu_ref[...], a0_ref[...], alu_ref[...], glu_ref[...],
                                        kk_ref[...], ka_ref[...], bd_ref[...])
    for i, z in enumerate((r, jnp.exp(ld), k2, v, kk, a, g)):
        o_ref[i] = z


def _rwkv_sample_prep(rp, prev, prm):
    nb = rp.shape[0]
    return pl.pallas_call(
        _rwkv_sample_prep_body,
        out_shape=jax.ShapeDtypeStruct((7, nb, RWKV_WIDTH), F32),
        compiler_params=pltpu.CompilerParams(vmem_limit_bytes=VMEM_LIMIT),
        name="rwkv_sample_prep",
    )(rp, prev, prm["mu"], prm["w0"], prm["wlu"], prm["a0"], prm["alu"], prm["glu"], prm["k_k"],
      prm["k_a"], _head_blockdiag())


def _rwkv_sample_step_body(p_ref, s_ref, rk_ref, gng_ref, gnb_ref, so_ref, o_ref, *, bb):
    ii = lax.broadcasted_iota(jnp.int32, (HEAD_DIM, HEAD_DIM), 0)
    jj = lax.broadcasted_iota(jnp.int32, (HEAD_DIM, HEAD_DIM), 1)
    eye = ii == jj
    for b in range(bb):
        for h in range(RWKV_HEADS):
            row = lambda i: p_ref[i, b, h:h + 1, :]
            r_r, w_r, k_r, v_r, kk_r, a_r = (row(i) for i in range(6))
            st = s_ref[b, h]
            sa = jnp.sum(st * kk_r, axis=1, keepdims=True)
            v_col = jnp.sum(jnp.where(eye, v_r, 0.0), axis=1, keepdims=True)
            st = st * w_r - sa * (kk_r * a_r) + v_col * k_r
            so_ref[b, h] = st
            y_col = jnp.sum(st * r_r, axis=1, keepdims=True)
            o_ref[b, h:h + 1, :] = jnp.sum(jnp.where(eye, y_col, 0.0), axis=0, keepdims=True)
    y = o_ref[...]
    r, k2, v, g = p_ref[0], p_ref[2], p_ref[3], p_ref[6]
    mu = jnp.mean(y, axis=-1, keepdims=True)
    yc = y - mu
    var = jnp.mean(yc * yc, axis=-1, keepdims=True)
    yn = yc * lax.rsqrt(var + GN_EPS) * gng_ref[...] + gnb_ref[...]
    bonus = jnp.sum(r * k2 * rk_ref[...], axis=-1, keepdims=True) * v
    o_ref[...] = (yn + bonus) * g


def _rwkv_sample_step(p4, state, r_k, gn_g, gn_b, bb=8):
    nb = state.shape[0]
    hv = pl.BlockSpec((RWKV_HEADS, HEAD_DIM), lambda i: (0, 0))
    return pl.pallas_call(
        functools.partial(_rwkv_sample_step_body, bb=bb),
        grid=(nb // bb,),
        in_specs=[pl.BlockSpec((7, bb, RWKV_HEADS, HEAD_DIM), lambda i: (0, i, 0, 0)),
                  pl.BlockSpec((bb, RWKV_HEADS, HEAD_DIM, HEAD_DIM), lambda i: (i, 0, 0, 0)),
                  hv, hv, hv],
        out_specs=[pl.BlockSpec((bb, RWKV_HEADS, HEAD_DIM, HEAD_DIM), lambda i: (i, 0, 0, 0)),
                   pl.BlockSpec((bb, RWKV_HEADS, HEAD_DIM), lambda i: (i, 0, 0))],
        out_shape=[jax.ShapeDtypeStruct(state.shape, F32),
                   jax.ShapeDtypeStruct((nb, RWKV_HEADS, HEAD_DIM), F32)],
        compiler_params=_cparams(("parallel",)),
        name="rwkv_sample_step",
    )(p4, state, r_k, gn_g, gn_b)


def _mix_router_body(att_ref, rw_ref, xn_ref, wo_ref, g_ref, b_ref, wrh_ref, wrl_ref, br_ref,
                     base_ref, x1_ref, info_ref, gate_ref, cnt_ref, base_scr, *, tm):
    i = pl.program_id(0)

    @pl.when(i == 0)
    def _():
        base_scr[...] = base_ref[...]

    mixed = (jnp.dot(att_ref[...].astype(BF16), wo_ref[0:ATT_WIDTH, :], preferred_element_type=F32)
             + jnp.dot(rw_ref[...].astype(BF16), wo_ref[ATT_WIDTH:, :], preferred_element_type=F32))
    x1 = _layer_norm(DEEPNORM_ALPHA * xn_ref[...] + mixed, g_ref[...], b_ref[...])
    x1_ref[...] = x1

    x1h = x1.astype(BF16)
    x1l = (x1 - x1h.astype(F32)).astype(BF16)
    logits = (jnp.dot(x1h, wrh_ref[...], preferred_element_type=F32)
              + jnp.dot(x1l, wrh_ref[...], preferred_element_type=F32)
              + jnp.dot(x1h, wrl_ref[...], preferred_element_type=F32)) + br_ref[...]
    lane = lax.broadcasted_iota(jnp.int32, (tm, LANES), 1)
    lanef = lane.astype(F32)
    vals, idxs, hots = [], [], []
    cur = logits
    for _ in range(TOP_K):
        m = jnp.max(cur, axis=1, keepdims=True)
        idx = jnp.min(jnp.where(cur == m, lanef, float(LANES)), axis=1, keepdims=True)
        hot = lanef == idx
        cur = jnp.where(hot, -jnp.inf, cur)
        vals.append(m)
        idxs.append(idx)
        hots.append(hot)
    es = [jnp.exp(vk - vals[0]) for vk in vals]
    den = es[0] + es[1] + es[2] + es[3]
    multi = jnp.zeros((tm, LANES), F32)
    for hot in hots:
        multi = multi + hot.astype(F32)
    ti = lax.broadcasted_iota(jnp.int32, (tm, tm), 0)
    tj = lax.broadcasted_iota(jnp.int32, (tm, tm), 1)
    before = jnp.dot((ti > tj).astype(BF16), multi.astype(BF16), preferred_element_type=F32)
    before = before + base_scr[...]
    info = jnp.zeros((tm, LANES), F32)
    gates = jnp.zeros((tm, LANES), F32)
    for k in range(TOP_K):
        rank = jnp.sum(jnp.where(hots[k], before, 0.0), axis=1, keepdims=True)
        info = jnp.where(lane == k, idxs[k], info)
        info = jnp.where(lane == TOP_K + k, rank, info)
        gates = jnp.where(lane == k, es[k] / den, gates)
    info_ref[...] = jnp.transpose(info)[0:2 * TOP_K, :].astype(jnp.int32)
    gate_ref[...] = gates
    base_scr[...] = base_scr[...] + jnp.sum(multi, axis=0, keepdims=True)
    cnt_ref[...] = base_scr[...]


def _mix_router(att, rw, xn, wo_bf16, g, b, wr_hi, wr_lo, br_pad, base, tm):
    t = att.shape[0]
    row = lambda i: (i, 0)
    fixed = lambda i: (0, 0)
    return pl.pallas_call(
        functools.partial(_mix_router_body, tm=tm),
        grid=(t // tm,),
        in_specs=[pl.BlockSpec((tm, ATT_WIDTH), row),
                  pl.BlockSpec((tm, RWKV_WIDTH), row),
                  pl.BlockSpec((tm, D_MODEL), row),
                  pl.BlockSpec((D_MODEL, D_MODEL), fixed),
                  pl.BlockSpec((1, D_MODEL), fixed),
                  pl.BlockSpec((1, D_MODEL), fixed),
                  pl.BlockSpec((D_MODEL, LANES), fixed),
                  pl.BlockSpec((D_MODEL, LANES), fixed),
                  pl.BlockSpec((1, LANES), fixed),
                  pl.BlockSpec((1, LANES), fixed)],
        out_specs=[pl.BlockSpec((tm, D_MODEL), row),
                   pl.BlockSpec((2 * TOP_K, tm), lambda i: (0, i)),
                   pl.BlockSpec((tm, LANES), row),
                   pl.BlockSpec((1, LANES), fixed)],
        out_shape=[jax.ShapeDtypeStruct((t, D_MODEL), F32),
                   jax.ShapeDtypeStruct((2 * TOP_K, t), jnp.int32),
                   jax.ShapeDtypeStruct((t, LANES), F32),
                   jax.ShapeDtypeStruct((1, LANES), F32)],
        scratch_shapes=[pltpu.VMEM((1, LANES), F32)],
        compiler_params=_cparams(("arbitrary",)),
        name="mix_router",
    )(att, rw, xn, wo_bf16, g, b, wr_hi, wr_lo, br_pad, base)


def _row_copies(pos_ref, base, tm, make_copy):
    tokens = pos_ref.shape[0] // TOP_K

    def group(gi, carry):
        r0 = pl.multiple_of(gi * ROW_DMA_UNROLL, ROW_DMA_UNROLL)
        for j in range(ROW_DMA_UNROLL):
            for k in range(TOP_K):
                make_copy(gi, j, k, pos_ref[k * tokens + base + r0 + j]).start(priority=k % 2)
        return carry

    lax.fori_loop(0, tm // ROW_DMA_UNROLL, group, 0)


def _dispatch_rows(pos_ref, base, x_ref, xs_ref, sem, rows):
    def make_copy(gi, j, k, p):
        r = pl.multiple_of(gi * ROW_DMA_UNROLL, ROW_DMA_UNROLL) + j
        return pltpu.make_async_copy(x_ref.at[pl.ds(r, 1)], xs_ref.at[pl.ds(p, 1)], sem)

    _row_copies(pos_ref, base, rows, make_copy)
    for _ in range(TOP_K):
        pltpu.make_async_copy(x_ref, xs_ref.at[pl.ds(0, rows)], sem).wait()


def _dispatch_body(pos_a_ref, pos_b_ref, fill_ref, xa_ref, xb_ref, xs_ref, zero_scr, sem, fill_sem,
                   *, tm, n_tiles):
    i = pl.program_id(0)
    last = pl.num_programs(0) - 1

    @pl.when(i == 0)
    def _():
        zero_scr[...] = jnp.zeros_like(zero_scr)

        def tile_fill(start):
            return pltpu.make_async_copy(
                zero_scr, xs_ref.at[pl.ds(pl.multiple_of(start, EXPERT_TILE), EXPERT_TILE)], fill_sem)

        fills = [tile_fill(fill_ref[e]) for e in range(N_EXPERTS)]
        for cp in fills:
            cp.start()
        first_unused = fill_ref[N_EXPERTS]
        lax.fori_loop(first_unused, n_tiles,
                      lambda t, c: (tile_fill(t * EXPERT_TILE).start(), c)[1], 0)
        for cp in fills:
            cp.wait()
        lax.fori_loop(first_unused, n_tiles,
                      lambda t, c: (tile_fill(t * EXPERT_TILE).wait(), c)[1], 0)

    @pl.when(i < last)
    def _():
        _dispatch_rows(pos_a_ref, i * tm, xa_ref, xs_ref, sem, tm)

    @pl.when(i == last)
    def _():
        _dispatch_rows(pos_b_ref, 0, xb_ref, xs_ref, sem, xb_ref.shape[0])


def _dispatch(pos_a, pos_b, fill_start, x_a, x_b, n_slots, tm):
    nta = x_a.shape[0] // tm
    return pl.pallas_call(
        functools.partial(_dispatch_body, tm=tm, n_tiles=n_slots // EXPERT_TILE),
        grid_spec=pltpu.PrefetchScalarGridSpec(
            num_scalar_prefetch=3,
            grid=(nta + 1,),
            in_specs=[pl.BlockSpec((tm, D_MODEL), lambda i, *_: (jnp.minimum(i, nta - 1), 0)),
                      pl.BlockSpec(x_b.shape, lambda i, *_: (0, 0))],
            out_specs=pl.BlockSpec(memory_space=pl.ANY),
            scratch_shapes=[pltpu.VMEM((EXPERT_TILE, D_MODEL), F32),
                            pltpu.SemaphoreType.DMA, pltpu.SemaphoreType.DMA]),
        out_shape=jax.ShapeDtypeStruct((n_slots, D_MODEL), F32),
        compiler_params=_cparams(("arbitrary",)),
        name="moe_dispatch",
    )(pos_a, pos_b, fill_start, x_a, x_b)


def _experts_body(te_ref, nu_ref, xs_ref, wgu_ref, wd_ref, bg_ref, bu_ref, bd_ref, sel_ref, ys_ref,
                  wg_scr, wu_scr, wd_scr):
    i = pl.program_id(0)
    new_expert = (i == 0) | (te_ref[i] != te_ref[jnp.maximum(i - 1, 0)])

    @pl.when(new_expert)
    def _():
        wd_scr[...] = wd_ref[...].astype(BF16)
        for m in range(D_FF // LANES):
            pair = wgu_ref[:, 2 * m * LANES:2 * (m + 1) * LANES].astype(BF16)
            split = jnp.dot(pair, sel_ref[...], preferred_element_type=F32)
            wg_scr[:, m * LANES:(m + 1) * LANES] = split[:, 0:LANES].astype(BF16)
            wu_scr[:, m * LANES:(m + 1) * LANES] = split[:, LANES:].astype(BF16)

    @pl.when(i < nu_ref[0])
    def _():
        x = xs_ref[...].astype(BF16)
        gate = jnp.dot(x, wg_scr[...], preferred_element_type=F32) + bg_ref[...]
        up = jnp.dot(x, wu_scr[...], preferred_element_type=F32) + bu_ref[...]
        gate = jnp.minimum(gate, SWIGLU_LIMIT)
        up = jnp.clip(up, -SWIGLU_LIMIT, SWIGLU_LIMIT)
        act = (up + 1.0) * gate * _sigmoid(SWIGLU_ALPHA * gate)
        ys_ref[...] = jnp.dot(act.astype(BF16), wd_scr[...], preferred_element_type=F32) + bd_ref[...]

    @pl.when(i >= nu_ref[0])
    def _():
        ys_ref[...] = jnp.zeros_like(ys_ref)


def _gate_up_selector():
    i = jnp.arange(2 * LANES)
    src = jnp.where(i < LANES, 2 * i, 2 * (i - LANES) + 1)
    return (i[:, None] == src[None, :]).astype(BF16)


def _experts(tile_e, n_used, xs, wgu, wd, bg, bu, bd):
    ns = xs.shape[0]
    tm = EXPERT_TILE
    wspec = lambda a, b: pl.BlockSpec((None, a, b), lambda i, te, nu: (te[i], 0, 0))
    return pl.pallas_call(
        _experts_body,
        grid_spec=pltpu.PrefetchScalarGridSpec(
            num_scalar_prefetch=2,
            grid=(ns // tm,),
            in_specs=[pl.BlockSpec((tm, D_MODEL), lambda i, te, nu: (jnp.minimum(i, nu[0] - 1), 0)),
                      wspec(D_MODEL, 2 * D_FF), wspec(D_FF, D_MODEL),
                      wspec(1, D_FF), wspec(1, D_FF), wspec(1, D_MODEL),
                      pl.BlockSpec((2 * LANES, 2 * LANES), lambda i, te, nu: (0, 0))],
            out_specs=pl.BlockSpec((tm, D_MODEL), lambda i, te, nu: (i, 0)),
            scratch_shapes=[pltpu.VMEM((D_MODEL, D_FF), BF16), pltpu.VMEM((D_MODEL, D_FF), BF16),
                            pltpu.VMEM((D_FF, D_MODEL), BF16)]),
        out_shape=jax.ShapeDtypeStruct((ns, D_MODEL), F32),
        compiler_params=pltpu.CompilerParams(dimension_semantics=("arbitrary",),
                                             vmem_limit_bytes=EXPERTS_VMEM_LIMIT),
        name="moe_experts",
    )(tile_e, n_used, xs, wgu, wd, bg, bu, bd, _gate_up_selector())


def _combine_body(pos_ref, ys_ref, gate_ref, x1_ref, pe_ref, g_ref, b_ref, wple_ref, wpg_ref,
                  o_ref, buf, sem, *, tm):
    i = pl.program_id(0)
    cur = i % 2

    def gather(tile, slot):
        def make_copy(gi, j, k, p):
            return pltpu.make_async_copy(ys_ref.at[pl.ds(p, 1)], buf.at[slot, k, gi, pl.ds(j, 1)],
                                         sem.at[slot])
        _row_copies(pos_ref, tile * tm, tm, make_copy)

    @pl.when(i == 0)
    def _():
        gather(0, 0)

    for slot in range(2):
        @pl.when((i + 1 < pl.num_programs(0)) & (cur != slot))
        def _():
            gather(i + 1, slot)

    for k in range(TOP_K):
        pltpu.make_async_copy(buf.at[cur, k], buf.at[cur, k], sem.at[cur]).wait()
    gates = gate_ref[...]
    picked = lambda k: buf[cur, k].reshape(tm, D_MODEL)
    ffn = gates[:, 0:1] * picked(0)
    for k in range(1, TOP_K):
        ffn = ffn + gates[:, k:k + 1] * picked(k)
    x2 = _layer_norm(DEEPNORM_ALPHA * x1_ref[...] + ffn, g_ref[...], b_ref[...])
    gate = _sigmoid(jnp.dot(x2.astype(BF16), wpg_ref[...], preferred_element_type=F32))
    emb = jnp.dot(pe_ref[...].astype(BF16), wple_ref[...], preferred_element_type=F32)
    o_ref[...] = x2 + gate * emb


def _combine(pos_flat, ys, gates, x1, pe, g, b, wple_bf16, wpg_bf16, tm):
    t = x1.shape[0]
    row = lambda i, *_: (i, 0)
    fixed = lambda i, *_: (0, 0)
    return pl.pallas_call(
        functools.partial(_combine_body, tm=tm),
        grid_spec=pltpu.PrefetchScalarGridSpec(
            num_scalar_prefetch=1,
            grid=(t // tm,),
            in_specs=[pl.BlockSpec(memory_space=pl.ANY),
                      pl.BlockSpec((tm, LANES), row),
                      pl.BlockSpec((tm, D_MODEL), row),
                      pl.BlockSpec((tm, PLE_DIM), row),
                      pl.BlockSpec((1, D_MODEL), fixed),
                      pl.BlockSpec((1, D_MODEL), fixed),
                      pl.BlockSpec((PLE_DIM, D_MODEL), fixed),
                      pl.BlockSpec((D_MODEL, D_MODEL), fixed)],
            out_specs=pl.BlockSpec((tm, D_MODEL), row),
            scratch_shapes=[pltpu.VMEM((2, TOP_K, tm // ROW_DMA_UNROLL, ROW_DMA_UNROLL, D_MODEL), F32),
                            pltpu.SemaphoreType.DMA((2,))]),
        out_shape=jax.ShapeDtypeStruct((t, D_MODEL), F32),
        compiler_params=_cparams(("arbitrary",)),
        name="moe_combine",
    )(pos_flat, ys, gates, x1, pe, g, b, wple_bf16, wpg_bf16)


def kernel(x_prompt, x_sample, cache_k, cache_v, state_shift, state_wkv, p_prompt, p_sample,
           ln_emb_g, ln_emb_b, w_in, attn_sinks, rwkv_mu, rwkv_w0, rwkv_w_lora_up, rwkv_a0,
           rwkv_a_lora_up, rwkv_g_lora_up, rwkv_k_k, rwkv_k_a, rwkv_r_k, rwkv_gn_g, rwkv_gn_b,
           w_out, ln1_g, ln1_b, w_router, b_router, w_gate_up, b_gate_up, w_down, b_down,
           ln2_g, ln2_b, w_ple, w_ple_gate):
    assert w_in.shape[0] == DEPTH == 1
    nbp, seq, _ = x_prompt.shape
    nbs, dec_seq, _ = x_sample.shape
    assert dec_seq == 1
    tp, ts = nbp * seq, nbs
    w_buf = cache_k.shape[2]
    rowv = lambda z: z.reshape(1, -1)

    w_in_b = w_in[0].astype(BF16)
    w_out_b = w_out[0].astype(BF16)
    w_ple_b = w_ple[0].astype(BF16)
    w_pg_b = w_ple_gate[0].astype(BF16)
    sinks = attn_sinks[0]
    prm = dict(mu=rowv(rwkv_mu[0]), w0=rowv(rwkv_w0[0]), wlu=rwkv_w_lora_up[0],
               a0=rowv(rwkv_a0[0]), alu=rwkv_a_lora_up[0], glu=rwkv_g_lora_up[0],
               k_k=rowv(rwkv_k_k[0]), k_a=rowv(rwkv_k_a[0]), r_k=rowv(rwkv_r_k[0]),
               gn_g=rowv(rwkv_gn_g[0]), gn_b=rowv(rwkv_gn_b[0]))
    ge, be = rowv(ln_emb_g), rowv(ln_emb_b)

    xn_p, q_p, kv_p, rp_p = _inproj(x_prompt.reshape(tp, D_MODEL), ge, be, w_in_b, 512)
    att_p = _swa_prompt(sinks, q_p, kv_p, nbp, seq)
    rw_p, wkv_p = _rwkv_prompt(rp_p, prm, nbp, seq, 256)

    xn_s, q_s, kv_s, rp_s = _inproj(x_sample.reshape(ts, D_MODEL), ge, be, w_in_b, ts)
    ck3 = cache_k[0].reshape(nbs, w_buf, KV_COLS)
    cv3 = cache_v[0].reshape(nbs, w_buf, KV_COLS)
    att_s = _swa_sample(sinks, q_s.reshape(nbs, ATT_HEADS, HEAD_DIM),
                        kv_s.reshape(nbs, 2 * KV_HEADS, HEAD_DIM), ck3, cv3)
    att_s = att_s.reshape(ts, ATT_WIDTH)
    p7 = _rwkv_sample_prep(rp_s, state_shift[0], prm)
    hv = lambda z: z.reshape(RWKV_HEADS, HEAD_DIM)
    wkv_s, rw_s = _rwkv_sample_step(p7.reshape(7, nbs, RWKV_HEADS, HEAD_DIM), state_wkv[0],
                                    hv(rwkv_r_k[0]), hv(rwkv_gn_g[0]), hv(rwkv_gn_b[0]))
    rw_s = rw_s.reshape(ts, RWKV_WIDTH)

    wr_pad = jnp.zeros((D_MODEL, LANES), F32).at[:, :N_EXPERTS].set(w_router[0])
    wr_hi = wr_pad.astype(BF16)
    wr_lo = (wr_pad - wr_hi.astype(F32)).astype(BF16)
    br_pad = jnp.full((1, LANES), -jnp.inf, F32).at[0, :N_EXPERTS].set(b_router[0])
    g1, b1 = rowv(ln1_g[0]), rowv(ln1_b[0])
    tmp, tmd = 512, 256
    x1_p, info_p, gate_p, cnt_p = _mix_router(att_p, rw_p, xn_p, w_out_b, g1, b1, wr_hi, wr_lo,
                                              br_pad, jnp.zeros((1, LANES), F32), tmp)
    x1_s, info_s, gate_s, cnt = _mix_router(att_s, rw_s, xn_s, w_out_b, g1, b1, wr_hi, wr_lo,
                                            br_pad, cnt_p, ts)

    counts = cnt[0, :N_EXPERTS].astype(jnp.int32)
    padded = ((counts + EXPERT_TILE - 1) // EXPERT_TILE) * EXPERT_TILE
    ends = jnp.cumsum(padded)
    offs = ends - padded
    n_slots = (tp + ts) * TOP_K + N_EXPERTS * EXPERT_TILE
    n_tiles = n_slots // EXPERT_TILE
    n_used = (ends[-1] // EXPERT_TILE).reshape(1).astype(jnp.int32)
    tile_start = jnp.arange(n_tiles, dtype=jnp.int32) * EXPERT_TILE
    tile_e = jnp.sum(ends[None, :] <= tile_start[:, None], axis=1).astype(jnp.int32)
    last_e = jnp.max(jnp.where(padded > 0, jnp.arange(N_EXPERTS), 0)).astype(jnp.int32)
    tile_e = jnp.minimum(tile_e, last_e)

    def slots(info):
        expert = jnp.arange(N_EXPERTS, dtype=jnp.int32)[:, None, None]
        first = jnp.sum(jnp.where(info[None, 0:TOP_K] == expert, offs[:, None, None], 0), axis=0)
        return (first + info[TOP_K:2 * TOP_K]).reshape(-1).astype(jnp.int32)

    pos_p, pos_s = slots(info_p), slots(info_s)

    fill_start = jnp.concatenate([jnp.clip(ends - EXPERT_TILE, 0, n_slots - EXPERT_TILE),
                                  n_used]).astype(jnp.int32)
    xs = _dispatch(pos_p, pos_s, fill_start, x1_p, x1_s, n_slots, tmd)
    bgu = b_gate_up[0]
    bg = bgu[:, 0::2].reshape(N_EXPERTS, 1, D_FF)
    bu = bgu[:, 1::2].reshape(N_EXPERTS, 1, D_FF)
    bdn = b_down[0].reshape(N_EXPERTS, 1, D_MODEL)
    ys = _experts(tile_e, n_used, xs, w_gate_up[0], w_down[0], bg, bu, bdn)

    g2, b2 = rowv(ln2_g[0]), rowv(ln2_b[0])
    y_p = _combine(pos_p, ys, gate_p, x1_p, p_prompt[0].reshape(tp, PLE_DIM), g2, b2,
                   w_ple_b, w_pg_b, tmd)
    y_s = _combine(pos_s, ys, gate_s, x1_s, p_sample[0].reshape(ts, PLE_DIM), g2, b2,
                   w_ple_b, w_pg_b, ts)

    w_keep = min(WINDOW, seq)
    kv_p3 = kv_p.reshape(nbp, seq, 2 * KV_COLS)[:, seq - w_keep:]
    k_win_p = kv_p3[:, :, 0:KV_COLS].reshape(1, nbp, w_keep, KV_HEADS, HEAD_DIM)
    v_win_p = kv_p3[:, :, KV_COLS:].reshape(1, nbp, w_keep, KV_HEADS, HEAD_DIM)
    shift_p = rp_p.reshape(nbp, seq, RWKV_PROJ)[:, seq - 1][None]
    k_new = kv_s[:, 0:KV_COLS].reshape(nbs, 1, KV_HEADS, HEAD_DIM)
    v_new = kv_s[:, KV_COLS:].reshape(nbs, 1, KV_HEADS, HEAD_DIM)
    k_win_s = jnp.concatenate([cache_k[0], k_new], axis=1)[:, 1:][None]
    v_win_s = jnp.concatenate([cache_v[0], v_new], axis=1)[:, 1:][None]
    return (y_p.reshape(nbp, seq, D_MODEL), y_s.reshape(nbs, 1, D_MODEL),
            k_win_p, v_win_p, shift_p, wkv_p[None],
            k_win_s, v_win_s, rp_s[None], wkv_s[None])
```

```python
import functools

import jax
import jax.numpy as jnp
from jax import lax
from jax.experimental import pallas as pl
from jax.experimental.pallas import tpu as pltpu

F32 = jnp.float32
BF16 = jnp.bfloat16
HIGHEST = lax.Precision.HIGHEST

D_MODEL = 1024
HEAD_DIM = 64
ATT_HEADS = 8
KV_HEADS = 2
GROUP = ATT_HEADS // KV_HEADS
ATT_WIDTH = ATT_HEADS * HEAD_DIM
KV_COLS = KV_HEADS * HEAD_DIM
RWKV_HEADS = 8
RWKV_WIDTH = RWKV_HEADS * HEAD_DIM
LORA_W = 64
LORA_A = 64
LORA_G = 128
RWKV_PROJ = 3 * RWKV_WIDTH + LORA_W + LORA_A + LORA_G
WINDOW = 128
BLOCK = 128
PLE_DIM = 256
N_EXPERTS = 32
TOP_K = 4
D_FF = 1024
SWIGLU_LIMIT = 7.0
SWIGLU_ALPHA = 1.702
LN_EPS = 1e-5
GN_EPS = 64e-5
DEPTH = 1
DEEPNORM_ALPHA = (2 * DEPTH) ** 0.25

LANES = 128
CHUNK = 64
HEAD_QUAD = 4
EXPERT_TILE = 512
ROW_DMA_UNROLL = 8
VMEM_LIMIT = 48 * 1024 * 1024
EXPERTS_VMEM_LIMIT = 58 * 1024 * 1024


def _cparams(sem):
    return pltpu.CompilerParams(dimension_semantics=sem, vmem_limit_bytes=VMEM_LIMIT)


def _bdot(a, b):
    return jnp.dot(a.astype(BF16), b.astype(BF16), preferred_element_type=F32)


def _bdot_nt(a, b):
    return lax.dot_general(a.astype(BF16), b.astype(BF16), (((1,), (1,)), ((), ())),
                           preferred_element_type=F32)


def _bdot_tn(a, b):
    return lax.dot_general(a.astype(BF16), b.astype(BF16), (((0,), (0,)), ((), ())),
                           preferred_element_type=F32)


def _split_dot(m_bf16, x, passes):
    acc = None
    rem = x
    for _ in range(passes):
        hi = rem.astype(BF16)
        part = jnp.dot(m_bf16, hi, preferred_element_type=F32)
        acc = part if acc is None else acc + part
        rem = rem - hi.astype(F32)
    return acc


def _head_sum(x, bd_bf16):
    width = bd_bf16.shape[0]
    outs = []
    for c in range(x.shape[1] // width):
        acc = None
        rem = x[:, c * width:(c + 1) * width]
        for _ in range(2):
            hi = rem.astype(BF16)
            part = jnp.dot(hi, bd_bf16, preferred_element_type=F32)
            acc = part if acc is None else acc + part
            rem = rem - hi.astype(F32)
        outs.append(acc)
    return jnp.concatenate(outs, axis=1)


def _sigmoid(x):
    return 1.0 / (1.0 + jnp.exp(-x))


def _layer_norm(x, g, b):
    mu = jnp.mean(x, axis=-1, keepdims=True)
    xc = x - mu
    var = jnp.mean(xc * xc, axis=-1, keepdims=True)
    return xc * lax.rsqrt(var + LN_EPS) * g + b


def _inproj_body(x_ref, g_ref, b_ref, w_ref, xn_ref, q_ref, kv_ref, rp_ref):
    xn = _layer_norm(x_ref[...], g_ref[...], b_ref[...])
    xn_ref[...] = xn
    xb = xn.astype(BF16)
    q_ref[...] = jnp.dot(xb, w_ref[:, 0:ATT_WIDTH], preferred_element_type=F32)
    kv_ref[...] = jnp.dot(xb, w_ref[:, ATT_WIDTH:ATT_WIDTH + 2 * KV_COLS],
                          preferred_element_type=F32)
    rp_ref[...] = jnp.dot(xb, w_ref[:, ATT_WIDTH + 2 * KV_COLS:], preferred_element_type=F32)


def _inproj(x2d, g, b, w_bf16, tm):
    t = x2d.shape[0]
    in_proj = w_bf16.shape[1]
    row = lambda i: (i, 0)
    fixed = lambda i: (0, 0)
    return pl.pallas_call(
        _inproj_body,
        grid=(t // tm,),
        in_specs=[pl.BlockSpec((tm, D_MODEL), row),
                  pl.BlockSpec((1, D_MODEL), fixed),
                  pl.BlockSpec((1, D_MODEL), fixed),
                  pl.BlockSpec((D_MODEL, in_proj), fixed)],
        out_specs=[pl.BlockSpec((tm, D_MODEL), row),
                   pl.BlockSpec((tm, ATT_WIDTH), row),
                   pl.BlockSpec((tm, 2 * KV_COLS), row),
                   pl.BlockSpec((tm, RWKV_PROJ), row)],
        out_shape=[jax.ShapeDtypeStruct((t, D_MODEL), F32),
                   jax.ShapeDtypeStruct((t, ATT_WIDTH), F32),
                   jax.ShapeDtypeStruct((t, 2 * KV_COLS), F32),
                   jax.ShapeDtypeStruct((t, RWKV_PROJ), F32)],
        compiler_params=_cparams(("parallel",)),
        name="inproj",
    )(x2d, g, b, w_bf16)


def _alibi_slope(h):
    return 2.0 ** (-8.0 * (h + 1) / ATT_HEADS)


def _swa_prompt_body(sink_ref, q_ref, kvc_ref, kvp_ref, o_ref):
    n = pl.program_id(1)
    q = (q_ref[...] * (HEAD_DIM ** -0.5)).astype(BF16)
    kvc = kvc_ref[...].astype(BF16)
    kvp = kvp_ref[...].astype(BF16)
    row = lax.broadcasted_iota(jnp.int32, (BLOCK, 2 * BLOCK), 0)
    col = lax.broadcasted_iota(jnp.int32, (BLOCK, 2 * BLOCK), 1)
    dist = row + BLOCK - col
    valid = (dist >= 0) & (dist <= WINDOW) & ((col >= BLOCK) | (n > 0))
    distf = dist.astype(F32)
    kbands, vbands = [], []
    for g in range(KV_HEADS):
        ks = slice(g * HEAD_DIM, (g + 1) * HEAD_DIM)
        vs = slice(KV_COLS + g * HEAD_DIM, KV_COLS + (g + 1) * HEAD_DIM)
        kbands.append(jnp.concatenate([kvp[:, ks], kvc[:, ks]], axis=0))
        vbands.append(jnp.concatenate([kvp[:, vs], kvc[:, vs]], axis=0))
    scores = [lax.dot_general(q[:, h * HEAD_DIM:(h + 1) * HEAD_DIM], kbands[h // GROUP],
                              (((1,), (1,)), ((), ())), preferred_element_type=F32)
              for h in range(ATT_HEADS)]
    probs = []
    for h in range(ATT_HEADS):
        s = jnp.where(valid, scores[h] - _alibi_slope(h) * distf, -jnp.inf)
        sink = sink_ref[h]
        m = jnp.maximum(jnp.max(s, axis=1, keepdims=True), sink)
        e = jnp.exp(s - m)
        den = jnp.sum(e, axis=1, keepdims=True) + jnp.exp(sink - m)
        probs.append((e / den).astype(BF16))
    outs = [jnp.dot(probs[h], vbands[h // GROUP], preferred_element_type=F32)
            for h in range(ATT_HEADS)]
    o_ref[...] = jnp.concatenate(outs, axis=1)


def _swa_prompt(sinks, q, kv, nbatch, seq):
    nb = seq // BLOCK
    cur = lambda b, n: (b * nb + n, 0)
    prv = lambda b, n: (b * nb + jnp.maximum(n - 1, 0), 0)
    return pl.pallas_call(
        _swa_prompt_body,
        grid=(nbatch, nb),
        in_specs=[pl.BlockSpec(memory_space=pltpu.SMEM),
                  pl.BlockSpec((BLOCK, ATT_WIDTH), cur),
                  pl.BlockSpec((BLOCK, 2 * KV_COLS), cur),
                  pl.BlockSpec((BLOCK, 2 * KV_COLS), prv)],
        out_specs=pl.BlockSpec((BLOCK, ATT_WIDTH), cur),
        out_shape=jax.ShapeDtypeStruct((nbatch * seq, ATT_WIDTH), F32),
        compiler_params=_cparams(("parallel", "parallel")),
        name="swa_prompt",
    )(sinks, q, kv, kv)


def _swa_sample_body(sink_ref, q_ref, kvn_ref, ck_ref, cv_ref, o_ref, *, bb, w_buf):
    hrow = lax.broadcasted_iota(jnp.int32, (ATT_HEADS, 1), 0)
    slope = jnp.zeros((ATT_HEADS, 1), F32)
    sink = jnp.zeros((ATT_HEADS, 1), F32)
    for h in range(ATT_HEADS):
        slope = jnp.where(hrow == h, _alibi_slope(h), slope)
        sink = jnp.where(hrow == h, sink_ref[h], sink)
    jcol = lax.broadcasted_iota(jnp.int32, (ATT_HEADS, w_buf), 1)
    bias = -slope * (w_buf - jcol).astype(F32)
    lower = hrow < GROUP
    q2b = [(q_ref[b] * (HEAD_DIM ** -0.5)).astype(BF16) for b in range(bb)]
    s01 = []
    for b in range(bb):
        ck = ck_ref[b].astype(BF16)
        s01.append((_bdot_nt(q2b[b], ck[:, 0:HEAD_DIM]),
                    _bdot_nt(q2b[b], ck[:, HEAD_DIM:2 * HEAD_DIM])))
    ps = []
    for b in range(bb):
        kvn = kvn_ref[b]
        s = jnp.where(lower, s01[b][0], s01[b][1]) + bias
        knew = jnp.where(lower, kvn[0:1, :], kvn[1:2, :])
        vnew = jnp.where(lower, kvn[2:3, :], kvn[3:4, :])
        snew = jnp.sum(q2b[b].astype(F32) * knew.astype(BF16).astype(F32), axis=1, keepdims=True)
        m = jnp.maximum(jnp.maximum(jnp.max(s, axis=1, keepdims=True), snew), sink)
        e = jnp.exp(s - m)
        enew = jnp.exp(snew - m)
        den = jnp.sum(e, axis=1, keepdims=True) + enew + jnp.exp(sink - m)
        pnew = (enew / den).astype(BF16).astype(F32)
        ps.append(((e / den).astype(BF16), pnew * vnew.astype(BF16).astype(F32)))
    for b in range(bb):
        cv = cv_ref[b].astype(BF16)
        o0 = jnp.dot(ps[b][0], cv[:, 0:HEAD_DIM], preferred_element_type=F32)
        o1 = jnp.dot(ps[b][0], cv[:, HEAD_DIM:2 * HEAD_DIM], preferred_element_type=F32)
        o_ref[b] = jnp.where(lower, o0, o1) + ps[b][1]


def _swa_sample(sinks, q3, kvn3, ck3, cv3, bb=8):
    nb, w_buf = ck3.shape[0], ck3.shape[1]
    blk = lambda i: (i, 0, 0)
    return pl.pallas_call(
        functools.partial(_swa_sample_body, bb=bb, w_buf=w_buf),
        grid=(nb // bb,),
        in_specs=[pl.BlockSpec(memory_space=pltpu.SMEM),
                  pl.BlockSpec((bb, ATT_HEADS, HEAD_DIM), blk),
                  pl.BlockSpec((bb, 2 * KV_HEADS, HEAD_DIM), blk),
                  pl.BlockSpec((bb, w_buf, KV_COLS), blk),
                  pl.BlockSpec((bb, w_buf, KV_COLS), blk)],
        out_specs=pl.BlockSpec((bb, ATT_HEADS, HEAD_DIM), blk),
        out_shape=jax.ShapeDtypeStruct((nb, ATT_HEADS, HEAD_DIM), F32),
        compiler_params=_cparams(("parallel",)),
        name="swa_sample",
    )(sinks, q3, kvn3, ck3, cv3)


def _rwkv_prep(rp, prev, mu, w0, wlu, a0, alu, glu, k_k, k_a, bd):
    xs = rp + mu * (prev - rp)
    r = xs[:, 0:RWKV_WIDTH]
    k = xs[:, RWKV_WIDTH:2 * RWKV_WIDTH]
    v = xs[:, 2 * RWKV_WIDTH:3 * RWKV_WIDTH]
    o = 3 * RWKV_WIDTH
    wd = xs[:, o:o + LORA_W]
    ad = xs[:, o + LORA_W:o + LORA_W + LORA_A]
    gd = xs[:, o + LORA_W + LORA_A:]
    z = -(w0 + _bdot(jnp.tanh(wd), wlu))
    softplus = jnp.maximum(z, 0.0) + jnp.log(1.0 + jnp.exp(-jnp.abs(z)))
    log_decay = -jnp.exp(-softplus - 0.5)
    a = _sigmoid(a0 + _bdot(ad, alu))
    g = _bdot(_sigmoid(gd), glu)
    kkr = k * k_k
    kk = kkr / jnp.maximum(jnp.sqrt(_head_sum(kkr * kkr, bd)), 1e-12)
    k2 = k * (1.0 + (a - 1.0) * k_a)
    return r, log_decay, k2, v, kk, a, g


def _rwkv_finish(y, r, k2, v, g, r_k, gn_g, gn_b, bd):
    inv = 1.0 / HEAD_DIM
    mu = _head_sum(y, bd) * inv
    yc = y - mu
    var = _head_sum(yc * yc, bd) * inv
    yn = yc * lax.rsqrt(var + GN_EPS) * gn_g + gn_b
    bonus = _head_sum(r * k2 * r_k, bd) * v
    return (yn + bonus) * g


def _rwkv_prompt_body(rp_ref, mu_ref, w0_ref, wlu_ref, a0_ref, alu_ref, glu_ref, kk_ref, ka_ref,
                      rk_ref, gng_ref, gnb_ref, bd_ref, ltri_ref,
                      out_ref, sfin_ref, prev_scr, st_scr, y_scr, *, tt):
    c = pl.program_id(1)

    @pl.when(c == 0)
    def _():
        prev_scr[...] = jnp.zeros_like(prev_scr)
        st_scr[...] = jnp.zeros_like(st_scr)

    rp = rp_ref[...]
    rolled = pltpu.roll(rp, 1, 0)
    rowi = lax.broadcasted_iota(jnp.int32, (tt, 1), 0)
    prev = jnp.where(rowi == 0, prev_scr[...], rolled)
    prev_scr[...] = rp[tt - 1:tt, :]
    bd = bd_ref[...]
    r, ld, k2, v, kk, a, g = _rwkv_prep(rp, prev, mu_ref[...], w0_ref[...], wlu_ref[...],
                                        a0_ref[...], alu_ref[...], glu_ref[...], kk_ref[...],
                                        ka_ref[...], bd)
    cum = _split_dot(ltri_ref[...], ld, 2)
    ecum = jnp.exp(cum)
    einv = jnp.exp(-cum)
    at = -kk * jnp.exp(cum - ld)
    bh = kk * a * einv
    kh = k2 * einv
    rt = r * ecum

    qw = HEAD_QUAD * HEAD_DIM
    ii = lax.broadcasted_iota(jnp.int32, (CHUNK, qw), 0)
    jl = lax.broadcasted_iota(jnp.int32, (CHUNK, qw), 1) % CHUNK
    strict = ii > jl
    incl = ii >= jl
    eye = (ii == jl).astype(F32)
    same_head = (lax.broadcasted_iota(jnp.int32, (qw, qw), 0) // HEAD_DIM
                 == lax.broadcasted_iota(jnp.int32, (qw, qw), 1) // HEAD_DIM)

    def blockdiag(m):
        mb = m.astype(BF16)
        return jnp.where(same_head, jnp.concatenate([mb] * HEAD_QUAD, axis=0), jnp.zeros((), BF16))

    def diag_blocks(full):
        out = None
        for hh in range(HEAD_QUAD):
            rs = slice(hh * HEAD_DIM, (hh + 1) * HEAD_DIM)
            blk = jnp.where(same_head[rs], full[rs], 0.0)
            out = blk if out is None else out + blk
        return out

    mm = lambda x, w_bf16: jnp.dot(x.astype(BF16), w_bf16, preferred_element_type=F32)
    mm_nt = lambda x, w_bf16: lax.dot_general(x.astype(BF16), w_bf16, (((1,), (1,)), ((), ())),
                                              preferred_element_type=F32)
    nsub = tt // CHUNK
    units = [(s, q) for s in range(nsub) for q in range(RWKV_HEADS // HEAD_QUAD)]
    rows = lambda s: slice(s * CHUNK, (s + 1) * CHUNK)
    cols = lambda q: slice(q * qw, (q + 1) * qw)
    cut = lambda z, u: z[rows(u[0]), cols(u[1])]
    kka = kk * a
    p_last, kh_end, bh_end = [], [], []
    for s in range(nsub):
        c_last = cum[(s + 1) * CHUNK - 1:(s + 1) * CHUNK, :]
        p_last.append(jnp.exp(c_last))
        tail = jnp.exp(c_last - cum[rows(s)])
        kh_end.append(k2[rows(s)] * tail)
        bh_end.append(kka[rows(s)] * tail)
    x2 = {u: jnp.concatenate([cut(at, u), cut(rt, u)], axis=0) for u in units}
    xb = {u: mm_nt(x2[u], blockdiag(cut(bh, u))) for u in units}
    xk = {u: mm_nt(x2[u], blockdiag(cut(kh, u))) for u in units}
    nmat = {u: jnp.where(strict, xb[u][0:CHUNK], 0.0) for u in units}
    a_rb = {u: jnp.where(incl, xb[u][CHUNK:], 0.0) for u in units}
    a_ak = {u: jnp.where(strict, xk[u][0:CHUNK], 0.0) for u in units}
    a_rk = {u: jnp.where(incl, xk[u][CHUNK:], 0.0) for u in units}
    tmat = {u: eye + nmat[u] for u in units}
    npow = nmat
    for _ in range(5):
        npow = {u: mm(npow[u], blockdiag(npow[u])) for u in units}
        tmat = {u: tmat[u] + mm(npow[u], blockdiag(tmat[u])) for u in units}
    vbd = {u: blockdiag(cut(v, u)) for u in units}
    av = {u: mm(a_ak[u], vbd[u]) for u in units}
    aprime = {u: mm(tmat[u], blockdiag(cut(at, u))) for u in units}
    wmat = {u: mm(tmat[u], blockdiag(av[u])) for u in units}
    rprime = {u: cut(rt, u) + mm(a_rb[u], blockdiag(aprime[u])) for u in units}
    y0 = {u: mm(a_rk[u], vbd[u]) + mm(a_rb[u], blockdiag(wmat[u])) for u in units}
    hmat = {u: diag_blocks(_bdot_tn(jnp.concatenate([cut(v, u), wmat[u]], axis=0),
                                    jnp.concatenate([cut(kh_end[u[0]], (0, u[1])),
                                                     cut(bh_end[u[0]], (0, u[1]))], axis=0)))
            for u in units}
    g0 = {u: diag_blocks(_bdot_tn(aprime[u], cut(bh_end[u[0]], (0, u[1])))) for u in units}

    states = [st_scr[q] for q in range(RWKV_HEADS // HEAD_QUAD)]
    for s in range(nsub):
        for q in range(RWKV_HEADS // HEAD_QUAD):
            u = (s, q)
            st = states[q]
            y_scr[rows(s), cols(q)] = mm_nt(rprime[u], blockdiag(st)) + y0[u]
            states[q] = st * p_last[s][:, cols(q)] + mm(st, blockdiag(g0[u])) + hmat[u]
    for q in range(RWKV_HEADS // HEAD_QUAD):
        st_scr[q] = states[q]
        for hh in range(HEAD_QUAD):
            sfin_ref[0, q * HEAD_QUAD + hh] = states[q][:, hh * HEAD_DIM:(hh + 1) * HEAD_DIM]
    out_ref[...] = _rwkv_finish(y_scr[...], r, k2, v, g, rk_ref[...], gng_ref[...], gnb_ref[...], bd)


def _chunk_tril(tt):
    i = jnp.arange(tt)
    same = (i[:, None] // CHUNK) == (i[None, :] // CHUNK)
    return (same & (i[:, None] >= i[None, :])).astype(BF16)


def _head_blockdiag():
    i = jnp.arange(HEAD_QUAD * HEAD_DIM) // HEAD_DIM
    return (i[:, None] == i[None, :]).astype(BF16)


def _rwkv_prompt(rp, prm, nbatch, seq, tt):
    nc = seq // tt
    fixed = lambda b, c: (0, 0)
    row = lambda b, c: (b * nc + c, 0)
    vec = lambda n: pl.BlockSpec((1, n), fixed)
    return pl.pallas_call(
        functools.partial(_rwkv_prompt_body, tt=tt),
        grid=(nbatch, nc),
        in_specs=[pl.BlockSpec((tt, RWKV_PROJ), row),
                  vec(RWKV_PROJ), vec(RWKV_WIDTH),
                  pl.BlockSpec((LORA_W, RWKV_WIDTH), fixed), vec(RWKV_WIDTH),
                  pl.BlockSpec((LORA_A, RWKV_WIDTH), fixed),
                  pl.BlockSpec((LORA_G, RWKV_WIDTH), fixed),
                  vec(RWKV_WIDTH), vec(RWKV_WIDTH), vec(RWKV_WIDTH), vec(RWKV_WIDTH),
                  vec(RWKV_WIDTH),
                  pl.BlockSpec((HEAD_QUAD * HEAD_DIM, HEAD_QUAD * HEAD_DIM), fixed),
                  pl.BlockSpec((tt, tt), fixed)],
        out_specs=[pl.BlockSpec((tt, RWKV_WIDTH), row),
                   pl.BlockSpec((1, RWKV_HEADS, HEAD_DIM, HEAD_DIM), lambda b, c: (b, 0, 0, 0))],
        out_shape=[jax.ShapeDtypeStruct((nbatch * seq, RWKV_WIDTH), F32),
                   jax.ShapeDtypeStruct((nbatch, RWKV_HEADS, HEAD_DIM, HEAD_DIM), F32)],
        scratch_shapes=[pltpu.VMEM((1, RWKV_PROJ), F32),
                        pltpu.VMEM((RWKV_HEADS // HEAD_QUAD, HEAD_DIM, HEAD_QUAD * HEAD_DIM), F32),
                        pltpu.VMEM((tt, RWKV_WIDTH), F32)],
        compiler_params=_cparams(("parallel", "arbitrary")),
        name="rwkv_prompt",
    )(rp, prm["mu"], prm["w0"], prm["wlu"], prm["a0"], prm["alu"], prm["glu"], prm["k_k"],
      prm["k_a"], prm["r_k"], prm["gn_g"], prm["gn_b"], _head_blockdiag(), _chunk_tril(tt))


def _rwkv_sample_prep_body(rp_ref, prev_ref, mu_ref, w0_ref, wlu_ref, a0_ref, alu_ref, glu_ref,
                           kk_ref, ka_ref, bd_ref, o_ref):
    r, ld, k2, v, kk, a, g = _rwkv_prep(rp_ref[...], prev_ref[...], mu_ref[...], w0_ref[...],
                                        wlu_ref[...], a0_ref[...], alu_ref[...], glu_ref[...],
                                        kk_ref[...], ka_ref[...], bd_ref[...])
    for i, z in enumerate((r, jnp.exp(ld), k2, v, kk, a, g)):
        o_ref[i] = z


def _rwkv_sample_prep(rp, prev, prm):
    nb = rp.shape[0]
    return pl.pallas_call(
        _rwkv_sample_prep_body,
        out_shape=jax.ShapeDtypeStruct((7, nb, RWKV_WIDTH), F32),
        compiler_params=pltpu.CompilerParams(vmem_limit_bytes=VMEM_LIMIT),
        name="rwkv_sample_prep",
    )(rp, prev, prm["mu"], prm["w0"], prm["wlu"], prm["a0"], prm["alu"], prm["glu"], prm["k_k"],
      prm["k_a"], _head_blockdiag())


def _rwkv_sample_step_body(p_ref, s_ref, rk_ref, gng_ref, gnb_ref, so_ref, o_ref, *, bb):
    ii = lax.broadcasted_iota(jnp.int32, (HEAD_DIM, HEAD_DIM), 0)
    jj = lax.broadcasted_iota(jnp.int32, (HEAD_DIM, HEAD_DIM), 1)
    eye = ii == jj
    for b in range(bb):
        for h in range(RWKV_HEADS):
            row = lambda i: p_ref[i, b, h:h + 1, :]
            r_r, w_r, k_r, v_r, kk_r, a_r = (row(i) for i in range(6))
            st = s_ref[b, h]
            sa = jnp.sum(st * kk_r, axis=1, keepdims=True)
            v_col = jnp.sum(jnp.where(eye, v_r, 0.0), axis=1, keepdims=True)
            st = st * w_r - sa * (kk_r * a_r) + v_col * k_r
            so_ref[b, h] = st
            y_col = jnp.sum(st * r_r, axis=1, keepdims=True)
            o_ref[b, h:h + 1, :] = jnp.sum(jnp.where(eye, y_col, 0.0), axis=0, keepdims=True)
    y = o_ref[...]
    r, k2, v, g = p_ref[0], p_ref[2], p_ref[3], p_ref[6]
    mu = jnp.mean(y, axis=-1, keepdims=True)
    yc = y - mu
    var = jnp.mean(yc * yc, axis=-1, keepdims=True)
    yn = yc * lax.rsqrt(var + GN_EPS) * gng_ref[...] + gnb_ref[...]
    bonus = jnp.sum(r * k2 * rk_ref[...], axis=-1, keepdims=True) * v
    o_ref[...] = (yn + bonus) * g


def _rwkv_sample_step(p4, state, r_k, gn_g, gn_b, bb=8):
    nb = state.shape[0]
    hv = pl.BlockSpec((RWKV_HEADS, HEAD_DIM), lambda i: (0, 0))
    return pl.pallas_call(
        functools.partial(_rwkv_sample_step_body, bb=bb),
        grid=(nb // bb,),
        in_specs=[pl.BlockSpec((7, bb, RWKV_HEADS, HEAD_DIM), lambda i: (0, i, 0, 0)),
                  pl.BlockSpec((bb, RWKV_HEADS, HEAD_DIM, HEAD_DIM), lambda i: (i, 0, 0, 0)),
                  hv, hv, hv],
        out_specs=[pl.BlockSpec((bb, RWKV_HEADS, HEAD_DIM, HEAD_DIM), lambda i: (i, 0, 0, 0)),
                   pl.BlockSpec((bb, RWKV_HEADS, HEAD_DIM), lambda i: (i, 0, 0))],
        out_shape=[jax.ShapeDtypeStruct(state.shape, F32),
                   jax.ShapeDtypeStruct((nb, RWKV_HEADS, HEAD_DIM), F32)],
        compiler_params=_cparams(("parallel",)),
        name="rwkv_sample_step",
    )(p4, state, r_k, gn_g, gn_b)


def _mix_router_body(att_ref, rw_ref, xn_ref, wo_ref, g_ref, b_ref, wrh_ref, wrl_ref, br_ref,
                     base_ref, x1_ref, info_ref, gate_ref, cnt_ref, base_scr, *, tm):
    i = pl.program_id(0)

    @pl.when(i == 0)
    def _():
        base_scr[...] = base_ref[...]

    mixed = (jnp.dot(att_ref[...].astype(BF16), wo_ref[0:ATT_WIDTH, :], preferred_element_type=F32)
             + jnp.dot(rw_ref[...].astype(BF16), wo_ref[ATT_WIDTH:, :], preferred_element_type=F32))
    x1 = _layer_norm(DEEPNORM_ALPHA * xn_ref[...] + mixed, g_ref[...], b_ref[...])
    x1_ref[...] = x1

    x1h = x1.astype(BF16)
    x1l = (x1 - x1h.astype(F32)).astype(BF16)
    hi_both = jnp.dot(x1h, jnp.concatenate([wrh_ref[...], wrl_ref[...]], axis=1),
                      preferred_element_type=F32)
    logits = (hi_both[:, 0:LANES] + jnp.dot(x1l, wrh_ref[...], preferred_element_type=F32)
              + hi_both[:, LANES:]) + br_ref[...]
    lane = lax.broadcasted_iota(jnp.int32, (tm, LANES), 1)
    lanef = lane.astype(F32)
    vals, idxs, hots = [], [], []
    cur = logits
    for _ in range(TOP_K):
        m = jnp.max(cur, axis=1, keepdims=True)
        idx = jnp.min(jnp.where(cur == m, lanef, float(LANES)), axis=1, keepdims=True)
        hot = lanef == idx
        cur = jnp.where(hot, -jnp.inf, cur)
        vals.append(m)
        idxs.append(idx)
        hots.append(hot)
    es = [jnp.exp(vk - vals[0]) for vk in vals]
    den = es[0] + es[1] + es[2] + es[3]
    multi = jnp.zeros((tm, LANES), F32)
    for hot in hots:
        multi = multi + hot.astype(F32)
    ti = lax.broadcasted_iota(jnp.int32, (tm, tm), 0)
    tj = lax.broadcasted_iota(jnp.int32, (tm, tm), 1)
    before = jnp.dot((ti > tj).astype(BF16), multi.astype(BF16), preferred_element_type=F32)
    before = before + base_scr[...]
    info = jnp.zeros((tm, LANES), F32)
    gates = jnp.zeros((tm, LANES), F32)
    for k in range(TOP_K):
        rank = jnp.sum(jnp.where(hots[k], before, 0.0), axis=1, keepdims=True)
        info = jnp.where(lane == k, idxs[k], info)
        info = jnp.where(lane == TOP_K + k, rank, info)
        gates = jnp.where(lane == k, es[k] / den, gates)
    info_ref[...] = jnp.transpose(info)[0:2 * TOP_K, :].astype(jnp.int32)
    gate_ref[...] = gates
    base_scr[...] = base_scr[...] + jnp.sum(multi, axis=0, keepdims=True)
    cnt_ref[...] = base_scr[...]


def _mix_router(att, rw, xn, wo_bf16, g, b, wr_hi, wr_lo, br_pad, base, tm):
    t = att.shape[0]
    row = lambda i: (i, 0)
    fixed = lambda i: (0, 0)
    return pl.pallas_call(
        functools.partial(_mix_router_body, tm=tm),
        grid=(t // tm,),
        in_specs=[pl.BlockSpec((tm, ATT_WIDTH), row),
                  pl.BlockSpec((tm, RWKV_WIDTH), row),
                  pl.BlockSpec((tm, D_MODEL), row),
                  pl.BlockSpec((D_MODEL, D_MODEL), fixed),
                  pl.BlockSpec((1, D_MODEL), fixed),
                  pl.BlockSpec((1, D_MODEL), fixed),
                  pl.BlockSpec((D_MODEL, LANES), fixed),
                  pl.BlockSpec((D_MODEL, LANES), fixed),
                  pl.BlockSpec((1, LANES), fixed),
                  pl.BlockSpec((1, LANES), fixed)],
        out_specs=[pl.BlockSpec((tm, D_MODEL), row),
                   pl.BlockSpec((2 * TOP_K, tm), lambda i: (0, i)),
                   pl.BlockSpec((tm, LANES), row),
                   pl.BlockSpec((1, LANES), fixed)],
        out_shape=[jax.ShapeDtypeStruct((t, D_MODEL), F32),
                   jax.ShapeDtypeStruct((2 * TOP_K, t), jnp.int32),
                   jax.ShapeDtypeStruct((t, LANES), F32),
                   jax.ShapeDtypeStruct((1, LANES), F32)],
        scratch_shapes=[pltpu.VMEM((1, LANES), F32)],
        compiler_params=_cparams(("arbitrary",)),
        name="mix_router",
    )(att, rw, xn, wo_bf16, g, b, wr_hi, wr_lo, br_pad, base)


def _row_copies(pos_ref, base, tm, make_copy):
    tokens = pos_ref.shape[0] // TOP_K

    def group(gi, carry):
        r0 = pl.multiple_of(gi * ROW_DMA_UNROLL, ROW_DMA_UNROLL)
        for j in range(ROW_DMA_UNROLL):
            for k in range(TOP_K):
                make_copy(gi, j, k, pos_ref[k * tokens + base + r0 + j]).start(priority=k % 2)
        return carry

    lax.fori_loop(0, tm // ROW_DMA_UNROLL, group, 0)


def _dispatch_rows(pos_ref, base, x_ref, xs_ref, sem, rows):
    def make_copy(gi, j, k, p):
        r = pl.multiple_of(gi * ROW_DMA_UNROLL, ROW_DMA_UNROLL) + j
        return pltpu.make_async_copy(x_ref.at[pl.ds(r, 1)], xs_ref.at[pl.ds(p, 1)], sem)

    _row_copies(pos_ref, base, rows, make_copy)
    for _ in range(TOP_K):
        pltpu.make_async_copy(x_ref, xs_ref.at[pl.ds(0, rows)], sem).wait()


def _dispatch_body(pos_a_ref, pos_b_ref, fill_ref, xa_ref, xb_ref, xs_ref, zero_scr, sem, fill_sem,
                   *, tm, n_tiles):
    i = pl.program_id(0)
    last = pl.num_programs(0) - 1

    @pl.when(i == 0)
    def _():
        zero_scr[...] = jnp.zeros_like(zero_scr)

        def tile_fill(start):
            return pltpu.make_async_copy(
                zero_scr, xs_ref.at[pl.ds(pl.multiple_of(start, EXPERT_TILE), EXPERT_TILE)], fill_sem)

        fills = [tile_fill(fill_ref[e]) for e in range(N_EXPERTS)]
        for cp in fills:
            cp.start()
        first_unused = fill_ref[N_EXPERTS]
        lax.fori_loop(first_unused, n_tiles,
                      lambda t, c: (tile_fill(t * EXPERT_TILE).start(), c)[1], 0)
        for cp in fills:
            cp.wait()
        lax.fori_loop(first_unused, n_tiles,
                      lambda t, c: (tile_fill(t * EXPERT_TILE).wait(), c)[1], 0)

    @pl.when(i < last)
    def _():
        _dispatch_rows(pos_a_ref, i * tm, xa_ref, xs_ref, sem, tm)

    @pl.when(i == last)
    def _():
        _dispatch_rows(pos_b_ref, 0, xb_ref, xs_ref, sem, xb_ref.shape[0])


def _dispatch(pos_a, pos_b, fill_start, x_a, x_b, n_slots, tm):
    nta = x_a.shape[0] // tm
    return pl.pallas_call(
        functools.partial(_dispatch_body, tm=tm, n_tiles=n_slots // EXPERT_TILE),
        grid_spec=pltpu.PrefetchScalarGridSpec(
            num_scalar_prefetch=3,
            grid=(nta + 1,),
            in_specs=[pl.BlockSpec((tm, D_MODEL), lambda i, *_: (jnp.minimum(i, nta - 1), 0)),
                      pl.BlockSpec(x_b.shape, lambda i, *_: (0, 0))],
            out_specs=pl.BlockSpec(memory_space=pl.ANY),
            scratch_shapes=[pltpu.VMEM((EXPERT_TILE, D_MODEL), F32),
                            pltpu.SemaphoreType.DMA, pltpu.SemaphoreType.DMA]),
        out_shape=jax.ShapeDtypeStruct((n_slots, D_MODEL), F32),
        compiler_params=_cparams(("arbitrary",)),
        name="moe_dispatch",
    )(pos_a, pos_b, fill_start, x_a, x_b)


def _experts_body(te_ref, nu_ref, xs_ref, wgu_ref, wd_ref, bg_ref, bu_ref, bd_ref, sel_ref, ys_ref,
                  wg_scr, wu_scr, wd_scr):
    i = pl.program_id(0)
    new_expert = (i == 0) | (te_ref[i] != te_ref[jnp.maximum(i - 1, 0)])

    @pl.when(new_expert)
    def _():
        wd_scr[...] = wd_ref[...].astype(BF16)
        for m in range(D_FF // LANES):
            pair = wgu_ref[:, 2 * m * LANES:2 * (m + 1) * LANES].astype(BF16)
            split = jnp.dot(pair, sel_ref[...], preferred_element_type=F32)
            wg_scr[:, m * LANES:(m + 1) * LANES] = split[:, 0:LANES].astype(BF16)
            wu_scr[:, m * LANES:(m + 1) * LANES] = split[:, LANES:].astype(BF16)

    @pl.when(i < nu_ref[0])
    def _():
        x = xs_ref[...].astype(BF16)
        gate = jnp.dot(x, wg_scr[...], preferred_element_type=F32) + bg_ref[...]
        up = jnp.dot(x, wu_scr[...], preferred_element_type=F32) + bu_ref[...]
        gate = jnp.minimum(gate, SWIGLU_LIMIT)
        up = jnp.clip(up, -SWIGLU_LIMIT, SWIGLU_LIMIT)
        act = (up + 1.0) * gate * _sigmoid(SWIGLU_ALPHA * gate)
        ys_ref[...] = jnp.dot(act.astype(BF16), wd_scr[...], preferred_element_type=F32) + bd_ref[...]

    @pl.when(i >= nu_ref[0])
    def _():
        ys_ref[...] = jnp.zeros_like(ys_ref)


def _gate_up_selector():
    i = jnp.arange(2 * LANES)
    src = jnp.where(i < LANES, 2 * i, 2 * (i - LANES) + 1)
    return (i[:, None] == src[None, :]).astype(BF16)


def _experts(tile_e, n_used, xs, wgu, wd, bg, bu, bd):
    ns = xs.shape[0]
    tm = EXPERT_TILE
    wspec = lambda a, b: pl.BlockSpec((None, a, b), lambda i, te, nu: (te[i], 0, 0))
    return pl.pallas_call(
        _experts_body,
        grid_spec=pltpu.PrefetchScalarGridSpec(
            num_scalar_prefetch=2,
            grid=(ns // tm,),
            in_specs=[pl.BlockSpec((tm, D_MODEL), lambda i, te, nu: (jnp.minimum(i, nu[0] - 1), 0)),
                      wspec(D_MODEL, 2 * D_FF), wspec(D_FF, D_MODEL),
                      wspec(1, D_FF), wspec(1, D_FF), wspec(1, D_MODEL),
                      pl.BlockSpec((2 * LANES, 2 * LANES), lambda i, te, nu: (0, 0))],
            out_specs=pl.BlockSpec((tm, D_MODEL), lambda i, te, nu: (i, 0)),
            scratch_shapes=[pltpu.VMEM((D_MODEL, D_FF), BF16), pltpu.VMEM((D_MODEL, D_FF), BF16),
                            pltpu.VMEM((D_FF, D_MODEL), BF16)]),
        out_shape=jax.ShapeDtypeStruct((ns, D_MODEL), F32),
        compiler_params=pltpu.CompilerParams(dimension_semantics=("arbitrary",),
                                             vmem_limit_bytes=EXPERTS_VMEM_LIMIT),
        name="moe_experts",
    )(tile_e, n_used, xs, wgu, wd, bg, bu, bd, _gate_up_selector())


def _combine_body(pos_ref, ys_ref, gate_ref, x1_ref, pe_ref, g_ref, b_ref, wple_ref, wpg_ref,
                  o_ref, buf, sem, *, tm):
    i = pl.program_id(0)
    cur = i % 2

    def gather(tile, slot):
        def make_copy(gi, j, k, p):
            return pltpu.make_async_copy(ys_ref.at[pl.ds(p, 1)], buf.at[slot, k, gi, pl.ds(j, 1)],
                                         sem.at[slot])
        _row_copies(pos_ref, tile * tm, tm, make_copy)

    @pl.when(i == 0)
    def _():
        gather(0, 0)

    for slot in range(2):
        @pl.when((i + 1 < pl.num_programs(0)) & (cur != slot))
        def _():
            gather(i + 1, slot)

    for k in range(TOP_K):
        pltpu.make_async_copy(buf.at[cur, k], buf.at[cur, k], sem.at[cur]).wait()
    gates = gate_ref[...]
    picked = lambda k: buf[cur, k].reshape(tm, D_MODEL)
    ffn = gates[:, 0:1] * picked(0)
    for k in range(1, TOP_K):
        ffn = ffn + gates[:, k:k + 1] * picked(k)
    x2 = _layer_norm(DEEPNORM_ALPHA * x1_ref[...] + ffn, g_ref[...], b_ref[...])
    gate = _sigmoid(jnp.dot(x2.astype(BF16), wpg_ref[...], preferred_element_type=F32))
    emb = jnp.dot(pe_ref[...].astype(BF16), wple_ref[...], preferred_element_type=F32)
    o_ref[...] = x2 + gate * emb


def _combine(pos_flat, ys, gates, x1, pe, g, b, wple_bf16, wpg_bf16, tm):
    t = x1.shape[0]
    row = lambda i, *_: (i, 0)
    fixed = lambda i, *_: (0, 0)
    return pl.pallas_call(
        functools.partial(_combine_body, tm=tm),
        grid_spec=pltpu.PrefetchScalarGridSpec(
            num_scalar_prefetch=1,
            grid=(t // tm,),
            in_specs=[pl.BlockSpec(memory_space=pl.ANY),
                      pl.BlockSpec((tm, LANES), row),
                      pl.BlockSpec((tm, D_MODEL), row),
                      pl.BlockSpec((tm, PLE_DIM), row),
                      pl.BlockSpec((1, D_MODEL), fixed),
                      pl.BlockSpec((1, D_MODEL), fixed),
                      pl.BlockSpec((PLE_DIM, D_MODEL), fixed),
                      pl.BlockSpec((D_MODEL, D_MODEL), fixed)],
            out_specs=pl.BlockSpec((tm, D_MODEL), row),
            scratch_shapes=[pltpu.VMEM((2, TOP_K, tm // ROW_DMA_UNROLL, ROW_DMA_UNROLL, D_MODEL), F32),
                            pltpu.SemaphoreType.DMA((2,))]),
        out_shape=jax.ShapeDtypeStruct((t, D_MODEL), F32),
        compiler_params=_cparams(("arbitrary",)),
        name="moe_combine",
    )(pos_flat, ys, gates, x1, pe, g, b, wple_bf16, wpg_bf16)


def kernel(x_prompt, x_sample, cache_k, cache_v, state_shift, state_wkv, p_prompt, p_sample,
           ln_emb_g, ln_emb_b, w_in, attn_sinks, rwkv_mu, rwkv_w0, rwkv_w_lora_up, rwkv_a0,
           rwkv_a_lora_up, rwkv_g_lora_up, rwkv_k_k, rwkv_k_a, rwkv_r_k, rwkv_gn_g, rwkv_gn_b,
           w_out, ln1_g, ln1_b, w_router, b_router, w_gate_up, b_gate_up, w_down, b_down,
           ln2_g, ln2_b, w_ple, w_ple_gate):
    assert w_in.shape[0] == DEPTH == 1
    nbp, seq, _ = x_prompt.shape
    nbs, dec_seq, _ = x_sample.shape
    assert dec_seq == 1
    tp, ts = nbp * seq, nbs
    w_buf = cache_k.shape[2]
    rowv = lambda z: z.reshape(1, -1)

    w_in_b = w_in[0].astype(BF16)
    w_out_b = w_out[0].astype(BF16)
    w_ple_b = w_ple[0].astype(BF16)
    w_pg_b = w_ple_gate[0].astype(BF16)
    sinks = attn_sinks[0]
    prm = dict(mu=rowv(rwkv_mu[0]), w0=rowv(rwkv_w0[0]), wlu=rwkv_w_lora_up[0],
               a0=rowv(rwkv_a0[0]), alu=rwkv_a_lora_up[0], glu=rwkv_g_lora_up[0],
               k_k=rowv(rwkv_k_k[0]), k_a=rowv(rwkv_k_a[0]), r_k=rowv(rwkv_r_k[0]),
               gn_g=rowv(rwkv_gn_g[0]), gn_b=rowv(rwkv_gn_b[0]))
    ge, be = rowv(ln_emb_g), rowv(ln_emb_b)

    xn_p, q_p, kv_p, rp_p = _inproj(x_prompt.reshape(tp, D_MODEL), ge, be, w_in_b, 512)
    att_p = _swa_prompt(sinks, q_p, kv_p, nbp, seq)
    rw_p, wkv_p = _rwkv_prompt(rp_p, prm, nbp, seq, 256)

    xn_s, q_s, kv_s, rp_s = _inproj(x_sample.reshape(ts, D_MODEL), ge, be, w_in_b, ts)
    ck3 = cache_k[0].reshape(nbs, w_buf, KV_COLS)
    cv3 = cache_v[0].reshape(nbs, w_buf, KV_COLS)
    att_s = _swa_sample(sinks, q_s.reshape(nbs, ATT_HEADS, HEAD_DIM),
                        kv_s.reshape(nbs, 2 * KV_HEADS, HEAD_DIM), ck3, cv3)
    att_s = att_s.reshape(ts, ATT_WIDTH)
    p7 = _rwkv_sample_prep(rp_s, state_shift[0], prm)
    hv = lambda z: z.reshape(RWKV_HEADS, HEAD_DIM)
    wkv_s, rw_s = _rwkv_sample_step(p7.reshape(7, nbs, RWKV_HEADS, HEAD_DIM), state_wkv[0],
                                    hv(rwkv_r_k[0]), hv(rwkv_gn_g[0]), hv(rwkv_gn_b[0]))
    rw_s = rw_s.reshape(ts, RWKV_WIDTH)

    wr_pad = jnp.zeros((D_MODEL, LANES), F32).at[:, :N_EXPERTS].set(w_router[0])
    wr_hi = wr_pad.astype(BF16)
    wr_lo = (wr_pad - wr_hi.astype(F32)).astype(BF16)
    br_pad = jnp.full((1, LANES), -jnp.inf, F32).at[0, :N_EXPERTS].set(b_router[0])
    g1, b1 = rowv(ln1_g[0]), rowv(ln1_b[0])
    tmp, tmd = 512, 256
    x1_p, info_p, gate_p, cnt_p = _mix_router(att_p, rw_p, xn_p, w_out_b, g1, b1, wr_hi, wr_lo,
                                              br_pad, jnp.zeros((1, LANES), F32), tmp)
    x1_s, info_s, gate_s, cnt = _mix_router(att_s, rw_s, xn_s, w_out_b, g1, b1, wr_hi, wr_lo,
                                            br_pad, cnt_p, ts)

    counts = cnt[0, :N_EXPERTS].astype(jnp.int32)
    padded = ((counts + EXPERT_TILE - 1) // EXPERT_TILE) * EXPERT_TILE
    ends = jnp.cumsum(padded)
    offs = ends - padded
    n_slots = (tp + ts) * TOP_K + N_EXPERTS * EXPERT_TILE
    n_tiles = n_slots // EXPERT_TILE
    n_used = (ends[-1] // EXPERT_TILE).reshape(1).astype(jnp.int32)
    tile_start = jnp.arange(n_tiles, dtype=jnp.int32) * EXPERT_TILE
    tile_e = jnp.sum(ends[None, :] <= tile_start[:, None], axis=1).astype(jnp.int32)
    last_e = jnp.max(jnp.where(padded > 0, jnp.arange(N_EXPERTS), 0)).astype(jnp.int32)
    tile_e = jnp.minimum(tile_e, last_e)

    def slots(info):
        expert = jnp.arange(N_EXPERTS, dtype=jnp.int32)[:, None, None]
        first = jnp.sum(jnp.where(info[None, 0:TOP_K] == expert, offs[:, None, None], 0), axis=0)
        return (first + info[TOP_K:2 * TOP_K]).reshape(-1).astype(jnp.int32)

    pos_p, pos_s = slots(info_p), slots(info_s)

    fill_start = jnp.concatenate([jnp.clip(ends - EXPERT_TILE, 0, n_slots - EXPERT_TILE),
                                  n_used]).astype(jnp.int32)
    xs = _dispatch(pos_p, pos_s, fill_start, x1_p, x1_s, n_slots, tmd)
    bgu = b_gate_up[0]
    bg = bgu[:, 0::2].reshape(N_EXPERTS, 1, D_FF)
    bu = bgu[:, 1::2].reshape(N_EXPERTS, 1, D_FF)
    bdn = b_down[0].reshape(N_EXPERTS, 1, D_MODEL)
    ys = _experts(tile_e, n_used, xs, w_gate_up[0], w_down[0], bg, bu, bdn)

    g2, b2 = rowv(ln2_g[0]), rowv(ln2_b[0])
    y_p = _combine(pos_p, ys, gate_p, x1_p, p_prompt[0].reshape(tp, PLE_DIM), g2, b2,
                   w_ple_b, w_pg_b, tmd)
    y_s = _combine(pos_s, ys, gate_s, x1_s, p_sample[0].reshape(ts, PLE_DIM), g2, b2,
                   w_ple_b, w_pg_b, ts)

    w_keep = min(WINDOW, seq)
    kv_p3 = kv_p.reshape(nbp, seq, 2 * KV_COLS)[:, seq - w_keep:]
    k_win_p = kv_p3[:, :, 0:KV_COLS].reshape(1, nbp, w_keep, KV_HEADS, HEAD_DIM)
    v_win_p = kv_p3[:, :, KV_COLS:].reshape(1, nbp, w_keep, KV_HEADS, HEAD_DIM)
    shift_p = rp_p.reshape(nbp, seq, RWKV_PROJ)[:, seq - 1][None]
    k_new = kv_s[:, 0:KV_COLS].reshape(nbs, 1, KV_HEADS, HEAD_DIM)
    v_new = kv_s[:, KV_COLS:].reshape(nbs, 1, KV_HEADS, HEAD_DIM)
    k_win_s = jnp.concatenate([cache_k[0], k_new], axis=1)[:, 1:][None]
    v_win_s = jnp.concatenate([cache_v[0], v_new], axis=1)[:, 1:][None]
    return (y_p.reshape(nbp, seq, D_MODEL), y_s.reshape(nbs, 1, D_MODEL),
            k_win_p, v_win_p, shift_p, wkv_p[None],
            k_win_s, v_win_s, rp_s[None], wkv_s[None])
```

```python
import functools

import jax
import jax.numpy as jnp
from jax import lax
from jax.experimental import pallas as pl
from jax.experimental.pallas import tpu as pltpu

F32 = jnp.float32
BF16 = jnp.bfloat16

D_MODEL = 1024
HEAD_DIM = 64
ATT_HEADS = 8
KV_HEADS = 2
GROUP = ATT_HEADS // KV_HEADS
ATT_WIDTH = ATT_HEADS * HEAD_DIM
KV_COLS = KV_HEADS * HEAD_DIM
RWKV_HEADS = 8
RWKV_WIDTH = RWKV_HEADS * HEAD_DIM
LORA_W = 64
LORA_A = 64
LORA_G = 128
RWKV_PROJ = 3 * RWKV_WIDTH + LORA_W + LORA_A + LORA_G
WINDOW = 128
BLOCK = 128
PLE_DIM = 256
N_EXPERTS = 32
TOP_K = 4
D_FF = 1024
SWIGLU_LIMIT = 7.0
SWIGLU_ALPHA = 1.702
LN_EPS = 1e-5
GN_EPS = 64e-5
DEPTH = 1
DEEPNORM_ALPHA = (2 * DEPTH) ** 0.25

LANES = 128
CHUNK = 64
HEAD_QUAD = 4
EXPERT_TILE = 512
INPROJ_TILE = 512
RWKV_TILE = 256
ROUTER_TILE = 512
TOKEN_TILE = 256
ROW_DMA_UNROLL = 8
VMEM_LIMIT = 48 * 1024 * 1024
EXPERTS_VMEM_LIMIT = 58 * 1024 * 1024


def _cparams(sem):
    return pltpu.CompilerParams(dimension_semantics=sem, vmem_limit_bytes=VMEM_LIMIT)


def _bdot(a, b):
    return jnp.dot(a.astype(BF16), b.astype(BF16), preferred_element_type=F32)


def _bdot_nt(a, b):
    return lax.dot_general(a.astype(BF16), b.astype(BF16), (((1,), (1,)), ((), ())),
                           preferred_element_type=F32)


def _bdot_tn(a, b):
    return lax.dot_general(a.astype(BF16), b.astype(BF16), (((0,), (0,)), ((), ())),
                           preferred_element_type=F32)


def _split_dot(m_bf16, x, passes):
    acc = None
    rem = x
    for _ in range(passes):
        hi = rem.astype(BF16)
        part = jnp.dot(m_bf16, hi, preferred_element_type=F32)
        acc = part if acc is None else acc + part
        rem = rem - hi.astype(F32)
    return acc


def _head_sum(x, bd_bf16):
    width = bd_bf16.shape[0]
    outs = []
    for c in range(x.shape[1] // width):
        acc = None
        rem = x[:, c * width:(c + 1) * width]
        for _ in range(2):
            hi = rem.astype(BF16)
            part = jnp.dot(hi, bd_bf16, preferred_element_type=F32)
            acc = part if acc is None else acc + part
            rem = rem - hi.astype(F32)
        outs.append(acc)
    return jnp.concatenate(outs, axis=1)


def _sigmoid(x):
    return 1.0 / (1.0 + jnp.exp(-x))


def _layer_norm(x, g, b):
    mu = jnp.mean(x, axis=-1, keepdims=True)
    xc = x - mu
    var = jnp.mean(xc * xc, axis=-1, keepdims=True)
    return xc * lax.rsqrt(var + LN_EPS) * g + b


def _inproj_body(x_ref, g_ref, b_ref, w_ref, xn_ref, q_ref, kv_ref, rp_ref):
    xn = _layer_norm(x_ref[...], g_ref[...], b_ref[...])
    xn_ref[...] = xn
    xb = xn.astype(BF16)
    q_ref[...] = jnp.dot(xb, w_ref[:, 0:ATT_WIDTH], preferred_element_type=F32)
    kv_ref[...] = jnp.dot(xb, w_ref[:, ATT_WIDTH:ATT_WIDTH + 2 * KV_COLS],
                          preferred_element_type=F32)
    rp_ref[...] = jnp.dot(xb, w_ref[:, ATT_WIDTH + 2 * KV_COLS:], preferred_element_type=F32)


def _inproj(x2d, g, b, w_bf16, tm):
    t = x2d.shape[0]
    in_proj = w_bf16.shape[1]
    row = lambda i: (i, 0)
    fixed = lambda i: (0, 0)
    return pl.pallas_call(
        _inproj_body,
        grid=(t // tm,),
        in_specs=[pl.BlockSpec((tm, D_MODEL), row),
                  pl.BlockSpec((1, D_MODEL), fixed),
                  pl.BlockSpec((1, D_MODEL), fixed),
                  pl.BlockSpec((D_MODEL, in_proj), fixed)],
        out_specs=[pl.BlockSpec((tm, D_MODEL), row),
                   pl.BlockSpec((tm, ATT_WIDTH), row),
                   pl.BlockSpec((tm, 2 * KV_COLS), row),
                   pl.BlockSpec((tm, RWKV_PROJ), row)],
        out_shape=[jax.ShapeDtypeStruct((t, D_MODEL), F32),
                   jax.ShapeDtypeStruct((t, ATT_WIDTH), F32),
                   jax.ShapeDtypeStruct((t, 2 * KV_COLS), F32),
                   jax.ShapeDtypeStruct((t, RWKV_PROJ), F32)],
        compiler_params=_cparams(("parallel",)),
        name="inproj",
    )(x2d, g, b, w_bf16)


def _alibi_slope(h):
    return 2.0 ** (-8.0 * (h + 1) / ATT_HEADS)


def _swa_prompt_body(sink_ref, q_ref, kvc_ref, kvp_ref, o_ref):
    n = pl.program_id(1)
    q = (q_ref[...] * (HEAD_DIM ** -0.5)).astype(BF16)
    kvc = kvc_ref[...].astype(BF16)
    kvp = kvp_ref[...].astype(BF16)
    row = lax.broadcasted_iota(jnp.int32, (BLOCK, 2 * BLOCK), 0)
    col = lax.broadcasted_iota(jnp.int32, (BLOCK, 2 * BLOCK), 1)
    dist = row + BLOCK - col
    valid = (dist >= 0) & (dist <= WINDOW) & ((col >= BLOCK) | (n > 0))
    distf = dist.astype(F32)
    kbands, vbands = [], []
    for g in range(KV_HEADS):
        ks = slice(g * HEAD_DIM, (g + 1) * HEAD_DIM)
        vs = slice(KV_COLS + g * HEAD_DIM, KV_COLS + (g + 1) * HEAD_DIM)
        kbands.append(jnp.concatenate([kvp[:, ks], kvc[:, ks]], axis=0))
        vbands.append(jnp.concatenate([kvp[:, vs], kvc[:, vs]], axis=0))
    scores = [lax.dot_general(q[:, h * HEAD_DIM:(h + 1) * HEAD_DIM], kbands[h // GROUP],
                              (((1,), (1,)), ((), ())), preferred_element_type=F32)
              for h in range(ATT_HEADS)]
    probs = []
    for h in range(ATT_HEADS):
        s = jnp.where(valid, scores[h] - _alibi_slope(h) * distf, -jnp.inf)
        sink = sink_ref[h]
        m = jnp.maximum(jnp.max(s, axis=1, keepdims=True), sink)
        e = jnp.exp(s - m)
        den = jnp.sum(e, axis=1, keepdims=True) + jnp.exp(sink - m)
        probs.append((e / den).astype(BF16))
    outs = [jnp.dot(probs[h], vbands[h // GROUP], preferred_element_type=F32)
            for h in range(ATT_HEADS)]
    o_ref[...] = jnp.concatenate(outs, axis=1)


def _swa_prompt(sinks, q, kv, nbatch, seq):
    nb = seq // BLOCK
    cur = lambda b, n: (b * nb + n, 0)
    prv = lambda b, n: (b * nb + jnp.maximum(n - 1, 0), 0)
    return pl.pallas_call(
        _swa_prompt_body,
        grid=(nbatch, nb),
        in_specs=[pl.BlockSpec(memory_space=pltpu.SMEM),
                  pl.BlockSpec((BLOCK, ATT_WIDTH), cur),
                  pl.BlockSpec((BLOCK, 2 * KV_COLS), cur),
                  pl.BlockSpec((BLOCK, 2 * KV_COLS), prv)],
        out_specs=pl.BlockSpec((BLOCK, ATT_WIDTH), cur),
        out_shape=jax.ShapeDtypeStruct((nbatch * seq, ATT_WIDTH), F32),
        compiler_params=_cparams(("parallel", "parallel")),
        name="swa_prompt",
    )(sinks, q, kv, kv)


def _swa_sample_body(sink_ref, q_ref, kvn_ref, ck_ref, cv_ref, o_ref, *, bb, w_buf):
    hrow = lax.broadcasted_iota(jnp.int32, (ATT_HEADS, 1), 0)
    slope = jnp.zeros((ATT_HEADS, 1), F32)
    sink = jnp.zeros((ATT_HEADS, 1), F32)
    for h in range(ATT_HEADS):
        slope = jnp.where(hrow == h, _alibi_slope(h), slope)
        sink = jnp.where(hrow == h, sink_ref[h], sink)
    jcol = lax.broadcasted_iota(jnp.int32, (ATT_HEADS, w_buf), 1)
    bias = -slope * (w_buf - jcol).astype(F32)
    lower = hrow < GROUP
    q2b = [(q_ref[b] * (HEAD_DIM ** -0.5)).astype(BF16) for b in range(bb)]
    s01 = []
    for b in range(bb):
        ck = ck_ref[b].astype(BF16)
        s01.append((_bdot_nt(q2b[b], ck[:, 0:HEAD_DIM]),
                    _bdot_nt(q2b[b], ck[:, HEAD_DIM:2 * HEAD_DIM])))
    ps = []
    for b in range(bb):
        kvn = kvn_ref[b]
        s = jnp.where(lower, s01[b][0], s01[b][1]) + bias
        knew = jnp.where(lower, kvn[0:1, :], kvn[1:2, :])
        vnew = jnp.where(lower, kvn[2:3, :], kvn[3:4, :])
        snew = jnp.sum(q2b[b].astype(F32) * knew.astype(BF16).astype(F32), axis=1, keepdims=True)
        m = jnp.maximum(jnp.maximum(jnp.max(s, axis=1, keepdims=True), snew), sink)
        e = jnp.exp(s - m)
        enew = jnp.exp(snew - m)
        den = jnp.sum(e, axis=1, keepdims=True) + enew + jnp.exp(sink - m)
        pnew = (enew / den).astype(BF16).astype(F32)
        ps.append(((e / den).astype(BF16), pnew * vnew.astype(BF16).astype(F32)))
    for b in range(bb):
        cv = cv_ref[b].astype(BF16)
        o0 = jnp.dot(ps[b][0], cv[:, 0:HEAD_DIM], preferred_element_type=F32)
        o1 = jnp.dot(ps[b][0], cv[:, HEAD_DIM:2 * HEAD_DIM], preferred_element_type=F32)
        o_ref[b] = jnp.where(lower, o0, o1) + ps[b][1]


def _swa_sample(sinks, q3, kvn3, ck3, cv3, bb=8):
    nb, w_buf = ck3.shape[0], ck3.shape[1]
    blk = lambda i: (i, 0, 0)
    return pl.pallas_call(
        functools.partial(_swa_sample_body, bb=bb, w_buf=w_buf),
        grid=(nb // bb,),
        in_specs=[pl.BlockSpec(memory_space=pltpu.SMEM),
                  pl.BlockSpec((bb, ATT_HEADS, HEAD_DIM), blk),
                  pl.BlockSpec((bb, 2 * KV_HEADS, HEAD_DIM), blk),
                  pl.BlockSpec((bb, w_buf, KV_COLS), blk),
                  pl.BlockSpec((bb, w_buf, KV_COLS), blk)],
        out_specs=pl.BlockSpec((bb, ATT_HEADS, HEAD_DIM), blk),
        out_shape=jax.ShapeDtypeStruct((nb, ATT_HEADS, HEAD_DIM), F32),
        compiler_params=_cparams(("parallel",)),
        name="swa_sample",
    )(sinks, q3, kvn3, ck3, cv3)


def _rwkv_prep(rp, prev, mu, w0, wlu, a0, alu, glu, k_k, k_a, bd):
    xs = rp + mu * (prev - rp)
    r = xs[:, 0:RWKV_WIDTH]
    k = xs[:, RWKV_WIDTH:2 * RWKV_WIDTH]
    v = xs[:, 2 * RWKV_WIDTH:3 * RWKV_WIDTH]
    o = 3 * RWKV_WIDTH
    wd = xs[:, o:o + LORA_W]
    ad = xs[:, o + LORA_W:o + LORA_W + LORA_A]
    gd = xs[:, o + LORA_W + LORA_A:]
    z = -(w0 + _bdot(jnp.tanh(wd), wlu))
    softplus = jnp.maximum(z, 0.0) + jnp.log(1.0 + jnp.exp(-jnp.abs(z)))
    log_decay = -jnp.exp(-softplus - 0.5)
    a = _sigmoid(a0 + _bdot(ad, alu))
    g = _bdot(_sigmoid(gd), glu)
    kkr = k * k_k
    kk = kkr / jnp.maximum(jnp.sqrt(_head_sum(kkr * kkr, bd)), 1e-12)
    k2 = k * (1.0 + (a - 1.0) * k_a)
    return r, log_decay, k2, v, kk, a, g


def _rwkv_finish(y, r, k2, v, g, r_k, gn_g, gn_b, bd):
    inv = 1.0 / HEAD_DIM
    mu = _head_sum(y, bd) * inv
    yc = y - mu
    var = _head_sum(yc * yc, bd) * inv
    yn = yc * lax.rsqrt(var + GN_EPS) * gn_g + gn_b
    bonus = _head_sum(r * k2 * r_k, bd) * v
    return (yn + bonus) * g


def _rwkv_prompt_body(rp_ref, mu_ref, w0_ref, wlu_ref, a0_ref, alu_ref, glu_ref, kk_ref, ka_ref,
                      rk_ref, gng_ref, gnb_ref, bd_ref, ltri_ref,
                      out_ref, sfin_ref, prev_scr, st_scr, y_scr, *, tt):
    c = pl.program_id(1)

    @pl.when(c == 0)
    def _():
        prev_scr[...] = jnp.zeros_like(prev_scr)
        st_scr[...] = jnp.zeros_like(st_scr)

    rp = rp_ref[...]
    rolled = pltpu.roll(rp, 1, 0)
    rowi = lax.broadcasted_iota(jnp.int32, (tt, 1), 0)
    prev = jnp.where(rowi == 0, prev_scr[...], rolled)
    prev_scr[...] = rp[tt - 1:tt, :]
    bd = bd_ref[...]
    r, ld, k2, v, kk, a, g = _rwkv_prep(rp, prev, mu_ref[...], w0_ref[...], wlu_ref[...],
                                        a0_ref[...], alu_ref[...], glu_ref[...], kk_ref[...],
                                        ka_ref[...], bd)
    cum = _split_dot(ltri_ref[...], ld, 2)
    ecum = jnp.exp(cum)
    einv = jnp.exp(-cum)
    at = -kk * jnp.exp(cum - ld)
    bh = kk * a * einv
    kh = k2 * einv
    rt = r * ecum

    qw = HEAD_QUAD * HEAD_DIM
    ii = lax.broadcasted_iota(jnp.int32, (CHUNK, qw), 0)
    jl = lax.broadcasted_iota(jnp.int32, (CHUNK, qw), 1) % CHUNK
    strict = ii > jl
    incl = ii >= jl
    eye = (ii == jl).astype(F32)
    same_head = (lax.broadcasted_iota(jnp.int32, (qw, qw), 0) // HEAD_DIM
                 == lax.broadcasted_iota(jnp.int32, (qw, qw), 1) // HEAD_DIM)

    def blockdiag(m):
        mb = m.astype(BF16)
        return jnp.where(same_head, jnp.concatenate([mb] * HEAD_QUAD, axis=0), jnp.zeros((), BF16))

    def diag_blocks(full):
        out = None
        for hh in range(HEAD_QUAD):
            rs = slice(hh * HEAD_DIM, (hh + 1) * HEAD_DIM)
            blk = jnp.where(same_head[rs], full[rs], 0.0)
            out = blk if out is None else out + blk
        return out

    mm = lambda x, w_bf16: jnp.dot(x.astype(BF16), w_bf16, preferred_element_type=F32)
    mm_nt = lambda x, w_bf16: lax.dot_general(x.astype(BF16), w_bf16, (((1,), (1,)), ((), ())),
                                              preferred_element_type=F32)
    nsub = tt // CHUNK
    units = [(s, q) for s in range(nsub) for q in range(RWKV_HEADS // HEAD_QUAD)]
    rows = lambda s: slice(s * CHUNK, (s + 1) * CHUNK)
    cols = lambda q: slice(q * qw, (q + 1) * qw)
    cut = lambda z, u: z[rows(u[0]), cols(u[1])]
    kka = kk * a
    p_last, kh_end, bh_end = [], [], []
    for s in range(nsub):
        c_last = cum[(s + 1) * CHUNK - 1:(s + 1) * CHUNK, :]
        p_last.append(jnp.exp(c_last))
        tail = jnp.exp(c_last - cum[rows(s)])
        kh_end.append(k2[rows(s)] * tail)
        bh_end.append(kka[rows(s)] * tail)
    x2 = {u: jnp.concatenate([cut(at, u), cut(rt, u)], axis=0) for u in units}
    xb = {u: mm_nt(x2[u], blockdiag(cut(bh, u))) for u in units}
    xk = {u: mm_nt(x2[u], blockdiag(cut(kh, u))) for u in units}
    nmat = {u: jnp.where(strict, xb[u][0:CHUNK], 0.0) for u in units}
    a_rb = {u: jnp.where(incl, xb[u][CHUNK:], 0.0) for u in units}
    a_ak = {u: jnp.where(strict, xk[u][0:CHUNK], 0.0) for u in units}
    a_rk = {u: jnp.where(incl, xk[u][CHUNK:], 0.0) for u in units}
    tmat = {u: eye + nmat[u] for u in units}
    npow = nmat
    for _ in range(5):
        npow = {u: mm(npow[u], blockdiag(npow[u])) for u in units}
        tmat = {u: tmat[u] + mm(npow[u], blockdiag(tmat[u])) for u in units}
    vbd = {u: blockdiag(cut(v, u)) for u in units}
    av = {u: mm(a_ak[u], vbd[u]) for u in units}
    aprime = {u: mm(tmat[u], blockdiag(cut(at, u))) for u in units}
    wmat = {u: mm(tmat[u], blockdiag(av[u])) for u in units}
    rprime = {u: cut(rt, u) + mm(a_rb[u], blockdiag(aprime[u])) for u in units}
    y0 = {u: mm(a_rk[u], vbd[u]) + mm(a_rb[u], blockdiag(wmat[u])) for u in units}
    hmat = {u: diag_blocks(_bdot_tn(jnp.concatenate([cut(v, u), wmat[u]], axis=0),
                                    jnp.concatenate([cut(kh_end[u[0]], (0, u[1])),
                                                     cut(bh_end[u[0]], (0, u[1]))], axis=0)))
            for u in units}
    g0 = {u: diag_blocks(_bdot_tn(aprime[u], cut(bh_end[u[0]], (0, u[1])))) for u in units}

    states = [st_scr[q] for q in range(RWKV_HEADS // HEAD_QUAD)]
    for s in range(nsub):
        for q in range(RWKV_HEADS // HEAD_QUAD):
            u = (s, q)
            st = states[q]
            y_scr[rows(s), cols(q)] = mm_nt(rprime[u], blockdiag(st)) + y0[u]
            states[q] = st * p_last[s][:, cols(q)] + mm(st, blockdiag(g0[u])) + hmat[u]
    for q in range(RWKV_HEADS // HEAD_QUAD):
        st_scr[q] = states[q]
        for hh in range(HEAD_QUAD):
            sfin_ref[0, q * HEAD_QUAD + hh] = states[q][:, hh * HEAD_DIM:(hh + 1) * HEAD_DIM]
    out_ref[...] = _rwkv_finish(y_scr[...], r, k2, v, g, rk_ref[...], gng_ref[...], gnb_ref[...], bd)


def _chunk_tril(tt):
    i = jnp.arange(tt)
    same = (i[:, None] // CHUNK) == (i[None, :] // CHUNK)
    return (same & (i[:, None] >= i[None, :])).astype(BF16)


def _head_blockdiag():
    i = jnp.arange(HEAD_QUAD * HEAD_DIM) // HEAD_DIM
    return (i[:, None] == i[None, :]).astype(BF16)


def _rwkv_prompt(rp, prm, nbatch, seq, tt):
    nc = seq // tt
    fixed = lambda b, c: (0, 0)
    row = lambda b, c: (b * nc + c, 0)
    vec = lambda n: pl.BlockSpec((1, n), fixed)
    return pl.pallas_call(
        functools.partial(_rwkv_prompt_body, tt=tt),
        grid=(nbatch, nc),
        in_specs=[pl.BlockSpec((tt, RWKV_PROJ), row),
                  vec(RWKV_PROJ), vec(RWKV_WIDTH),
                  pl.BlockSpec((LORA_W, RWKV_WIDTH), fixed), vec(RWKV_WIDTH),
                  pl.BlockSpec((LORA_A, RWKV_WIDTH), fixed),
                  pl.BlockSpec((LORA_G, RWKV_WIDTH), fixed),
                  vec(RWKV_WIDTH), vec(RWKV_WIDTH), vec(RWKV_WIDTH), vec(RWKV_WIDTH),
                  vec(RWKV_WIDTH),
                  pl.BlockSpec((HEAD_QUAD * HEAD_DIM, HEAD_QUAD * HEAD_DIM), fixed),
                  pl.BlockSpec((tt, tt), fixed)],
        out_specs=[pl.BlockSpec((tt, RWKV_WIDTH), row),
                   pl.BlockSpec((1, RWKV_HEADS, HEAD_DIM, HEAD_DIM), lambda b, c: (b, 0, 0, 0))],
        out_shape=[jax.ShapeDtypeStruct((nbatch * seq, RWKV_WIDTH), F32),
                   jax.ShapeDtypeStruct((nbatch, RWKV_HEADS, HEAD_DIM, HEAD_DIM), F32)],
        scratch_shapes=[pltpu.VMEM((1, RWKV_PROJ), F32),
                        pltpu.VMEM((RWKV_HEADS // HEAD_QUAD, HEAD_DIM, HEAD_QUAD * HEAD_DIM), F32),
                        pltpu.VMEM((tt, RWKV_WIDTH), F32)],
        compiler_params=_cparams(("parallel", "arbitrary")),
        name="rwkv_prompt",
    )(rp, prm["mu"], prm["w0"], prm["wlu"], prm["a0"], prm["alu"], prm["glu"], prm["k_k"],
      prm["k_a"], prm["r_k"], prm["gn_g"], prm["gn_b"], _head_blockdiag(), _chunk_tril(tt))


def _rwkv_sample_prep_body(rp_ref, prev_ref, mu_ref, w0_ref, wlu_ref, a0_ref, alu_ref, glu_ref,
                           kk_ref, ka_ref, bd_ref, o_ref):
    r, ld, k2, v, kk, a, g = _rwkv_prep(rp_ref[...], prev_ref[...], mu_ref[...], w0_ref[...],
                                        wlu_ref[...], a0_ref[...], alu_ref[...], glu_ref[...],
                                        kk_ref[...], ka_ref[...], bd_ref[...])
    for i, z in enumerate((r, jnp.exp(ld), k2, v, kk, a, g)):
        o_ref[i] = z


def _rwkv_sample_prep(rp, prev, prm):
    nb = rp.shape[0]
    return pl.pallas_call(
        _rwkv_sample_prep_body,
        out_shape=jax.ShapeDtypeStruct((7, nb, RWKV_WIDTH), F32),
        compiler_params=pltpu.CompilerParams(vmem_limit_bytes=VMEM_LIMIT),
        name="rwkv_sample_prep",
    )(rp, prev, prm["mu"], prm["w0"], prm["wlu"], prm["a0"], prm["alu"], prm["glu"], prm["k_k"],
      prm["k_a"], _head_blockdiag())


def _rwkv_sample_step_body(p_ref, s_ref, rk_ref, gng_ref, gnb_ref, so_ref, o_ref, *, bb):
    ii = lax.broadcasted_iota(jnp.int32, (HEAD_DIM, HEAD_DIM), 0)
    jj = lax.broadcasted_iota(jnp.int32, (HEAD_DIM, HEAD_DIM), 1)
    eye = ii == jj
    for b in range(bb):
        for h in range(RWKV_HEADS):
            row = lambda i: p_ref[i, b, h:h + 1, :]
            r_r, w_r, k_r, v_r, kk_r, a_r = (row(i) for i in range(6))
            st = s_ref[b, h]
            sa = jnp.sum(st * kk_r, axis=1, keepdims=True)
            v_col = jnp.sum(jnp.where(eye, v_r, 0.0), axis=1, keepdims=True)
            st = st * w_r - sa * (kk_r * a_r) + v_col * k_r
            so_ref[b, h] = st
            y_col = jnp.sum(st * r_r, axis=1, keepdims=True)
            o_ref[b, h:h + 1, :] = jnp.sum(jnp.where(eye, y_col, 0.0), axis=0, keepdims=True)
    y = o_ref[...]
    r, k2, v, g = p_ref[0], p_ref[2], p_ref[3], p_ref[6]
    mu = jnp.mean(y, axis=-1, keepdims=True)
    yc = y - mu
    var = jnp.mean(yc * yc, axis=-1, keepdims=True)
    yn = yc * lax.rsqrt(var + GN_EPS) * gng_ref[...] + gnb_ref[...]
    bonus = jnp.sum(r * k2 * rk_ref[...], axis=-1, keepdims=True) * v
    o_ref[...] = (yn + bonus) * g


def _rwkv_sample_step(p4, state, r_k, gn_g, gn_b, bb=8):
    nb = state.shape[0]
    hv = pl.BlockSpec((RWKV_HEADS, HEAD_DIM), lambda i: (0, 0))
    return pl.pallas_call(
        functools.partial(_rwkv_sample_step_body, bb=bb),
        grid=(nb // bb,),
        in_specs=[pl.BlockSpec((7, bb, RWKV_HEADS, HEAD_DIM), lambda i: (0, i, 0, 0)),
                  pl.BlockSpec((bb, RWKV_HEADS, HEAD_DIM, HEAD_DIM), lambda i: (i, 0, 0, 0)),
                  hv, hv, hv],
        out_specs=[pl.BlockSpec((bb, RWKV_HEADS, HEAD_DIM, HEAD_DIM), lambda i: (i, 0, 0, 0)),
                   pl.BlockSpec((bb, RWKV_HEADS, HEAD_DIM), lambda i: (i, 0, 0))],
        out_shape=[jax.ShapeDtypeStruct(state.shape, F32),
                   jax.ShapeDtypeStruct((nb, RWKV_HEADS, HEAD_DIM), F32)],
        compiler_params=_cparams(("parallel",)),
        name="rwkv_sample_step",
    )(p4, state, r_k, gn_g, gn_b)


def _mix_router_body(att_ref, rw_ref, xn_ref, wo_ref, g_ref, b_ref, wrh_ref, wrl_ref, br_ref,
                     base_ref, x1_ref, info_ref, gate_ref, cnt_ref, base_scr, *, tm):
    i = pl.program_id(0)

    @pl.when(i == 0)
    def _():
        base_scr[...] = base_ref[...]

    mixed = (jnp.dot(att_ref[...].astype(BF16), wo_ref[0:ATT_WIDTH, :], preferred_element_type=F32)
             + jnp.dot(rw_ref[...].astype(BF16), wo_ref[ATT_WIDTH:, :], preferred_element_type=F32))
    x1 = _layer_norm(DEEPNORM_ALPHA * xn_ref[...] + mixed, g_ref[...], b_ref[...])
    x1_ref[...] = x1

    x1h = x1.astype(BF16)
    x1l = (x1 - x1h.astype(F32)).astype(BF16)
    hi_both = jnp.dot(x1h, jnp.concatenate([wrh_ref[...], wrl_ref[...]], axis=1),
                      preferred_element_type=F32)
    logits = (hi_both[:, 0:LANES] + jnp.dot(x1l, wrh_ref[...], preferred_element_type=F32)
              + hi_both[:, LANES:]) + br_ref[...]
    lane = lax.broadcasted_iota(jnp.int32, (tm, LANES), 1)
    lanef = lane.astype(F32)
    vals, idxs, hots = [], [], []
    cur = logits
    for _ in range(TOP_K):
        m = jnp.max(cur, axis=1, keepdims=True)
        idx = jnp.min(jnp.where(cur == m, lanef, float(LANES)), axis=1, keepdims=True)
        hot = lanef == idx
        cur = jnp.where(hot, -jnp.inf, cur)
        vals.append(m)
        idxs.append(idx)
        hots.append(hot)
    es = [jnp.exp(vk - vals[0]) for vk in vals]
    den = es[0] + es[1] + es[2] + es[3]
    multi = jnp.zeros((tm, LANES), F32)
    for hot in hots:
        multi = multi + hot.astype(F32)
    ti = lax.broadcasted_iota(jnp.int32, (tm, tm), 0)
    tj = lax.broadcasted_iota(jnp.int32, (tm, tm), 1)
    before = jnp.dot((ti > tj).astype(BF16), multi.astype(BF16), preferred_element_type=F32)
    before = before + base_scr[...]
    info = jnp.zeros((tm, LANES), F32)
    gates = jnp.zeros((tm, LANES), F32)
    for k in range(TOP_K):
        rank = jnp.sum(jnp.where(hots[k], before, 0.0), axis=1, keepdims=True)
        info = jnp.where(lane == k, idxs[k], info)
        info = jnp.where(lane == TOP_K + k, rank, info)
        gates = jnp.where(lane == k, es[k] / den, gates)
    info_ref[...] = jnp.transpose(info)[0:2 * TOP_K, :].astype(jnp.int32)
    gate_ref[...] = gates
    base_scr[...] = base_scr[...] + jnp.sum(multi, axis=0, keepdims=True)
    cnt_ref[...] = base_scr[...]


def _mix_router(att, rw, xn, wo_bf16, g, b, wr_hi, wr_lo, br_pad, base, tm):
    t = att.shape[0]
    row = lambda i: (i, 0)
    fixed = lambda i: (0, 0)
    return pl.pallas_call(
        functools.partial(_mix_router_body, tm=tm),
        grid=(t // tm,),
        in_specs=[pl.BlockSpec((tm, ATT_WIDTH), row),
                  pl.BlockSpec((tm, RWKV_WIDTH), row),
                  pl.BlockSpec((tm, D_MODEL), row),
                  pl.BlockSpec((D_MODEL, D_MODEL), fixed),
                  pl.BlockSpec((1, D_MODEL), fixed),
                  pl.BlockSpec((1, D_MODEL), fixed),
                  pl.BlockSpec((D_MODEL, LANES), fixed),
                  pl.BlockSpec((D_MODEL, LANES), fixed),
                  pl.BlockSpec((1, LANES), fixed),
                  pl.BlockSpec((1, LANES), fixed)],
        out_specs=[pl.BlockSpec((tm, D_MODEL), row),
                   pl.BlockSpec((2 * TOP_K, tm), lambda i: (0, i)),
                   pl.BlockSpec((tm, LANES), row),
                   pl.BlockSpec((1, LANES), fixed)],
        out_shape=[jax.ShapeDtypeStruct((t, D_MODEL), F32),
                   jax.ShapeDtypeStruct((2 * TOP_K, t), jnp.int32),
                   jax.ShapeDtypeStruct((t, LANES), F32),
                   jax.ShapeDtypeStruct((1, LANES), F32)],
        scratch_shapes=[pltpu.VMEM((1, LANES), F32)],
        compiler_params=_cparams(("arbitrary",)),
        name="mix_router",
    )(att, rw, xn, wo_bf16, g, b, wr_hi, wr_lo, br_pad, base)


def _row_copies(pos_ref, base, tm, make_copy):
    tokens = pos_ref.shape[0] // TOP_K

    def group(gi, carry):
        r0 = pl.multiple_of(gi * ROW_DMA_UNROLL, ROW_DMA_UNROLL)
        for j in range(ROW_DMA_UNROLL):
            for k in range(TOP_K):
                make_copy(gi, j, k, pos_ref[k * tokens + base + r0 + j]).start(priority=k % 2)
        return carry

    lax.fori_loop(0, tm // ROW_DMA_UNROLL, group, 0)


def _dispatch_rows(pos_ref, base, x_ref, xs_ref, sem, rows):
    def make_copy(gi, j, k, p):
        r = pl.multiple_of(gi * ROW_DMA_UNROLL, ROW_DMA_UNROLL) + j
        return pltpu.make_async_copy(x_ref.at[pl.ds(r, 1)], xs_ref.at[pl.ds(p, 1)], sem)

    _row_copies(pos_ref, base, rows, make_copy)
    for _ in range(TOP_K):
        pltpu.make_async_copy(x_ref, xs_ref.at[pl.ds(0, rows)], sem).wait()


def _dispatch_body(pos_a_ref, pos_b_ref, fill_ref, xa_ref, xb_ref, xs_ref, zero_scr, sem, fill_sem,
                   *, tm, n_tiles):
    i = pl.program_id(0)
    last = pl.num_programs(0) - 1

    @pl.when(i == 0)
    def _():
        zero_scr[...] = jnp.zeros_like(zero_scr)

        def tile_fill(start):
            return pltpu.make_async_copy(
                zero_scr, xs_ref.at[pl.ds(pl.multiple_of(start, EXPERT_TILE), EXPERT_TILE)], fill_sem)

        fills = [tile_fill(fill_ref[e]) for e in range(N_EXPERTS)]
        for cp in fills:
            cp.start()
        first_unused = fill_ref[N_EXPERTS]
        lax.fori_loop(first_unused, n_tiles,
                      lambda t, c: (tile_fill(t * EXPERT_TILE).start(), c)[1], 0)
        for cp in fills:
            cp.wait()
        lax.fori_loop(first_unused, n_tiles,
                      lambda t, c: (tile_fill(t * EXPERT_TILE).wait(), c)[1], 0)

    @pl.when(i < last)
    def _():
        _dispatch_rows(pos_a_ref, i * tm, xa_ref, xs_ref, sem, tm)

    @pl.when(i == last)
    def _():
        _dispatch_rows(pos_b_ref, 0, xb_ref, xs_ref, sem, xb_ref.shape[0])


def _dispatch(pos_a, pos_b, fill_start, x_a, x_b, n_slots, tm):
    nta = x_a.shape[0] // tm
    return pl.pallas_call(
        functools.partial(_dispatch_body, tm=tm, n_tiles=n_slots // EXPERT_TILE),
        grid_spec=pltpu.PrefetchScalarGridSpec(
            num_scalar_prefetch=3,
            grid=(nta + 1,),
            in_specs=[pl.BlockSpec((tm, D_MODEL), lambda i, *_: (jnp.minimum(i, nta - 1), 0)),
                      pl.BlockSpec(x_b.shape, lambda i, *_: (0, 0))],
            out_specs=pl.BlockSpec(memory_space=pl.ANY),
            scratch_shapes=[pltpu.VMEM((EXPERT_TILE, D_MODEL), F32),
                            pltpu.SemaphoreType.DMA, pltpu.SemaphoreType.DMA]),
        out_shape=jax.ShapeDtypeStruct((n_slots, D_MODEL), F32),
        compiler_params=_cparams(("arbitrary",)),
        name="moe_dispatch",
    )(pos_a, pos_b, fill_start, x_a, x_b)


def _experts_body(te_ref, nu_ref, xs_ref, wgu_ref, wd_ref, bg_ref, bu_ref, bd_ref, sel_ref, ys_ref,
                  wg_scr, wu_scr, wd_scr):
    i = pl.program_id(0)
    new_expert = (i == 0) | (te_ref[i] != te_ref[jnp.maximum(i - 1, 0)])

    @pl.when(new_expert)
    def _():
        wd_scr[...] = wd_ref[...].astype(BF16)
        for m in range(D_FF // LANES):
            pair = wgu_ref[:, 2 * m * LANES:2 * (m + 1) * LANES].astype(BF16)
            split = jnp.dot(pair, sel_ref[...], preferred_element_type=F32)
            wg_scr[:, m * LANES:(m + 1) * LANES] = split[:, 0:LANES].astype(BF16)
            wu_scr[:, m * LANES:(m + 1) * LANES] = split[:, LANES:].astype(BF16)

    @pl.when(i < nu_ref[0])
    def _():
        x = xs_ref[...].astype(BF16)
        gate = jnp.dot(x, wg_scr[...], preferred_element_type=F32) + bg_ref[...]
        up = jnp.dot(x, wu_scr[...], preferred_element_type=F32) + bu_ref[...]
        gate = jnp.minimum(gate, SWIGLU_LIMIT)
        up = jnp.clip(up, -SWIGLU_LIMIT, SWIGLU_LIMIT)
        act = (up + 1.0) * gate * _sigmoid(SWIGLU_ALPHA * gate)
        ys_ref[...] = jnp.dot(act.astype(BF16), wd_scr[...], preferred_element_type=F32) + bd_ref[...]

    @pl.when(i >= nu_ref[0])
    def _():
        ys_ref[...] = jnp.zeros_like(ys_ref)


def _gate_up_selector():
    i = jnp.arange(2 * LANES)
    src = jnp.where(i < LANES, 2 * i, 2 * (i - LANES) + 1)
    return (i[:, None] == src[None, :]).astype(BF16)


def _experts(tile_e, n_used, xs, wgu, wd, bg, bu, bd):
    ns = xs.shape[0]
    tm = EXPERT_TILE
    wspec = lambda a, b: pl.BlockSpec((None, a, b), lambda i, te, nu: (te[i], 0, 0))
    return pl.pallas_call(
        _experts_body,
        grid_spec=pltpu.PrefetchScalarGridSpec(
            num_scalar_prefetch=2,
            grid=(ns // tm,),
            in_specs=[pl.BlockSpec((tm, D_MODEL), lambda i, te, nu: (jnp.minimum(i, nu[0] - 1), 0)),
                      wspec(D_MODEL, 2 * D_FF), wspec(D_FF, D_MODEL),
                      wspec(1, D_FF), wspec(1, D_FF), wspec(1, D_MODEL),
                      pl.BlockSpec((2 * LANES, 2 * LANES), lambda i, te, nu: (0, 0))],
            out_specs=pl.BlockSpec((tm, D_MODEL), lambda i, te, nu: (i, 0)),
            scratch_shapes=[pltpu.VMEM((D_MODEL, D_FF), BF16), pltpu.VMEM((D_MODEL, D_FF), BF16),
                            pltpu.VMEM((D_FF, D_MODEL), BF16)]),
        out_shape=jax.ShapeDtypeStruct((ns, D_MODEL), F32),
        compiler_params=pltpu.CompilerParams(dimension_semantics=("arbitrary",),
                                             vmem_limit_bytes=EXPERTS_VMEM_LIMIT),
        name="moe_experts",
    )(tile_e, n_used, xs, wgu, wd, bg, bu, bd, _gate_up_selector())


def _combine_body(pos_ref, ys_ref, gate_ref, x1_ref, pe_ref, g_ref, b_ref, wple_ref, wpg_ref,
                  o_ref, buf, sem, *, tm):
    i = pl.program_id(0)
    cur = i % 2

    def gather(tile, slot):
        def make_copy(gi, j, k, p):
            return pltpu.make_async_copy(ys_ref.at[pl.ds(p, 1)], buf.at[slot, k, gi, pl.ds(j, 1)],
                                         sem.at[slot])
        _row_copies(pos_ref, tile * tm, tm, make_copy)

    @pl.when(i == 0)
    def _():
        gather(0, 0)

    for slot in range(2):
        @pl.when((i + 1 < pl.num_programs(0)) & (cur != slot))
        def _():
            gather(i + 1, slot)

    for k in range(TOP_K):
        pltpu.make_async_copy(buf.at[cur, k], buf.at[cur, k], sem.at[cur]).wait()
    gates = gate_ref[...]
    picked = lambda k: buf[cur, k].reshape(tm, D_MODEL)
    ffn = gates[:, 0:1] * picked(0)
    for k in range(1, TOP_K):
        ffn = ffn + gates[:, k:k + 1] * picked(k)
    x2 = _layer_norm(DEEPNORM_ALPHA * x1_ref[...] + ffn, g_ref[...], b_ref[...])
    gate = _sigmoid(jnp.dot(x2.astype(BF16), wpg_ref[...], preferred_element_type=F32))
    emb = jnp.dot(pe_ref[...].astype(BF16), wple_ref[...], preferred_element_type=F32)
    o_ref[...] = x2 + gate * emb


def _combine(pos_flat, ys, gates, x1, pe, g, b, wple_bf16, wpg_bf16, tm):
    t = x1.shape[0]
    row = lambda i, *_: (i, 0)
    fixed = lambda i, *_: (0, 0)
    return pl.pallas_call(
        functools.partial(_combine_body, tm=tm),
        grid_spec=pltpu.PrefetchScalarGridSpec(
            num_scalar_prefetch=1,
            grid=(t // tm,),
            in_specs=[pl.BlockSpec(memory_space=pl.ANY),
                      pl.BlockSpec((tm, LANES), row),
                      pl.BlockSpec((tm, D_MODEL), row),
                      pl.BlockSpec((tm, PLE_DIM), row),
                      pl.BlockSpec((1, D_MODEL), fixed),
                      pl.BlockSpec((1, D_MODEL), fixed),
                      pl.BlockSpec((PLE_DIM, D_MODEL), fixed),
                      pl.BlockSpec((D_MODEL, D_MODEL), fixed)],
            out_specs=pl.BlockSpec((tm, D_MODEL), row),
            scratch_shapes=[pltpu.VMEM((2, TOP_K, tm // ROW_DMA_UNROLL, ROW_DMA_UNROLL, D_MODEL), F32),
                            pltpu.SemaphoreType.DMA((2,))]),
        out_shape=jax.ShapeDtypeStruct((t, D_MODEL), F32),
        compiler_params=_cparams(("arbitrary",)),
        name="moe_combine",
    )(pos_flat, ys, gates, x1, pe, g, b, wple_bf16, wpg_bf16)


def kernel(x_prompt, x_sample, cache_k, cache_v, state_shift, state_wkv, p_prompt, p_sample,
           ln_emb_g, ln_emb_b, w_in, attn_sinks, rwkv_mu, rwkv_w0, rwkv_w_lora_up, rwkv_a0,
           rwkv_a_lora_up, rwkv_g_lora_up, rwkv_k_k, rwkv_k_a, rwkv_r_k, rwkv_gn_g, rwkv_gn_b,
           w_out, ln1_g, ln1_b, w_router, b_router, w_gate_up, b_gate_up, w_down, b_down,
           ln2_g, ln2_b, w_ple, w_ple_gate):
    assert w_in.shape[0] == DEPTH == 1
    nbp, seq, _ = x_prompt.shape
    nbs, dec_seq, _ = x_sample.shape
    assert dec_seq == 1
    tp, ts = nbp * seq, nbs
    w_buf = cache_k.shape[2]
    assert w_buf <= WINDOW
    assert seq % max(BLOCK, RWKV_TILE, INPROJ_TILE, ROUTER_TILE, TOKEN_TILE) == 0
    rowv = lambda z: z.reshape(1, -1)

    w_in_b = w_in[0].astype(BF16)
    w_out_b = w_out[0].astype(BF16)
    w_ple_b = w_ple[0].astype(BF16)
    w_pg_b = w_ple_gate[0].astype(BF16)
    sinks = attn_sinks[0]
    prm = dict(mu=rowv(rwkv_mu[0]), w0=rowv(rwkv_w0[0]), wlu=rwkv_w_lora_up[0],
               a0=rowv(rwkv_a0[0]), alu=rwkv_a_lora_up[0], glu=rwkv_g_lora_up[0],
               k_k=rowv(rwkv_k_k[0]), k_a=rowv(rwkv_k_a[0]), r_k=rowv(rwkv_r_k[0]),
               gn_g=rowv(rwkv_gn_g[0]), gn_b=rowv(rwkv_gn_b[0]))
    ge, be = rowv(ln_emb_g), rowv(ln_emb_b)

    xn_p, q_p, kv_p, rp_p = _inproj(x_prompt.reshape(tp, D_MODEL), ge, be, w_in_b, INPROJ_TILE)
    att_p = _swa_prompt(sinks, q_p, kv_p, nbp, seq)
    rw_p, wkv_p = _rwkv_prompt(rp_p, prm, nbp, seq, RWKV_TILE)

    xn_s, q_s, kv_s, rp_s = _inproj(x_sample.reshape(ts, D_MODEL), ge, be, w_in_b, ts)
    ck3 = cache_k[0].reshape(nbs, w_buf, KV_COLS)
    cv3 = cache_v[0].reshape(nbs, w_buf, KV_COLS)
    att_s = _swa_sample(sinks, q_s.reshape(nbs, ATT_HEADS, HEAD_DIM),
                        kv_s.reshape(nbs, 2 * KV_HEADS, HEAD_DIM), ck3, cv3)
    att_s = att_s.reshape(ts, ATT_WIDTH)
    p7 = _rwkv_sample_prep(rp_s, state_shift[0], prm)
    hv = lambda z: z.reshape(RWKV_HEADS, HEAD_DIM)
    wkv_s, rw_s = _rwkv_sample_step(p7.reshape(7, nbs, RWKV_HEADS, HEAD_DIM), state_wkv[0],
                                    hv(rwkv_r_k[0]), hv(rwkv_gn_g[0]), hv(rwkv_gn_b[0]))
    rw_s = rw_s.reshape(ts, RWKV_WIDTH)

    wr_pad = jnp.zeros((D_MODEL, LANES), F32).at[:, :N_EXPERTS].set(w_router[0])
    wr_hi = wr_pad.astype(BF16)
    wr_lo = (wr_pad - wr_hi.astype(F32)).astype(BF16)
    br_pad = jnp.full((1, LANES), -jnp.inf, F32).at[0, :N_EXPERTS].set(b_router[0])
    g1, b1 = rowv(ln1_g[0]), rowv(ln1_b[0])
    tmp, tmd = ROUTER_TILE, TOKEN_TILE
    x1_p, info_p, gate_p, cnt_p = _mix_router(att_p, rw_p, xn_p, w_out_b, g1, b1, wr_hi, wr_lo,
                                              br_pad, jnp.zeros((1, LANES), F32), tmp)
    x1_s, info_s, gate_s, cnt = _mix_router(att_s, rw_s, xn_s, w_out_b, g1, b1, wr_hi, wr_lo,
                                            br_pad, cnt_p, ts)

    counts = cnt[0, :N_EXPERTS].astype(jnp.int32)
    padded = ((counts + EXPERT_TILE - 1) // EXPERT_TILE) * EXPERT_TILE
    ends = jnp.cumsum(padded)
    offs = ends - padded
    n_slots = (tp + ts) * TOP_K + N_EXPERTS * EXPERT_TILE
    n_tiles = n_slots // EXPERT_TILE
    n_used = (ends[-1] // EXPERT_TILE).reshape(1).astype(jnp.int32)
    tile_start = jnp.arange(n_tiles, dtype=jnp.int32) * EXPERT_TILE
    tile_e = jnp.sum(ends[None, :] <= tile_start[:, None], axis=1).astype(jnp.int32)
    last_e = jnp.max(jnp.where(padded > 0, jnp.arange(N_EXPERTS), 0)).astype(jnp.int32)
    tile_e = jnp.minimum(tile_e, last_e)

    def slots(info):
        expert = jnp.arange(N_EXPERTS, dtype=jnp.int32)[:, None, None]
        first = jnp.sum(jnp.where(info[None, 0:TOP_K] == expert, offs[:, None, None], 0), axis=0)
        return (first + info[TOP_K:2 * TOP_K]).reshape(-1).astype(jnp.int32)

    pos_p, pos_s = slots(info_p), slots(info_s)

    fill_start = jnp.concatenate([jnp.clip(ends - EXPERT_TILE, 0, n_slots - EXPERT_TILE),
                                  n_used]).astype(jnp.int32)
    xs = _dispatch(pos_p, pos_s, fill_start, x1_p, x1_s, n_slots, tmd)
    bgu = b_gate_up[0]
    bg = bgu[:, 0::2].reshape(N_EXPERTS, 1, D_FF)
    bu = bgu[:, 1::2].reshape(N_EXPERTS, 1, D_FF)
    bdn = b_down[0].reshape(N_EXPERTS, 1, D_MODEL)
    ys = _experts(tile_e, n_used, xs, w_gate_up[0], w_down[0], bg, bu, bdn)

    g2, b2 = rowv(ln2_g[0]), rowv(ln2_b[0])
    y_p = _combine(pos_p, ys, gate_p, x1_p, p_prompt[0].reshape(tp, PLE_DIM), g2, b2,
                   w_ple_b, w_pg_b, tmd)
    y_s = _combine(pos_s, ys, gate_s, x1_s, p_sample[0].reshape(ts, PLE_DIM), g2, b2,
                   w_ple_b, w_pg_b, ts)

    w_keep = min(WINDOW, seq)
    kv_p3 = kv_p.reshape(nbp, seq, 2 * KV_COLS)[:, seq - w_keep:]
    k_win_p = kv_p3[:, :, 0:KV_COLS].reshape(1, nbp, w_keep, KV_HEADS, HEAD_DIM)
    v_win_p = kv_p3[:, :, KV_COLS:].reshape(1, nbp, w_keep, KV_HEADS, HEAD_DIM)
    shift_p = rp_p.reshape(nbp, seq, RWKV_PROJ)[:, seq - 1][None]
    k_new = kv_s[:, 0:KV_COLS].reshape(nbs, 1, KV_HEADS, HEAD_DIM)
    v_new = kv_s[:, KV_COLS:].reshape(nbs, 1, KV_HEADS, HEAD_DIM)
    k_win_s = jnp.concatenate([cache_k[0], k_new], axis=1)[:, 1:][None]
    v_win_s = jnp.concatenate([cache_v[0], v_new], axis=1)[:, 1:][None]
    return (y_p.reshape(nbp, seq, D_MODEL), y_s.reshape(nbs, 1, D_MODEL),
            k_win_p, v_win_p, shift_p, wkv_p[None],
            k_win_s, v_win_s, rp_s[None], wkv_s[None])
```

```python
import functools

import jax
import jax.numpy as jnp
from jax import lax
from jax.experimental import pallas as pl
from jax.experimental.pallas import tpu as pltpu

F32 = jnp.float32
BF16 = jnp.bfloat16

D_MODEL = 1024
HEAD_DIM = 64
ATT_HEADS = 8
KV_HEADS = 2
GROUP = ATT_HEADS // KV_HEADS
ATT_WIDTH = ATT_HEADS * HEAD_DIM
KV_COLS = KV_HEADS * HEAD_DIM
RWKV_HEADS = 8
RWKV_WIDTH = RWKV_HEADS * HEAD_DIM
LORA_W = 64
LORA_A = 64
LORA_G = 128
RWKV_PROJ = 3 * RWKV_WIDTH + LORA_W + LORA_A + LORA_G
WINDOW = 128
BLOCK = 128
PLE_DIM = 256
N_EXPERTS = 32
TOP_K = 4
D_FF = 1024
SWIGLU_LIMIT = 7.0
SWIGLU_ALPHA = 1.702
LN_EPS = 1e-5
GN_EPS = 64e-5
DEPTH = 1
DEEPNORM_ALPHA = (2 * DEPTH) ** 0.25

LANES = 128
CHUNK = 64
HEAD_QUAD = 4
EXPERT_TILE = 512
SWA_BLOCKS = 2
INPROJ_TILE = 512
RWKV_TILE = 256
ROUTER_TILE = 512
TOKEN_TILE = 256
ROW_DMA_UNROLL = 8
VMEM_LIMIT = 48 * 1024 * 1024
EXPERTS_VMEM_LIMIT = 58 * 1024 * 1024


def _cparams(sem):
    return pltpu.CompilerParams(dimension_semantics=sem, vmem_limit_bytes=VMEM_LIMIT)


def _bdot(a, b):
    return jnp.dot(a.astype(BF16), b.astype(BF16), preferred_element_type=F32)


def _bdot_nt(a, b):
    return lax.dot_general(a.astype(BF16), b.astype(BF16), (((1,), (1,)), ((), ())),
                           preferred_element_type=F32)


def _bdot_tn(a, b):
    return lax.dot_general(a.astype(BF16), b.astype(BF16), (((0,), (0,)), ((), ())),
                           preferred_element_type=F32)


def _split_dot(m_bf16, x, passes):
    acc = None
    rem = x
    for _ in range(passes):
        hi = rem.astype(BF16)
        part = jnp.dot(m_bf16, hi, preferred_element_type=F32)
        acc = part if acc is None else acc + part
        rem = rem - hi.astype(F32)
    return acc


def _head_sum(x, bd_bf16):
    width = bd_bf16.shape[0]
    outs = []
    for c in range(x.shape[1] // width):
        acc = None
        rem = x[:, c * width:(c + 1) * width]
        for _ in range(2):
            hi = rem.astype(BF16)
            part = jnp.dot(hi, bd_bf16, preferred_element_type=F32)
            acc = part if acc is None else acc + part
            rem = rem - hi.astype(F32)
        outs.append(acc)
    return jnp.concatenate(outs, axis=1)


def _sigmoid(x):
    return 1.0 / (1.0 + jnp.exp(-x))


def _layer_norm(x, g, b):
    mu = jnp.mean(x, axis=-1, keepdims=True)
    xc = x - mu
    var = jnp.mean(xc * xc, axis=-1, keepdims=True)
    return xc * lax.rsqrt(var + LN_EPS) * g + b


def _inproj_body(x_ref, g_ref, b_ref, w_ref, xn_ref, q_ref, kv_ref, rp_ref):
    xn = _layer_norm(x_ref[...], g_ref[...], b_ref[...])
    xn_ref[...] = xn
    xb = xn.astype(BF16)
    q_ref[...] = jnp.dot(xb, w_ref[:, 0:ATT_WIDTH], preferred_element_type=F32)
    kv_ref[...] = jnp.dot(xb, w_ref[:, ATT_WIDTH:ATT_WIDTH + 2 * KV_COLS],
                          preferred_element_type=F32)
    rp_ref[...] = jnp.dot(xb, w_ref[:, ATT_WIDTH + 2 * KV_COLS:], preferred_element_type=F32)


def _inproj(x2d, g, b, w_bf16, tm):
    t = x2d.shape[0]
    in_proj = w_bf16.shape[1]
    row = lambda i: (i, 0)
    fixed = lambda i: (0, 0)
    return pl.pallas_call(
        _inproj_body,
        grid=(t // tm,),
        in_specs=[pl.BlockSpec((tm, D_MODEL), row),
                  pl.BlockSpec((1, D_MODEL), fixed),
                  pl.BlockSpec((1, D_MODEL), fixed),
                  pl.BlockSpec((D_MODEL, in_proj), fixed)],
        out_specs=[pl.BlockSpec((tm, D_MODEL), row),
                   pl.BlockSpec((tm, ATT_WIDTH), row),
                   pl.BlockSpec((tm, 2 * KV_COLS), row),
                   pl.BlockSpec((tm, RWKV_PROJ), row)],
        out_shape=[jax.ShapeDtypeStruct((t, D_MODEL), F32),
                   jax.ShapeDtypeStruct((t, ATT_WIDTH), F32),
                   jax.ShapeDtypeStruct((t, 2 * KV_COLS), F32),
                   jax.ShapeDtypeStruct((t, RWKV_PROJ), F32)],
        compiler_params=_cparams(("parallel",)),
        name="inproj",
    )(x2d, g, b, w_bf16)


def _alibi_slope(h):
    return 2.0 ** (-8.0 * (h + 1) / ATT_HEADS)


def _swa_prompt_body(sink_ref, q_ref, kvc_ref, kvp_ref, o_ref):
    n = pl.program_id(1)
    q = (q_ref[...] * (HEAD_DIM ** -0.5)).astype(BF16)
    kvc = kvc_ref[...].astype(BF16)
    kvp = kvp_ref[...].astype(BF16)
    row = lax.broadcasted_iota(jnp.int32, (BLOCK, 2 * BLOCK), 0)
    col = lax.broadcasted_iota(jnp.int32, (BLOCK, 2 * BLOCK), 1)
    dist = row + BLOCK - col
    in_band = (dist >= 0) & (dist <= WINDOW)
    distf = dist.astype(F32)
    units = [(b, h) for b in range(SWA_BLOCKS) for h in range(ATT_HEADS)]
    rows = lambda b: slice(b * BLOCK, (b + 1) * BLOCK)
    valid, kbands, vbands = [], [], []
    for b in range(SWA_BLOCKS):
        prev = kvp if b == 0 else kvc[rows(b - 1)]
        valid.append(in_band & ((col >= BLOCK) | (n > 0)) if b == 0 else in_band)
        kb, vb = [], []
        for g in range(KV_HEADS):
            ks = slice(g * HEAD_DIM, (g + 1) * HEAD_DIM)
            vs = slice(KV_COLS + g * HEAD_DIM, KV_COLS + (g + 1) * HEAD_DIM)
            kb.append(jnp.concatenate([prev[:, ks], kvc[rows(b), ks]], axis=0))
            vb.append(jnp.concatenate([prev[:, vs], kvc[rows(b), vs]], axis=0))
        kbands.append(kb)
        vbands.append(vb)
    scores = {(b, h): lax.dot_general(q[rows(b), h * HEAD_DIM:(h + 1) * HEAD_DIM],
                                      kbands[b][h // GROUP], (((1,), (1,)), ((), ())),
                                      preferred_element_type=F32) for b, h in units}
    probs = {}
    for b, h in units:
        s = jnp.where(valid[b], scores[b, h] - _alibi_slope(h) * distf, -jnp.inf)
        sink = sink_ref[h]
        m = jnp.maximum(jnp.max(s, axis=1, keepdims=True), sink)
        e = jnp.exp(s - m)
        den = jnp.sum(e, axis=1, keepdims=True) + jnp.exp(sink - m)
        probs[b, h] = (e / den).astype(BF16)
    outs = {(b, h): jnp.dot(probs[b, h], vbands[b][h // GROUP], preferred_element_type=F32)
            for b, h in units}
    for b in range(SWA_BLOCKS):
        o_ref[rows(b), :] = jnp.concatenate([outs[b, h] for h in range(ATT_HEADS)], axis=1)


def _swa_prompt(sinks, q, kv, nbatch, seq):
    nb = seq // (SWA_BLOCKS * BLOCK)
    cur = lambda b, n: (b * nb + n, 0)
    prv = lambda b, n: (SWA_BLOCKS * (b * nb + n) - jnp.minimum(n, 1), 0)
    return pl.pallas_call(
        _swa_prompt_body,
        grid=(nbatch, nb),
        in_specs=[pl.BlockSpec(memory_space=pltpu.SMEM),
                  pl.BlockSpec((SWA_BLOCKS * BLOCK, ATT_WIDTH), cur),
                  pl.BlockSpec((SWA_BLOCKS * BLOCK, 2 * KV_COLS), cur),
                  pl.BlockSpec((BLOCK, 2 * KV_COLS), prv)],
        out_specs=pl.BlockSpec((SWA_BLOCKS * BLOCK, ATT_WIDTH), cur),
        out_shape=jax.ShapeDtypeStruct((nbatch * seq, ATT_WIDTH), F32),
        compiler_params=_cparams(("parallel", "parallel")),
        name="swa_prompt",
    )(sinks, q, kv, kv)


def _swa_sample_body(sink_ref, q_ref, kvn_ref, ck_ref, cv_ref, o_ref, *, bb, w_buf):
    hrow = lax.broadcasted_iota(jnp.int32, (ATT_HEADS, 1), 0)
    slope = jnp.zeros((ATT_HEADS, 1), F32)
    sink = jnp.zeros((ATT_HEADS, 1), F32)
    for h in range(ATT_HEADS):
        slope = jnp.where(hrow == h, _alibi_slope(h), slope)
        sink = jnp.where(hrow == h, sink_ref[h], sink)
    jcol = lax.broadcasted_iota(jnp.int32, (ATT_HEADS, w_buf), 1)
    bias = -slope * (w_buf - jcol).astype(F32)
    lower = hrow < GROUP
    q2b = [(q_ref[b] * (HEAD_DIM ** -0.5)).astype(BF16) for b in range(bb)]
    s01 = []
    for b in range(bb):
        ck = ck_ref[b].astype(BF16)
        s01.append((_bdot_nt(q2b[b], ck[:, 0:HEAD_DIM]),
                    _bdot_nt(q2b[b], ck[:, HEAD_DIM:2 * HEAD_DIM])))
    ps = []
    for b in range(bb):
        kvn = kvn_ref[b]
        s = jnp.where(lower, s01[b][0], s01[b][1]) + bias
        knew = jnp.where(lower, kvn[0:1, :], kvn[1:2, :])
        vnew = jnp.where(lower, kvn[2:3, :], kvn[3:4, :])
        snew = jnp.sum(q2b[b].astype(F32) * knew.astype(BF16).astype(F32), axis=1, keepdims=True)
        m = jnp.maximum(jnp.maximum(jnp.max(s, axis=1, keepdims=True), snew), sink)
        e = jnp.exp(s - m)
        enew = jnp.exp(snew - m)
        den = jnp.sum(e, axis=1, keepdims=True) + enew + jnp.exp(sink - m)
        pnew = (enew / den).astype(BF16).astype(F32)
        ps.append(((e / den).astype(BF16), pnew * vnew.astype(BF16).astype(F32)))
    for b in range(bb):
        cv = cv_ref[b].astype(BF16)
        o0 = jnp.dot(ps[b][0], cv[:, 0:HEAD_DIM], preferred_element_type=F32)
        o1 = jnp.dot(ps[b][0], cv[:, HEAD_DIM:2 * HEAD_DIM], preferred_element_type=F32)
        o_ref[b] = jnp.where(lower, o0, o1) + ps[b][1]


def _swa_sample(sinks, q3, kvn3, ck3, cv3, bb=8):
    nb, w_buf = ck3.shape[0], ck3.shape[1]
    blk = lambda i: (i, 0, 0)
    return pl.pallas_call(
        functools.partial(_swa_sample_body, bb=bb, w_buf=w_buf),
        grid=(nb // bb,),
        in_specs=[pl.BlockSpec(memory_space=pltpu.SMEM),
                  pl.BlockSpec((bb, ATT_HEADS, HEAD_DIM), blk),
                  pl.BlockSpec((bb, 2 * KV_HEADS, HEAD_DIM), blk),
                  pl.BlockSpec((bb, w_buf, KV_COLS), blk),
                  pl.BlockSpec((bb, w_buf, KV_COLS), blk)],
        out_specs=pl.BlockSpec((bb, ATT_HEADS, HEAD_DIM), blk),
        out_shape=jax.ShapeDtypeStruct((nb, ATT_HEADS, HEAD_DIM), F32),
        compiler_params=_cparams(("parallel",)),
        name="swa_sample",
    )(sinks, q3, kvn3, ck3, cv3)


def _rwkv_prep(rp, prev, mu, w0, wlu, a0, alu, glu, k_k, k_a, bd):
    xs = rp + mu * (prev - rp)
    r = xs[:, 0:RWKV_WIDTH]
    k = xs[:, RWKV_WIDTH:2 * RWKV_WIDTH]
    v = xs[:, 2 * RWKV_WIDTH:3 * RWKV_WIDTH]
    o = 3 * RWKV_WIDTH
    wd = xs[:, o:o + LORA_W]
    ad = xs[:, o + LORA_W:o + LORA_W + LORA_A]
    gd = xs[:, o + LORA_W + LORA_A:]
    z = -(w0 + _bdot(jnp.tanh(wd), wlu))
    softplus = jnp.maximum(z, 0.0) + jnp.log(1.0 + jnp.exp(-jnp.abs(z)))
    log_decay = -jnp.exp(-softplus - 0.5)
    a = _sigmoid(a0 + _bdot(ad, alu))
    g = _bdot(_sigmoid(gd), glu)
    kkr = k * k_k
    kk = kkr / jnp.maximum(jnp.sqrt(_head_sum(kkr * kkr, bd)), 1e-12)
    k2 = k * (1.0 + (a - 1.0) * k_a)
    return r, log_decay, k2, v, kk, a, g


def _rwkv_finish(y, r, k2, v, g, r_k, gn_g, gn_b, bd):
    inv = 1.0 / HEAD_DIM
    mu = _head_sum(y, bd) * inv
    yc = y - mu
    var = _head_sum(yc * yc, bd) * inv
    yn = yc * lax.rsqrt(var + GN_EPS) * gn_g + gn_b
    bonus = _head_sum(r * k2 * r_k, bd) * v
    return (yn + bonus) * g


def _rwkv_prompt_body(rp_ref, mu_ref, w0_ref, wlu_ref, a0_ref, alu_ref, glu_ref, kk_ref, ka_ref,
                      rk_ref, gng_ref, gnb_ref, bd_ref, ltri_ref,
                      out_ref, sfin_ref, prev_scr, st_scr, y_scr, *, tt):
    c = pl.program_id(1)

    @pl.when(c == 0)
    def _():
        prev_scr[...] = jnp.zeros_like(prev_scr)
        st_scr[...] = jnp.zeros_like(st_scr)

    rp = rp_ref[...]
    rolled = pltpu.roll(rp, 1, 0)
    rowi = lax.broadcasted_iota(jnp.int32, (tt, 1), 0)
    prev = jnp.where(rowi == 0, prev_scr[...], rolled)
    prev_scr[...] = rp[tt - 1:tt, :]
    bd = bd_ref[...]
    r, ld, k2, v, kk, a, g = _rwkv_prep(rp, prev, mu_ref[...], w0_ref[...], wlu_ref[...],
                                        a0_ref[...], alu_ref[...], glu_ref[...], kk_ref[...],
                                        ka_ref[...], bd)
    cum = _split_dot(ltri_ref[...], ld, 2)
    ecum = jnp.exp(cum)
    einv = jnp.exp(-cum)
    at = -kk * jnp.exp(cum - ld)
    bh = kk * a * einv
    kh = k2 * einv
    rt = r * ecum

    qw = HEAD_QUAD * HEAD_DIM
    ii = lax.broadcasted_iota(jnp.int32, (CHUNK, qw), 0)
    jl = lax.broadcasted_iota(jnp.int32, (CHUNK, qw), 1) % CHUNK
    strict = ii > jl
    incl = ii >= jl
    eye = (ii == jl).astype(F32)
    same_head = (lax.broadcasted_iota(jnp.int32, (qw, qw), 0) // HEAD_DIM
                 == lax.broadcasted_iota(jnp.int32, (qw, qw), 1) // HEAD_DIM)

    def blockdiag(m):
        mb = m.astype(BF16)
        return jnp.where(same_head, jnp.concatenate([mb] * HEAD_QUAD, axis=0), jnp.zeros((), BF16))

    def diag_blocks(full):
        out = None
        for hh in range(HEAD_QUAD):
            rs = slice(hh * HEAD_DIM, (hh + 1) * HEAD_DIM)
            blk = jnp.where(same_head[rs], full[rs], 0.0)
            out = blk if out is None else out + blk
        return out

    mm = lambda x, w_bf16: jnp.dot(x.astype(BF16), w_bf16, preferred_element_type=F32)
    mm_nt = lambda x, w_bf16: lax.dot_general(x.astype(BF16), w_bf16, (((1,), (1,)), ((), ())),
                                              preferred_element_type=F32)
    nsub = tt // CHUNK
    units = [(s, q) for s in range(nsub) for q in range(RWKV_HEADS // HEAD_QUAD)]
    rows = lambda s: slice(s * CHUNK, (s + 1) * CHUNK)
    cols = lambda q: slice(q * qw, (q + 1) * qw)
    cut = lambda z, u: z[rows(u[0]), cols(u[1])]
    kka = kk * a
    p_last, kh_end, bh_end = [], [], []
    for s in range(nsub):
        c_last = cum[(s + 1) * CHUNK - 1:(s + 1) * CHUNK, :]
        p_last.append(jnp.exp(c_last))
        tail = jnp.exp(c_last - cum[rows(s)])
        kh_end.append(k2[rows(s)] * tail)
        bh_end.append(kka[rows(s)] * tail)
    x2 = {u: jnp.concatenate([cut(at, u), cut(rt, u)], axis=0) for u in units}
    xb = {u: mm_nt(x2[u], blockdiag(cut(bh, u))) for u in units}
    xk = {u: mm_nt(x2[u], blockdiag(cut(kh, u))) for u in units}
    nmat = {u: jnp.where(strict, xb[u][0:CHUNK], 0.0) for u in units}
    a_rb = {u: jnp.where(incl, xb[u][CHUNK:], 0.0) for u in units}
    a_ak = {u: jnp.where(strict, xk[u][0:CHUNK], 0.0) for u in units}
    a_rk = {u: jnp.where(incl, xk[u][CHUNK:], 0.0) for u in units}
    tmat = {u: eye + nmat[u] for u in units}
    npow = nmat
    for _ in range(5):
        npow = {u: mm(npow[u], blockdiag(npow[u])) for u in units}
        tmat = {u: tmat[u] + mm(npow[u], blockdiag(tmat[u])) for u in units}
    vbd = {u: blockdiag(cut(v, u)) for u in units}
    av = {u: mm(a_ak[u], vbd[u]) for u in units}
    aprime = {u: mm(tmat[u], blockdiag(cut(at, u))) for u in units}
    wmat = {u: mm(tmat[u], blockdiag(av[u])) for u in units}
    rprime = {u: cut(rt, u) + mm(a_rb[u], blockdiag(aprime[u])) for u in units}
    y0 = {u: mm(a_rk[u], vbd[u]) + mm(a_rb[u], blockdiag(wmat[u])) for u in units}
    hmat = {u: diag_blocks(_bdot_tn(jnp.concatenate([cut(v, u), wmat[u]], axis=0),
                                    jnp.concatenate([cut(kh_end[u[0]], (0, u[1])),
                                                     cut(bh_end[u[0]], (0, u[1]))], axis=0)))
            for u in units}
    g0 = {u: diag_blocks(_bdot_tn(aprime[u], cut(bh_end[u[0]], (0, u[1])))) for u in units}

    states = [st_scr[q] for q in range(RWKV_HEADS // HEAD_QUAD)]
    for s in range(nsub):
        for q in range(RWKV_HEADS // HEAD_QUAD):
            u = (s, q)
            st = states[q]
            y_scr[rows(s), cols(q)] = mm_nt(rprime[u], blockdiag(st)) + y0[u]
            states[q] = st * p_last[s][:, cols(q)] + mm(st, blockdiag(g0[u])) + hmat[u]
    for q in range(RWKV_HEADS // HEAD_QUAD):
        st_scr[q] = states[q]
        for hh in range(HEAD_QUAD):
            sfin_ref[0, q * HEAD_QUAD + hh] = states[q][:, hh * HEAD_DIM:(hh + 1) * HEAD_DIM]
    out_ref[...] = _rwkv_finish(y_scr[...], r, k2, v, g, rk_ref[...], gng_ref[...], gnb_ref[...], bd)


def _chunk_tril(tt):
    i = jnp.arange(tt)
    same = (i[:, None] // CHUNK) == (i[None, :] // CHUNK)
    return (same & (i[:, None] >= i[None, :])).astype(BF16)


def _head_blockdiag():
    i = jnp.arange(HEAD_QUAD * HEAD_DIM) // HEAD_DIM
    return (i[:, None] == i[None, :]).astype(BF16)


def _rwkv_prompt(rp, prm, nbatch, seq, tt):
    nc = seq // tt
    fixed = lambda b, c: (0, 0)
    row = lambda b, c: (b * nc + c, 0)
    vec = lambda n: pl.BlockSpec((1, n), fixed)
    return pl.pallas_call(
        functools.partial(_rwkv_prompt_body, tt=tt),
        grid=(nbatch, nc),
        in_specs=[pl.BlockSpec((tt, RWKV_PROJ), row),
                  vec(RWKV_PROJ), vec(RWKV_WIDTH),
                  pl.BlockSpec((LORA_W, RWKV_WIDTH), fixed), vec(RWKV_WIDTH),
                  pl.BlockSpec((LORA_A, RWKV_WIDTH), fixed),
                  pl.BlockSpec((LORA_G, RWKV_WIDTH), fixed),
                  vec(RWKV_WIDTH), vec(RWKV_WIDTH), vec(RWKV_WIDTH), vec(RWKV_WIDTH),
                  vec(RWKV_WIDTH),
                  pl.BlockSpec((HEAD_QUAD * HEAD_DIM, HEAD_QUAD * HEAD_DIM), fixed),
                  pl.BlockSpec((tt, tt), fixed)],
        out_specs=[pl.BlockSpec((tt, RWKV_WIDTH), row),
                   pl.BlockSpec((1, RWKV_HEADS, HEAD_DIM, HEAD_DIM), lambda b, c: (b, 0, 0, 0))],
        out_shape=[jax.ShapeDtypeStruct((nbatch * seq, RWKV_WIDTH), F32),
                   jax.ShapeDtypeStruct((nbatch, RWKV_HEADS, HEAD_DIM, HEAD_DIM), F32)],
        scratch_shapes=[pltpu.VMEM((1, RWKV_PROJ), F32),
                        pltpu.VMEM((RWKV_HEADS // HEAD_QUAD, HEAD_DIM, HEAD_QUAD * HEAD_DIM), F32),
                        pltpu.VMEM((tt, RWKV_WIDTH), F32)],
        compiler_params=_cparams(("parallel", "arbitrary")),
        name="rwkv_prompt",
    )(rp, prm["mu"], prm["w0"], prm["wlu"], prm["a0"], prm["alu"], prm["glu"], prm["k_k"],
      prm["k_a"], prm["r_k"], prm["gn_g"], prm["gn_b"], _head_blockdiag(), _chunk_tril(tt))


def _rwkv_sample_prep_body(rp_ref, prev_ref, mu_ref, w0_ref, wlu_ref, a0_ref, alu_ref, glu_ref,
                           kk_ref, ka_ref, bd_ref, o_ref):
    r, ld, k2, v, kk, a, g = _rwkv_prep(rp_ref[...], prev_ref[...], mu_ref[...], w0_ref[...],
                                        wlu_ref[...], a0_ref[...], alu_ref[...], glu_ref[...],
                                        kk_ref[...], ka_ref[...], bd_ref[...])
    for i, z in enumerate((r, jnp.exp(ld), k2, v, kk, a, g)):
        o_ref[i] = z


def _rwkv_sample_prep(rp, prev, prm):
    nb = rp.shape[0]
    return pl.pallas_call(
        _rwkv_sample_prep_body,
        out_shape=jax.ShapeDtypeStruct((7, nb, RWKV_WIDTH), F32),
        compiler_params=pltpu.CompilerParams(vmem_limit_bytes=VMEM_LIMIT),
        name="rwkv_sample_prep",
    )(rp, prev, prm["mu"], prm["w0"], prm["wlu"], prm["a0"], prm["alu"], prm["glu"], prm["k_k"],
      prm["k_a"], _head_blockdiag())


def _rwkv_sample_step_body(p_ref, s_ref, rk_ref, gng_ref, gnb_ref, so_ref, o_ref, *, bb):
    ii = lax.broadcasted_iota(jnp.int32, (HEAD_DIM, HEAD_DIM), 0)
    jj = lax.broadcasted_iota(jnp.int32, (HEAD_DIM, HEAD_DIM), 1)
    eye = ii == jj
    for b in range(bb):
        for h in range(RWKV_HEADS):
            row = lambda i: p_ref[i, b, h:h + 1, :]
            r_r, w_r, k_r, v_r, kk_r, a_r = (row(i) for i in range(6))
            st = s_ref[b, h]
            sa = jnp.sum(st * kk_r, axis=1, keepdims=True)
            v_col = jnp.sum(jnp.where(eye, v_r, 0.0), axis=1, keepdims=True)
            st = st * w_r - sa * (kk_r * a_r) + v_col * k_r
            so_ref[b, h] = st
            y_col = jnp.sum(st * r_r, axis=1, keepdims=True)
            o_ref[b, h:h + 1, :] = jnp.sum(jnp.where(eye, y_col, 0.0), axis=0, keepdims=True)
    y = o_ref[...]
    r, k2, v, g = p_ref[0], p_ref[2], p_ref[3], p_ref[6]
    mu = jnp.mean(y, axis=-1, keepdims=True)
    yc = y - mu
    var = jnp.mean(yc * yc, axis=-1, keepdims=True)
    yn = yc * lax.rsqrt(var + GN_EPS) * gng_ref[...] + gnb_ref[...]
    bonus = jnp.sum(r * k2 * rk_ref[...], axis=-1, keepdims=True) * v
    o_ref[...] = (yn + bonus) * g


def _rwkv_sample_step(p4, state, r_k, gn_g, gn_b, bb=8):
    nb = state.shape[0]
    hv = pl.BlockSpec((RWKV_HEADS, HEAD_DIM), lambda i: (0, 0))
    return pl.pallas_call(
        functools.partial(_rwkv_sample_step_body, bb=bb),
        grid=(nb // bb,),
        in_specs=[pl.BlockSpec((7, bb, RWKV_HEADS, HEAD_DIM), lambda i: (0, i, 0, 0)),
                  pl.BlockSpec((bb, RWKV_HEADS, HEAD_DIM, HEAD_DIM), lambda i: (i, 0, 0, 0)),
                  hv, hv, hv],
        out_specs=[pl.BlockSpec((bb, RWKV_HEADS, HEAD_DIM, HEAD_DIM), lambda i: (i, 0, 0, 0)),
                   pl.BlockSpec((bb, RWKV_HEADS, HEAD_DIM), lambda i: (i, 0, 0))],
        out_shape=[jax.ShapeDtypeStruct(state.shape, F32),
                   jax.ShapeDtypeStruct((nb, RWKV_HEADS, HEAD_DIM), F32)],
        compiler_params=_cparams(("parallel",)),
        name="rwkv_sample_step",
    )(p4, state, r_k, gn_g, gn_b)


def _mix_router_body(att_ref, rw_ref, xn_ref, wo_ref, g_ref, b_ref, wrh_ref, wrl_ref, br_ref,
                     base_ref, x1_ref, info_ref, gate_ref, cnt_ref, base_scr, *, tm):
    i = pl.program_id(0)

    @pl.when(i == 0)
    def _():
        base_scr[...] = base_ref[...]

    mixed = (jnp.dot(att_ref[...].astype(BF16), wo_ref[0:ATT_WIDTH, :], preferred_element_type=F32)
             + jnp.dot(rw_ref[...].astype(BF16), wo_ref[ATT_WIDTH:, :], preferred_element_type=F32))
    x1 = _layer_norm(DEEPNORM_ALPHA * xn_ref[...] + mixed, g_ref[...], b_ref[...])
    x1_ref[...] = x1

    x1h = x1.astype(BF16)
    x1l = (x1 - x1h.astype(F32)).astype(BF16)
    hi_both = jnp.dot(x1h, jnp.concatenate([wrh_ref[...], wrl_ref[...]], axis=1),
                      preferred_element_type=F32)
    logits = (hi_both[:, 0:LANES] + jnp.dot(x1l, wrh_ref[...], preferred_element_type=F32)
              + hi_both[:, LANES:]) + br_ref[...]
    lane = lax.broadcasted_iota(jnp.int32, (tm, LANES), 1)
    lanef = lane.astype(F32)
    vals, idxs, hots = [], [], []
    cur = logits
    for _ in range(TOP_K):
        m = jnp.max(cur, axis=1, keepdims=True)
        idx = jnp.min(jnp.where(cur == m, lanef, float(LANES)), axis=1, keepdims=True)
        hot = lanef == idx
        cur = jnp.where(hot, -jnp.inf, cur)
        vals.append(m)
        idxs.append(idx)
        hots.append(hot)
    es = [jnp.exp(vk - vals[0]) for vk in vals]
    den = es[0] + es[1] + es[2] + es[3]
    multi = jnp.zeros((tm, LANES), F32)
    for hot in hots:
        multi = multi + hot.astype(F32)
    ti = lax.broadcasted_iota(jnp.int32, (tm, tm), 0)
    tj = lax.broadcasted_iota(jnp.int32, (tm, tm), 1)
    before = jnp.dot((ti > tj).astype(BF16), multi.astype(BF16), preferred_element_type=F32)
    before = before + base_scr[...]
    info = jnp.zeros((tm, LANES), F32)
    gates = jnp.zeros((tm, LANES), F32)
    for k in range(TOP_K):
        rank = jnp.sum(jnp.where(hots[k], before, 0.0), axis=1, keepdims=True)
        info = jnp.where(lane == k, idxs[k], info)
        info = jnp.where(lane == TOP_K + k, rank, info)
        gates = jnp.where(lane == k, es[k] / den, gates)
    info_ref[...] = jnp.transpose(info)[0:2 * TOP_K, :].astype(jnp.int32)
    gate_ref[...] = gates
    base_scr[...] = base_scr[...] + jnp.sum(multi, axis=0, keepdims=True)
    cnt_ref[...] = base_scr[...]


def _mix_router(att, rw, xn, wo_bf16, g, b, wr_hi, wr_lo, br_pad, base, tm):
    t = att.shape[0]
    row = lambda i: (i, 0)
    fixed = lambda i: (0, 0)
    return pl.pallas_call(
        functools.partial(_mix_router_body, tm=tm),
        grid=(t // tm,),
        in_specs=[pl.BlockSpec((tm, ATT_WIDTH), row),
                  pl.BlockSpec((tm, RWKV_WIDTH), row),
                  pl.BlockSpec((tm, D_MODEL), row),
                  pl.BlockSpec((D_MODEL, D_MODEL), fixed),
                  pl.BlockSpec((1, D_MODEL), fixed),
                  pl.BlockSpec((1, D_MODEL), fixed),
                  pl.BlockSpec((D_MODEL, LANES), fixed),
                  pl.BlockSpec((D_MODEL, LANES), fixed),
                  pl.BlockSpec((1, LANES), fixed),
                  pl.BlockSpec((1, LANES), fixed)],
        out_specs=[pl.BlockSpec((tm, D_MODEL), row),
                   pl.BlockSpec((2 * TOP_K, tm), lambda i: (0, i)),
                   pl.BlockSpec((tm, LANES), row),
                   pl.BlockSpec((1, LANES), fixed)],
        out_shape=[jax.ShapeDtypeStruct((t, D_MODEL), F32),
                   jax.ShapeDtypeStruct((2 * TOP_K, t), jnp.int32),
                   jax.ShapeDtypeStruct((t, LANES), F32),
                   jax.ShapeDtypeStruct((1, LANES), F32)],
        scratch_shapes=[pltpu.VMEM((1, LANES), F32)],
        compiler_params=_cparams(("arbitrary",)),
        name="mix_router",
    )(att, rw, xn, wo_bf16, g, b, wr_hi, wr_lo, br_pad, base)


def _row_copies(pos_ref, base, tm, make_copy):
    tokens = pos_ref.shape[0] // TOP_K

    def group(gi, carry):
        r0 = pl.multiple_of(gi * ROW_DMA_UNROLL, ROW_DMA_UNROLL)
        for j in range(ROW_DMA_UNROLL):
            for k in range(TOP_K):
                make_copy(gi, j, k, pos_ref[k * tokens + base + r0 + j]).start(priority=k % 2)
        return carry

    lax.fori_loop(0, tm // ROW_DMA_UNROLL, group, 0)


def _dispatch_rows(pos_ref, base, x_ref, xs_ref, sem, rows):
    def make_copy(gi, j, k, p):
        r = pl.multiple_of(gi * ROW_DMA_UNROLL, ROW_DMA_UNROLL) + j
        return pltpu.make_async_copy(x_ref.at[pl.ds(r, 1)], xs_ref.at[pl.ds(p, 1)], sem)

    _row_copies(pos_ref, base, rows, make_copy)
    for _ in range(TOP_K):
        pltpu.make_async_copy(x_ref, xs_ref.at[pl.ds(0, rows)], sem).wait()


def _dispatch_body(pos_a_ref, pos_b_ref, fill_ref, xa_ref, xb_ref, xs_ref, zero_scr, sem, fill_sem,
                   *, tm, n_tiles):
    i = pl.program_id(0)
    last = pl.num_programs(0) - 1

    @pl.when(i == 0)
    def _():
        zero_scr[...] = jnp.zeros_like(zero_scr)

        def tile_fill(start):
            return pltpu.make_async_copy(
                zero_scr, xs_ref.at[pl.ds(pl.multiple_of(start, EXPERT_TILE), EXPERT_TILE)], fill_sem)

        fills = [tile_fill(fill_ref[e]) for e in range(N_EXPERTS)]
        for cp in fills:
            cp.start()
        first_unused = fill_ref[N_EXPERTS]
        lax.fori_loop(first_unused, n_tiles,
                      lambda t, c: (tile_fill(t * EXPERT_TILE).start(), c)[1], 0)
        for cp in fills:
            cp.wait()
        lax.fori_loop(first_unused, n_tiles,
                      lambda t, c: (tile_fill(t * EXPERT_TILE).wait(), c)[1], 0)

    @pl.when(i < last)
    def _():
        _dispatch_rows(pos_a_ref, i * tm, xa_ref, xs_ref, sem, tm)

    @pl.when(i == last)
    def _():
        _dispatch_rows(pos_b_ref, 0, xb_ref, xs_ref, sem, xb_ref.shape[0])


def _dispatch(pos_a, pos_b, fill_start, x_a, x_b, n_slots, tm):
    nta = x_a.shape[0] // tm
    return pl.pallas_call(
        functools.partial(_dispatch_body, tm=tm, n_tiles=n_slots // EXPERT_TILE),
        grid_spec=pltpu.PrefetchScalarGridSpec(
            num_scalar_prefetch=3,
            grid=(nta + 1,),
            in_specs=[pl.BlockSpec((tm, D_MODEL), lambda i, *_: (jnp.minimum(i, nta - 1), 0)),
                      pl.BlockSpec(x_b.shape, lambda i, *_: (0, 0))],
            out_specs=pl.BlockSpec(memory_space=pl.ANY),
            scratch_shapes=[pltpu.VMEM((EXPERT_TILE, D_MODEL), F32),
                            pltpu.SemaphoreType.DMA, pltpu.SemaphoreType.DMA]),
        out_shape=jax.ShapeDtypeStruct((n_slots, D_MODEL), F32),
        compiler_params=_cparams(("arbitrary",)),
        name="moe_dispatch",
    )(pos_a, pos_b, fill_start, x_a, x_b)


def _experts_body(te_ref, nu_ref, xs_ref, wgu_ref, wd_ref, bg_ref, bu_ref, bd_ref, sel_ref, ys_ref,
                  wg_scr, wu_scr, wd_scr):
    i = pl.program_id(0)
    new_expert = (i == 0) | (te_ref[i] != te_ref[jnp.maximum(i - 1, 0)])

    @pl.when(new_expert)
    def _():
        wd_scr[...] = wd_ref[...].astype(BF16)
        for m in range(D_FF // LANES):
            pair = wgu_ref[:, 2 * m * LANES:2 * (m + 1) * LANES].astype(BF16)
            split = jnp.dot(pair, sel_ref[...], preferred_element_type=F32)
            wg_scr[:, m * LANES:(m + 1) * LANES] = split[:, 0:LANES].astype(BF16)
            wu_scr[:, m * LANES:(m + 1) * LANES] = split[:, LANES:].astype(BF16)

    @pl.when(i < nu_ref[0])
    def _():
        x = xs_ref[...].astype(BF16)
        gate = jnp.dot(x, wg_scr[...], preferred_element_type=F32) + bg_ref[...]
        up = jnp.dot(x, wu_scr[...], preferred_element_type=F32) + bu_ref[...]
        gate = jnp.minimum(gate, SWIGLU_LIMIT)
        up = jnp.clip(up, -SWIGLU_LIMIT, SWIGLU_LIMIT)
        act = (up + 1.0) * gate * _sigmoid(SWIGLU_ALPHA * gate)
        ys_ref[...] = jnp.dot(act.astype(BF16), wd_scr[...], preferred_element_type=F32) + bd_ref[...]

    @pl.when(i >= nu_ref[0])
    def _():
        ys_ref[...] = jnp.zeros_like(ys_ref)


def _gate_up_selector():
    i = jnp.arange(2 * LANES)
    src = jnp.where(i < LANES, 2 * i, 2 * (i - LANES) + 1)
    return (i[:, None] == src[None, :]).astype(BF16)


def _experts(tile_e, n_used, xs, wgu, wd, bg, bu, bd):
    ns = xs.shape[0]
    tm = EXPERT_TILE
    wspec = lambda a, b: pl.BlockSpec((None, a, b), lambda i, te, nu: (te[i], 0, 0))
    return pl.pallas_call(
        _experts_body,
        grid_spec=pltpu.PrefetchScalarGridSpec(
            num_scalar_prefetch=2,
            grid=(ns // tm,),
            in_specs=[pl.BlockSpec((tm, D_MODEL), lambda i, te, nu: (jnp.minimum(i, nu[0] - 1), 0)),
                      wspec(D_MODEL, 2 * D_FF), wspec(D_FF, D_MODEL),
                      wspec(1, D_FF), wspec(1, D_FF), wspec(1, D_MODEL),
                      pl.BlockSpec((2 * LANES, 2 * LANES), lambda i, te, nu: (0, 0))],
            out_specs=pl.BlockSpec((tm, D_MODEL), lambda i, te, nu: (i, 0)),
            scratch_shapes=[pltpu.VMEM((D_MODEL, D_FF), BF16), pltpu.VMEM((D_MODEL, D_FF), BF16),
                            pltpu.VMEM((D_FF, D_MODEL), BF16)]),
        out_shape=jax.ShapeDtypeStruct((ns, D_MODEL), F32),
        compiler_params=pltpu.CompilerParams(dimension_semantics=("arbitrary",),
                                             vmem_limit_bytes=EXPERTS_VMEM_LIMIT),
        name="moe_experts",
    )(tile_e, n_used, xs, wgu, wd, bg, bu, bd, _gate_up_selector())


def _combine_body(pos_ref, ys_ref, gate_ref, x1_ref, pe_ref, g_ref, b_ref, wple_ref, wpg_ref,
                  o_ref, buf, sem, *, tm):
    i = pl.program_id(0)
    cur = i % 2

    def gather(tile, slot):
        def make_copy(gi, j, k, p):
            return pltpu.make_async_copy(ys_ref.at[pl.ds(p, 1)], buf.at[slot, k, gi, pl.ds(j, 1)],
                                         sem.at[slot])
        _row_copies(pos_ref, tile * tm, tm, make_copy)

    @pl.when(i == 0)
    def _():
        gather(0, 0)

    for slot in range(2):
        @pl.when((i + 1 < pl.num_programs(0)) & (cur != slot))
        def _():
            gather(i + 1, slot)

    for k in range(TOP_K):
        pltpu.make_async_copy(buf.at[cur, k], buf.at[cur, k], sem.at[cur]).wait()
    gates = gate_ref[...]
    picked = lambda k: buf[cur, k].reshape(tm, D_MODEL)
    ffn = gates[:, 0:1] * picked(0)
    for k in range(1, TOP_K):
        ffn = ffn + gates[:, k:k + 1] * picked(k)
    x2 = _layer_norm(DEEPNORM_ALPHA * x1_ref[...] + ffn, g_ref[...], b_ref[...])
    gate = _sigmoid(jnp.dot(x2.astype(BF16), wpg_ref[...], preferred_element_type=F32))
    emb = jnp.dot(pe_ref[...].astype(BF16), wple_ref[...], preferred_element_type=F32)
    o_ref[...] = x2 + gate * emb


def _combine(pos_flat, ys, gates, x1, pe, g, b, wple_bf16, wpg_bf16, tm):
    t = x1.shape[0]
    row = lambda i, *_: (i, 0)
    fixed = lambda i, *_: (0, 0)
    return pl.pallas_call(
        functools.partial(_combine_body, tm=tm),
        grid_spec=pltpu.PrefetchScalarGridSpec(
            num_scalar_prefetch=1,
            grid=(t // tm,),
            in_specs=[pl.BlockSpec(memory_space=pl.ANY),
                      pl.BlockSpec((tm, LANES), row),
                      pl.BlockSpec((tm, D_MODEL), row),
                      pl.BlockSpec((tm, PLE_DIM), row),
                      pl.BlockSpec((1, D_MODEL), fixed),
                      pl.BlockSpec((1, D_MODEL), fixed),
                      pl.BlockSpec((PLE_DIM, D_MODEL), fixed),
                      pl.BlockSpec((D_MODEL, D_MODEL), fixed)],
            out_specs=pl.BlockSpec((tm, D_MODEL), row),
            scratch_shapes=[pltpu.VMEM((2, TOP_K, tm // ROW_DMA_UNROLL, ROW_DMA_UNROLL, D_MODEL), F32),
                            pltpu.SemaphoreType.DMA((2,))]),
        out_shape=jax.ShapeDtypeStruct((t, D_MODEL), F32),
        compiler_params=_cparams(("arbitrary",)),
        name="moe_combine",
    )(pos_flat, ys, gates, x1, pe, g, b, wple_bf16, wpg_bf16)


def kernel(x_prompt, x_sample, cache_k, cache_v, state_shift, state_wkv, p_prompt, p_sample,
           ln_emb_g, ln_emb_b, w_in, attn_sinks, rwkv_mu, rwkv_w0, rwkv_w_lora_up, rwkv_a0,
           rwkv_a_lora_up, rwkv_g_lora_up, rwkv_k_k, rwkv_k_a, rwkv_r_k, rwkv_gn_g, rwkv_gn_b,
           w_out, ln1_g, ln1_b, w_router, b_router, w_gate_up, b_gate_up, w_down, b_down,
           ln2_g, ln2_b, w_ple, w_ple_gate):
    assert w_in.shape[0] == DEPTH == 1
    nbp, seq, _ = x_prompt.shape
    nbs, dec_seq, _ = x_sample.shape
    assert dec_seq == 1
    tp, ts = nbp * seq, nbs
    w_buf = cache_k.shape[2]
    assert w_buf <= WINDOW
    assert seq % max(BLOCK, RWKV_TILE, INPROJ_TILE, ROUTER_TILE, TOKEN_TILE) == 0
    rowv = lambda z: z.reshape(1, -1)

    w_in_b = w_in[0].astype(BF16)
    w_out_b = w_out[0].astype(BF16)
    w_ple_b = w_ple[0].astype(BF16)
    w_pg_b = w_ple_gate[0].astype(BF16)
    sinks = attn_sinks[0]
    prm = dict(mu=rowv(rwkv_mu[0]), w0=rowv(rwkv_w0[0]), wlu=rwkv_w_lora_up[0],
               a0=rowv(rwkv_a0[0]), alu=rwkv_a_lora_up[0], glu=rwkv_g_lora_up[0],
               k_k=rowv(rwkv_k_k[0]), k_a=rowv(rwkv_k_a[0]), r_k=rowv(rwkv_r_k[0]),
               gn_g=rowv(rwkv_gn_g[0]), gn_b=rowv(rwkv_gn_b[0]))
    ge, be = rowv(ln_emb_g), rowv(ln_emb_b)

    xn_p, q_p, kv_p, rp_p = _inproj(x_prompt.reshape(tp, D_MODEL), ge, be, w_in_b, INPROJ_TILE)
    att_p = _swa_prompt(sinks, q_p, kv_p, nbp, seq)
    rw_p, wkv_p = _rwkv_prompt(rp_p, prm, nbp, seq, RWKV_TILE)

    xn_s, q_s, kv_s, rp_s = _inproj(x_sample.reshape(ts, D_MODEL), ge, be, w_in_b, ts)
    ck3 = cache_k[0].reshape(nbs, w_buf, KV_COLS)
    cv3 = cache_v[0].reshape(nbs, w_buf, KV_COLS)
    att_s = _swa_sample(sinks, q_s.reshape(nbs, ATT_HEADS, HEAD_DIM),
                        kv_s.reshape(nbs, 2 * KV_HEADS, HEAD_DIM), ck3, cv3)
    att_s = att_s.reshape(ts, ATT_WIDTH)
    p7 = _rwkv_sample_prep(rp_s, state_shift[0], prm)
    hv = lambda z: z.reshape(RWKV_HEADS, HEAD_DIM)
    wkv_s, rw_s = _rwkv_sample_step(p7.reshape(7, nbs, RWKV_HEADS, HEAD_DIM), state_wkv[0],
                                    hv(rwkv_r_k[0]), hv(rwkv_gn_g[0]), hv(rwkv_gn_b[0]))
    rw_s = rw_s.reshape(ts, RWKV_WIDTH)

    wr_pad = jnp.zeros((D_MODEL, LANES), F32).at[:, :N_EXPERTS].set(w_router[0])
    wr_hi = wr_pad.astype(BF16)
    wr_lo = (wr_pad - wr_hi.astype(F32)).astype(BF16)
    br_pad = jnp.full((1, LANES), -jnp.inf, F32).at[0, :N_EXPERTS].set(b_router[0])
    g1, b1 = rowv(ln1_g[0]), rowv(ln1_b[0])
    tmp, tmd = ROUTER_TILE, TOKEN_TILE
    x1_p, info_p, gate_p, cnt_p = _mix_router(att_p, rw_p, xn_p, w_out_b, g1, b1, wr_hi, wr_lo,
                                              br_pad, jnp.zeros((1, LANES), F32), tmp)
    x1_s, info_s, gate_s, cnt = _mix_router(att_s, rw_s, xn_s, w_out_b, g1, b1, wr_hi, wr_lo,
                                            br_pad, cnt_p, ts)

    counts = cnt[0, :N_EXPERTS].astype(jnp.int32)
    padded = ((counts + EXPERT_TILE - 1) // EXPERT_TILE) * EXPERT_TILE
    ends = jnp.cumsum(padded)
    offs = ends - padded
    n_slots = (tp + ts) * TOP_K + N_EXPERTS * EXPERT_TILE
    n_tiles = n_slots // EXPERT_TILE
    n_used = (ends[-1] // EXPERT_TILE).reshape(1).astype(jnp.int32)
    tile_start = jnp.arange(n_tiles, dtype=jnp.int32) * EXPERT_TILE
    tile_e = jnp.sum(ends[None, :] <= tile_start[:, None], axis=1).astype(jnp.int32)
    last_e = jnp.max(jnp.where(padded > 0, jnp.arange(N_EXPERTS), 0)).astype(jnp.int32)
    tile_e = jnp.minimum(tile_e, last_e)

    def slots(info):
        expert = jnp.arange(N_EXPERTS, dtype=jnp.int32)[:, None, None]
        first = jnp.sum(jnp.where(info[None, 0:TOP_K] == expert, offs[:, None, None], 0), axis=0)
        return (first + info[TOP_K:2 * TOP_K]).reshape(-1).astype(jnp.int32)

    pos_p, pos_s = slots(info_p), slots(info_s)

    fill_start = jnp.concatenate([jnp.clip(ends - EXPERT_TILE, 0, n_slots - EXPERT_TILE),
                                  n_used]).astype(jnp.int32)
    xs = _dispatch(pos_p, pos_s, fill_start, x1_p, x1_s, n_slots, tmd)
    bgu = b_gate_up[0]
    bg = bgu[:, 0::2].reshape(N_EXPERTS, 1, D_FF)
    bu = bgu[:, 1::2].reshape(N_EXPERTS, 1, D_FF)
    bdn = b_down[0].reshape(N_EXPERTS, 1, D_MODEL)
    ys = _experts(tile_e, n_used, xs, w_gate_up[0], w_down[0], bg, bu, bdn)

    g2, b2 = rowv(ln2_g[0]), rowv(ln2_b[0])
    y_p = _combine(pos_p, ys, gate_p, x1_p, p_prompt[0].reshape(tp, PLE_DIM), g2, b2,
                   w_ple_b, w_pg_b, tmd)
    y_s = _combine(pos_s, ys, gate_s, x1_s, p_sample[0].reshape(ts, PLE_DIM), g2, b2,
                   w_ple_b, w_pg_b, ts)

    w_keep = min(WINDOW, seq)
    kv_p3 = kv_p.reshape(nbp, seq, 2 * KV_COLS)[:, seq - w_keep:]
    k_win_p = kv_p3[:, :, 0:KV_COLS].reshape(1, nbp, w_keep, KV_HEADS, HEAD_DIM)
    v_win_p = kv_p3[:, :, KV_COLS:].reshape(1, nbp, w_keep, KV_HEADS, HEAD_DIM)
    shift_p = rp_p.reshape(nbp, seq, RWKV_PROJ)[:, seq - 1][None]
    k_new = kv_s[:, 0:KV_COLS].reshape(nbs, 1, KV_HEADS, HEAD_DIM)
    v_new = kv_s[:, KV_COLS:].reshape(nbs, 1, KV_HEADS, HEAD_DIM)
    k_win_s = jnp.concatenate([cache_k[0], k_new], axis=1)[:, 1:][None]
    v_win_s = jnp.concatenate([cache_v[0], v_new], axis=1)[:, 1:][None]
    return (y_p.reshape(nbp, seq, D_MODEL), y_s.reshape(nbs, 1, D_MODEL),
            k_win_p, v_win_p, shift_p, wkv_p[None],
            k_win_s, v_win_s, rp_s[None], wkv_s[None])
```

```python
import functools

import jax
import jax.numpy as jnp
from jax import lax
from jax.experimental import pallas as pl
from jax.experimental.pallas import tpu as pltpu

F32 = jnp.float32
BF16 = jnp.bfloat16

D_MODEL = 1024
HEAD_DIM = 64
ATT_HEADS = 8
KV_HEADS = 2
GROUP = ATT_HEADS // KV_HEADS
ATT_WIDTH = ATT_HEADS * HEAD_DIM
KV_COLS = KV_HEADS * HEAD_DIM
RWKV_HEADS = 8
RWKV_WIDTH = RWKV_HEADS * HEAD_DIM
LORA_W = 64
LORA_A = 64
LORA_G = 128
RWKV_PROJ = 3 * RWKV_WIDTH + LORA_W + LORA_A + LORA_G
WINDOW = 128
BLOCK = 128
PLE_DIM = 256
N_EXPERTS = 32
TOP_K = 4
D_FF = 1024
SWIGLU_LIMIT = 7.0
SWIGLU_ALPHA = 1.702
LN_EPS = 1e-5
GN_EPS = 64e-5
DEPTH = 1
DEEPNORM_ALPHA = (2 * DEPTH) ** 0.25

LANES = 128
CHUNK = 64
HEAD_QUAD = 4
EXPERT_TILE = 512
SWA_BLOCKS = 2
INPROJ_TILE = 512
RWKV_TILE = 256
ROUTER_TILE = 512
TOKEN_TILE = 512
ROW_DMA_UNROLL = 8
VMEM_LIMIT = 48 * 1024 * 1024
EXPERTS_VMEM_LIMIT = 58 * 1024 * 1024


def _cparams(sem):
    return pltpu.CompilerParams(dimension_semantics=sem, vmem_limit_bytes=VMEM_LIMIT)


def _bdot(a, b):
    return jnp.dot(a.astype(BF16), b.astype(BF16), preferred_element_type=F32)


def _bdot_nt(a, b):
    return lax.dot_general(a.astype(BF16), b.astype(BF16), (((1,), (1,)), ((), ())),
                           preferred_element_type=F32)


def _bdot_tn(a, b):
    return lax.dot_general(a.astype(BF16), b.astype(BF16), (((0,), (0,)), ((), ())),
                           preferred_element_type=F32)


def _split_dot(m_bf16, x, passes):
    acc = None
    rem = x
    for _ in range(passes):
        hi = rem.astype(BF16)
        part = jnp.dot(m_bf16, hi, preferred_element_type=F32)
        acc = part if acc is None else acc + part
        rem = rem - hi.astype(F32)
    return acc


def _head_sum(x, bd_bf16):
    width = bd_bf16.shape[0]
    outs = []
    for c in range(x.shape[1] // width):
        acc = None
        rem = x[:, c * width:(c + 1) * width]
        for _ in range(2):
            hi = rem.astype(BF16)
            part = jnp.dot(hi, bd_bf16, preferred_element_type=F32)
            acc = part if acc is None else acc + part
            rem = rem - hi.astype(F32)
        outs.append(acc)
    return jnp.concatenate(outs, axis=1)


def _sigmoid(x):
    return 1.0 / (1.0 + jnp.exp(-x))


def _layer_norm(x, g, b):
    mu = jnp.mean(x, axis=-1, keepdims=True)
    xc = x - mu
    var = jnp.mean(xc * xc, axis=-1, keepdims=True)
    return xc * lax.rsqrt(var + LN_EPS) * g + b


def _inproj_body(x_ref, g_ref, b_ref, w_ref, xn_ref, q_ref, kv_ref, rp_ref):
    xn = _layer_norm(x_ref[...], g_ref[...], b_ref[...])
    xn_ref[...] = xn
    xb = xn.astype(BF16)
    q_ref[...] = jnp.dot(xb, w_ref[:, 0:ATT_WIDTH], preferred_element_type=F32)
    kv_ref[...] = jnp.dot(xb, w_ref[:, ATT_WIDTH:ATT_WIDTH + 2 * KV_COLS],
                          preferred_element_type=F32)
    rp_ref[...] = jnp.dot(xb, w_ref[:, ATT_WIDTH + 2 * KV_COLS:], preferred_element_type=F32)


def _inproj(x2d, g, b, w_bf16, tm):
    t = x2d.shape[0]
    in_proj = w_bf16.shape[1]
    row = lambda i: (i, 0)
    fixed = lambda i: (0, 0)
    return pl.pallas_call(
        _inproj_body,
        grid=(t // tm,),
        in_specs=[pl.BlockSpec((tm, D_MODEL), row),
                  pl.BlockSpec((1, D_MODEL), fixed),
                  pl.BlockSpec((1, D_MODEL), fixed),
                  pl.BlockSpec((D_MODEL, in_proj), fixed)],
        out_specs=[pl.BlockSpec((tm, D_MODEL), row),
                   pl.BlockSpec((tm, ATT_WIDTH), row),
                   pl.BlockSpec((tm, 2 * KV_COLS), row),
                   pl.BlockSpec((tm, RWKV_PROJ), row)],
        out_shape=[jax.ShapeDtypeStruct((t, D_MODEL), F32),
                   jax.ShapeDtypeStruct((t, ATT_WIDTH), F32),
                   jax.ShapeDtypeStruct((t, 2 * KV_COLS), F32),
                   jax.ShapeDtypeStruct((t, RWKV_PROJ), F32)],
        compiler_params=_cparams(("parallel",)),
        name="inproj",
    )(x2d, g, b, w_bf16)


def _alibi_slope(h):
    return 2.0 ** (-8.0 * (h + 1) / ATT_HEADS)


def _swa_prompt_body(sink_ref, q_ref, kvc_ref, kvp_ref, o_ref):
    n = pl.program_id(1)
    q = (q_ref[...] * (HEAD_DIM ** -0.5)).astype(BF16)
    kvc = kvc_ref[...].astype(BF16)
    kvp = kvp_ref[...].astype(BF16)
    row = lax.broadcasted_iota(jnp.int32, (BLOCK, 2 * BLOCK), 0)
    col = lax.broadcasted_iota(jnp.int32, (BLOCK, 2 * BLOCK), 1)
    dist = row + BLOCK - col
    in_band = (dist >= 0) & (dist <= WINDOW)
    distf = dist.astype(F32)
    units = [(b, h) for b in range(SWA_BLOCKS) for h in range(ATT_HEADS)]
    rows = lambda b: slice(b * BLOCK, (b + 1) * BLOCK)
    valid, kbands, vbands = [], [], []
    for b in range(SWA_BLOCKS):
        prev = kvp if b == 0 else kvc[rows(b - 1)]
        valid.append(in_band & ((col >= BLOCK) | (n > 0)) if b == 0 else in_band)
        kb, vb = [], []
        for g in range(KV_HEADS):
            ks = slice(g * HEAD_DIM, (g + 1) * HEAD_DIM)
            vs = slice(KV_COLS + g * HEAD_DIM, KV_COLS + (g + 1) * HEAD_DIM)
            kb.append(jnp.concatenate([prev[:, ks], kvc[rows(b), ks]], axis=0))
            vb.append(jnp.concatenate([prev[:, vs], kvc[rows(b), vs]], axis=0))
        kbands.append(kb)
        vbands.append(vb)
    scores = {(b, h): lax.dot_general(q[rows(b), h * HEAD_DIM:(h + 1) * HEAD_DIM],
                                      kbands[b][h // GROUP], (((1,), (1,)), ((), ())),
                                      preferred_element_type=F32) for b, h in units}
    probs = {}
    for b, h in units:
        s = jnp.where(valid[b], scores[b, h] - _alibi_slope(h) * distf, -jnp.inf)
        sink = sink_ref[h]
        m = jnp.maximum(jnp.max(s, axis=1, keepdims=True), sink)
        e = jnp.exp(s - m)
        den = jnp.sum(e, axis=1, keepdims=True) + jnp.exp(sink - m)
        probs[b, h] = (e / den).astype(BF16)
    outs = {(b, h): jnp.dot(probs[b, h], vbands[b][h // GROUP], preferred_element_type=F32)
            for b, h in units}
    for b in range(SWA_BLOCKS):
        o_ref[rows(b), :] = jnp.concatenate([outs[b, h] for h in range(ATT_HEADS)], axis=1)


def _swa_prompt(sinks, q, kv, nbatch, seq):
    nb = seq // (SWA_BLOCKS * BLOCK)
    cur = lambda b, n: (b * nb + n, 0)
    prv = lambda b, n: (SWA_BLOCKS * (b * nb + n) - jnp.minimum(n, 1), 0)
    return pl.pallas_call(
        _swa_prompt_body,
        grid=(nbatch, nb),
        in_specs=[pl.BlockSpec(memory_space=pltpu.SMEM),
                  pl.BlockSpec((SWA_BLOCKS * BLOCK, ATT_WIDTH), cur),
                  pl.BlockSpec((SWA_BLOCKS * BLOCK, 2 * KV_COLS), cur),
                  pl.BlockSpec((BLOCK, 2 * KV_COLS), prv)],
        out_specs=pl.BlockSpec((SWA_BLOCKS * BLOCK, ATT_WIDTH), cur),
        out_shape=jax.ShapeDtypeStruct((nbatch * seq, ATT_WIDTH), F32),
        compiler_params=_cparams(("parallel", "parallel")),
        name="swa_prompt",
    )(sinks, q, kv, kv)


def _swa_sample_body(sink_ref, q_ref, kvn_ref, ck_ref, cv_ref, o_ref, *, bb, w_buf):
    hrow = lax.broadcasted_iota(jnp.int32, (ATT_HEADS, 1), 0)
    slope = jnp.zeros((ATT_HEADS, 1), F32)
    sink = jnp.zeros((ATT_HEADS, 1), F32)
    for h in range(ATT_HEADS):
        slope = jnp.where(hrow == h, _alibi_slope(h), slope)
        sink = jnp.where(hrow == h, sink_ref[h], sink)
    jcol = lax.broadcasted_iota(jnp.int32, (ATT_HEADS, w_buf), 1)
    bias = -slope * (w_buf - jcol).astype(F32)
    lower = hrow < GROUP
    q2b = [(q_ref[b] * (HEAD_DIM ** -0.5)).astype(BF16) for b in range(bb)]
    s01 = []
    for b in range(bb):
        ck = ck_ref[b].astype(BF16)
        s01.append((_bdot_nt(q2b[b], ck[:, 0:HEAD_DIM]),
                    _bdot_nt(q2b[b], ck[:, HEAD_DIM:2 * HEAD_DIM])))
    ps = []
    for b in range(bb):
        kvn = kvn_ref[b]
        s = jnp.where(lower, s01[b][0], s01[b][1]) + bias
        knew = jnp.where(lower, kvn[0:1, :], kvn[1:2, :])
        vnew = jnp.where(lower, kvn[2:3, :], kvn[3:4, :])
        snew = jnp.sum(q2b[b].astype(F32) * knew.astype(BF16).astype(F32), axis=1, keepdims=True)
        m = jnp.maximum(jnp.maximum(jnp.max(s, axis=1, keepdims=True), snew), sink)
        e = jnp.exp(s - m)
        enew = jnp.exp(snew - m)
        den = jnp.sum(e, axis=1, keepdims=True) + enew + jnp.exp(sink - m)
        pnew = (enew / den).astype(BF16).astype(F32)
        ps.append(((e / den).astype(BF16), pnew * vnew.astype(BF16).astype(F32)))
    for b in range(bb):
        cv = cv_ref[b].astype(BF16)
        o0 = jnp.dot(ps[b][0], cv[:, 0:HEAD_DIM], preferred_element_type=F32)
        o1 = jnp.dot(ps[b][0], cv[:, HEAD_DIM:2 * HEAD_DIM], preferred_element_type=F32)
        o_ref[b] = jnp.where(lower, o0, o1) + ps[b][1]


def _swa_sample(sinks, q3, kvn3, ck3, cv3, bb=8):
    nb, w_buf = ck3.shape[0], ck3.shape[1]
    blk = lambda i: (i, 0, 0)
    return pl.pallas_call(
        functools.partial(_swa_sample_body, bb=bb, w_buf=w_buf),
        grid=(nb // bb,),
        in_specs=[pl.BlockSpec(memory_space=pltpu.SMEM),
                  pl.BlockSpec((bb, ATT_HEADS, HEAD_DIM), blk),
                  pl.BlockSpec((bb, 2 * KV_HEADS, HEAD_DIM), blk),
                  pl.BlockSpec((bb, w_buf, KV_COLS), blk),
                  pl.BlockSpec((bb, w_buf, KV_COLS), blk)],
        out_specs=pl.BlockSpec((bb, ATT_HEADS, HEAD_DIM), blk),
        out_shape=jax.ShapeDtypeStruct((nb, ATT_HEADS, HEAD_DIM), F32),
        compiler_params=_cparams(("parallel",)),
        name="swa_sample",
    )(sinks, q3, kvn3, ck3, cv3)


def _rwkv_prep(rp, prev, mu, w0, wlu, a0, alu, glu, k_k, k_a, bd):
    xs = rp + mu * (prev - rp)
    r = xs[:, 0:RWKV_WIDTH]
    k = xs[:, RWKV_WIDTH:2 * RWKV_WIDTH]
    v = xs[:, 2 * RWKV_WIDTH:3 * RWKV_WIDTH]
    o = 3 * RWKV_WIDTH
    wd = xs[:, o:o + LORA_W]
    ad = xs[:, o + LORA_W:o + LORA_W + LORA_A]
    gd = xs[:, o + LORA_W + LORA_A:]
    z = -(w0 + _bdot(jnp.tanh(wd), wlu))
    softplus = jnp.maximum(z, 0.0) + jnp.log(1.0 + jnp.exp(-jnp.abs(z)))
    log_decay = -jnp.exp(-softplus - 0.5)
    a = _sigmoid(a0 + _bdot(ad, alu))
    g = _bdot(_sigmoid(gd), glu)
    kkr = k * k_k
    kk = kkr / jnp.maximum(jnp.sqrt(_head_sum(kkr * kkr, bd)), 1e-12)
    k2 = k * (1.0 + (a - 1.0) * k_a)
    return r, log_decay, k2, v, kk, a, g


def _rwkv_finish(y, r, k2, v, g, r_k, gn_g, gn_b, bd):
    inv = 1.0 / HEAD_DIM
    mu = _head_sum(y, bd) * inv
    yc = y - mu
    var = _head_sum(yc * yc, bd) * inv
    yn = yc * lax.rsqrt(var + GN_EPS) * gn_g + gn_b
    bonus = _head_sum(r * k2 * r_k, bd) * v
    return (yn + bonus) * g


def _rwkv_prompt_body(rp_ref, mu_ref, w0_ref, wlu_ref, a0_ref, alu_ref, glu_ref, kk_ref, ka_ref,
                      rk_ref, gng_ref, gnb_ref, bd_ref, ltri_ref,
                      out_ref, sfin_ref, prev_scr, st_scr, y_scr, *, tt):
    c = pl.program_id(1)

    @pl.when(c == 0)
    def _():
        prev_scr[...] = jnp.zeros_like(prev_scr)
        st_scr[...] = jnp.zeros_like(st_scr)

    rp = rp_ref[...]
    rolled = pltpu.roll(rp, 1, 0)
    rowi = lax.broadcasted_iota(jnp.int32, (tt, 1), 0)
    prev = jnp.where(rowi == 0, prev_scr[...], rolled)
    prev_scr[...] = rp[tt - 1:tt, :]
    bd = bd_ref[...]
    r, ld, k2, v, kk, a, g = _rwkv_prep(rp, prev, mu_ref[...], w0_ref[...], wlu_ref[...],
                                        a0_ref[...], alu_ref[...], glu_ref[...], kk_ref[...],
                                        ka_ref[...], bd)
    cum = _split_dot(ltri_ref[...], ld, 2)
    ecum = jnp.exp(cum)
    einv = jnp.exp(-cum)
    at = -kk * jnp.exp(cum - ld)
    bh = kk * a * einv
    kh = k2 * einv
    rt = r * ecum

    qw = HEAD_QUAD * HEAD_DIM
    ii = lax.broadcasted_iota(jnp.int32, (CHUNK, qw), 0)
    jl = lax.broadcasted_iota(jnp.int32, (CHUNK, qw), 1) % CHUNK
    strict = ii > jl
    incl = ii >= jl
    eye = (ii == jl).astype(F32)
    same_head = (lax.broadcasted_iota(jnp.int32, (qw, qw), 0) // HEAD_DIM
                 == lax.broadcasted_iota(jnp.int32, (qw, qw), 1) // HEAD_DIM)

    def blockdiag(m):
        mb = m.astype(BF16)
        return jnp.where(same_head, jnp.concatenate([mb] * HEAD_QUAD, axis=0), jnp.zeros((), BF16))

    def diag_blocks(full):
        out = None
        for hh in range(HEAD_QUAD):
            rs = slice(hh * HEAD_DIM, (hh + 1) * HEAD_DIM)
            blk = jnp.where(same_head[rs], full[rs], 0.0)
            out = blk if out is None else out + blk
        return out

    mm = lambda x, w_bf16: jnp.dot(x.astype(BF16), w_bf16, preferred_element_type=F32)
    mm_nt = lambda x, w_bf16: lax.dot_general(x.astype(BF16), w_bf16, (((1,), (1,)), ((), ())),
                                              preferred_element_type=F32)
    nsub = tt // CHUNK
    units = [(s, q) for s in range(nsub) for q in range(RWKV_HEADS // HEAD_QUAD)]
    rows = lambda s: slice(s * CHUNK, (s + 1) * CHUNK)
    cols = lambda q: slice(q * qw, (q + 1) * qw)
    cut = lambda z, u: z[rows(u[0]), cols(u[1])]
    kka = kk * a
    p_last, kh_end, bh_end = [], [], []
    for s in range(nsub):
        c_last = cum[(s + 1) * CHUNK - 1:(s + 1) * CHUNK, :]
        p_last.append(jnp.exp(c_last))
        tail = jnp.exp(c_last - cum[rows(s)])
        kh_end.append(k2[rows(s)] * tail)
        bh_end.append(kka[rows(s)] * tail)
    x2 = {u: jnp.concatenate([cut(at, u), cut(rt, u)], axis=0) for u in units}
    xb = {u: mm_nt(x2[u], blockdiag(cut(bh, u))) for u in units}
    xk = {u: mm_nt(x2[u], blockdiag(cut(kh, u))) for u in units}
    nmat = {u: jnp.where(strict, xb[u][0:CHUNK], 0.0) for u in units}
    a_rb = {u: jnp.where(incl, xb[u][CHUNK:], 0.0) for u in units}
    a_ak = {u: jnp.where(strict, xk[u][0:CHUNK], 0.0) for u in units}
    a_rk = {u: jnp.where(incl, xk[u][CHUNK:], 0.0) for u in units}
    tmat = {u: eye + nmat[u] for u in units}
    npow = nmat
    for _ in range(5):
        npow = {u: mm(npow[u], blockdiag(npow[u])) for u in units}
        tmat = {u: tmat[u] + mm(npow[u], blockdiag(tmat[u])) for u in units}
    vbd = {u: blockdiag(cut(v, u)) for u in units}
    av = {u: mm(a_ak[u], vbd[u]) for u in units}
    aprime = {u: mm(tmat[u], blockdiag(cut(at, u))) for u in units}
    wmat = {u: mm(tmat[u], blockdiag(av[u])) for u in units}
    rprime = {u: cut(rt, u) + mm(a_rb[u], blockdiag(aprime[u])) for u in units}
    y0 = {u: mm(a_rk[u], vbd[u]) + mm(a_rb[u], blockdiag(wmat[u])) for u in units}
    hmat = {u: diag_blocks(_bdot_tn(jnp.concatenate([cut(v, u), wmat[u]], axis=0),
                                    jnp.concatenate([cut(kh_end[u[0]], (0, u[1])),
                                                     cut(bh_end[u[0]], (0, u[1]))], axis=0)))
            for u in units}
    g0 = {u: diag_blocks(_bdot_tn(aprime[u], cut(bh_end[u[0]], (0, u[1])))) for u in units}

    states = [st_scr[q] for q in range(RWKV_HEADS // HEAD_QUAD)]
    for s in range(nsub):
        for q in range(RWKV_HEADS // HEAD_QUAD):
            u = (s, q)
            st = states[q]
            y_scr[rows(s), cols(q)] = mm_nt(rprime[u], blockdiag(st)) + y0[u]
            states[q] = st * p_last[s][:, cols(q)] + mm(st, blockdiag(g0[u])) + hmat[u]
    for q in range(RWKV_HEADS // HEAD_QUAD):
        st_scr[q] = states[q]
        for hh in range(HEAD_QUAD):
            sfin_ref[0, q * HEAD_QUAD + hh] = states[q][:, hh * HEAD_DIM:(hh + 1) * HEAD_DIM]
    out_ref[...] = _rwkv_finish(y_scr[...], r, k2, v, g, rk_ref[...], gng_ref[...], gnb_ref[...], bd)


def _chunk_tril(tt):
    i = jnp.arange(tt)
    same = (i[:, None] // CHUNK) == (i[None, :] // CHUNK)
    return (same & (i[:, None] >= i[None, :])).astype(BF16)


def _head_blockdiag():
    i = jnp.arange(HEAD_QUAD * HEAD_DIM) // HEAD_DIM
    return (i[:, None] == i[None, :]).astype(BF16)


def _rwkv_prompt(rp, prm, nbatch, seq, tt):
    nc = seq // tt
    fixed = lambda b, c: (0, 0)
    row = lambda b, c: (b * nc + c, 0)
    vec = lambda n: pl.BlockSpec((1, n), fixed)
    return pl.pallas_call(
        functools.partial(_rwkv_prompt_body, tt=tt),
        grid=(nbatch, nc),
        in_specs=[pl.BlockSpec((tt, RWKV_PROJ), row),
                  vec(RWKV_PROJ), vec(RWKV_WIDTH),
                  pl.BlockSpec((LORA_W, RWKV_WIDTH), fixed), vec(RWKV_WIDTH),
                  pl.BlockSpec((LORA_A, RWKV_WIDTH), fixed),
                  pl.BlockSpec((LORA_G, RWKV_WIDTH), fixed),
                  vec(RWKV_WIDTH), vec(RWKV_WIDTH), vec(RWKV_WIDTH), vec(RWKV_WIDTH),
                  vec(RWKV_WIDTH),
                  pl.BlockSpec((HEAD_QUAD * HEAD_DIM, HEAD_QUAD * HEAD_DIM), fixed),
                  pl.BlockSpec((tt, tt), fixed)],
        out_specs=[pl.BlockSpec((tt, RWKV_WIDTH), row),
                   pl.BlockSpec((1, RWKV_HEADS, HEAD_DIM, HEAD_DIM), lambda b, c: (b, 0, 0, 0))],
        out_shape=[jax.ShapeDtypeStruct((nbatch * seq, RWKV_WIDTH), F32),
                   jax.ShapeDtypeStruct((nbatch, RWKV_HEADS, HEAD_DIM, HEAD_DIM), F32)],
        scratch_shapes=[pltpu.VMEM((1, RWKV_PROJ), F32),
                        pltpu.VMEM((RWKV_HEADS // HEAD_QUAD, HEAD_DIM, HEAD_QUAD * HEAD_DIM), F32),
                        pltpu.VMEM((tt, RWKV_WIDTH), F32)],
        compiler_params=_cparams(("parallel", "arbitrary")),
        name="rwkv_prompt",
    )(rp, prm["mu"], prm["w0"], prm["wlu"], prm["a0"], prm["alu"], prm["glu"], prm["k_k"],
      prm["k_a"], prm["r_k"], prm["gn_g"], prm["gn_b"], _head_blockdiag(), _chunk_tril(tt))


def _rwkv_sample_prep_body(rp_ref, prev_ref, mu_ref, w0_ref, wlu_ref, a0_ref, alu_ref, glu_ref,
                           kk_ref, ka_ref, bd_ref, o_ref):
    r, ld, k2, v, kk, a, g = _rwkv_prep(rp_ref[...], prev_ref[...], mu_ref[...], w0_ref[...],
                                        wlu_ref[...], a0_ref[...], alu_ref[...], glu_ref[...],
                                        kk_ref[...], ka_ref[...], bd_ref[...])
    for i, z in enumerate((r, jnp.exp(ld), k2, v, kk, a, g)):
        o_ref[i] = z


def _rwkv_sample_prep(rp, prev, prm):
    nb = rp.shape[0]
    return pl.pallas_call(
        _rwkv_sample_prep_body,
        out_shape=jax.ShapeDtypeStruct((7, nb, RWKV_WIDTH), F32),
        compiler_params=pltpu.CompilerParams(vmem_limit_bytes=VMEM_LIMIT),
        name="rwkv_sample_prep",
    )(rp, prev, prm["mu"], prm["w0"], prm["wlu"], prm["a0"], prm["alu"], prm["glu"], prm["k_k"],
      prm["k_a"], _head_blockdiag())


def _rwkv_sample_step_body(p_ref, s_ref, rk_ref, gng_ref, gnb_ref, so_ref, o_ref, *, bb):
    ii = lax.broadcasted_iota(jnp.int32, (HEAD_DIM, HEAD_DIM), 0)
    jj = lax.broadcasted_iota(jnp.int32, (HEAD_DIM, HEAD_DIM), 1)
    eye = ii == jj
    for b in range(bb):
        for h in range(RWKV_HEADS):
            row = lambda i: p_ref[i, b, h:h + 1, :]
            r_r, w_r, k_r, v_r, kk_r, a_r = (row(i) for i in range(6))
            st = s_ref[b, h]
            sa = jnp.sum(st * kk_r, axis=1, keepdims=True)
            v_col = jnp.sum(jnp.where(eye, v_r, 0.0), axis=1, keepdims=True)
            st = st * w_r - sa * (kk_r * a_r) + v_col * k_r
            so_ref[b, h] = st
            y_col = jnp.sum(st * r_r, axis=1, keepdims=True)
            o_ref[b, h:h + 1, :] = jnp.sum(jnp.where(eye, y_col, 0.0), axis=0, keepdims=True)
    y = o_ref[...]
    r, k2, v, g = p_ref[0], p_ref[2], p_ref[3], p_ref[6]
    mu = jnp.mean(y, axis=-1, keepdims=True)
    yc = y - mu
    var = jnp.mean(yc * yc, axis=-1, keepdims=True)
    yn = yc * lax.rsqrt(var + GN_EPS) * gng_ref[...] + gnb_ref[...]
    bonus = jnp.sum(r * k2 * rk_ref[...], axis=-1, keepdims=True) * v
    o_ref[...] = (yn + bonus) * g


def _rwkv_sample_step(p4, state, r_k, gn_g, gn_b, bb=8):
    nb = state.shape[0]
    hv = pl.BlockSpec((RWKV_HEADS, HEAD_DIM), lambda i: (0, 0))
    return pl.pallas_call(
        functools.partial(_rwkv_sample_step_body, bb=bb),
        grid=(nb // bb,),
        in_specs=[pl.BlockSpec((7, bb, RWKV_HEADS, HEAD_DIM), lambda i: (0, i, 0, 0)),
                  pl.BlockSpec((bb, RWKV_HEADS, HEAD_DIM, HEAD_DIM), lambda i: (i, 0, 0, 0)),
                  hv, hv, hv],
        out_specs=[pl.BlockSpec((bb, RWKV_HEADS, HEAD_DIM, HEAD_DIM), lambda i: (i, 0, 0, 0)),
                   pl.BlockSpec((bb, RWKV_HEADS, HEAD_DIM), lambda i: (i, 0, 0))],
        out_shape=[jax.ShapeDtypeStruct(state.shape, F32),
                   jax.ShapeDtypeStruct((nb, RWKV_HEADS, HEAD_DIM), F32)],
        compiler_params=_cparams(("parallel",)),
        name="rwkv_sample_step",
    )(p4, state, r_k, gn_g, gn_b)


def _mix_router_body(att_ref, rw_ref, xn_ref, wo_ref, g_ref, b_ref, wrh_ref, wrl_ref, br_ref,
                     base_ref, x1_ref, info_ref, gate_ref, cnt_ref, base_scr, *, tm):
    i = pl.program_id(0)

    @pl.when(i == 0)
    def _():
        base_scr[...] = base_ref[...]

    mixed = (jnp.dot(att_ref[...].astype(BF16), wo_ref[0:ATT_WIDTH, :], preferred_element_type=F32)
             + jnp.dot(rw_ref[...].astype(BF16), wo_ref[ATT_WIDTH:, :], preferred_element_type=F32))
    x1 = _layer_norm(DEEPNORM_ALPHA * xn_ref[...] + mixed, g_ref[...], b_ref[...])
    x1_ref[...] = x1

    x1h = x1.astype(BF16)
    x1l = (x1 - x1h.astype(F32)).astype(BF16)
    hi_both = jnp.dot(x1h, jnp.concatenate([wrh_ref[...], wrl_ref[...]], axis=1),
                      preferred_element_type=F32)
    logits = (hi_both[:, 0:LANES] + jnp.dot(x1l, wrh_ref[...], preferred_element_type=F32)
              + hi_both[:, LANES:]) + br_ref[...]
    lane = lax.broadcasted_iota(jnp.int32, (tm, LANES), 1)
    lanef = lane.astype(F32)
    vals, idxs, hots = [], [], []
    cur = logits
    for _ in range(TOP_K):
        m = jnp.max(cur, axis=1, keepdims=True)
        idx = jnp.min(jnp.where(cur == m, lanef, float(LANES)), axis=1, keepdims=True)
        hot = lanef == idx
        cur = jnp.where(hot, -jnp.inf, cur)
        vals.append(m)
        idxs.append(idx)
        hots.append(hot)
    es = [jnp.exp(vk - vals[0]) for vk in vals]
    den = es[0] + es[1] + es[2] + es[3]
    multi = jnp.zeros((tm, LANES), F32)
    for hot in hots:
        multi = multi + hot.astype(F32)
    ti = lax.broadcasted_iota(jnp.int32, (tm, tm), 0)
    tj = lax.broadcasted_iota(jnp.int32, (tm, tm), 1)
    before = jnp.dot((ti > tj).astype(BF16), multi.astype(BF16), preferred_element_type=F32)
    before = before + base_scr[...]
    info = jnp.zeros((tm, LANES), F32)
    gates = jnp.zeros((tm, LANES), F32)
    for k in range(TOP_K):
        rank = jnp.sum(jnp.where(hots[k], before, 0.0), axis=1, keepdims=True)
        info = jnp.where(lane == k, idxs[k], info)
        info = jnp.where(lane == TOP_K + k, rank, info)
        gates = jnp.where(lane == k, es[k] / den, gates)
    info_ref[...] = jnp.transpose(info)[0:2 * TOP_K, :].astype(jnp.int32)
    gate_ref[...] = gates
    base_scr[...] = base_scr[...] + jnp.sum(multi, axis=0, keepdims=True)
    cnt_ref[...] = base_scr[...]


def _mix_router(att, rw, xn, wo_bf16, g, b, wr_hi, wr_lo, br_pad, base, tm):
    t = att.shape[0]
    row = lambda i: (i, 0)
    fixed = lambda i: (0, 0)
    return pl.pallas_call(
        functools.partial(_mix_router_body, tm=tm),
        grid=(t // tm,),
        in_specs=[pl.BlockSpec((tm, ATT_WIDTH), row),
                  pl.BlockSpec((tm, RWKV_WIDTH), row),
                  pl.BlockSpec((tm, D_MODEL), row),
                  pl.BlockSpec((D_MODEL, D_MODEL), fixed),
                  pl.BlockSpec((1, D_MODEL), fixed),
                  pl.BlockSpec((1, D_MODEL), fixed),
                  pl.BlockSpec((D_MODEL, LANES), fixed),
                  pl.BlockSpec((D_MODEL, LANES), fixed),
                  pl.BlockSpec((1, LANES), fixed),
                  pl.BlockSpec((1, LANES), fixed)],
        out_specs=[pl.BlockSpec((tm, D_MODEL), row),
                   pl.BlockSpec((2 * TOP_K, tm), lambda i: (0, i)),
                   pl.BlockSpec((tm, LANES), row),
                   pl.BlockSpec((1, LANES), fixed)],
        out_shape=[jax.ShapeDtypeStruct((t, D_MODEL), F32),
                   jax.ShapeDtypeStruct((2 * TOP_K, t), jnp.int32),
                   jax.ShapeDtypeStruct((t, LANES), F32),
                   jax.ShapeDtypeStruct((1, LANES), F32)],
        scratch_shapes=[pltpu.VMEM((1, LANES), F32)],
        compiler_params=_cparams(("arbitrary",)),
        name="mix_router",
    )(att, rw, xn, wo_bf16, g, b, wr_hi, wr_lo, br_pad, base)


def _row_copies(pos_ref, base, tm, make_copy):
    tokens = pos_ref.shape[0] // TOP_K

    def group(gi, carry):
        r0 = pl.multiple_of(gi * ROW_DMA_UNROLL, ROW_DMA_UNROLL)
        for j in range(ROW_DMA_UNROLL):
            for k in range(TOP_K):
                make_copy(gi, j, k, pos_ref[k * tokens + base + r0 + j]).start(priority=k % 2)
        return carry

    lax.fori_loop(0, tm // ROW_DMA_UNROLL, group, 0)


def _dispatch_rows(pos_ref, base, x_ref, xs_ref, sem, rows):
    def make_copy(gi, j, k, p):
        r = pl.multiple_of(gi * ROW_DMA_UNROLL, ROW_DMA_UNROLL) + j
        return pltpu.make_async_copy(x_ref.at[pl.ds(r, 1)], xs_ref.at[pl.ds(p, 1)], sem)

    _row_copies(pos_ref, base, rows, make_copy)
    for _ in range(TOP_K):
        pltpu.make_async_copy(x_ref, xs_ref.at[pl.ds(0, rows)], sem).wait()


def _dispatch_body(pos_a_ref, pos_b_ref, fill_ref, xa_ref, xb_ref, xs_ref, zero_scr, sem, fill_sem,
                   *, tm, n_tiles):
    i = pl.program_id(0)
    last = pl.num_programs(0) - 1

    @pl.when(i == 0)
    def _():
        zero_scr[...] = jnp.zeros_like(zero_scr)

        def tile_fill(start):
            return pltpu.make_async_copy(
                zero_scr, xs_ref.at[pl.ds(pl.multiple_of(start, EXPERT_TILE), EXPERT_TILE)], fill_sem)

        fills = [tile_fill(fill_ref[e]) for e in range(N_EXPERTS)]
        for cp in fills:
            cp.start()
        first_unused = fill_ref[N_EXPERTS]
        lax.fori_loop(first_unused, n_tiles,
                      lambda t, c: (tile_fill(t * EXPERT_TILE).start(), c)[1], 0)
        for cp in fills:
            cp.wait()
        lax.fori_loop(first_unused, n_tiles,
                      lambda t, c: (tile_fill(t * EXPERT_TILE).wait(), c)[1], 0)

    @pl.when(i < last)
    def _():
        _dispatch_rows(pos_a_ref, i * tm, xa_ref, xs_ref, sem, tm)

    @pl.when(i == last)
    def _():
        _dispatch_rows(pos_b_ref, 0, xb_ref, xs_ref, sem, xb_ref.shape[0])


def _dispatch(pos_a, pos_b, fill_start, x_a, x_b, n_slots, tm):
    nta = x_a.shape[0] // tm
    return pl.pallas_call(
        functools.partial(_dispatch_body, tm=tm, n_tiles=n_slots // EXPERT_TILE),
        grid_spec=pltpu.PrefetchScalarGridSpec(
            num_scalar_prefetch=3,
            grid=(nta + 1,),
            in_specs=[pl.BlockSpec((tm, D_MODEL), lambda i, *_: (jnp.minimum(i, nta - 1), 0)),
                      pl.BlockSpec(x_b.shape, lambda i, *_: (0, 0))],
            out_specs=pl.BlockSpec(memory_space=pl.ANY),
            scratch_shapes=[pltpu.VMEM((EXPERT_TILE, D_MODEL), F32),
                            pltpu.SemaphoreType.DMA, pltpu.SemaphoreType.DMA]),
        out_shape=jax.ShapeDtypeStruct((n_slots, D_MODEL), F32),
        compiler_params=_cparams(("arbitrary",)),
        name="moe_dispatch",
    )(pos_a, pos_b, fill_start, x_a, x_b)


def _experts_body(te_ref, nu_ref, xs_ref, wgu_ref, wd_ref, bg_ref, bu_ref, bd_ref, sel_ref, ys_ref,
                  wg_scr, wu_scr, wd_scr):
    i = pl.program_id(0)
    new_expert = (i == 0) | (te_ref[i] != te_ref[jnp.maximum(i - 1, 0)])

    @pl.when(new_expert)
    def _():
        wd_scr[...] = wd_ref[...].astype(BF16)
        for m in range(D_FF // LANES):
            pair = wgu_ref[:, 2 * m * LANES:2 * (m + 1) * LANES].astype(BF16)
            split = jnp.dot(pair, sel_ref[...], preferred_element_type=F32)
            wg_scr[:, m * LANES:(m + 1) * LANES] = split[:, 0:LANES].astype(BF16)
            wu_scr[:, m * LANES:(m + 1) * LANES] = split[:, LANES:].astype(BF16)

    @pl.when(i < nu_ref[0])
    def _():
        x = xs_ref[...].astype(BF16)
        gate = jnp.dot(x, wg_scr[...], preferred_element_type=F32) + bg_ref[...]
        up = jnp.dot(x, wu_scr[...], preferred_element_type=F32) + bu_ref[...]
        gate = jnp.minimum(gate, SWIGLU_LIMIT)
        up = jnp.clip(up, -SWIGLU_LIMIT, SWIGLU_LIMIT)
        act = (up + 1.0) * gate * _sigmoid(SWIGLU_ALPHA * gate)
        ys_ref[...] = jnp.dot(act.astype(BF16), wd_scr[...], preferred_element_type=F32) + bd_ref[...]

    @pl.when(i >= nu_ref[0])
    def _():
        ys_ref[...] = jnp.zeros_like(ys_ref)


def _gate_up_selector():
    i = jnp.arange(2 * LANES)
    src = jnp.where(i < LANES, 2 * i, 2 * (i - LANES) + 1)
    return (i[:, None] == src[None, :]).astype(BF16)


def _experts(tile_e, n_used, xs, wgu, wd, bg, bu, bd):
    ns = xs.shape[0]
    tm = EXPERT_TILE
    wspec = lambda a, b: pl.BlockSpec((None, a, b), lambda i, te, nu: (te[i], 0, 0))
    return pl.pallas_call(
        _experts_body,
        grid_spec=pltpu.PrefetchScalarGridSpec(
            num_scalar_prefetch=2,
            grid=(ns // tm,),
            in_specs=[pl.BlockSpec((tm, D_MODEL), lambda i, te, nu: (jnp.minimum(i, nu[0] - 1), 0)),
                      wspec(D_MODEL, 2 * D_FF), wspec(D_FF, D_MODEL),
                      wspec(1, D_FF), wspec(1, D_FF), wspec(1, D_MODEL),
                      pl.BlockSpec((2 * LANES, 2 * LANES), lambda i, te, nu: (0, 0))],
            out_specs=pl.BlockSpec((tm, D_MODEL), lambda i, te, nu: (i, 0)),
            scratch_shapes=[pltpu.VMEM((D_MODEL, D_FF), BF16), pltpu.VMEM((D_MODEL, D_FF), BF16),
                            pltpu.VMEM((D_FF, D_MODEL), BF16)]),
        out_shape=jax.ShapeDtypeStruct((ns, D_MODEL), F32),
        compiler_params=pltpu.CompilerParams(dimension_semantics=("arbitrary",),
                                             vmem_limit_bytes=EXPERTS_VMEM_LIMIT),
        name="moe_experts",
    )(tile_e, n_used, xs, wgu, wd, bg, bu, bd, _gate_up_selector())


def _combine_body(pos_ref, ys_ref, gate_ref, x1_ref, pe_ref, g_ref, b_ref, wple_ref, wpg_ref,
                  o_ref, buf, sem, *, tm):
    i = pl.program_id(0)
    cur = i % 2

    def gather(tile, slot):
        def make_copy(gi, j, k, p):
            return pltpu.make_async_copy(ys_ref.at[pl.ds(p, 1)], buf.at[slot, k, gi, pl.ds(j, 1)],
                                         sem.at[slot])
        _row_copies(pos_ref, tile * tm, tm, make_copy)

    @pl.when(i == 0)
    def _():
        gather(0, 0)

    for slot in range(2):
        @pl.when((i + 1 < pl.num_programs(0)) & (cur != slot))
        def _():
            gather(i + 1, slot)

    for k in range(TOP_K):
        pltpu.make_async_copy(buf.at[cur, k], buf.at[cur, k], sem.at[cur]).wait()
    gates = gate_ref[...]
    picked = lambda k: buf[cur, k].reshape(tm, D_MODEL)
    ffn = gates[:, 0:1] * picked(0)
    for k in range(1, TOP_K):
        ffn = ffn + gates[:, k:k + 1] * picked(k)
    x2 = _layer_norm(DEEPNORM_ALPHA * x1_ref[...] + ffn, g_ref[...], b_ref[...])
    gate = _sigmoid(jnp.dot(x2.astype(BF16), wpg_ref[...], preferred_element_type=F32))
    emb = jnp.dot(pe_ref[...].astype(BF16), wple_ref[...], preferred_element_type=F32)
    o_ref[...] = x2 + gate * emb


def _combine(pos_flat, ys, gates, x1, pe, g, b, wple_bf16, wpg_bf16, tm):
    t = x1.shape[0]
    row = lambda i, *_: (i, 0)
    fixed = lambda i, *_: (0, 0)
    return pl.pallas_call(
        functools.partial(_combine_body, tm=tm),
        grid_spec=pltpu.PrefetchScalarGridSpec(
            num_scalar_prefetch=1,
            grid=(t // tm,),
            in_specs=[pl.BlockSpec(memory_space=pl.ANY),
                      pl.BlockSpec((tm, LANES), row),
                      pl.BlockSpec((tm, D_MODEL), row),
                      pl.BlockSpec((tm, PLE_DIM), row),
                      pl.BlockSpec((1, D_MODEL), fixed),
                      pl.BlockSpec((1, D_MODEL), fixed),
                      pl.BlockSpec((PLE_DIM, D_MODEL), fixed),
                      pl.BlockSpec((D_MODEL, D_MODEL), fixed)],
            out_specs=pl.BlockSpec((tm, D_MODEL), row),
            scratch_shapes=[pltpu.VMEM((2, TOP_K, tm // ROW_DMA_UNROLL, ROW_DMA_UNROLL, D_MODEL), F32),
                            pltpu.SemaphoreType.DMA((2,))]),
        out_shape=jax.ShapeDtypeStruct((t, D_MODEL), F32),
        compiler_params=_cparams(("arbitrary",)),
        name="moe_combine",
    )(pos_flat, ys, gates, x1, pe, g, b, wple_bf16, wpg_bf16)


def kernel(x_prompt, x_sample, cache_k, cache_v, state_shift, state_wkv, p_prompt, p_sample,
           ln_emb_g, ln_emb_b, w_in, attn_sinks, rwkv_mu, rwkv_w0, rwkv_w_lora_up, rwkv_a0,
           rwkv_a_lora_up, rwkv_g_lora_up, rwkv_k_k, rwkv_k_a, rwkv_r_k, rwkv_gn_g, rwkv_gn_b,
           w_out, ln1_g, ln1_b, w_router, b_router, w_gate_up, b_gate_up, w_down, b_down,
           ln2_g, ln2_b, w_ple, w_ple_gate):
    assert w_in.shape[0] == DEPTH == 1
    nbp, seq, _ = x_prompt.shape
    nbs, dec_seq, _ = x_sample.shape
    assert dec_seq == 1
    tp, ts = nbp * seq, nbs
    w_buf = cache_k.shape[2]
    assert w_buf <= WINDOW
    assert seq % max(BLOCK, RWKV_TILE, INPROJ_TILE, ROUTER_TILE, TOKEN_TILE) == 0
    rowv = lambda z: z.reshape(1, -1)

    w_in_b = w_in[0].astype(BF16)
    w_out_b = w_out[0].astype(BF16)
    w_ple_b = w_ple[0].astype(BF16)
    w_pg_b = w_ple_gate[0].astype(BF16)
    sinks = attn_sinks[0]
    prm = dict(mu=rowv(rwkv_mu[0]), w0=rowv(rwkv_w0[0]), wlu=rwkv_w_lora_up[0],
               a0=rowv(rwkv_a0[0]), alu=rwkv_a_lora_up[0], glu=rwkv_g_lora_up[0],
               k_k=rowv(rwkv_k_k[0]), k_a=rowv(rwkv_k_a[0]), r_k=rowv(rwkv_r_k[0]),
               gn_g=rowv(rwkv_gn_g[0]), gn_b=rowv(rwkv_gn_b[0]))
    ge, be = rowv(ln_emb_g), rowv(ln_emb_b)

    xn_p, q_p, kv_p, rp_p = _inproj(x_prompt.reshape(tp, D_MODEL), ge, be, w_in_b, INPROJ_TILE)
    att_p = _swa_prompt(sinks, q_p, kv_p, nbp, seq)
    rw_p, wkv_p = _rwkv_prompt(rp_p, prm, nbp, seq, RWKV_TILE)

    xn_s, q_s, kv_s, rp_s = _inproj(x_sample.reshape(ts, D_MODEL), ge, be, w_in_b, ts)
    ck3 = cache_k[0].reshape(nbs, w_buf, KV_COLS)
    cv3 = cache_v[0].reshape(nbs, w_buf, KV_COLS)
    att_s = _swa_sample(sinks, q_s.reshape(nbs, ATT_HEADS, HEAD_DIM),
                        kv_s.reshape(nbs, 2 * KV_HEADS, HEAD_DIM), ck3, cv3)
    att_s = att_s.reshape(ts, ATT_WIDTH)
    p7 = _rwkv_sample_prep(rp_s, state_shift[0], prm)
    hv = lambda z: z.reshape(RWKV_HEADS, HEAD_DIM)
    wkv_s, rw_s = _rwkv_sample_step(p7.reshape(7, nbs, RWKV_HEADS, HEAD_DIM), state_wkv[0],
                                    hv(rwkv_r_k[0]), hv(rwkv_gn_g[0]), hv(rwkv_gn_b[0]))
    rw_s = rw_s.reshape(ts, RWKV_WIDTH)

    wr_pad = jnp.zeros((D_MODEL, LANES), F32).at[:, :N_EXPERTS].set(w_router[0])
    wr_hi = wr_pad.astype(BF16)
    wr_lo = (wr_pad - wr_hi.astype(F32)).astype(BF16)
    br_pad = jnp.full((1, LANES), -jnp.inf, F32).at[0, :N_EXPERTS].set(b_router[0])
    g1, b1 = rowv(ln1_g[0]), rowv(ln1_b[0])
    tmp, tmd = ROUTER_TILE, TOKEN_TILE
    x1_p, info_p, gate_p, cnt_p = _mix_router(att_p, rw_p, xn_p, w_out_b, g1, b1, wr_hi, wr_lo,
                                              br_pad, jnp.zeros((1, LANES), F32), tmp)
    x1_s, info_s, gate_s, cnt = _mix_router(att_s, rw_s, xn_s, w_out_b, g1, b1, wr_hi, wr_lo,
                                            br_pad, cnt_p, ts)

    counts = cnt[0, :N_EXPERTS].astype(jnp.int32)
    padded = ((counts + EXPERT_TILE - 1) // EXPERT_TILE) * EXPERT_TILE
    ends = jnp.cumsum(padded)
    offs = ends - padded
    n_slots = (tp + ts) * TOP_K + N_EXPERTS * EXPERT_TILE
    n_tiles = n_slots // EXPERT_TILE
    n_used = (ends[-1] // EXPERT_TILE).reshape(1).astype(jnp.int32)
    tile_start = jnp.arange(n_tiles, dtype=jnp.int32) * EXPERT_TILE
    tile_e = jnp.sum(ends[None, :] <= tile_start[:, None], axis=1).astype(jnp.int32)
    last_e = jnp.max(jnp.where(padded > 0, jnp.arange(N_EXPERTS), 0)).astype(jnp.int32)
    tile_e = jnp.minimum(tile_e, last_e)

    def slots(info):
        expert = jnp.arange(N_EXPERTS, dtype=jnp.int32)[:, None, None]
        first = jnp.sum(jnp.where(info[None, 0:TOP_K] == expert, offs[:, None, None], 0), axis=0)
        return (first + info[TOP_K:2 * TOP_K]).reshape(-1).astype(jnp.int32)

    pos_p, pos_s = slots(info_p), slots(info_s)

    fill_start = jnp.concatenate([jnp.clip(ends - EXPERT_TILE, 0, n_slots - EXPERT_TILE),
                                  n_used]).astype(jnp.int32)
    xs = _dispatch(pos_p, pos_s, fill_start, x1_p, x1_s, n_slots, tmd)
    bgu = b_gate_up[0]
    bg = bgu[:, 0::2].reshape(N_EXPERTS, 1, D_FF)
    bu = bgu[:, 1::2].reshape(N_EXPERTS, 1, D_FF)
    bdn = b_down[0].reshape(N_EXPERTS, 1, D_MODEL)
    ys = _experts(tile_e, n_used, xs, w_gate_up[0], w_down[0], bg, bu, bdn)

    g2, b2 = rowv(ln2_g[0]), rowv(ln2_b[0])
    y_p = _combine(pos_p, ys, gate_p, x1_p, p_prompt[0].reshape(tp, PLE_DIM), g2, b2,
                   w_ple_b, w_pg_b, tmd)
    y_s = _combine(pos_s, ys, gate_s, x1_s, p_sample[0].reshape(ts, PLE_DIM), g2, b2,
                   w_ple_b, w_pg_b, ts)

    w_keep = min(WINDOW, seq)
    kv_p3 = kv_p.reshape(nbp, seq, 2 * KV_COLS)[:, seq - w_keep:]
    k_win_p = kv_p3[:, :, 0:KV_COLS].reshape(1, nbp, w_keep, KV_HEADS, HEAD_DIM)
    v_win_p = kv_p3[:, :, KV_COLS:].reshape(1, nbp, w_keep, KV_HEADS, HEAD_DIM)
    shift_p = rp_p.reshape(nbp, seq, RWKV_PROJ)[:, seq - 1][None]
    k_new = kv_s[:, 0:KV_COLS].reshape(nbs, 1, KV_HEADS, HEAD_DIM)
    v_new = kv_s[:, KV_COLS:].reshape(nbs, 1, KV_HEADS, HEAD_DIM)
    k_win_s = jnp.concatenate([cache_k[0], k_new], axis=1)[:, 1:][None]
    v_win_s = jnp.concatenate([cache_v[0], v_new], axis=1)[:, 1:][None]
    return (y_p.reshape(nbp, seq, D_MODEL), y_s.reshape(nbs, 1, D_MODEL),
            k_win_p, v_win_p, shift_p, wkv_p[None],
            k_win_s, v_win_s, rp_s[None], wkv_s[None])
```

```python
import functools

import jax
import jax.numpy as jnp
from jax import lax
from jax.experimental import pallas as pl
from jax.experimental.pallas import tpu as pltpu

F32 = jnp.float32
BF16 = jnp.bfloat16

D_MODEL = 1024
HEAD_DIM = 64
ATT_HEADS = 8
KV_HEADS = 2
GROUP = ATT_HEADS // KV_HEADS
ATT_WIDTH = ATT_HEADS * HEAD_DIM
KV_COLS = KV_HEADS * HEAD_DIM
RWKV_HEADS = 8
RWKV_WIDTH = RWKV_HEADS * HEAD_DIM
LORA_W = 64
LORA_A = 64
LORA_G = 128
RWKV_PROJ = 3 * RWKV_WIDTH + LORA_W + LORA_A + LORA_G
WINDOW = 128
BLOCK = 128
PLE_DIM = 256
N_EXPERTS = 32
TOP_K = 4
D_FF = 1024
SWIGLU_LIMIT = 7.0
SWIGLU_ALPHA = 1.702
LN_EPS = 1e-5
GN_EPS = 64e-5
DEPTH = 1
DEEPNORM_ALPHA = (2 * DEPTH) ** 0.25

LANES = 128
CHUNK = 64
HEAD_QUAD = 4
EXPERT_TILE = 512
SWA_BLOCKS = 2
INPROJ_TILE = 512
RWKV_TILE = 256
ROUTER_TILE = 1024
TOKEN_TILE = 512
ROW_DMA_UNROLL = 8
VMEM_LIMIT = 48 * 1024 * 1024
EXPERTS_VMEM_LIMIT = 58 * 1024 * 1024


def _cparams(sem):
    return pltpu.CompilerParams(dimension_semantics=sem, vmem_limit_bytes=VMEM_LIMIT)


def _bdot(a, b):
    return jnp.dot(a.astype(BF16), b.astype(BF16), preferred_element_type=F32)


def _bdot_nt(a, b):
    return lax.dot_general(a.astype(BF16), b.astype(BF16), (((1,), (1,)), ((), ())),
                           preferred_element_type=F32)


def _bdot_tn(a, b):
    return lax.dot_general(a.astype(BF16), b.astype(BF16), (((0,), (0,)), ((), ())),
                           preferred_element_type=F32)


def _split_dot(m_bf16, x, passes):
    acc = None
    rem = x
    for _ in range(passes):
        hi = rem.astype(BF16)
        part = jnp.dot(m_bf16, hi, preferred_element_type=F32)
        acc = part if acc is None else acc + part
        rem = rem - hi.astype(F32)
    return acc


def _head_sum(x, bd_bf16):
    width = bd_bf16.shape[0]
    outs = []
    for c in range(x.shape[1] // width):
        acc = None
        rem = x[:, c * width:(c + 1) * width]
        for _ in range(2):
            hi = rem.astype(BF16)
            part = jnp.dot(hi, bd_bf16, preferred_element_type=F32)
            acc = part if acc is None else acc + part
            rem = rem - hi.astype(F32)
        outs.append(acc)
    return jnp.concatenate(outs, axis=1)


def _sigmoid(x):
    return 1.0 / (1.0 + jnp.exp(-x))


def _layer_norm(x, g, b):
    mu = jnp.mean(x, axis=-1, keepdims=True)
    xc = x - mu
    var = jnp.mean(xc * xc, axis=-1, keepdims=True)
    return xc * lax.rsqrt(var + LN_EPS) * g + b


def _inproj_body(x_ref, g_ref, b_ref, w_ref, xn_ref, q_ref, kv_ref, rp_ref):
    xn = _layer_norm(x_ref[...], g_ref[...], b_ref[...])
    xn_ref[...] = xn
    xb = xn.astype(BF16)
    q_ref[...] = jnp.dot(xb, w_ref[:, 0:ATT_WIDTH], preferred_element_type=F32)
    kv_ref[...] = jnp.dot(xb, w_ref[:, ATT_WIDTH:ATT_WIDTH + 2 * KV_COLS],
                          preferred_element_type=F32)
    rp_ref[...] = jnp.dot(xb, w_ref[:, ATT_WIDTH + 2 * KV_COLS:], preferred_element_type=F32)


def _inproj(x2d, g, b, w_bf16, tm):
    t = x2d.shape[0]
    in_proj = w_bf16.shape[1]
    row = lambda i: (i, 0)
    fixed = lambda i: (0, 0)
    return pl.pallas_call(
        _inproj_body,
        grid=(t // tm,),
        in_specs=[pl.BlockSpec((tm, D_MODEL), row),
                  pl.BlockSpec((1, D_MODEL), fixed),
                  pl.BlockSpec((1, D_MODEL), fixed),
                  pl.BlockSpec((D_MODEL, in_proj), fixed)],
        out_specs=[pl.BlockSpec((tm, D_MODEL), row),
                   pl.BlockSpec((tm, ATT_WIDTH), row),
                   pl.BlockSpec((tm, 2 * KV_COLS), row),
                   pl.BlockSpec((tm, RWKV_PROJ), row)],
        out_shape=[jax.ShapeDtypeStruct((t, D_MODEL), F32),
                   jax.ShapeDtypeStruct((t, ATT_WIDTH), F32),
                   jax.ShapeDtypeStruct((t, 2 * KV_COLS), F32),
                   jax.ShapeDtypeStruct((t, RWKV_PROJ), F32)],
        compiler_params=_cparams(("parallel",)),
        name="inproj",
    )(x2d, g, b, w_bf16)


def _alibi_slope(h):
    return 2.0 ** (-8.0 * (h + 1) / ATT_HEADS)


def _swa_prompt_body(sink_ref, q_ref, kvc_ref, kvp_ref, o_ref):
    n = pl.program_id(1)
    q = (q_ref[...] * (HEAD_DIM ** -0.5)).astype(BF16)
    kvc = kvc_ref[...].astype(BF16)
    kvp = kvp_ref[...].astype(BF16)
    row = lax.broadcasted_iota(jnp.int32, (BLOCK, 2 * BLOCK), 0)
    col = lax.broadcasted_iota(jnp.int32, (BLOCK, 2 * BLOCK), 1)
    dist = row + BLOCK - col
    in_band = (dist >= 0) & (dist <= WINDOW)
    distf = dist.astype(F32)
    units = [(b, h) for b in range(SWA_BLOCKS) for h in range(ATT_HEADS)]
    rows = lambda b: slice(b * BLOCK, (b + 1) * BLOCK)
    valid, kbands, vbands = [], [], []
    for b in range(SWA_BLOCKS):
        prev = kvp if b == 0 else kvc[rows(b - 1)]
        valid.append(in_band & ((col >= BLOCK) | (n > 0)) if b == 0 else in_band)
        kb, vb = [], []
        for g in range(KV_HEADS):
            ks = slice(g * HEAD_DIM, (g + 1) * HEAD_DIM)
            vs = slice(KV_COLS + g * HEAD_DIM, KV_COLS + (g + 1) * HEAD_DIM)
            kb.append(jnp.concatenate([prev[:, ks], kvc[rows(b), ks]], axis=0))
            vb.append(jnp.concatenate([prev[:, vs], kvc[rows(b), vs]], axis=0))
        kbands.append(kb)
        vbands.append(vb)
    scores = {(b, h): lax.dot_general(q[rows(b), h * HEAD_DIM:(h + 1) * HEAD_DIM],
                                      kbands[b][h // GROUP], (((1,), (1,)), ((), ())),
                                      preferred_element_type=F32) for b, h in units}
    probs = {}
    for b, h in units:
        s = jnp.where(valid[b], scores[b, h] - _alibi_slope(h) * distf, -jnp.inf)
        sink = sink_ref[h]
        m = jnp.maximum(jnp.max(s, axis=1, keepdims=True), sink)
        e = jnp.exp(s - m)
        den = jnp.sum(e, axis=1, keepdims=True) + jnp.exp(sink - m)
        probs[b, h] = (e / den).astype(BF16)
    outs = {(b, h): jnp.dot(probs[b, h], vbands[b][h // GROUP], preferred_element_type=F32)
            for b, h in units}
    for b in range(SWA_BLOCKS):
        o_ref[rows(b), :] = jnp.concatenate([outs[b, h] for h in range(ATT_HEADS)], axis=1)


def _swa_prompt(sinks, q, kv, nbatch, seq):
    nb = seq // (SWA_BLOCKS * BLOCK)
    cur = lambda b, n: (b * nb + n, 0)
    prv = lambda b, n: (SWA_BLOCKS * (b * nb + n) - jnp.minimum(n, 1), 0)
    return pl.pallas_call(
        _swa_prompt_body,
        grid=(nbatch, nb),
        in_specs=[pl.BlockSpec(memory_space=pltpu.SMEM),
                  pl.BlockSpec((SWA_BLOCKS * BLOCK, ATT_WIDTH), cur),
                  pl.BlockSpec((SWA_BLOCKS * BLOCK, 2 * KV_COLS), cur),
                  pl.BlockSpec((BLOCK, 2 * KV_COLS), prv)],
        out_specs=pl.BlockSpec((SWA_BLOCKS * BLOCK, ATT_WIDTH), cur),
        out_shape=jax.ShapeDtypeStruct((nbatch * seq, ATT_WIDTH), F32),
        compiler_params=_cparams(("parallel", "parallel")),
        name="swa_prompt",
    )(sinks, q, kv, kv)


def _swa_sample_body(sink_ref, q_ref, kvn_ref, ck_ref, cv_ref, o_ref, *, bb, w_buf):
    hrow = lax.broadcasted_iota(jnp.int32, (ATT_HEADS, 1), 0)
    slope = jnp.zeros((ATT_HEADS, 1), F32)
    sink = jnp.zeros((ATT_HEADS, 1), F32)
    for h in range(ATT_HEADS):
        slope = jnp.where(hrow == h, _alibi_slope(h), slope)
        sink = jnp.where(hrow == h, sink_ref[h], sink)
    jcol = lax.broadcasted_iota(jnp.int32, (ATT_HEADS, w_buf), 1)
    bias = -slope * (w_buf - jcol).astype(F32)
    lower = hrow < GROUP
    q2b = [(q_ref[b] * (HEAD_DIM ** -0.5)).astype(BF16) for b in range(bb)]
    s01 = []
    for b in range(bb):
        ck = ck_ref[b].astype(BF16)
        s01.append((_bdot_nt(q2b[b], ck[:, 0:HEAD_DIM]),
                    _bdot_nt(q2b[b], ck[:, HEAD_DIM:2 * HEAD_DIM])))
    ps = []
    for b in range(bb):
        kvn = kvn_ref[b]
        s = jnp.where(lower, s01[b][0], s01[b][1]) + bias
        knew = jnp.where(lower, kvn[0:1, :], kvn[1:2, :])
        vnew = jnp.where(lower, kvn[2:3, :], kvn[3:4, :])
        snew = jnp.sum(q2b[b].astype(F32) * knew.astype(BF16).astype(F32), axis=1, keepdims=True)
        m = jnp.maximum(jnp.maximum(jnp.max(s, axis=1, keepdims=True), snew), sink)
        e = jnp.exp(s - m)
        enew = jnp.exp(snew - m)
        den = jnp.sum(e, axis=1, keepdims=True) + enew + jnp.exp(sink - m)
        pnew = (enew / den).astype(BF16).astype(F32)
        ps.append(((e / den).astype(BF16), pnew * vnew.astype(BF16).astype(F32)))
    for b in range(bb):
        cv = cv_ref[b].astype(BF16)
        o0 = jnp.dot(ps[b][0], cv[:, 0:HEAD_DIM], preferred_element_type=F32)
        o1 = jnp.dot(ps[b][0], cv[:, HEAD_DIM:2 * HEAD_DIM], preferred_element_type=F32)
        o_ref[b] = jnp.where(lower, o0, o1) + ps[b][1]


def _swa_sample(sinks, q3, kvn3, ck3, cv3, bb=8):
    nb, w_buf = ck3.shape[0], ck3.shape[1]
    blk = lambda i: (i, 0, 0)
    return pl.pallas_call(
        functools.partial(_swa_sample_body, bb=bb, w_buf=w_buf),
        grid=(nb // bb,),
        in_specs=[pl.BlockSpec(memory_space=pltpu.SMEM),
                  pl.BlockSpec((bb, ATT_HEADS, HEAD_DIM), blk),
                  pl.BlockSpec((bb, 2 * KV_HEADS, HEAD_DIM), blk),
                  pl.BlockSpec((bb, w_buf, KV_COLS), blk),
                  pl.BlockSpec((bb, w_buf, KV_COLS), blk)],
        out_specs=pl.BlockSpec((bb, ATT_HEADS, HEAD_DIM), blk),
        out_shape=jax.ShapeDtypeStruct((nb, ATT_HEADS, HEAD_DIM), F32),
        compiler_params=_cparams(("parallel",)),
        name="swa_sample",
    )(sinks, q3, kvn3, ck3, cv3)


def _rwkv_prep(rp, prev, mu, w0, wlu, a0, alu, glu, k_k, k_a, bd):
    xs = rp + mu * (prev - rp)
    r = xs[:, 0:RWKV_WIDTH]
    k = xs[:, RWKV_WIDTH:2 * RWKV_WIDTH]
    v = xs[:, 2 * RWKV_WIDTH:3 * RWKV_WIDTH]
    o = 3 * RWKV_WIDTH
    wd = xs[:, o:o + LORA_W]
    ad = xs[:, o + LORA_W:o + LORA_W + LORA_A]
    gd = xs[:, o + LORA_W + LORA_A:]
    z = -(w0 + _bdot(jnp.tanh(wd), wlu))
    softplus = jnp.maximum(z, 0.0) + jnp.log(1.0 + jnp.exp(-jnp.abs(z)))
    log_decay = -jnp.exp(-softplus - 0.5)
    a = _sigmoid(a0 + _bdot(ad, alu))
    g = _bdot(_sigmoid(gd), glu)
    kkr = k * k_k
    kk = kkr / jnp.maximum(jnp.sqrt(_head_sum(kkr * kkr, bd)), 1e-12)
    k2 = k * (1.0 + (a - 1.0) * k_a)
    return r, log_decay, k2, v, kk, a, g


def _rwkv_finish(y, r, k2, v, g, r_k, gn_g, gn_b, bd):
    inv = 1.0 / HEAD_DIM
    mu = _head_sum(y, bd) * inv
    yc = y - mu
    var = _head_sum(yc * yc, bd) * inv
    yn = yc * lax.rsqrt(var + GN_EPS) * gn_g + gn_b
    bonus = _head_sum(r * k2 * r_k, bd) * v
    return (yn + bonus) * g


def _rwkv_prompt_body(rp_ref, mu_ref, w0_ref, wlu_ref, a0_ref, alu_ref, glu_ref, kk_ref, ka_ref,
                      rk_ref, gng_ref, gnb_ref, bd_ref, ltri_ref,
                      out_ref, sfin_ref, prev_scr, st_scr, y_scr, *, tt):
    c = pl.program_id(1)

    @pl.when(c == 0)
    def _():
        prev_scr[...] = jnp.zeros_like(prev_scr)
        st_scr[...] = jnp.zeros_like(st_scr)

    rp = rp_ref[...]
    rolled = pltpu.roll(rp, 1, 0)
    rowi = lax.broadcasted_iota(jnp.int32, (tt, 1), 0)
    prev = jnp.where(rowi == 0, prev_scr[...], rolled)
    prev_scr[...] = rp[tt - 1:tt, :]
    bd = bd_ref[...]
    r, ld, k2, v, kk, a, g = _rwkv_prep(rp, prev, mu_ref[...], w0_ref[...], wlu_ref[...],
                                        a0_ref[...], alu_ref[...], glu_ref[...], kk_ref[...],
                                        ka_ref[...], bd)
    cum = _split_dot(ltri_ref[...], ld, 2)
    ecum = jnp.exp(cum)
    einv = jnp.exp(-cum)
    at = -kk * jnp.exp(cum - ld)
    bh = kk * a * einv
    kh = k2 * einv
    rt = r * ecum

    qw = HEAD_QUAD * HEAD_DIM
    ii = lax.broadcasted_iota(jnp.int32, (CHUNK, qw), 0)
    jl = lax.broadcasted_iota(jnp.int32, (CHUNK, qw), 1) % CHUNK
    strict = ii > jl
    incl = ii >= jl
    eye = (ii == jl).astype(F32)
    same_head = (lax.broadcasted_iota(jnp.int32, (qw, qw), 0) // HEAD_DIM
                 == lax.broadcasted_iota(jnp.int32, (qw, qw), 1) // HEAD_DIM)

    def blockdiag(m):
        mb = m.astype(BF16)
        return jnp.where(same_head, jnp.concatenate([mb] * HEAD_QUAD, axis=0), jnp.zeros((), BF16))

    def diag_blocks(full):
        out = None
        for hh in range(HEAD_QUAD):
            rs = slice(hh * HEAD_DIM, (hh + 1) * HEAD_DIM)
            blk = jnp.where(same_head[rs], full[rs], 0.0)
            out = blk if out is None else out + blk
        return out

    mm = lambda x, w_bf16: jnp.dot(x.astype(BF16), w_bf16, preferred_element_type=F32)
    mm_nt = lambda x, w_bf16: lax.dot_general(x.astype(BF16), w_bf16, (((1,), (1,)), ((), ())),
                                              preferred_element_type=F32)
    nsub = tt // CHUNK
    units = [(s, q) for s in range(nsub) for q in range(RWKV_HEADS // HEAD_QUAD)]
    rows = lambda s: slice(s * CHUNK, (s + 1) * CHUNK)
    cols = lambda q: slice(q * qw, (q + 1) * qw)
    cut = lambda z, u: z[rows(u[0]), cols(u[1])]
    kka = kk * a
    p_last, kh_end, bh_end = [], [], []
    for s in range(nsub):
        c_last = cum[(s + 1) * CHUNK - 1:(s + 1) * CHUNK, :]
        p_last.append(jnp.exp(c_last))
        tail = jnp.exp(c_last - cum[rows(s)])
        kh_end.append(k2[rows(s)] * tail)
        bh_end.append(kka[rows(s)] * tail)
    x2 = {u: jnp.concatenate([cut(at, u), cut(rt, u)], axis=0) for u in units}
    xb = {u: mm_nt(x2[u], blockdiag(cut(bh, u))) for u in units}
    xk = {u: mm_nt(x2[u], blockdiag(cut(kh, u))) for u in units}
    nmat = {u: jnp.where(strict, xb[u][0:CHUNK], 0.0) for u in units}
    a_rb = {u: jnp.where(incl, xb[u][CHUNK:], 0.0) for u in units}
    a_ak = {u: jnp.where(strict, xk[u][0:CHUNK], 0.0) for u in units}
    a_rk = {u: jnp.where(incl, xk[u][CHUNK:], 0.0) for u in units}
    tmat = {u: eye + nmat[u] for u in units}
    npow = nmat
    for _ in range(5):
        npow = {u: mm(npow[u], blockdiag(npow[u])) for u in units}
        tmat = {u: tmat[u] + mm(npow[u], blockdiag(tmat[u])) for u in units}
    vbd = {u: blockdiag(cut(v, u)) for u in units}
    av = {u: mm(a_ak[u], vbd[u]) for u in units}
    aprime = {u: mm(tmat[u], blockdiag(cut(at, u))) for u in units}
    wmat = {u: mm(tmat[u], blockdiag(av[u])) for u in units}
    rprime = {u: cut(rt, u) + mm(a_rb[u], blockdiag(aprime[u])) for u in units}
    y0 = {u: mm(a_rk[u], vbd[u]) + mm(a_rb[u], blockdiag(wmat[u])) for u in units}
    hmat = {u: diag_blocks(_bdot_tn(jnp.concatenate([cut(v, u), wmat[u]], axis=0),
                                    jnp.concatenate([cut(kh_end[u[0]], (0, u[1])),
                                                     cut(bh_end[u[0]], (0, u[1]))], axis=0)))
            for u in units}
    g0 = {u: diag_blocks(_bdot_tn(aprime[u], cut(bh_end[u[0]], (0, u[1])))) for u in units}

    states = [st_scr[q] for q in range(RWKV_HEADS // HEAD_QUAD)]
    for s in range(nsub):
        for q in range(RWKV_HEADS // HEAD_QUAD):
            u = (s, q)
            st = states[q]
            y_scr[rows(s), cols(q)] = mm_nt(rprime[u], blockdiag(st)) + y0[u]
            states[q] = st * p_last[s][:, cols(q)] + mm(st, blockdiag(g0[u])) + hmat[u]
    for q in range(RWKV_HEADS // HEAD_QUAD):
        st_scr[q] = states[q]
        for hh in range(HEAD_QUAD):
            sfin_ref[0, q * HEAD_QUAD + hh] = states[q][:, hh * HEAD_DIM:(hh + 1) * HEAD_DIM]
    out_ref[...] = _rwkv_finish(y_scr[...], r, k2, v, g, rk_ref[...], gng_ref[...], gnb_ref[...], bd)


def _chunk_tril(tt):
    i = jnp.arange(tt)
    same = (i[:, None] // CHUNK) == (i[None, :] // CHUNK)
    return (same & (i[:, None] >= i[None, :])).astype(BF16)


def _head_blockdiag():
    i = jnp.arange(HEAD_QUAD * HEAD_DIM) // HEAD_DIM
    return (i[:, None] == i[None, :]).astype(BF16)


def _rwkv_prompt(rp, prm, nbatch, seq, tt):
    nc = seq // tt
    fixed = lambda b, c: (0, 0)
    row = lambda b, c: (b * nc + c, 0)
    vec = lambda n: pl.BlockSpec((1, n), fixed)
    return pl.pallas_call(
        functools.partial(_rwkv_prompt_body, tt=tt),
        grid=(nbatch, nc),
        in_specs=[pl.BlockSpec((tt, RWKV_PROJ), row),
                  vec(RWKV_PROJ), vec(RWKV_WIDTH),
                  pl.BlockSpec((LORA_W, RWKV_WIDTH), fixed), vec(RWKV_WIDTH),
                  pl.BlockSpec((LORA_A, RWKV_WIDTH), fixed),
                  pl.BlockSpec((LORA_G, RWKV_WIDTH), fixed),
                  vec(RWKV_WIDTH), vec(RWKV_WIDTH), vec(RWKV_WIDTH), vec(RWKV_WIDTH),
                  vec(RWKV_WIDTH),
                  pl.BlockSpec((HEAD_QUAD * HEAD_DIM, HEAD_QUAD * HEAD_DIM), fixed),
                  pl.BlockSpec((tt, tt), fixed)],
        out_specs=[pl.BlockSpec((tt, RWKV_WIDTH), row),
                   pl.BlockSpec((1, RWKV_HEADS, HEAD_DIM, HEAD_DIM), lambda b, c: (b, 0, 0, 0))],
        out_shape=[jax.ShapeDtypeStruct((nbatch * seq, RWKV_WIDTH), F32),
                   jax.ShapeDtypeStruct((nbatch, RWKV_HEADS, HEAD_DIM, HEAD_DIM), F32)],
        scratch_shapes=[pltpu.VMEM((1, RWKV_PROJ), F32),
                        pltpu.VMEM((RWKV_HEADS // HEAD_QUAD, HEAD_DIM, HEAD_QUAD * HEAD_DIM), F32),
                        pltpu.VMEM((tt, RWKV_WIDTH), F32)],
        compiler_params=_cparams(("parallel", "arbitrary")),
        name="rwkv_prompt",
    )(rp, prm["mu"], prm["w0"], prm["wlu"], prm["a0"], prm["alu"], prm["glu"], prm["k_k"],
      prm["k_a"], prm["r_k"], prm["gn_g"], prm["gn_b"], _head_blockdiag(), _chunk_tril(tt))


def _rwkv_sample_prep_body(rp_ref, prev_ref, mu_ref, w0_ref, wlu_ref, a0_ref, alu_ref, glu_ref,
                           kk_ref, ka_ref, bd_ref, o_ref):
    r, ld, k2, v, kk, a, g = _rwkv_prep(rp_ref[...], prev_ref[...], mu_ref[...], w0_ref[...],
                                        wlu_ref[...], a0_ref[...], alu_ref[...], glu_ref[...],
                                        kk_ref[...], ka_ref[...], bd_ref[...])
    for i, z in enumerate((r, jnp.exp(ld), k2, v, kk, a, g)):
        o_ref[i] = z


def _rwkv_sample_prep(rp, prev, prm):
    nb = rp.shape[0]
    return pl.pallas_call(
        _rwkv_sample_prep_body,
        out_shape=jax.ShapeDtypeStruct((7, nb, RWKV_WIDTH), F32),
        compiler_params=pltpu.CompilerParams(vmem_limit_bytes=VMEM_LIMIT),
        name="rwkv_sample_prep",
    )(rp, prev, prm["mu"], prm["w0"], prm["wlu"], prm["a0"], prm["alu"], prm["glu"], prm["k_k"],
      prm["k_a"], _head_blockdiag())


def _rwkv_sample_step_body(p_ref, s_ref, rk_ref, gng_ref, gnb_ref, so_ref, o_ref, *, bb):
    ii = lax.broadcasted_iota(jnp.int32, (HEAD_DIM, HEAD_DIM), 0)
    jj = lax.broadcasted_iota(jnp.int32, (HEAD_DIM, HEAD_DIM), 1)
    eye = ii == jj
    for b in range(bb):
        for h in range(RWKV_HEADS):
            row = lambda i: p_ref[i, b, h:h + 1, :]
            r_r, w_r, k_r, v_r, kk_r, a_r = (row(i) for i in range(6))
            st = s_ref[b, h]
            sa = jnp.sum(st * kk_r, axis=1, keepdims=True)
            v_col = jnp.sum(jnp.where(eye, v_r, 0.0), axis=1, keepdims=True)
            st = st * w_r - sa * (kk_r * a_r) + v_col * k_r
            so_ref[b, h] = st
            y_col = jnp.sum(st * r_r, axis=1, keepdims=True)
            o_ref[b, h:h + 1, :] = jnp.sum(jnp.where(eye, y_col, 0.0), axis=0, keepdims=True)
    y = o_ref[...]
    r, k2, v, g = p_ref[0], p_ref[2], p_ref[3], p_ref[6]
    mu = jnp.mean(y, axis=-1, keepdims=True)
    yc = y - mu
    var = jnp.mean(yc * yc, axis=-1, keepdims=True)
    yn = yc * lax.rsqrt(var + GN_EPS) * gng_ref[...] + gnb_ref[...]
    bonus = jnp.sum(r * k2 * rk_ref[...], axis=-1, keepdims=True) * v
    o_ref[...] = (yn + bonus) * g


def _rwkv_sample_step(p4, state, r_k, gn_g, gn_b, bb=8):
    nb = state.shape[0]
    hv = pl.BlockSpec((RWKV_HEADS, HEAD_DIM), lambda i: (0, 0))
    return pl.pallas_call(
        functools.partial(_rwkv_sample_step_body, bb=bb),
        grid=(nb // bb,),
        in_specs=[pl.BlockSpec((7, bb, RWKV_HEADS, HEAD_DIM), lambda i: (0, i, 0, 0)),
                  pl.BlockSpec((bb, RWKV_HEADS, HEAD_DIM, HEAD_DIM), lambda i: (i, 0, 0, 0)),
                  hv, hv, hv],
        out_specs=[pl.BlockSpec((bb, RWKV_HEADS, HEAD_DIM, HEAD_DIM), lambda i: (i, 0, 0, 0)),
                   pl.BlockSpec((bb, RWKV_HEADS, HEAD_DIM), lambda i: (i, 0, 0))],
        out_shape=[jax.ShapeDtypeStruct(state.shape, F32),
                   jax.ShapeDtypeStruct((nb, RWKV_HEADS, HEAD_DIM), F32)],
        compiler_params=_cparams(("parallel",)),
        name="rwkv_sample_step",
    )(p4, state, r_k, gn_g, gn_b)


def _mix_router_body(att_ref, rw_ref, xn_ref, wo_ref, g_ref, b_ref, wrh_ref, wrl_ref, br_ref,
                     base_ref, x1_ref, info_ref, gate_ref, cnt_ref, base_scr, *, tm):
    i = pl.program_id(0)

    @pl.when(i == 0)
    def _():
        base_scr[...] = base_ref[...]

    mixed = (jnp.dot(att_ref[...].astype(BF16), wo_ref[0:ATT_WIDTH, :], preferred_element_type=F32)
             + jnp.dot(rw_ref[...].astype(BF16), wo_ref[ATT_WIDTH:, :], preferred_element_type=F32))
    x1 = _layer_norm(DEEPNORM_ALPHA * xn_ref[...] + mixed, g_ref[...], b_ref[...])
    x1_ref[...] = x1

    x1h = x1.astype(BF16)
    x1l = (x1 - x1h.astype(F32)).astype(BF16)
    hi_both = jnp.dot(x1h, jnp.concatenate([wrh_ref[...], wrl_ref[...]], axis=1),
                      preferred_element_type=F32)
    logits = (hi_both[:, 0:LANES] + jnp.dot(x1l, wrh_ref[...], preferred_element_type=F32)
              + hi_both[:, LANES:]) + br_ref[...]
    lane = lax.broadcasted_iota(jnp.int32, (tm, LANES), 1)
    lanef = lane.astype(F32)
    vals, idxs, hots = [], [], []
    cur = logits
    for _ in range(TOP_K):
        m = jnp.max(cur, axis=1, keepdims=True)
        idx = jnp.min(jnp.where(cur == m, lanef, float(LANES)), axis=1, keepdims=True)
        hot = lanef == idx
        cur = jnp.where(hot, -jnp.inf, cur)
        vals.append(m)
        idxs.append(idx)
        hots.append(hot)
    es = [jnp.exp(vk - vals[0]) for vk in vals]
    den = es[0] + es[1] + es[2] + es[3]
    multi = jnp.zeros((tm, LANES), F32)
    for hot in hots:
        multi = multi + hot.astype(F32)
    ti = lax.broadcasted_iota(jnp.int32, (tm, tm), 0)
    tj = lax.broadcasted_iota(jnp.int32, (tm, tm), 1)
    before = jnp.dot((ti > tj).astype(BF16), multi.astype(BF16), preferred_element_type=F32)
    before = before + base_scr[...]
    info = jnp.zeros((tm, LANES), F32)
    gates = jnp.zeros((tm, LANES), F32)
    for k in range(TOP_K):
        rank = jnp.sum(jnp.where(hots[k], before, 0.0), axis=1, keepdims=True)
        info = jnp.where(lane == k, idxs[k], info)
        info = jnp.where(lane == TOP_K + k, rank, info)
        gates = jnp.where(lane == k, es[k] / den, gates)
    info_ref[...] = jnp.transpose(info)[0:2 * TOP_K, :].astype(jnp.int32)
    gate_ref[...] = gates
    base_scr[...] = base_scr[...] + jnp.sum(multi, axis=0, keepdims=True)
    cnt_ref[...] = base_scr[...]


def _mix_router(att, rw, xn, wo_bf16, g, b, wr_hi, wr_lo, br_pad, base, tm):
    t = att.shape[0]
    row = lambda i: (i, 0)
    fixed = lambda i: (0, 0)
    return pl.pallas_call(
        functools.partial(_mix_router_body, tm=tm),
        grid=(t // tm,),
        in_specs=[pl.BlockSpec((tm, ATT_WIDTH), row),
                  pl.BlockSpec((tm, RWKV_WIDTH), row),
                  pl.BlockSpec((tm, D_MODEL), row),
                  pl.BlockSpec((D_MODEL, D_MODEL), fixed),
                  pl.BlockSpec((1, D_MODEL), fixed),
                  pl.BlockSpec((1, D_MODEL), fixed),
                  pl.BlockSpec((D_MODEL, LANES), fixed),
                  pl.BlockSpec((D_MODEL, LANES), fixed),
                  pl.BlockSpec((1, LANES), fixed),
                  pl.BlockSpec((1, LANES), fixed)],
        out_specs=[pl.BlockSpec((tm, D_MODEL), row),
                   pl.BlockSpec((2 * TOP_K, tm), lambda i: (0, i)),
                   pl.BlockSpec((tm, LANES), row),
                   pl.BlockSpec((1, LANES), fixed)],
        out_shape=[jax.ShapeDtypeStruct((t, D_MODEL), F32),
                   jax.ShapeDtypeStruct((2 * TOP_K, t), jnp.int32),
                   jax.ShapeDtypeStruct((t, LANES), F32),
                   jax.ShapeDtypeStruct((1, LANES), F32)],
        scratch_shapes=[pltpu.VMEM((1, LANES), F32)],
        compiler_params=_cparams(("arbitrary",)),
        name="mix_router",
    )(att, rw, xn, wo_bf16, g, b, wr_hi, wr_lo, br_pad, base)


def _row_copies(pos_ref, base, tm, make_copy):
    tokens = pos_ref.shape[0] // TOP_K

    def group(gi, carry):
        r0 = pl.multiple_of(gi * ROW_DMA_UNROLL, ROW_DMA_UNROLL)
        for j in range(ROW_DMA_UNROLL):
            for k in range(TOP_K):
                make_copy(gi, j, k, pos_ref[k * tokens + base + r0 + j]).start(priority=k % 2)
        return carry

    lax.fori_loop(0, tm // ROW_DMA_UNROLL, group, 0)


def _dispatch_rows(pos_ref, base, x_ref, xs_ref, sem, rows):
    def make_copy(gi, j, k, p):
        r = pl.multiple_of(gi * ROW_DMA_UNROLL, ROW_DMA_UNROLL) + j
        return pltpu.make_async_copy(x_ref.at[pl.ds(r, 1)], xs_ref.at[pl.ds(p, 1)], sem)

    _row_copies(pos_ref, base, rows, make_copy)
    for _ in range(TOP_K):
        pltpu.make_async_copy(x_ref, xs_ref.at[pl.ds(0, rows)], sem).wait()


def _dispatch_body(pos_a_ref, pos_b_ref, fill_ref, xa_ref, xb_ref, xs_ref, zero_scr, sem, fill_sem,
                   *, tm, n_tiles):
    i = pl.program_id(0)
    last = pl.num_programs(0) - 1

    @pl.when(i == 0)
    def _():
        zero_scr[...] = jnp.zeros_like(zero_scr)

        def tile_fill(start):
            return pltpu.make_async_copy(
                zero_scr, xs_ref.at[pl.ds(pl.multiple_of(start, EXPERT_TILE), EXPERT_TILE)], fill_sem)

        fills = [tile_fill(fill_ref[e]) for e in range(N_EXPERTS)]
        for cp in fills:
            cp.start()
        first_unused = fill_ref[N_EXPERTS]
        lax.fori_loop(first_unused, n_tiles,
                      lambda t, c: (tile_fill(t * EXPERT_TILE).start(), c)[1], 0)
        for cp in fills:
            cp.wait()
        lax.fori_loop(first_unused, n_tiles,
                      lambda t, c: (tile_fill(t * EXPERT_TILE).wait(), c)[1], 0)

    @pl.when(i < last)
    def _():
        _dispatch_rows(pos_a_ref, i * tm, xa_ref, xs_ref, sem, tm)

    @pl.when(i == last)
    def _():
        _dispatch_rows(pos_b_ref, 0, xb_ref, xs_ref, sem, xb_ref.shape[0])


def _dispatch(pos_a, pos_b, fill_start, x_a, x_b, n_slots, tm):
    nta = x_a.shape[0] // tm
    return pl.pallas_call(
        functools.partial(_dispatch_body, tm=tm, n_tiles=n_slots // EXPERT_TILE),
        grid_spec=pltpu.PrefetchScalarGridSpec(
            num_scalar_prefetch=3,
            grid=(nta + 1,),
            in_specs=[pl.BlockSpec((tm, D_MODEL), lambda i, *_: (jnp.minimum(i, nta - 1), 0)),
                      pl.BlockSpec(x_b.shape, lambda i, *_: (0, 0))],
            out_specs=pl.BlockSpec(memory_space=pl.ANY),
            scratch_shapes=[pltpu.VMEM((EXPERT_TILE, D_MODEL), F32),
                            pltpu.SemaphoreType.DMA, pltpu.SemaphoreType.DMA]),
        out_shape=jax.ShapeDtypeStruct((n_slots, D_MODEL), F32),
        compiler_params=_cparams(("arbitrary",)),
        name="moe_dispatch",
    )(pos_a, pos_b, fill_start, x_a, x_b)


def _experts_body(te_ref, nu_ref, xs_ref, wgu_ref, wd_ref, bg_ref, bu_ref, bd_ref, sel_ref, ys_ref,
                  wg_scr, wu_scr, wd_scr):
    i = pl.program_id(0)
    new_expert = (i == 0) | (te_ref[i] != te_ref[jnp.maximum(i - 1, 0)])

    @pl.when(new_expert)
    def _():
        wd_scr[...] = wd_ref[...].astype(BF16)
        for m in range(D_FF // LANES):
            pair = wgu_ref[:, 2 * m * LANES:2 * (m + 1) * LANES].astype(BF16)
            split = jnp.dot(pair, sel_ref[...], preferred_element_type=F32)
            wg_scr[:, m * LANES:(m + 1) * LANES] = split[:, 0:LANES].astype(BF16)
            wu_scr[:, m * LANES:(m + 1) * LANES] = split[:, LANES:].astype(BF16)

    @pl.when(i < nu_ref[0])
    def _():
        x = xs_ref[...].astype(BF16)
        gate = jnp.dot(x, wg_scr[...], preferred_element_type=F32) + bg_ref[...]
        up = jnp.dot(x, wu_scr[...], preferred_element_type=F32) + bu_ref[...]
        gate = jnp.minimum(gate, SWIGLU_LIMIT)
        up = jnp.clip(up, -SWIGLU_LIMIT, SWIGLU_LIMIT)
        act = (up + 1.0) * gate * _sigmoid(SWIGLU_ALPHA * gate)
        ys_ref[...] = jnp.dot(act.astype(BF16), wd_scr[...], preferred_element_type=F32) + bd_ref[...]

    @pl.when(i >= nu_ref[0])
    def _():
        ys_ref[...] = jnp.zeros_like(ys_ref)


def _gate_up_selector():
    i = jnp.arange(2 * LANES)
    src = jnp.where(i < LANES, 2 * i, 2 * (i - LANES) + 1)
    return (i[:, None] == src[None, :]).astype(BF16)


def _experts(tile_e, n_used, xs, wgu, wd, bg, bu, bd):
    ns = xs.shape[0]
    tm = EXPERT_TILE
    wspec = lambda a, b: pl.BlockSpec((None, a, b), lambda i, te, nu: (te[i], 0, 0))
    return pl.pallas_call(
        _experts_body,
        grid_spec=pltpu.PrefetchScalarGridSpec(
            num_scalar_prefetch=2,
            grid=(ns // tm,),
            in_specs=[pl.BlockSpec((tm, D_MODEL), lambda i, te, nu: (jnp.minimum(i, nu[0] - 1), 0)),
                      wspec(D_MODEL, 2 * D_FF), wspec(D_FF, D_MODEL),
                      wspec(1, D_FF), wspec(1, D_FF), wspec(1, D_MODEL),
                      pl.BlockSpec((2 * LANES, 2 * LANES), lambda i, te, nu: (0, 0))],
            out_specs=pl.BlockSpec((tm, D_MODEL), lambda i, te, nu: (i, 0)),
            scratch_shapes=[pltpu.VMEM((D_MODEL, D_FF), BF16), pltpu.VMEM((D_MODEL, D_FF), BF16),
                            pltpu.VMEM((D_FF, D_MODEL), BF16)]),
        out_shape=jax.ShapeDtypeStruct((ns, D_MODEL), F32),
        compiler_params=pltpu.CompilerParams(dimension_semantics=("arbitrary",),
                                             vmem_limit_bytes=EXPERTS_VMEM_LIMIT),
        name="moe_experts",
    )(tile_e, n_used, xs, wgu, wd, bg, bu, bd, _gate_up_selector())


def _combine_body(pos_ref, ys_ref, gate_ref, x1_ref, pe_ref, g_ref, b_ref, wple_ref, wpg_ref,
                  o_ref, buf, sem, *, tm):
    i = pl.program_id(0)
    cur = i % 2

    def gather(tile, slot):
        def make_copy(gi, j, k, p):
            return pltpu.make_async_copy(ys_ref.at[pl.ds(p, 1)], buf.at[slot, k, gi, pl.ds(j, 1)],
                                         sem.at[slot])
        _row_copies(pos_ref, tile * tm, tm, make_copy)

    @pl.when(i == 0)
    def _():
        gather(0, 0)

    for slot in range(2):
        @pl.when((i + 1 < pl.num_programs(0)) & (cur != slot))
        def _():
            gather(i + 1, slot)

    for k in range(TOP_K):
        pltpu.make_async_copy(buf.at[cur, k], buf.at[cur, k], sem.at[cur]).wait()
    gates = gate_ref[...]
    picked = lambda k: buf[cur, k].reshape(tm, D_MODEL)
    ffn = gates[:, 0:1] * picked(0)
    for k in range(1, TOP_K):
        ffn = ffn + gates[:, k:k + 1] * picked(k)
    x2 = _layer_norm(DEEPNORM_ALPHA * x1_ref[...] + ffn, g_ref[...], b_ref[...])
    gate = _sigmoid(jnp.dot(x2.astype(BF16), wpg_ref[...], preferred_element_type=F32))
    emb = jnp.dot(pe_ref[...].astype(BF16), wple_ref[...], preferred_element_type=F32)
    o_ref[...] = x2 + gate * emb


def _combine(pos_flat, ys, gates, x1, pe, g, b, wple_bf16, wpg_bf16, tm):
    t = x1.shape[0]
    row = lambda i, *_: (i, 0)
    fixed = lambda i, *_: (0, 0)
    return pl.pallas_call(
        functools.partial(_combine_body, tm=tm),
        grid_spec=pltpu.PrefetchScalarGridSpec(
            num_scalar_prefetch=1,
            grid=(t // tm,),
            in_specs=[pl.BlockSpec(memory_space=pl.ANY),
                      pl.BlockSpec((tm, LANES), row),
                      pl.BlockSpec((tm, D_MODEL), row),
                      pl.BlockSpec((tm, PLE_DIM), row),
                      pl.BlockSpec((1, D_MODEL), fixed),
                      pl.BlockSpec((1, D_MODEL), fixed),
                      pl.BlockSpec((PLE_DIM, D_MODEL), fixed),
                      pl.BlockSpec((D_MODEL, D_MODEL), fixed)],
            out_specs=pl.BlockSpec((tm, D_MODEL), row),
            scratch_shapes=[pltpu.VMEM((2, TOP_K, tm // ROW_DMA_UNROLL, ROW_DMA_UNROLL, D_MODEL), F32),
                            pltpu.SemaphoreType.DMA((2,))]),
        out_shape=jax.ShapeDtypeStruct((t, D_MODEL), F32),
        compiler_params=_cparams(("arbitrary",)),
        name="moe_combine",
    )(pos_flat, ys, gates, x1, pe, g, b, wple_bf16, wpg_bf16)


def kernel(x_prompt, x_sample, cache_k, cache_v, state_shift, state_wkv, p_prompt, p_sample,
           ln_emb_g, ln_emb_b, w_in, attn_sinks, rwkv_mu, rwkv_w0, rwkv_w_lora_up, rwkv_a0,
           rwkv_a_lora_up, rwkv_g_lora_up, rwkv_k_k, rwkv_k_a, rwkv_r_k, rwkv_gn_g, rwkv_gn_b,
           w_out, ln1_g, ln1_b, w_router, b_router, w_gate_up, b_gate_up, w_down, b_down,
           ln2_g, ln2_b, w_ple, w_ple_gate):
    assert w_in.shape[0] == DEPTH == 1
    nbp, seq, _ = x_prompt.shape
    nbs, dec_seq, _ = x_sample.shape
    assert dec_seq == 1
    tp, ts = nbp * seq, nbs
    w_buf = cache_k.shape[2]
    assert w_buf <= WINDOW
    assert seq % max(BLOCK, RWKV_TILE, INPROJ_TILE, ROUTER_TILE, TOKEN_TILE) == 0
    rowv = lambda z: z.reshape(1, -1)

    w_in_b = w_in[0].astype(BF16)
    w_out_b = w_out[0].astype(BF16)
    w_ple_b = w_ple[0].astype(BF16)
    w_pg_b = w_ple_gate[0].astype(BF16)
    sinks = attn_sinks[0]
    prm = dict(mu=rowv(rwkv_mu[0]), w0=rowv(rwkv_w0[0]), wlu=rwkv_w_lora_up[0],
               a0=rowv(rwkv_a0[0]), alu=rwkv_a_lora_up[0], glu=rwkv_g_lora_up[0],
               k_k=rowv(rwkv_k_k[0]), k_a=rowv(rwkv_k_a[0]), r_k=rowv(rwkv_r_k[0]),
               gn_g=rowv(rwkv_gn_g[0]), gn_b=rowv(rwkv_gn_b[0]))
    ge, be = rowv(ln_emb_g), rowv(ln_emb_b)

    xn_p, q_p, kv_p, rp_p = _inproj(x_prompt.reshape(tp, D_MODEL), ge, be, w_in_b, INPROJ_TILE)
    att_p = _swa_prompt(sinks, q_p, kv_p, nbp, seq)
    rw_p, wkv_p = _rwkv_prompt(rp_p, prm, nbp, seq, RWKV_TILE)

    xn_s, q_s, kv_s, rp_s = _inproj(x_sample.reshape(ts, D_MODEL), ge, be, w_in_b, ts)
    ck3 = cache_k[0].reshape(nbs, w_buf, KV_COLS)
    cv3 = cache_v[0].reshape(nbs, w_buf, KV_COLS)
    att_s = _swa_sample(sinks, q_s.reshape(nbs, ATT_HEADS, HEAD_DIM),
                        kv_s.reshape(nbs, 2 * KV_HEADS, HEAD_DIM), ck3, cv3)
    att_s = att_s.reshape(ts, ATT_WIDTH)
    p7 = _rwkv_sample_prep(rp_s, state_shift[0], prm)
    hv = lambda z: z.reshape(RWKV_HEADS, HEAD_DIM)
    wkv_s, rw_s = _rwkv_sample_step(p7.reshape(7, nbs, RWKV_HEADS, HEAD_DIM), state_wkv[0],
                                    hv(rwkv_r_k[0]), hv(rwkv_gn_g[0]), hv(rwkv_gn_b[0]))
    rw_s = rw_s.reshape(ts, RWKV_WIDTH)

    wr_pad = jnp.zeros((D_MODEL, LANES), F32).at[:, :N_EXPERTS].set(w_router[0])
    wr_hi = wr_pad.astype(BF16)
    wr_lo = (wr_pad - wr_hi.astype(F32)).astype(BF16)
    br_pad = jnp.full((1, LANES), -jnp.inf, F32).at[0, :N_EXPERTS].set(b_router[0])
    g1, b1 = rowv(ln1_g[0]), rowv(ln1_b[0])
    tmp, tmd = ROUTER_TILE, TOKEN_TILE
    x1_p, info_p, gate_p, cnt_p = _mix_router(att_p, rw_p, xn_p, w_out_b, g1, b1, wr_hi, wr_lo,
                                              br_pad, jnp.zeros((1, LANES), F32), tmp)
    x1_s, info_s, gate_s, cnt = _mix_router(att_s, rw_s, xn_s, w_out_b, g1, b1, wr_hi, wr_lo,
                                            br_pad, cnt_p, ts)

    counts = cnt[0, :N_EXPERTS].astype(jnp.int32)
    padded = ((counts + EXPERT_TILE - 1) // EXPERT_TILE) * EXPERT_TILE
    ends = jnp.cumsum(padded)
    offs = ends - padded
    n_slots = (tp + ts) * TOP_K + N_EXPERTS * EXPERT_TILE
    n_tiles = n_slots // EXPERT_TILE
    n_used = (ends[-1] // EXPERT_TILE).reshape(1).astype(jnp.int32)
    tile_start = jnp.arange(n_tiles, dtype=jnp.int32) * EXPERT_TILE
    tile_e = jnp.sum(ends[None, :] <= tile_start[:, None], axis=1).astype(jnp.int32)
    last_e = jnp.max(jnp.where(padded > 0, jnp.arange(N_EXPERTS), 0)).astype(jnp.int32)
    tile_e = jnp.minimum(tile_e, last_e)

    def slots(info):
        expert = jnp.arange(N_EXPERTS, dtype=jnp.int32)[:, None, None]
        first = jnp.sum(jnp.where(info[None, 0:TOP_K] == expert, offs[:, None, None], 0), axis=0)
        return (first + info[TOP_K:2 * TOP_K]).reshape(-1).astype(jnp.int32)

    pos_p, pos_s = slots(info_p), slots(info_s)

    fill_start = jnp.concatenate([jnp.clip(ends - EXPERT_TILE, 0, n_slots - EXPERT_TILE),
                                  n_used]).astype(jnp.int32)
    xs = _dispatch(pos_p, pos_s, fill_start, x1_p, x1_s, n_slots, tmd)
    bgu = b_gate_up[0]
    bg = bgu[:, 0::2].reshape(N_EXPERTS, 1, D_FF)
    bu = bgu[:, 1::2].reshape(N_EXPERTS, 1, D_FF)
    bdn = b_down[0].reshape(N_EXPERTS, 1, D_MODEL)
    ys = _experts(tile_e, n_used, xs, w_gate_up[0], w_down[0], bg, bu, bdn)

    g2, b2 = rowv(ln2_g[0]), rowv(ln2_b[0])
    y_p = _combine(pos_p, ys, gate_p, x1_p, p_prompt[0].reshape(tp, PLE_DIM), g2, b2,
                   w_ple_b, w_pg_b, tmd)
    y_s = _combine(pos_s, ys, gate_s, x1_s, p_sample[0].reshape(ts, PLE_DIM), g2, b2,
                   w_ple_b, w_pg_b, ts)

    w_keep = min(WINDOW, seq)
    kv_p3 = kv_p.reshape(nbp, seq, 2 * KV_COLS)[:, seq - w_keep:]
    k_win_p = kv_p3[:, :, 0:KV_COLS].reshape(1, nbp, w_keep, KV_HEADS, HEAD_DIM)
    v_win_p = kv_p3[:, :, KV_COLS:].reshape(1, nbp, w_keep, KV_HEADS, HEAD_DIM)
    shift_p = rp_p.reshape(nbp, seq, RWKV_PROJ)[:, seq - 1][None]
    k_new = kv_s[:, 0:KV_COLS].reshape(nbs, 1, KV_HEADS, HEAD_DIM)
    v_new = kv_s[:, KV_COLS:].reshape(nbs, 1, KV_HEADS, HEAD_DIM)
    k_win_s = jnp.concatenate([cache_k[0], k_new], axis=1)[:, 1:][None]
    v_win_s = jnp.concatenate([cache_v[0], v_new], axis=1)[:, 1:][None]
    return (y_p.reshape(nbp, seq, D_MODEL), y_s.reshape(nbs, 1, D_MODEL),
            k_win_p, v_win_p, shift_p, wkv_p[None],
            k_win_s, v_win_s, rp_s[None], wkv_s[None])
```
